```python
import math
import jax, jax.numpy as jnp
from jax import lax
import numpy as np

D_MODEL = 2048
BATCH = 4
SEQ = 4096
DEPTH = 4

N_MIXERS = 4
MEM_LEN = 256
RMS_EPS = 1e-6
NEG = -1e30
BIG = 1e30

SSD_EXPAND = 2
SSD_D_INNER = SSD_EXPAND * D_MODEL
SSD_HEAD_DIM = 64
SSD_N_HEADS = SSD_D_INNER // SSD_HEAD_DIM
SSD_N_GROUPS = 8
SSD_HPG = SSD_N_HEADS // SSD_N_GROUPS
SSD_D_STATE = 128
SSD_CONV = 4
SSD_CHUNK = 128
SSD_CONV_DIM = SSD_D_INNER + 2 * SSD_N_GROUPS * SSD_D_STATE
SSD_IN_DIM = SSD_D_INNER + SSD_CONV_DIM + SSD_N_HEADS

NSA_HEAD_DIM = 128
NSA_N_HEADS = D_MODEL // NSA_HEAD_DIM
NSA_KV_GROUPS = 4
NSA_HPG = NSA_N_HEADS // NSA_KV_GROUPS
NSA_KV_DIM = NSA_KV_GROUPS * NSA_HEAD_DIM
NSA_CMP_BLOCK = 32
NSA_CMP_STRIDE = 16
NSA_CMP_HIDDEN = 256
NSA_SLC_BLOCK = 64
NSA_TOPK = 16
NSA_N_LOCAL = 2
NSA_WINDOW = 512
NSA_Q_BLOCK = 64
NSA_WIN_Q_BLOCK = 128
NSA_IN_DIM = NSA_N_HEADS * NSA_HEAD_DIM + 6 * NSA_KV_DIM + 3 * NSA_N_HEADS

SGU_CHUNK = 128
SGU_WIDTH = 2 * D_MODEL
SGU_GROUPS = 8
SGU_GROUP_DIM = SGU_WIDTH // SGU_GROUPS

POOL_WINDOWS = (2, 4, 8, 16)
POOL_GROUPS = len(POOL_WINDOWS)
POOL_GROUP_DIM = D_MODEL // POOL_GROUPS

XA_HEADS = 4
XA_HEAD_DIM = 128
XA_DIM = XA_HEADS * XA_HEAD_DIM

FFN_HIDDEN = 5632
FFN_CONV = 3

kernel_name = "hybrid_interleaved_ssd_nsa_sgu_pool_decoder"


def _rmsnorm(x, g):
    xf = x.astype(jnp.float32)
    y = xf * lax.rsqrt(jnp.mean(xf * xf, axis=-1, keepdims=True) + RMS_EPS)
    return (y * g.astype(jnp.float32)).astype(x.dtype)


def _layernorm(x, g):
    xf = x.astype(jnp.float32)
    mu = jnp.mean(xf, axis=-1, keepdims=True)
    xc = xf - mu
    y = xc * lax.rsqrt(jnp.mean(xc * xc, axis=-1, keepdims=True) + RMS_EPS)
    return (y * g.astype(jnp.float32)).astype(x.dtype)


def _causal_dwconv(x, w, b):
    k_w, L = w.shape[0], x.shape[1]
    xp = jnp.pad(x, ((0, 0), (k_w - 1, 0), (0, 0)))
    y = b + xp[:, 0:L] * w[0]
    for k in range(1, k_w):
        y = y + xp[:, k:k + L] * w[k]
    return y


def _ssd_mixer(h, w_in, conv_w, conv_b, dt_bias, a_log, d_skip, norm_g, w_out):
    f32 = jnp.float32
    bsz, L, _ = h.shape
    G, J, P, N, Q = SSD_N_GROUPS, SSD_HPG, SSD_HEAD_DIM, SSD_D_STATE, SSD_CHUNK
    nc = L // Q
    proj = h @ w_in
    z, xbc, dt = jnp.split(proj, [SSD_D_INNER, SSD_D_INNER + SSD_CONV_DIM], axis=-1)
    xbc = jax.nn.silu(_causal_dwconv(xbc, conv_w, conv_b)).astype(f32)
    xs, bm, cm = jnp.split(xbc, [SSD_D_INNER, SSD_D_INNER + G * N], axis=-1)
    xs = xs.reshape(bsz, nc, Q, G, J, P)
    bm = bm.reshape(bsz, nc, Q, G, N)
    cm = cm.reshape(bsz, nc, Q, G, N)
    dt = jax.nn.softplus(dt.astype(f32) + dt_bias.astype(f32)).reshape(bsz, nc, Q, G, J)
    a = dt * (-jnp.exp(a_log.astype(f32))).reshape(G, J)
    a_cs = jnp.cumsum(a.transpose(0, 3, 4, 1, 2), axis=-1)
    xdt = xs * dt[..., None]
    causal = jnp.tril(jnp.ones((Q, Q), dtype=bool))
    decay = jnp.exp(jnp.where(causal, a_cs[..., :, None] - a_cs[..., None, :], -jnp.inf))
    cb = jnp.einsum('bclgn,bcsgn->bgcls', cm, bm)
    y_diag = jnp.einsum('bgjcls,bcsgjp->bclgjp', cb[:, :, None] * decay, xdt)
    decay_to_end = jnp.exp(a_cs[..., -1:] - a_cs)
    states = jnp.einsum('bcsgn,bgjcs,bcsgjp->bcgjpn', bm, decay_to_end, xdt)
    chunk_decay = jnp.exp(a_cs[..., -1])

    def step(carry, inp):
        st, dec = inp
        return carry * dec[..., None, None] + st, carry

    init = jnp.zeros((bsz, G, J, P, N), f32)
    _, prev = lax.scan(step, init, (jnp.moveaxis(states, 1, 0), jnp.moveaxis(chunk_decay, 3, 0)))
    y_off = jnp.einsum('bclgn,cbgjpn,bgjcl->bclgjp', cm, prev, jnp.exp(a_cs))
    y = y_diag + y_off + xs * d_skip.astype(f32).reshape(G, J)[:, :, None]
    y = y.reshape(bsz, L, SSD_D_INNER)
    y = _rmsnorm(y * jax.nn.silu(z.astype(f32)), norm_g)
    return y.astype(h.dtype) @ w_out


def _nsa_mixer(h, w_in, cmp_pos, cmp_w1, cmp_w2, w_out):
    f32 = jnp.float32
    bsz, L, _ = h.shape
    G, J, Dh = NSA_KV_GROUPS, NSA_HPG, NSA_HEAD_DIM
    scale = Dh ** -0.5
    proj = h @ w_in
    offs = np.cumsum([NSA_N_HEADS * Dh] + [NSA_KV_DIM] * 6).tolist()
    q, kc_in, vc_in, ks, vs, kw, vw, gate_logits = jnp.split(proj, offs, axis=-1)
    q = q.reshape(bsz, L, G, J, Dh)
    kc_in, vc_in, ks, vs, kw, vw = [t.reshape(bsz, L, G, Dh) for t in (kc_in, vc_in, ks, vs, kw, vw)]
    gates = jax.nn.sigmoid(gate_logits.astype(f32)).reshape(bsz, L, G, J, 3).astype(h.dtype)
    t_pos = jnp.arange(L)

    ncmp = (L - NSA_CMP_BLOCK) // NSA_CMP_STRIDE + 1
    cmp_idx = np.arange(ncmp)[:, None] * NSA_CMP_STRIDE + np.arange(NSA_CMP_BLOCK)[None, :]

    def compress(t, pos, w1, w2):
        blk = t[:, cmp_idx] + pos[None, None, :, None, :]
        flat = blk.transpose(0, 1, 3, 2, 4).reshape(bsz, ncmp, G, NSA_CMP_BLOCK * Dh)
        return jax.nn.gelu(flat @ w1) @ w2

    kc = compress(kc_in, cmp_pos[0], cmp_w1[0], cmp_w2[0])
    vc = compress(vc_in, cmp_pos[1], cmp_w1[1], cmp_w2[1])
    s_c = jnp.einsum('btgjd,bigd->bgjti', q, kc).astype(f32) * scale
    cmp_end = jnp.arange(ncmp) * NSA_CMP_STRIDE + NSA_CMP_BLOCK - 1
    valid_c = cmp_end[None, :] <= t_pos[:, None]
    p_c = jnp.where(valid_c, jax.nn.softmax(jnp.where(valid_c, s_c, NEG), axis=-1), 0.0)
    o_cmp = jnp.einsum('bgjti,bigd->btgjd', p_c.astype(vc.dtype), vc)

    nslc = L // NSA_SLC_BLOCK
    ci = np.arange(ncmp)[:, None] * NSA_CMP_STRIDE
    sj = np.arange(nslc)[None, :] * NSA_SLC_BLOCK
    cover = jnp.asarray(((ci <= sj + NSA_SLC_BLOCK - 1) & (ci + NSA_CMP_BLOCK - 1 >= sj)).astype(np.float32))
    imp = jnp.einsum('bgjti,ik->bgtk', p_c, cover)
    blk = jnp.arange(nslc)[None, :]
    cur = (t_pos // NSA_SLC_BLOCK)[:, None]
    forced = (blk == 0) | ((blk <= cur) & (blk > cur - 1 - NSA_N_LOCAL))
    future = blk * NSA_SLC_BLOCK > t_pos[:, None]
    sel_score = jnp.where(forced, BIG, jnp.where(future, NEG, imp))
    topk = min(NSA_TOPK, nslc)
    _, sel_idx = lax.top_k(sel_score, topk)

    ks_blk = ks.reshape(bsz, nslc, NSA_SLC_BLOCK, G, Dh).transpose(0, 3, 1, 2, 4)
    vs_blk = vs.reshape(bsz, nslc, NSA_SLC_BLOCK, G, Dh).transpose(0, 3, 1, 2, 4)
    nqb = L // NSA_Q_BLOCK
    q_blocks = q.reshape(bsz, nqb, NSA_Q_BLOCK, G, J, Dh).transpose(1, 0, 2, 3, 4, 5)
    idx_blocks = sel_idx.reshape(bsz, G, nqb, NSA_Q_BLOCK, topk).transpose(2, 0, 1, 3, 4)
    starts = jnp.arange(nqb, dtype=jnp.int32) * NSA_Q_BLOCK
    b_i = jnp.arange(bsz)[:, None, None, None]
    g_i = jnp.arange(G)[None, :, None, None]
    in_blk = jnp.arange(NSA_SLC_BLOCK)

    def sel_block(args):
        qb, ib, t0 = args
        kg = ks_blk[b_i, g_i, ib]
        vg = vs_blk[b_i, g_i, ib]
        s = jnp.einsum('bqgjd,bgqksd->bgjqks', qb, kg).astype(f32) * scale
        pos = ib[..., None] * NSA_SLC_BLOCK + in_blk
        tq = t0 + jnp.arange(NSA_Q_BLOCK)
        valid = (pos <= tq[None, None, :, None, None])[:, :, None]
        p = jax.nn.softmax(jnp.where(valid, s, NEG), axis=(-2, -1))
        return jnp.einsum('bgjqks,bgqksd->bqgjd', p.astype(vg.dtype), vg)

    o_slc = lax.map(sel_block, (q_blocks, idx_blocks, starts))
    o_slc = o_slc.transpose(1, 0, 2, 3, 4, 5).reshape(bsz, L, G, J, Dh)

    WB = NSA_WIN_Q_BLOCK
    nb = L // WB
    nslab = NSA_WINDOW // WB + 1
    slab_len = nslab * WB
    slab_idx = np.arange(nb)[:, None] + np.arange(nslab)[None, :]
    kp = jnp.pad(kw, ((0, 0), (NSA_WINDOW, 0), (0, 0), (0, 0))).reshape(bsz, nb + nslab - 1, WB, G, Dh)
    vp = jnp.pad(vw, ((0, 0), (NSA_WINDOW, 0), (0, 0), (0, 0))).reshape(bsz, nb + nslab - 1, WB, G, Dh)
    k_slab = kp[:, slab_idx].reshape(bsz, nb, slab_len, G, Dh)
    v_slab = vp[:, slab_idx].reshape(bsz, nb, slab_len, G, Dh)
    qw = q.reshape(bsz, nb, WB, G, J, Dh)
    s_w = jnp.einsum('bnqgjd,bnkgd->bgjnqk', qw, k_slab).astype(f32) * scale
    spos = jnp.arange(nb)[:, None] * WB - NSA_WINDOW + jnp.arange(slab_len)[None, :]
    tpos = jnp.arange(nb)[:, None] * WB + jnp.arange(WB)[None, :]
    diff = tpos[:, :, None] - spos[:, None, :]
    valid_w = (spos[:, None, :] >= 0) & (diff >= 0) & (diff < NSA_WINDOW)
    p_w = jax.nn.softmax(jnp.where(valid_w, s_w, NEG), axis=-1)
    o_win = jnp.einsum('bgjnqk,bnkgd->bnqgjd', p_w.astype(v_slab.dtype), v_slab).reshape(bsz, L, G, J, Dh)

    o = gates[..., 0:1] * o_cmp + gates[..., 1:2] * o_slc + gates[..., 2:3] * o_win
    return o.reshape(bsz, L, NSA_N_HEADS * Dh) @ w_out


def _sgu_mixer(h, w_in, b_in, ln_g, w_spatial, b_spatial, w_out):
    bsz, L, _ = h.shape
    nc = L // SGU_CHUNK
    proj = jax.nn.gelu(h @ w_in + b_in)
    u, v = jnp.split(proj, 2, axis=-1)
    v = _layernorm(v, ln_g).reshape(bsz, nc, SGU_CHUNK, SGU_GROUPS, SGU_GROUP_DIM)
    tri = jnp.tril(jnp.ones((SGU_CHUNK, SGU_CHUNK), dtype=bool))
    w_m = jnp.where(tri[None], w_spatial, 0.0).astype(v.dtype)
    sv = jnp.einsum('gts,bcsgd->bctgd', w_m, v) + b_spatial.T[:, :, None]
    return (u * sv.reshape(bsz, L, SGU_WIDTH)) @ w_out


def _pool_mixer(h, w_in, w_group, scale, w_out):
    bsz, L, _ = h.shape
    z = (h @ w_in).reshape(bsz, L, POOL_GROUPS, POOL_GROUP_DIM).astype(jnp.float32)
    cs0 = jnp.concatenate([jnp.zeros_like(z[:, :1]), jnp.cumsum(z, axis=1)], axis=1)
    t_pos = jnp.arange(L)
    outs = []
    for gi, win in enumerate(POOL_WINDOWS):
        c = cs0[:, :, gi]
        lower = jnp.concatenate([jnp.zeros_like(c[:, :win - 1]), c[:, :L + 1 - win]], axis=1)
        count = jnp.minimum(t_pos + 1, win).astype(jnp.float32)[None, :, None]
        outs.append((c[:, 1:] - lower) / count - z[:, :, gi])
    pooled = jnp.stack(outs, axis=2).astype(h.dtype)
    y = jnp.einsum('blgd,gde->blge', pooled, w_group) * scale
    return y.reshape(bsz, L, D_MODEL) @ w_out


def _memory_cross_attention(h, mem_n, w_q, w_kv, w_o):
    bsz, L, _ = h.shape
    q = (h @ w_q).reshape(bsz, L, XA_HEADS, XA_HEAD_DIM)
    k, v = jnp.split(mem_n @ w_kv, 2, axis=-1)
    k = k.reshape(bsz, -1, XA_HEADS, XA_HEAD_DIM)
    v = v.reshape(bsz, -1, XA_HEADS, XA_HEAD_DIM)
    s = jnp.einsum('bthd,bmhd->bhtm', q, k).astype(jnp.float32) * (XA_HEAD_DIM ** -0.5)
    p = jax.nn.softmax(s, axis=-1).astype(v.dtype)
    o = jnp.einsum('bhtm,bmhd->bthd', p, v).reshape(bsz, L, XA_DIM)
    return o @ w_o


def _conv_ffn(h, w_up, conv_w, conv_b, w_down):
    u = _causal_dwconv(h @ w_up, conv_w, conv_b)
    gate, val = jnp.split(u, 2, axis=-1)
    return (jax.nn.silu(gate) * val) @ w_down


def setup_inputs(seed: int = 0) -> dict:
    key = jax.random.key(seed)
    keys = iter(jax.random.split(key, 64))
    f32 = jnp.float32

    def nrm(shape, fan_in):
        return jax.random.normal(next(keys), shape, f32) * (fan_in ** -0.5)

    def gain(shape):
        return 1.0 + 0.05 * jax.random.normal(next(keys), shape, f32)

    def small(shape, s=0.02):
        return s * jax.random.normal(next(keys), shape, f32)

    n_a, n_b, n_c, n_d = [len(range(m, DEPTH, N_MIXERS)) for m in range(N_MIXERS)]
    x = jax.random.normal(next(keys), (BATCH, SEQ, D_MODEL), f32)
    mem = jax.random.normal(next(keys), (BATCH, MEM_LEN, D_MODEL), f32)
    norm_pre = gain((DEPTH, 3, D_MODEL))
    norm_post = gain((DEPTH, 3, D_MODEL))
    norm_mem = gain((DEPTH, D_MODEL))
    ssd_w_in = nrm((n_a, D_MODEL, SSD_IN_DIM), D_MODEL)
    ssd_conv_w = nrm((n_a, SSD_CONV, SSD_CONV_DIM), SSD_CONV)
    ssd_conv_b = small((n_a, SSD_CONV_DIM))
    dt0 = jnp.exp(jax.random.uniform(next(keys), (n_a, SSD_N_HEADS), f32, math.log(1e-3), math.log(1e-1)))
    ssd_dt_bias = dt0 + jnp.log(-jnp.expm1(-dt0))
    ssd_a_log = jnp.log(jax.random.uniform(next(keys), (n_a, SSD_N_HEADS), f32, 1.0, 16.0))
    ssd_d = gain((n_a, SSD_N_HEADS))
    ssd_norm_g = gain((n_a, SSD_D_INNER))
    ssd_w_out = nrm((n_a, SSD_D_INNER, D_MODEL), SSD_D_INNER)
    nsa_w_in = nrm((n_b, D_MODEL, NSA_IN_DIM), D_MODEL)
    nsa_cmp_pos = small((n_b, 2, NSA_CMP_BLOCK, NSA_HEAD_DIM), 0.1)
    nsa_cmp_w1 = nrm((n_b, 2, NSA_CMP_BLOCK * NSA_HEAD_DIM, NSA_CMP_HIDDEN), NSA_CMP_BLOCK * NSA_HEAD_DIM)
    nsa_cmp_w2 = nrm((n_b, 2, NSA_CMP_HIDDEN, NSA_HEAD_DIM), NSA_CMP_HIDDEN)
    nsa_w_out = nrm((n_b, NSA_N_HEADS * NSA_HEAD_DIM, D_MODEL), NSA_N_HEADS * NSA_HEAD_DIM)
    sgu_w_in = nrm((n_c, D_MODEL, 2 * SGU_WIDTH), D_MODEL)
    sgu_b_in = small((n_c, 2 * SGU_WIDTH))
    sgu_ln_g = gain((n_c, SGU_WIDTH))
    sgu_w_spatial = nrm((n_c, SGU_GROUPS, SGU_CHUNK, SGU_CHUNK), SGU_CHUNK)
    sgu_b_spatial = 1.0 + small((n_c, SGU_GROUPS, SGU_CHUNK), 0.1)
    sgu_w_out = nrm((n_c, SGU_WIDTH, D_MODEL), SGU_WIDTH)
    pool_w_in = nrm((n_d, D_MODEL, D_MODEL), D_MODEL)
    pool_w_group = nrm((n_d, POOL_GROUPS, POOL_GROUP_DIM, POOL_GROUP_DIM), POOL_GROUP_DIM)
    pool_scale = 1.0 + small((n_d, POOL_GROUPS, POOL_GROUP_DIM), 0.1)
    pool_w_out = nrm((n_d, D_MODEL, D_MODEL), D_MODEL)
    xa_w_q = nrm((DEPTH, D_MODEL, XA_DIM), D_MODEL)
    xa_w_kv = nrm((DEPTH, D_MODEL, 2 * XA_DIM), D_MODEL)
    xa_w_o = nrm((DEPTH, XA_DIM, D_MODEL), XA_DIM)
    ffn_w_up = nrm((DEPTH, D_MODEL, 2 * FFN_HIDDEN), D_MODEL)
    ffn_conv_w = nrm((DEPTH, FFN_CONV, 2 * FFN_HIDDEN), FFN_CONV)
    ffn_conv_b = small((DEPTH, 2 * FFN_HIDDEN))
    ffn_w_down = nrm((DEPTH, FFN_HIDDEN, D_MODEL), FFN_HIDDEN)
    return {
        "x": x, "mem": mem,
        "norm_pre": norm_pre, "norm_post": norm_post, "norm_mem": norm_mem,
        "ssd_w_in": ssd_w_in, "ssd_conv_w": ssd_conv_w, "ssd_conv_b": ssd_conv_b,
        "ssd_dt_bias": ssd_dt_bias, "ssd_a_log": ssd_a_log, "ssd_d": ssd_d,
        "ssd_norm_g": ssd_norm_g, "ssd_w_out": ssd_w_out,
        "nsa_w_in": nsa_w_in, "nsa_cmp_pos": nsa_cmp_pos, "nsa_cmp_w1": nsa_cmp_w1,
        "nsa_cmp_w2": nsa_cmp_w2, "nsa_w_out": nsa_w_out,
        "sgu_w_in": sgu_w_in, "sgu_b_in": sgu_b_in, "sgu_ln_g": sgu_ln_g,
        "sgu_w_spatial": sgu_w_spatial, "sgu_b_spatial": sgu_b_spatial, "sgu_w_out": sgu_w_out,
        "pool_w_in": pool_w_in, "pool_w_group": pool_w_group, "pool_scale": pool_scale,
        "pool_w_out": pool_w_out,
        "xa_w_q": xa_w_q, "xa_w_kv": xa_w_kv, "xa_w_o": xa_w_o,
        "ffn_w_up": ffn_w_up, "ffn_conv_w": ffn_conv_w, "ffn_conv_b": ffn_conv_b,
        "ffn_w_down": ffn_w_down,
    }


def reference(x, mem, norm_pre, norm_post, norm_mem,
              ssd_w_in, ssd_conv_w, ssd_conv_b, ssd_dt_bias, ssd_a_log, ssd_d, ssd_norm_g, ssd_w_out,
              nsa_w_in, nsa_cmp_pos, nsa_cmp_w1, nsa_cmp_w2, nsa_w_out,
              sgu_w_in, sgu_b_in, sgu_ln_g, sgu_w_spatial, sgu_b_spatial, sgu_w_out,
              pool_w_in, pool_w_group, pool_scale, pool_w_out,
              xa_w_q, xa_w_kv, xa_w_o,
              ffn_w_up, ffn_conv_w, ffn_conv_b, ffn_w_down):
    for i in range(DEPTH):
        kind, j = i % N_MIXERS, i // N_MIXERS
        h = _rmsnorm(x, norm_pre[i, 0])
        if kind == 0:
            m = _ssd_mixer(h, ssd_w_in[j], ssd_conv_w[j], ssd_conv_b[j], ssd_dt_bias[j],
                           ssd_a_log[j], ssd_d[j], ssd_norm_g[j], ssd_w_out[j])
        elif kind == 1:
            m = _nsa_mixer(h, nsa_w_in[j], nsa_cmp_pos[j], nsa_cmp_w1[j], nsa_cmp_w2[j], nsa_w_out[j])
        elif kind == 2:
            m = _sgu_mixer(h, sgu_w_in[j], sgu_b_in[j], sgu_ln_g[j], sgu_w_spatial[j],
                           sgu_b_spatial[j], sgu_w_out[j])
        else:
            m = _pool_mixer(h, pool_w_in[j], pool_w_group[j], pool_scale[j], pool_w_out[j])
        x = x + _rmsnorm(m, norm_post[i, 0])
        h = _rmsnorm(x, norm_pre[i, 1])
        a = _memory_cross_attention(h, _rmsnorm(mem, norm_mem[i]), xa_w_q[i], xa_w_kv[i], xa_w_o[i])
        x = x + _rmsnorm(a, norm_post[i, 1])
        h = _rmsnorm(x, norm_pre[i, 2])
        f = _conv_ffn(h, ffn_w_up[i], ffn_conv_w[i], ffn_conv_b[i], ffn_w_down[i])
        x = x + _rmsnorm(f, norm_post[i, 2])
    return x
```

```python
import functools
import math

import jax
import jax.numpy as jnp
import numpy as np
from jax import lax
from jax.experimental import pallas as pl
from jax.experimental.pallas import tpu as pltpu

F32 = jnp.float32
BF16 = jnp.bfloat16

D_MODEL = 2048
RMS_EPS = 1e-6
NEG = -1e30
BIG = 1e30
MEM_LEN = 256

SSD_D_INNER = 4096
SSD_HEAD_DIM = 64
SSD_N_HEADS = 64
SSD_N_GROUPS = 8
SSD_HPG = 8
SSD_D_STATE = 128
SSD_CONV = 4
SSD_CHUNK = 128
SSD_GROUP_W = SSD_HPG * SSD_HEAD_DIM
SSD_BC_DIM = SSD_N_GROUPS * SSD_D_STATE
SSD_IN_PAD = 10752

NSA_HEAD_DIM = 128
NSA_N_HEADS = 16
NSA_KV_GROUPS = 4
NSA_HPG = 4
NSA_KV_DIM = 512
NSA_CMP_BLOCK = 32
NSA_CMP_STRIDE = 16
NSA_CMP_HIDDEN = 256
NSA_SLC_BLOCK = 64
NSA_TOPK = 16
NSA_N_LOCAL = 2
NSA_WINDOW = 512
NSA_TQ = 128
NSA_IN_PAD = 5632
NSA_GATE_COL = 5120

SGU_CHUNK = 128
SGU_WIDTH = 4096
SGU_GROUPS = 8
SGU_GROUP_DIM = 512

POOL_WINDOWS = (2, 4, 8, 16)
POOL_GROUP_DIM = 512
POOL_HALO = 16

XA_HEADS = 4
XA_HEAD_DIM = 128
XA_DIM = 512

FFN_HIDDEN = 5632
FFN_CONV = 3
FFN_HALO = 16

LANE = 128
VMEM_LIMIT = 56 * 1024 * 1024


def _cparams(*sem):
    return pltpu.CompilerParams(dimension_semantics=sem, vmem_limit_bytes=VMEM_LIMIT)


def _dot(a, b):
    return jnp.dot(a, b, preferred_element_type=F32)


def _dot_nt(a, b):
    return lax.dot_general(a, b, (((1,), (1,)), ((), ())), preferred_element_type=F32)


def _split3(v):
    hi = v.astype(BF16)
    r1 = v - hi.astype(F32)
    mid = r1.astype(BF16)
    lo = (r1 - mid.astype(F32)).astype(BF16)
    return hi, mid, lo


def _dot_f32_lhs(v, e):
    hi, mid, lo = _split3(v)
    return _dot(hi, e) + _dot(mid, e) + _dot(lo, e)


def _dot_f32_rhs(e, v):
    hi, mid, lo = _split3(v)
    return _dot(e, hi) + _dot(e, mid) + _dot(e, lo)


def _rms(x, g):
    ms = jnp.mean(x * x, axis=-1, keepdims=True)
    return x * lax.rsqrt(ms + RMS_EPS) * g


def _sigmoid(x):
    return 1.0 / (1.0 + jnp.exp(-x))


def _silu(x):
    return x * _sigmoid(x)


def _gelu(x):
    c = math.sqrt(2.0 / math.pi)
    return x * (0.5 * (1.0 + jnp.tanh(c * (x + 0.044715 * (x * x * x)))))


def _softplus(x):
    return jnp.maximum(x, 0.0) + jnp.log1p(jnp.exp(-jnp.abs(x)))


def _norm_mm_kernel(x_ref, g_ref, w_ref, b_ref, o_ref, h_ref, *, act):
    @pl.when(pl.program_id(1) == 0)
    def _():
        h_ref[...] = _rms(x_ref[...], g_ref[...]).astype(BF16)

    y = _dot(h_ref[...], w_ref[...]) + b_ref[...]
    if act == "gelu":
        y = _gelu(y)
    o_ref[...] = y.astype(o_ref.dtype)


def _norm_matmul(x, g, w, b, *, act=None, out_dtype=F32, tm=512, tn=512):
    m, k = x.shape
    n = w.shape[1]
    tm = min(tm, m)
    assert m % tm == 0 and n % tn == 0, (m, n, tm, tn)
    return pl.pallas_call(
        functools.partial(_norm_mm_kernel, act=act),
        grid=(m // tm, n // tn),
        in_specs=[
            pl.BlockSpec((tm, k), lambda i, j: (i, 0)),
            pl.BlockSpec((1, k), lambda i, j: (0, 0)),
            pl.BlockSpec((k, tn), lambda i, j: (0, j)),
            pl.BlockSpec((1, tn), lambda i, j: (0, j)),
        ],
        out_specs=pl.BlockSpec((tm, tn), lambda i, j: (i, j)),
        out_shape=jax.ShapeDtypeStruct((m, n), out_dtype),
        scratch_shapes=[pltpu.VMEM((tm, k), BF16)],
        compiler_params=_cparams("parallel", "arbitrary"),
        name="norm_matmul",
    )(x, g.reshape(1, k), w, b.reshape(1, n))


def _mm_post_kernel(a_ref, w_ref, g_ref, r_ref, o_ref, acc_ref, *, nk):
    k = pl.program_id(1)
    part = _dot(a_ref[...], w_ref[...])

    @pl.when(k == 0)
    def _():
        acc_ref[...] = part

    @pl.when(k > 0)
    def _():
        acc_ref[...] += part

    @pl.when(k == nk - 1)
    def _():
        o_ref[...] = r_ref[...] + _rms(acc_ref[...], g_ref[...])


def _matmul_post(a, w, g, res, *, tm=512, tk=512):
    m, kdim = a.shape
    n = w.shape[1]
    tm = min(tm, m)
    assert m % tm == 0 and kdim % tk == 0
    nk = kdim // tk
    return pl.pallas_call(
        functools.partial(_mm_post_kernel, nk=nk),
        grid=(m // tm, nk),
        in_specs=[
            pl.BlockSpec((tm, tk), lambda i, k: (i, k)),
            pl.BlockSpec((tk, n), lambda i, k: (k, 0)),
            pl.BlockSpec((1, n), lambda i, k: (0, 0)),
            pl.BlockSpec((tm, n), lambda i, k: (i, 0)),
        ],
        out_specs=pl.BlockSpec((tm, n), lambda i, k: (i, 0)),
        out_shape=jax.ShapeDtypeStruct((m, n), F32),
        scratch_shapes=[pltpu.VMEM((tm, n), F32)],
        compiler_params=_cparams("parallel", "arbitrary"),
        name="matmul_postnorm_residual",
    )(a, w, g.reshape(1, n), res)


def _ffn_up_kernel(x_ref, xh_ref, g_ref, wg_ref, wv_ref, cwg_ref, cwv_ref, cbg_ref, cbv_ref,
                   o_ref, h_ref, ug_ref, uv_ref, *, tm, blocks_per_seq):
    i = pl.program_id(0)

    @pl.when(pl.program_id(1) == 0)
    def _():
        keep = (i % blocks_per_seq != 0).astype(F32)
        h_ref[0:FFN_HALO, :] = (_rms(xh_ref[...], g_ref[...]) * keep).astype(BF16)
        h_ref[FFN_HALO:, :] = _rms(x_ref[...], g_ref[...]).astype(BF16)

    h = h_ref[...]
    ug_ref[...] = _dot(h, wg_ref[...])
    uv_ref[...] = _dot(h, wv_ref[...])

    def conv(u_ref, cw_ref, cb_ref):
        y = cb_ref[...]
        for tap in range(FFN_CONV):
            lo = FFN_HALO - (FFN_CONV - 1) + tap
            y = y + u_ref[lo:lo + tm, :] * cw_ref[tap:tap + 1, :]
        return y

    gate = conv(ug_ref, cwg_ref, cbg_ref)
    val = conv(uv_ref, cwv_ref, cbv_ref)
    o_ref[...] = (_silu(gate) * val).astype(o_ref.dtype)


def _ffn_up(x, g, w_up, conv_w, conv_b, seq_len, *, tm=512, tn=512):
    m, k = x.shape
    tm = min(tm, seq_len)
    assert m % tm == 0 and seq_len % tm == 0 and FFN_HIDDEN % tn == 0
    nj = FFN_HIDDEN // tn
    halo_blocks = tm // FFN_HALO
    return pl.pallas_call(
        functools.partial(_ffn_up_kernel, tm=tm, blocks_per_seq=seq_len // tm),
        grid=(m // tm, nj),
        in_specs=[
            pl.BlockSpec((tm, k), lambda i, j: (i, 0)),
            pl.BlockSpec((FFN_HALO, k), lambda i, j: (jnp.maximum(i * halo_blocks - 1, 0), 0)),
            pl.BlockSpec((1, k), lambda i, j: (0, 0)),
            pl.BlockSpec((k, tn), lambda i, j: (0, j)),
            pl.BlockSpec((k, tn), lambda i, j: (0, j + nj)),
            pl.BlockSpec((FFN_CONV, tn), lambda i, j: (0, j)),
            pl.BlockSpec((FFN_CONV, tn), lambda i, j: (0, j + nj)),
            pl.BlockSpec((1, tn), lambda i, j: (0, j)),
            pl.BlockSpec((1, tn), lambda i, j: (0, j + nj)),
        ],
        out_specs=pl.BlockSpec((tm, tn), lambda i, j: (i, j)),
        out_shape=jax.ShapeDtypeStruct((m, FFN_HIDDEN), BF16),
        scratch_shapes=[
            pltpu.VMEM((tm + FFN_HALO, k), BF16),
            pltpu.VMEM((tm + FFN_HALO, tn), F32),
            pltpu.VMEM((tm + FFN_HALO, tn), F32),
        ],
        compiler_params=_cparams("parallel", "arbitrary"),
        name="ffn_up_conv_gate",
    )(x, x, g.reshape(1, k), w_up, w_up, conv_w, conv_w,
      conv_b.reshape(1, -1), conv_b.reshape(1, -1))


def _xattn_kernel(x_ref, kv_ref, gpre_ref, gpost_ref, wq_ref, wo_ref, o_ref, att_ref):
    x = x_ref[...]
    h = _rms(x, gpre_ref[...]).astype(BF16)
    q = _dot(h, wq_ref[...]).astype(BF16)
    scale = XA_HEAD_DIM ** -0.5
    for hd in range(XA_HEADS):
        lo = hd * XA_HEAD_DIM
        s = _dot_nt(q[:, lo:lo + XA_HEAD_DIM], kv_ref[:, lo:lo + XA_HEAD_DIM]) * scale
        e = jnp.exp(s - jnp.max(s, axis=-1, keepdims=True))
        p = e / jnp.sum(e, axis=-1, keepdims=True)
        att_ref[:, lo:lo + XA_HEAD_DIM] = _dot(
            p.astype(BF16), kv_ref[:, XA_DIM + lo:XA_DIM + lo + XA_HEAD_DIM]).astype(BF16)
    a = _dot(att_ref[...], wo_ref[...])
    o_ref[...] = x + _rms(a, gpost_ref[...])


def _xattn(x, kv, g_pre, g_post, w_q, w_o, seq_len, *, tm=512):
    m, k = x.shape
    tm = min(tm, seq_len)
    bps = seq_len // tm
    return pl.pallas_call(
        _xattn_kernel,
        grid=(m // tm,),
        in_specs=[
            pl.BlockSpec((tm, k), lambda i: (i, 0)),
            pl.BlockSpec((MEM_LEN, 2 * XA_DIM), lambda i: (i // bps, 0)),
            pl.BlockSpec((1, k), lambda i: (0, 0)),
            pl.BlockSpec((1, k), lambda i: (0, 0)),
            pl.BlockSpec((k, XA_DIM), lambda i: (0, 0)),
            pl.BlockSpec((XA_DIM, k), lambda i: (0, 0)),
        ],
        out_specs=pl.BlockSpec((tm, k), lambda i: (i, 0)),
        out_shape=jax.ShapeDtypeStruct((m, k), F32),
        scratch_shapes=[pltpu.VMEM((tm, XA_DIM), BF16)],
        compiler_params=_cparams("parallel"),
        name="memory_cross_attention",
    )(x, kv, g_pre.reshape(1, k), g_post.reshape(1, k), w_q, w_o)


def _ssd_kernel(z_ref, xr_ref, br_ref, cr_ref, dtr_ref, cw_ref, cbias_ref, dtb_ref, alog_ref,
                dskip_ref, ng_ref, expand_ref, tril_ref, o_ref,
                xin_ref, xs_ref, bm_ref, cm_ref, dtx_ref, dtex_ref, expax_ref, cdx_ref,
                acsg_ref, acst_ref, state_ref, y_ref):
    q = SSD_CHUNK
    c = pl.program_id(1)
    halo = 8
    xw = SSD_D_INNER

    @pl.when(c == 0)
    def _():
        xin_ref[0:halo, :] = jnp.zeros((halo, xin_ref.shape[1]), F32)
        state_ref[...] = jnp.zeros(state_ref.shape, F32)

    @pl.when(c > 0)
    def _():
        xin_ref[0:halo, :] = xin_ref[q:q + halo, :]

    xin_ref[halo:, 0:xw] = xr_ref[...]
    xin_ref[halo:, xw:xw + SSD_BC_DIM] = br_ref[...]
    xin_ref[halo:, xw + SSD_BC_DIM:] = cr_ref[...]

    def conv_tile(lo, width):
        y = cbias_ref[:, pl.ds(lo, width)]
        for tap in range(SSD_CONV):
            r0 = halo - (SSD_CONV - 1) + tap
            y = y + xin_ref[r0:r0 + q, pl.ds(lo, width)] * cw_ref[tap:tap + 1, pl.ds(lo, width)]
        return _silu(y)

    def conv_x(t, carry):
        lo = pl.multiple_of(t * 512, 512)
        xs_ref[:, pl.ds(lo, 512)] = conv_tile(lo, 512)
        return carry

    lax.fori_loop(0, xw // 512, conv_x, 0)
    for t in range(SSD_BC_DIM // 512):
        bm_ref[:, t * 512:(t + 1) * 512] = conv_tile(xw + t * 512, 512)
        cm_ref[:, t * 512:(t + 1) * 512] = conv_tile(xw + SSD_BC_DIM + t * 512, 512)

    dt = _softplus(dtr_ref[...] + dtb_ref[...])
    a = dt * (-jnp.exp(alog_ref[...]))
    a_cs = _dot_f32_rhs(tril_ref[...], a)
    a_last = a_cs[q - 1:q, :]
    expand = expand_ref[...]
    dtx_ref[...] = _dot_f32_lhs(dt, expand)
    dtex_ref[...] = _dot_f32_lhs(jnp.exp(a_last - a_cs), expand)
    expax_ref[...] = _dot_f32_lhs(jnp.exp(a_cs), expand)
    cdx_ref[...] = _dot_f32_lhs(jnp.broadcast_to(jnp.exp(a_last), (8, LANE)), expand)
    acst_ref[...] = a_cs.T.reshape(LANE // SSD_HPG, SSD_HPG, q)
    for g in range(SSD_N_GROUPS):
        acsg_ref[g] = a_cs if g == 0 else pltpu.roll(a_cs, LANE - g * SSD_HPG, axis=1)

    row = lax.broadcasted_iota(jnp.int32, (q, q), 0)
    col = lax.broadcasted_iota(jnp.int32, (q, q), 1)
    causal = row >= col
    left_half = col < SSD_HEAD_DIM

    def group_body(g, carry):
        off = pl.multiple_of(g * SSD_GROUP_W, SSD_GROUP_W)
        offn = pl.multiple_of(g * SSD_D_STATE, SSD_D_STATE)
        cm = cm_ref[:, pl.ds(offn, SSD_D_STATE)]
        bm = bm_ref[:, pl.ds(offn, SSD_D_STATE)]
        cmb = cm.astype(BF16)
        cb = _dot_nt(cmb, bm.astype(BF16))
        xs = xs_ref[:, pl.ds(off, SSD_GROUP_W)]
        xdt = xs * dtx_ref[:, pl.ds(off, SSD_GROUP_W)]
        st = state_ref[g]
        y_off = _dot(cmb, st.astype(BF16)) * expax_ref[:, pl.ds(off, SSD_GROUP_W)]
        s_new = _dot(bm.T.astype(BF16), (xdt * dtex_ref[:, pl.ds(off, SSD_GROUP_W)]).astype(BF16))
        state_ref[g] = st * cdx_ref[0:1, pl.ds(off, SSD_GROUP_W)] + s_new
        acs_g = acsg_ref[g]
        acst_g = acst_ref[g]
        skip = xs * dskip_ref[:, pl.ds(off, SSD_GROUP_W)]
        for pr in range(SSD_HPG // 2):
            scs = []
            for e in range(2):
                j = 2 * pr + e
                diff = acs_g[:, j:j + 1] - acst_g[j:j + 1, :]
                dec = jnp.where(causal, jnp.exp(diff), 0.0)
                scs.append((cb * dec).astype(BF16))
            xp = xdt[:, pr * LANE:(pr + 1) * LANE]
            x0 = jnp.where(left_half, xp, 0.0).astype(BF16)
            x1 = jnp.where(left_half, 0.0, xp).astype(BF16)
            y_diag = _dot(jnp.concatenate(scs, axis=1), jnp.concatenate([x0, x1], axis=0))
            y = y_diag + y_off[:, pr * LANE:(pr + 1) * LANE] + skip[:, pr * LANE:(pr + 1) * LANE]
            y_ref[:, pl.ds(pl.multiple_of(off + pr * LANE, LANE), LANE)] = y
        return carry

    lax.fori_loop(0, SSD_N_GROUPS, group_body, 0)

    gated = y_ref[...] * _silu(z_ref[...])
    o_ref[...] = _rms(gated, ng_ref[...]).astype(o_ref.dtype)


def _ssd_core(proj, conv_w, conv_b, dt_bias, a_log, d_skip, norm_g, bsz, seq_len):
    q = SSD_CHUNK
    nc = seq_len // q
    m = bsz * seq_len
    conv_dim = SSD_D_INNER + 2 * SSD_BC_DIM
    pad = LANE - SSD_N_HEADS
    dtb = jnp.pad(dt_bias, (0, pad)).reshape(1, LANE)
    alog = jnp.pad(a_log, (0, pad)).reshape(1, LANE)
    dskip = jnp.repeat(d_skip, SSD_HEAD_DIM).reshape(1, SSD_D_INNER)
    heads = np.arange(LANE)[:, None]
    chans = np.arange(SSD_D_INNER)[None, :] // SSD_HEAD_DIM
    expand = jnp.asarray((heads == chans).astype(np.float32), dtype=BF16)
    tril = jnp.asarray(np.tril(np.ones((q, q), np.float32)), dtype=BF16)
    row = lambda b, c: b * nc + c
    const = lambda b, c: (0, 0)
    return pl.pallas_call(
        _ssd_kernel,
        grid=(bsz, nc),
        in_specs=[
            pl.BlockSpec((q, SSD_D_INNER), lambda b, c: (row(b, c), 0)),
            pl.BlockSpec((q, SSD_D_INNER), lambda b, c: (row(b, c), 1)),
            pl.BlockSpec((q, SSD_BC_DIM), lambda b, c: (row(b, c), 8)),
            pl.BlockSpec((q, SSD_BC_DIM), lambda b, c: (row(b, c), 9)),
            pl.BlockSpec((q, LANE), lambda b, c: (row(b, c), 80)),
            pl.BlockSpec((SSD_CONV, conv_dim), const),
            pl.BlockSpec((1, conv_dim), const),
            pl.BlockSpec((1, LANE), const),
            pl.BlockSpec((1, LANE), const),
            pl.BlockSpec((1, SSD_D_INNER), const),
            pl.BlockSpec((1, SSD_D_INNER), const),
            pl.BlockSpec((LANE, SSD_D_INNER), const),
            pl.BlockSpec((q, q), const),
        ],
        out_specs=pl.BlockSpec((q, SSD_D_INNER), lambda b, c: (row(b, c), 0)),
        out_shape=jax.ShapeDtypeStruct((m, SSD_D_INNER), BF16),
        scratch_shapes=[
            pltpu.VMEM((q + 8, conv_dim), F32),
            pltpu.VMEM((q, SSD_D_INNER), F32),
            pltpu.VMEM((q, SSD_BC_DIM), F32),
            pltpu.VMEM((q, SSD_BC_DIM), F32),
            pltpu.VMEM((q, SSD_D_INNER), F32),
            pltpu.VMEM((q, SSD_D_INNER), F32),
            pltpu.VMEM((q, SSD_D_INNER), F32),
            pltpu.VMEM((8, SSD_D_INNER), F32),
            pltpu.VMEM((SSD_N_GROUPS, q, LANE), F32),
            pltpu.VMEM((LANE // SSD_HPG, SSD_HPG, q), F32),
            pltpu.VMEM((SSD_N_GROUPS, SSD_D_STATE, SSD_GROUP_W), F32),
            pltpu.VMEM((q, SSD_D_INNER), F32),
        ],
        compiler_params=_cparams("parallel", "arbitrary"),
        name="ssd_core",
    )(proj, proj, proj, proj, proj, conv_w, conv_b.reshape(1, conv_dim), dtb, alog, dskip,
      norm_g.reshape(1, SSD_D_INNER), expand, tril)


def _nsa_compress_kernel(t_ref, pos_ref, w1_ref, w2_ref, o_ref, *, nchunk):
    half = NSA_CMP_STRIDE
    dh = NSA_HEAD_DIM
    p_lo = jnp.zeros((nchunk, NSA_CMP_HIDDEN), F32)
    p_hi = jnp.zeros((nchunk, NSA_CMP_HIDDEN), F32)
    for l in range(half):
        rows = t_ref[pl.ds(l, nchunk, stride=half), :]
        a_lo = (rows + pos_ref[l:l + 1, :]).astype(BF16)
        a_hi = (rows + pos_ref[half + l:half + l + 1, :]).astype(BF16)
        p_lo = p_lo + _dot(a_lo, w1_ref[l * dh:(l + 1) * dh, :].astype(BF16))
        p_hi = p_hi + _dot(a_hi, w1_ref[(half + l) * dh:(half + l + 1) * dh, :].astype(BF16))
    pre = p_lo + pltpu.roll(p_hi, nchunk - 1, axis=0)
    o_ref[...] = _dot(_gelu(pre).astype(BF16), w2_ref[...].astype(BF16))


def _nsa_compress(proj, cmp_pos, cmp_w1, cmp_w2, bsz, seq_len):
    nchunk = seq_len // NSA_CMP_STRIDE
    g, dh = NSA_KV_GROUPS, NSA_HEAD_DIM
    col0 = NSA_N_HEADS * dh // dh
    return pl.pallas_call(
        functools.partial(_nsa_compress_kernel, nchunk=nchunk),
        grid=(bsz, 2, g),
        in_specs=[
            pl.BlockSpec((seq_len, dh), lambda b, s, gi: (b, col0 + s * g + gi)),
            pl.BlockSpec((None, NSA_CMP_BLOCK, dh), lambda b, s, gi: (s, 0, 0)),
            pl.BlockSpec((None, NSA_CMP_BLOCK * dh, NSA_CMP_HIDDEN), lambda b, s, gi: (s, 0, 0)),
            pl.BlockSpec((None, NSA_CMP_HIDDEN, dh), lambda b, s, gi: (s, 0, 0)),
        ],
        out_specs=pl.BlockSpec((None, None, None, nchunk, dh), lambda b, s, gi: (b, s, gi, 0, 0)),
        out_shape=jax.ShapeDtypeStruct((bsz, 2, g, nchunk, dh), F32),
        compiler_params=_cparams("parallel", "parallel", "parallel"),
        name="nsa_compress",
    )(proj, cmp_pos, cmp_w1, cmp_w2)


def _nsa_select_kernel(q_ref, kc_ref, vc_ref, cover_ref, ocmp_ref, sel_ref, *, ncmp, nslc):
    tq = NSA_TQ
    dh = NSA_HEAD_DIM
    scale = dh ** -0.5
    nck = kc_ref.shape[0]
    t = pl.program_id(2) * tq + lax.broadcasted_iota(jnp.int32, (tq, 1), 0)
    ci = lax.broadcasted_iota(jnp.int32, (1, nck), 1)
    valid = (ci * NSA_CMP_STRIDE + (NSA_CMP_BLOCK - 1) <= t) & (ci < ncmp)
    kcb = kc_ref[...].astype(BF16)
    vcb = vc_ref[...].astype(BF16)
    psum = jnp.zeros((tq, nck), F32)
    for j in range(NSA_HPG):
        qj = q_ref[:, j * dh:(j + 1) * dh].astype(BF16)
        s = jnp.where(valid, _dot_nt(qj, kcb) * scale, NEG)
        e = jnp.exp(s - jnp.max(s, axis=-1, keepdims=True))
        p = jnp.where(valid, e / jnp.sum(e, axis=-1, keepdims=True), 0.0)
        psum = psum + p
        ocmp_ref[:, j * dh:(j + 1) * dh] = _dot(p.astype(BF16), vcb)
    imp = _dot_f32_lhs(psum, cover_ref[...])
    blk = lax.broadcasted_iota(jnp.int32, (1, nslc), 1)
    cur = t // NSA_SLC_BLOCK
    forced = (blk == 0) | ((blk <= cur) & (blk > cur - 1 - NSA_N_LOCAL))
    future = blk * NSA_SLC_BLOCK > t
    score = jnp.where(forced, BIG, jnp.where(future, NEG, imp))
    rank = jnp.zeros((tq, nslc), F32)
    for j in range(nslc):
        cj = score[:, j:j + 1]
        before = (cj > score) | ((cj == score) & (blk > j))
        rank = rank + jnp.where(before, 1.0, 0.0)
    sel_ref[...] = jnp.where(rank < float(min(NSA_TOPK, nslc)), 1.0, 0.0)


def _nsa_select(proj, kvc, bsz, seq_len):
    tq, dh, g = NSA_TQ, NSA_HEAD_DIM, NSA_KV_GROUPS
    nq = seq_len // tq
    ncmp = (seq_len - NSA_CMP_BLOCK) // NSA_CMP_STRIDE + 1
    nck = seq_len // NSA_CMP_STRIDE
    nslc = seq_len // NSA_SLC_BLOCK
    ci = np.arange(nck)[:, None] * NSA_CMP_STRIDE
    sj = np.arange(nslc)[None, :] * NSA_SLC_BLOCK
    cover = ((ci <= sj + NSA_SLC_BLOCK - 1) & (ci + NSA_CMP_BLOCK - 1 >= sj)).astype(np.float32)
    cover = jnp.asarray(cover, dtype=BF16)
    return pl.pallas_call(
        functools.partial(_nsa_select_kernel, ncmp=ncmp, nslc=nslc),
        grid=(bsz, g, nq),
        in_specs=[
            pl.BlockSpec((tq, NSA_HPG * dh), lambda b, gi, qi: (b * nq + qi, gi)),
            pl.BlockSpec((None, None, None, nck, dh), lambda b, gi, qi: (b, 0, gi, 0, 0)),
            pl.BlockSpec((None, None, None, nck, dh), lambda b, gi, qi: (b, 1, gi, 0, 0)),
            pl.BlockSpec((nck, nslc), lambda b, gi, qi: (0, 0)),
        ],
        out_specs=[
            pl.BlockSpec((tq, NSA_HPG * dh), lambda b, gi, qi: (b * nq + qi, gi)),
            pl.BlockSpec((None, None, tq, nslc), lambda b, gi, qi: (b, gi, qi, 0)),
        ],
        out_shape=[
            jax.ShapeDtypeStruct((bsz * seq_len, NSA_N_HEADS * dh), F32),
            jax.ShapeDtypeStruct((bsz, g, seq_len, nslc), F32),
        ],
        compiler_params=_cparams("parallel", "parallel", "parallel"),
        name="nsa_cmp_select",
    )(proj, kvc, kvc, cover)


def _nsa_attn_kernel(q_ref, ks_ref, vs_ref, kw_ref, vw_ref, sel_ref, expand_ref, ocmp_ref, gl_ref,
                     o_ref, selx_ref, q4_ref):
    tq = NSA_TQ
    dh = NSA_HEAD_DIM
    kt_w = LANE
    scale = dh ** -0.5
    qi = pl.program_id(2)
    selx_ref[...] = _dot(sel_ref[...].astype(BF16), expand_ref[...])
    for j in range(NSA_HPG):
        q4_ref[j * tq:(j + 1) * tq, :] = q_ref[:, j * dh:(j + 1) * dh].astype(BF16)
    q4 = q4_ref[...]
    t1 = qi * tq + lax.broadcasted_iota(jnp.int32, (tq, 1), 0)
    t4 = jnp.concatenate([t1] * NSA_HPG, axis=0)
    key_iota = lax.broadcasted_iota(jnp.int32, (1, kt_w), 1)

    def flash(k_ref, v_ref, lo, hi, mask_fn):
        def body(kt, carry):
            m, l, acc = carry
            off = pl.multiple_of(kt * kt_w, kt_w)
            k = k_ref[pl.ds(off, kt_w), :].astype(BF16)
            v = v_ref[pl.ds(off, kt_w), :].astype(BF16)
            pos = kt * kt_w + key_iota
            s = jnp.where(mask_fn(off, pos), _dot_nt(q4, k) * scale, NEG)
            m_new = jnp.maximum(m, jnp.max(s, axis=-1, keepdims=True))
            alpha = jnp.exp(m - m_new)
            p = jnp.exp(s - m_new)
            l = alpha * l + jnp.sum(p, axis=-1, keepdims=True)
            acc = alpha * acc + _dot(p.astype(BF16), v)
            return m_new, l, acc

        init = (jnp.full((NSA_HPG * tq, 1), NEG, F32), jnp.zeros((NSA_HPG * tq, 1), F32),
                jnp.zeros((NSA_HPG * tq, dh), F32))
        _, l, acc = lax.fori_loop(lo, hi, body, init)
        return acc / l

    def slc_mask(off, pos):
        chosen = jnp.concatenate([selx_ref[:, pl.ds(off, kt_w)]] * NSA_HPG, axis=0)
        return (chosen > 0.5) & (pos <= t4)

    def win_mask(off, pos):
        diff = t4 - pos
        return (diff >= 0) & (diff < NSA_WINDOW)

    o_slc = flash(ks_ref, vs_ref, 0, qi + 1, slc_mask)
    o_win = flash(kw_ref, vw_ref, jnp.maximum(qi - NSA_WINDOW // kt_w, 0), qi + 1, win_mask)
    gates = _sigmoid(gl_ref[...])
    for j in range(NSA_HPG):
        rows = slice(j * tq, (j + 1) * tq)
        o = (gates[:, 3 * j:3 * j + 1] * ocmp_ref[:, j * dh:(j + 1) * dh]
             + gates[:, 3 * j + 1:3 * j + 2] * o_slc[rows]
             + gates[:, 3 * j + 2:3 * j + 3] * o_win[rows])
        o_ref[:, j * dh:(j + 1) * dh] = o.astype(o_ref.dtype)


def _nsa_attn(proj, sel, ocmp, gate_logits, bsz, seq_len):
    tq, dh, g = NSA_TQ, NSA_HEAD_DIM, NSA_KV_GROUPS
    nq = seq_len // tq
    nslc = seq_len // NSA_SLC_BLOCK
    expand = (np.arange(nslc)[:, None] == np.arange(seq_len)[None, :] // NSA_SLC_BLOCK)
    expand = jnp.asarray(expand.astype(np.float32), dtype=BF16)
    kv0 = (NSA_N_HEADS * dh + 2 * NSA_KV_DIM) // dh

    def kv_spec(idx):
        return pl.BlockSpec((seq_len, dh), lambda b, gi, qi: (b, kv0 + idx * g + gi))

    qo_spec = pl.BlockSpec((tq, NSA_HPG * dh), lambda b, gi, qi: (b * nq + qi, gi))
    return pl.pallas_call(
        _nsa_attn_kernel,
        grid=(bsz, g, nq),
        in_specs=[
            qo_spec,
            kv_spec(0), kv_spec(1), kv_spec(2), kv_spec(3),
            pl.BlockSpec((None, None, tq, nslc), lambda b, gi, qi: (b, gi, qi, 0)),
            pl.BlockSpec((nslc, seq_len), lambda b, gi, qi: (0, 0)),
            qo_spec,
            pl.BlockSpec((None, tq, 3 * NSA_HPG), lambda b, gi, qi: (gi, b * nq + qi, 0)),
        ],
        out_specs=qo_spec,
        out_shape=jax.ShapeDtypeStruct((bsz * seq_len, NSA_N_HEADS * dh), BF16),
        scratch_shapes=[
            pltpu.VMEM((tq, seq_len), F32),
            pltpu.VMEM((NSA_HPG * tq, dh), BF16),
        ],
        compiler_params=_cparams("parallel", "parallel", "arbitrary"),
        name="nsa_slc_win_attention",
    )(proj, proj, proj, proj, proj, sel, expand, ocmp, gate_logits)


def _sgu_kernel(u_ref, v_ref, g_ref, ws_ref, bst_ref, o_ref):
    v = v_ref[...]
    mu = jnp.mean(v, axis=-1, keepdims=True)
    vc = v - mu
    vn = (vc * lax.rsqrt(jnp.mean(vc * vc, axis=-1, keepdims=True) + RMS_EPS) * g_ref[...]).astype(BF16)
    q = SGU_CHUNK
    tri = lax.broadcasted_iota(jnp.int32, (q, q), 0) >= lax.broadcasted_iota(jnp.int32, (q, q), 1)
    for g in range(SGU_GROUPS):
        cols = slice(g * SGU_GROUP_DIM, (g + 1) * SGU_GROUP_DIM)
        w_m = jnp.where(tri, ws_ref[g], 0.0).astype(BF16)
        sv = _dot(w_m, vn[:, cols]) + bst_ref[:, g:g + 1]
        o_ref[:, cols] = (u_ref[:, cols] * sv).astype(o_ref.dtype)


def _sgu_core(proj, ln_g, w_spatial, b_spatial):
    m = proj.shape[0]
    q = SGU_CHUNK
    return pl.pallas_call(
        _sgu_kernel,
        grid=(m // q,),
        in_specs=[
            pl.BlockSpec((q, SGU_WIDTH), lambda i: (i, 0)),
            pl.BlockSpec((q, SGU_WIDTH), lambda i: (i, 1)),
            pl.BlockSpec((1, SGU_WIDTH), lambda i: (0, 0)),
            pl.BlockSpec((SGU_GROUPS, q, q), lambda i: (0, 0, 0)),
            pl.BlockSpec((q, SGU_GROUPS), lambda i: (0, 0)),
        ],
        out_specs=pl.BlockSpec((q, SGU_WIDTH), lambda i: (i, 0)),
        out_shape=jax.ShapeDtypeStruct((m, SGU_WIDTH), BF16),
        compiler_params=_cparams("parallel"),
        name="sgu_core",
    )(proj, proj, ln_g.reshape(1, SGU_WIDTH), w_spatial, b_spatial.T)


def _pool_kernel(z_ref, zh_ref, wg_ref, sc_ref, o_ref, zz_ref, *, tm, blocks_per_seq):
    i = pl.program_id(0)
    keep = (i % blocks_per_seq != 0).astype(F32)
    zz_ref[0:POOL_HALO, :] = zh_ref[...] * keep
    zz_ref[POOL_HALO:, :] = z_ref[...]
    t = (i % blocks_per_seq) * tm + lax.broadcasted_iota(jnp.int32, (tm, 1), 0)
    for gi, win in enumerate(POOL_WINDOWS):
        cols = slice(gi * POOL_GROUP_DIM, (gi + 1) * POOL_GROUP_DIM)
        s = zz_ref[POOL_HALO:POOL_HALO + tm, cols]
        for k in range(1, win):
            s = s + zz_ref[POOL_HALO - k:POOL_HALO - k + tm, cols]
        count = jnp.minimum(t + 1, win).astype(F32)
        pooled = s / count - z_ref[:, cols]
        y = _dot(pooled.astype(BF16), wg_ref[gi]) * sc_ref[:, cols]
        o_ref[:, cols] = y.astype(o_ref.dtype)


def _pool_core(z, w_group, scale, seq_len, *, tm=512):
    m, n = z.shape
    tm = min(tm, seq_len)
    halo_blocks = tm // POOL_HALO
    return pl.pallas_call(
        functools.partial(_pool_kernel, tm=tm, blocks_per_seq=seq_len // tm),
        grid=(m // tm,),
        in_specs=[
            pl.BlockSpec((tm, n), lambda i: (i, 0)),
            pl.BlockSpec((POOL_HALO, n), lambda i: (jnp.maximum(i * halo_blocks - 1, 0), 0)),
            pl.BlockSpec(w_group.shape, lambda i: (0, 0, 0)),
            pl.BlockSpec((1, n), lambda i: (0, 0)),
        ],
        out_specs=pl.BlockSpec((tm, n), lambda i: (i, 0)),
        out_shape=jax.ShapeDtypeStruct((m, n), BF16),
        scratch_shapes=[pltpu.VMEM((tm + POOL_HALO, n), F32)],
        compiler_params=_cparams("parallel"),
        name="pool_core",
    )(z, z, w_group, scale.reshape(1, n))


def _pad_cols(w, n):
    return jnp.pad(w, ((0, 0), (0, n - w.shape[1])))


def _ssd_layer(x, g_pre, g_post, w_in, conv_w, conv_b, dt_bias, a_log, d_skip, norm_g, w_out, bsz, seq_len):
    w = _pad_cols(w_in, SSD_IN_PAD).astype(BF16)
    proj = _norm_matmul(x, g_pre, w, jnp.zeros((SSD_IN_PAD,), F32))
    y = _ssd_core(proj, conv_w, conv_b, dt_bias, a_log, d_skip, norm_g, bsz, seq_len)
    return _matmul_post(y, w_out.astype(BF16), g_post, x)


def _nsa_layer(x, g_pre, g_post, w_in, cmp_pos, cmp_w1, cmp_w2, w_out, bsz, seq_len):
    w = _pad_cols(w_in, NSA_IN_PAD).astype(BF16)
    proj = _norm_matmul(x, g_pre, w, jnp.zeros((NSA_IN_PAD,), F32))
    kvc = _nsa_compress(proj, cmp_pos, cmp_w1, cmp_w2, bsz, seq_len)
    ocmp, sel = _nsa_select(proj, kvc, bsz, seq_len)
    ngate = 3 * NSA_N_HEADS
    gl = proj[:, NSA_GATE_COL:NSA_GATE_COL + ngate].reshape(-1, NSA_KV_GROUPS, 3 * NSA_HPG).transpose(1, 0, 2)
    o = _nsa_attn(proj, sel, ocmp, gl, bsz, seq_len)
    return _matmul_post(o, w_out.astype(BF16), g_post, x)


def _sgu_layer(x, g_pre, g_post, w_in, b_in, ln_g, w_spatial, b_spatial, w_out):
    proj = _norm_matmul(x, g_pre, w_in.astype(BF16), b_in, act="gelu")
    y = _sgu_core(proj, ln_g, w_spatial, b_spatial)
    return _matmul_post(y, w_out.astype(BF16), g_post, x)


def _pool_layer(x, g_pre, g_post, w_in, w_group, scale, w_out, seq_len):
    z = _norm_matmul(x, g_pre, w_in.astype(BF16), jnp.zeros((D_MODEL,), F32))
    y = _pool_core(z, w_group.astype(BF16), scale, seq_len)
    return _matmul_post(y, w_out.astype(BF16), g_post, x)


def kernel(x, mem, norm_pre, norm_post, norm_mem, ssd_w_in, ssd_conv_w, ssd_conv_b, ssd_dt_bias, ssd_a_log, ssd_d, ssd_norm_g, ssd_w_out, nsa_w_in, nsa_cmp_pos, nsa_cmp_w1, nsa_cmp_w2, nsa_w_out, sgu_w_in, sgu_b_in, sgu_ln_g, sgu_w_spatial, sgu_b_spatial, sgu_w_out, pool_w_in, pool_w_group, pool_scale, pool_w_out, xa_w_q, xa_w_kv, xa_w_o, ffn_w_up, ffn_conv_w, ffn_conv_b, ffn_w_down):
    bsz, seq_len, d = x.shape
    depth = norm_pre.shape[0]
    xf = x.reshape(bsz * seq_len, d)
    memf = mem.reshape(bsz * mem.shape[1], d)
    for i in range(depth):
        kind, j = i % 4, i // 4
        if kind == 0:
            xf = _ssd_layer(xf, norm_pre[i, 0], norm_post[i, 0], ssd_w_in[j], ssd_conv_w[j], ssd_conv_b[j],
                            ssd_dt_bias[j], ssd_a_log[j], ssd_d[j], ssd_norm_g[j], ssd_w_out[j], bsz, seq_len)
        elif kind == 1:
            xf = _nsa_layer(xf, norm_pre[i, 0], norm_post[i, 0], nsa_w_in[j], nsa_cmp_pos[j], nsa_cmp_w1[j],
                            nsa_cmp_w2[j], nsa_w_out[j], bsz, seq_len)
        elif kind == 2:
            xf = _sgu_layer(xf, norm_pre[i, 0], norm_post[i, 0], sgu_w_in[j], sgu_b_in[j], sgu_ln_g[j],
                            sgu_w_spatial[j], sgu_b_spatial[j], sgu_w_out[j])
        else:
            xf = _pool_layer(xf, norm_pre[i, 0], norm_post[i, 0], pool_w_in[j], pool_w_group[j],
                             pool_scale[j], pool_w_out[j], seq_len)
        kv = _norm_matmul(memf, norm_mem[i], xa_w_kv[i].astype(BF16), jnp.zeros((2 * XA_DIM,), F32),
                          out_dtype=BF16, tm=MEM_LEN)
        xf = _xattn(xf, kv, norm_pre[i, 1], norm_post[i, 1], xa_w_q[i].astype(BF16),
                    xa_w_o[i].astype(BF16), seq_len)
        act = _ffn_up(xf, norm_pre[i, 2], ffn_w_up[i].astype(BF16), ffn_conv_w[i], ffn_conv_b[i], seq_len)
        xf = _matmul_post(act, ffn_w_down[i].astype(BF16), norm_post[i, 2], xf)
    return xf.reshape(bsz, seq_len, d)
```

```python
import functools
import math

import jax
import jax.numpy as jnp
import numpy as np
from jax import lax
from jax.experimental import pallas as pl
from jax.experimental.pallas import tpu as pltpu

F32 = jnp.float32
BF16 = jnp.bfloat16

D_MODEL = 2048
RMS_EPS = 1e-6
NEG = -1e30
BIG = 1e30
MEM_LEN = 256

SSD_D_INNER = 4096
SSD_HEAD_DIM = 64
SSD_N_HEADS = 64
SSD_N_GROUPS = 8
SSD_HPG = 8
SSD_D_STATE = 128
SSD_CONV = 4
SSD_CHUNK = 128
SSD_GROUP_W = SSD_HPG * SSD_HEAD_DIM
SSD_BC_DIM = SSD_N_GROUPS * SSD_D_STATE
SSD_IN_PAD = 10752

NSA_HEAD_DIM = 128
NSA_N_HEADS = 16
NSA_KV_GROUPS = 4
NSA_HPG = 4
NSA_KV_DIM = 512
NSA_CMP_BLOCK = 32
NSA_CMP_STRIDE = 16
NSA_CMP_HIDDEN = 256
NSA_SLC_BLOCK = 64
NSA_TOPK = 16
NSA_N_LOCAL = 2
NSA_WINDOW = 512
NSA_TQ = 128
NSA_KEY_TILE = 512
NSA_IN_PAD = 5632
NSA_GATE_COL = 5120

SGU_CHUNK = 128
SGU_WIDTH = 4096
SGU_GROUPS = 8
SGU_GROUP_DIM = 512

POOL_WINDOWS = (2, 4, 8, 16)
POOL_GROUP_DIM = 512
POOL_HALO = 16

XA_HEADS = 4
XA_HEAD_DIM = 128
XA_DIM = 512

FFN_HIDDEN = 5632
FFN_CONV = 3
FFN_HALO = 16

LANE = 128
VMEM_LIMIT = 56 * 1024 * 1024


def _cparams(*sem):
    return pltpu.CompilerParams(dimension_semantics=sem, vmem_limit_bytes=VMEM_LIMIT)


def _dot(a, b):
    return jnp.dot(a, b, preferred_element_type=F32)


def _dot_nt(a, b):
    return lax.dot_general(a, b, (((1,), (1,)), ((), ())), preferred_element_type=F32)


def _split3(v):
    hi = v.astype(BF16)
    r1 = v - hi.astype(F32)
    mid = r1.astype(BF16)
    lo = (r1 - mid.astype(F32)).astype(BF16)
    return hi, mid, lo


def _dot_f32_lhs(v, e):
    hi, mid, lo = _split3(v)
    return _dot(hi, e) + _dot(mid, e) + _dot(lo, e)


def _dot_f32_rhs(e, v):
    hi, mid, lo = _split3(v)
    return _dot(e, hi) + _dot(e, mid) + _dot(e, lo)


def _rms(x, g):
    ms = jnp.mean(x * x, axis=-1, keepdims=True)
    return x * lax.rsqrt(ms + RMS_EPS) * g


def _sigmoid(x):
    return 1.0 / (1.0 + jnp.exp(-x))


def _silu(x):
    return x * _sigmoid(x)


def _gelu(x):
    c = math.sqrt(2.0 / math.pi)
    return x * (0.5 * (1.0 + jnp.tanh(c * (x + 0.044715 * (x * x * x)))))


def _softplus(x):
    return jnp.maximum(x, 0.0) + jnp.log1p(jnp.exp(-jnp.abs(x)))


def _norm_mm_kernel(x_ref, g_ref, w_ref, b_ref, o_ref, h_ref, *, act):
    @pl.when(pl.program_id(1) == 0)
    def _():
        h_ref[...] = _rms(x_ref[...], g_ref[...]).astype(BF16)

    y = _dot(h_ref[...], w_ref[...]) + b_ref[...]
    if act == "gelu":
        y = _gelu(y)
    o_ref[...] = y.astype(o_ref.dtype)


def _norm_matmul(x, g, w, b, *, act=None, out_dtype=F32, tm=1024, tn=512):
    m, k = x.shape
    n = w.shape[1]
    tm = min(tm, m)
    assert m % tm == 0 and n % tn == 0, (m, n, tm, tn)
    return pl.pallas_call(
        functools.partial(_norm_mm_kernel, act=act),
        grid=(m // tm, n // tn),
        in_specs=[
            pl.BlockSpec((tm, k), lambda i, j: (i, 0)),
            pl.BlockSpec((1, k), lambda i, j: (0, 0)),
            pl.BlockSpec((k, tn), lambda i, j: (0, j)),
            pl.BlockSpec((1, tn), lambda i, j: (0, j)),
        ],
        out_specs=pl.BlockSpec((tm, tn), lambda i, j: (i, j)),
        out_shape=jax.ShapeDtypeStruct((m, n), out_dtype),
        scratch_shapes=[pltpu.VMEM((tm, k), BF16)],
        compiler_params=_cparams("parallel", "arbitrary"),
        name="norm_matmul",
    )(x, g.reshape(1, k), w, b.reshape(1, n))


def _mm_post_kernel(a_ref, w_ref, g_ref, r_ref, o_ref, *, nj, tn):
    j = pl.program_id(1)
    o_ref[:, pl.ds(pl.multiple_of(j * tn, tn), tn)] = _dot(a_ref[...], w_ref[...])

    @pl.when(j == nj - 1)
    def _():
        o_ref[...] = r_ref[...] + _rms(o_ref[...], g_ref[...])


def _matmul_post(a, w, g, res, *, tm=512, tn=512):
    m, kdim = a.shape
    n = w.shape[1]
    tm = min(tm, m)
    assert m % tm == 0 and n % tn == 0
    nj = n // tn
    return pl.pallas_call(
        functools.partial(_mm_post_kernel, nj=nj, tn=tn),
        grid=(m // tm, nj),
        in_specs=[
            pl.BlockSpec((tm, kdim), lambda i, j: (i, 0)),
            pl.BlockSpec((kdim, tn), lambda i, j: (0, j)),
            pl.BlockSpec((1, n), lambda i, j: (0, 0)),
            pl.BlockSpec((tm, n), lambda i, j: (i, 0)),
        ],
        out_specs=pl.BlockSpec((tm, n), lambda i, j: (i, 0)),
        out_shape=jax.ShapeDtypeStruct((m, n), F32),
        compiler_params=_cparams("parallel", "arbitrary"),
        name="matmul_postnorm_residual",
    )(a, w, g.reshape(1, n), res)


def _ffn_up_kernel(x_ref, xh_ref, g_ref, wg_ref, wv_ref, cwg_ref, cwv_ref, cbg_ref, cbv_ref,
                   o_ref, h_ref, ug_ref, uv_ref, *, tm, blocks_per_seq):
    i = pl.program_id(0)

    @pl.when(pl.program_id(1) == 0)
    def _():
        keep = (i % blocks_per_seq != 0).astype(F32)
        h_ref[0:FFN_HALO, :] = (_rms(xh_ref[...], g_ref[...]) * keep).astype(BF16)
        h_ref[FFN_HALO:, :] = _rms(x_ref[...], g_ref[...]).astype(BF16)

    h = h_ref[...]
    ug_ref[...] = _dot(h, wg_ref[...])
    uv_ref[...] = _dot(h, wv_ref[...])

    def conv(u_ref, cw_ref, cb_ref):
        y = cb_ref[...]
        for tap in range(FFN_CONV):
            lo = FFN_HALO - (FFN_CONV - 1) + tap
            y = y + u_ref[lo:lo + tm, :] * cw_ref[tap:tap + 1, :]
        return y

    gate = conv(ug_ref, cwg_ref, cbg_ref)
    val = conv(uv_ref, cwv_ref, cbv_ref)
    o_ref[...] = (_silu(gate) * val).astype(o_ref.dtype)


def _ffn_up(x, g, w_up, conv_w, conv_b, seq_len, *, tm=1024, tn=512):
    m, k = x.shape
    tm = min(tm, seq_len)
    assert m % tm == 0 and seq_len % tm == 0 and FFN_HIDDEN % tn == 0
    nj = FFN_HIDDEN // tn
    halo_blocks = tm // FFN_HALO
    return pl.pallas_call(
        functools.partial(_ffn_up_kernel, tm=tm, blocks_per_seq=seq_len // tm),
        grid=(m // tm, nj),
        in_specs=[
            pl.BlockSpec((tm, k), lambda i, j: (i, 0)),
            pl.BlockSpec((FFN_HALO, k), lambda i, j: (jnp.maximum(i * halo_blocks - 1, 0), 0)),
            pl.BlockSpec((1, k), lambda i, j: (0, 0)),
            pl.BlockSpec((k, tn), lambda i, j: (0, j)),
            pl.BlockSpec((k, tn), lambda i, j: (0, j + nj)),
            pl.BlockSpec((FFN_CONV, tn), lambda i, j: (0, j)),
            pl.BlockSpec((FFN_CONV, tn), lambda i, j: (0, j + nj)),
            pl.BlockSpec((1, tn), lambda i, j: (0, j)),
            pl.BlockSpec((1, tn), lambda i, j: (0, j + nj)),
        ],
        out_specs=pl.BlockSpec((tm, tn), lambda i, j: (i, j)),
        out_shape=jax.ShapeDtypeStruct((m, FFN_HIDDEN), BF16),
        scratch_shapes=[
            pltpu.VMEM((tm + FFN_HALO, k), BF16),
            pltpu.VMEM((tm + FFN_HALO, tn), F32),
            pltpu.VMEM((tm + FFN_HALO, tn), F32),
        ],
        compiler_params=_cparams("parallel", "arbitrary"),
        name="ffn_up_conv_gate",
    )(x, x, g.reshape(1, k), w_up, w_up, conv_w, conv_w,
      conv_b.reshape(1, -1), conv_b.reshape(1, -1))


def _xattn_kernel(x_ref, kv_ref, gpre_ref, gpost_ref, wq_ref, wo_ref, o_ref, att_ref):
    x = x_ref[...]
    h = _rms(x, gpre_ref[...]).astype(BF16)
    q = _dot(h, wq_ref[...]).astype(BF16)
    scale = XA_HEAD_DIM ** -0.5
    for hd in range(XA_HEADS):
        lo = hd * XA_HEAD_DIM
        s = _dot_nt(q[:, lo:lo + XA_HEAD_DIM], kv_ref[:, lo:lo + XA_HEAD_DIM]) * scale
        e = jnp.exp(s - jnp.max(s, axis=-1, keepdims=True))
        p = e / jnp.sum(e, axis=-1, keepdims=True)
        att_ref[:, lo:lo + XA_HEAD_DIM] = _dot(
            p.astype(BF16), kv_ref[:, XA_DIM + lo:XA_DIM + lo + XA_HEAD_DIM]).astype(BF16)
    a = _dot(att_ref[...], wo_ref[...])
    o_ref[...] = x + _rms(a, gpost_ref[...])


def _xattn(x, kv, g_pre, g_post, w_q, w_o, seq_len, *, tm=512):
    m, k = x.shape
    tm = min(tm, seq_len)
    bps = seq_len // tm
    return pl.pallas_call(
        _xattn_kernel,
        grid=(m // tm,),
        in_specs=[
            pl.BlockSpec((tm, k), lambda i: (i, 0)),
            pl.BlockSpec((MEM_LEN, 2 * XA_DIM), lambda i: (i // bps, 0)),
            pl.BlockSpec((1, k), lambda i: (0, 0)),
            pl.BlockSpec((1, k), lambda i: (0, 0)),
            pl.BlockSpec((k, XA_DIM), lambda i: (0, 0)),
            pl.BlockSpec((XA_DIM, k), lambda i: (0, 0)),
        ],
        out_specs=pl.BlockSpec((tm, k), lambda i: (i, 0)),
        out_shape=jax.ShapeDtypeStruct((m, k), F32),
        scratch_shapes=[pltpu.VMEM((tm, XA_DIM), BF16)],
        compiler_params=_cparams("parallel"),
        name="memory_cross_attention",
    )(x, kv, g_pre.reshape(1, k), g_post.reshape(1, k), w_q, w_o)


def _ssd_kernel(z_ref, xr_ref, br_ref, cr_ref, dtr_ref, cw_ref, cbias_ref, dtb_ref, alog_ref,
                dskip_ref, ng_ref, expand_ref, tril_ref, o_ref,
                xin_ref, xs_ref, bm_ref, cm_ref, dtx_ref, dtex_ref, expax_ref, cdx_ref,
                acsg_ref, acst_ref, state_ref, y_ref):
    q = SSD_CHUNK
    c = pl.program_id(1)
    halo = 8
    xw = SSD_D_INNER

    @pl.when(c == 0)
    def _():
        xin_ref[0:halo, :] = jnp.zeros((halo, xin_ref.shape[1]), F32)
        state_ref[...] = jnp.zeros(state_ref.shape, F32)

    @pl.when(c > 0)
    def _():
        xin_ref[0:halo, :] = xin_ref[q:q + halo, :]

    xin_ref[halo:, 0:xw] = xr_ref[...]
    xin_ref[halo:, xw:xw + SSD_BC_DIM] = br_ref[...]
    xin_ref[halo:, xw + SSD_BC_DIM:] = cr_ref[...]

    def conv_tile(lo, width):
        y = cbias_ref[:, pl.ds(lo, width)]
        for tap in range(SSD_CONV):
            r0 = halo - (SSD_CONV - 1) + tap
            y = y + xin_ref[r0:r0 + q, pl.ds(lo, width)] * cw_ref[tap:tap + 1, pl.ds(lo, width)]
        return _silu(y)

    def conv_x(t, carry):
        lo = pl.multiple_of(t * 512, 512)
        xs_ref[:, pl.ds(lo, 512)] = conv_tile(lo, 512)
        return carry

    lax.fori_loop(0, xw // 512, conv_x, 0)
    for t in range(SSD_BC_DIM // 512):
        bm_ref[:, t * 512:(t + 1) * 512] = conv_tile(xw + t * 512, 512)
        cm_ref[:, t * 512:(t + 1) * 512] = conv_tile(xw + SSD_BC_DIM + t * 512, 512)

    dt = _softplus(dtr_ref[...] + dtb_ref[...])
    a = dt * (-jnp.exp(alog_ref[...]))
    a_cs = _dot_f32_rhs(tril_ref[...], a)
    a_last = a_cs[q - 1:q, :]
    expand = expand_ref[...]
    dtx_ref[...] = _dot_f32_lhs(dt, expand)
    dtex_ref[...] = _dot_f32_lhs(jnp.exp(a_last - a_cs), expand)
    expax_ref[...] = _dot_f32_lhs(jnp.exp(a_cs), expand)
    cdx_ref[...] = _dot_f32_lhs(jnp.broadcast_to(jnp.exp(a_last), (8, LANE)), expand)
    acst_ref[...] = a_cs.T.reshape(LANE // SSD_HPG, SSD_HPG, q)
    for g in range(SSD_N_GROUPS):
        acsg_ref[g] = a_cs if g == 0 else pltpu.roll(a_cs, LANE - g * SSD_HPG, axis=1)

    row = lax.broadcasted_iota(jnp.int32, (q, q), 0)
    col = lax.broadcasted_iota(jnp.int32, (q, q), 1)
    causal = row >= col
    left_half = col < SSD_HEAD_DIM

    def group_body(g, carry):
        off = pl.multiple_of(g * SSD_GROUP_W, SSD_GROUP_W)
        offn = pl.multiple_of(g * SSD_D_STATE, SSD_D_STATE)
        cm = cm_ref[:, pl.ds(offn, SSD_D_STATE)]
        bm = bm_ref[:, pl.ds(offn, SSD_D_STATE)]
        cmb = cm.astype(BF16)
        cb = _dot_nt(cmb, bm.astype(BF16))
        xs = xs_ref[:, pl.ds(off, SSD_GROUP_W)]
        xdt = xs * dtx_ref[:, pl.ds(off, SSD_GROUP_W)]
        st = state_ref[g]
        y_off = _dot(cmb, st.astype(BF16)) * expax_ref[:, pl.ds(off, SSD_GROUP_W)]
        s_new = _dot(bm.T.astype(BF16), (xdt * dtex_ref[:, pl.ds(off, SSD_GROUP_W)]).astype(BF16))
        state_ref[g] = st * cdx_ref[0:1, pl.ds(off, SSD_GROUP_W)] + s_new
        acs_g = acsg_ref[g]
        acst_g = acst_ref[g]
        skip = xs * dskip_ref[:, pl.ds(off, SSD_GROUP_W)]
        for pr in range(SSD_HPG // 2):
            scs = []
            for e in range(2):
                j = 2 * pr + e
                diff = acs_g[:, j:j + 1] - acst_g[j:j + 1, :]
                dec = jnp.where(causal, jnp.exp(diff), 0.0)
                scs.append((cb * dec).astype(BF16))
            xp = xdt[:, pr * LANE:(pr + 1) * LANE]
            x0 = jnp.where(left_half, xp, 0.0).astype(BF16)
            x1 = jnp.where(left_half, 0.0, xp).astype(BF16)
            y_diag = _dot(jnp.concatenate(scs, axis=1), jnp.concatenate([x0, x1], axis=0))
            y = y_diag + y_off[:, pr * LANE:(pr + 1) * LANE] + skip[:, pr * LANE:(pr + 1) * LANE]
            y_ref[:, pl.ds(pl.multiple_of(off + pr * LANE, LANE), LANE)] = y
        return carry

    lax.fori_loop(0, SSD_N_GROUPS, group_body, 0)

    gated = y_ref[...] * _silu(z_ref[...])
    o_ref[...] = _rms(gated, ng_ref[...]).astype(o_ref.dtype)


def _ssd_core(proj, conv_w, conv_b, dt_bias, a_log, d_skip, norm_g, bsz, seq_len):
    q = SSD_CHUNK
    nc = seq_len // q
    m = bsz * seq_len
    conv_dim = SSD_D_INNER + 2 * SSD_BC_DIM
    pad = LANE - SSD_N_HEADS
    dtb = jnp.pad(dt_bias, (0, pad)).reshape(1, LANE)
    alog = jnp.pad(a_log, (0, pad)).reshape(1, LANE)
    dskip = jnp.repeat(d_skip, SSD_HEAD_DIM).reshape(1, SSD_D_INNER)
    heads = np.arange(LANE)[:, None]
    chans = np.arange(SSD_D_INNER)[None, :] // SSD_HEAD_DIM
    expand = jnp.asarray((heads == chans).astype(np.float32), dtype=BF16)
    tril = jnp.asarray(np.tril(np.ones((q, q), np.float32)), dtype=BF16)
    row = lambda b, c: b * nc + c
    const = lambda b, c: (0, 0)
    return pl.pallas_call(
        _ssd_kernel,
        grid=(bsz, nc),
        in_specs=[
            pl.BlockSpec((q, SSD_D_INNER), lambda b, c: (row(b, c), 0)),
            pl.BlockSpec((q, SSD_D_INNER), lambda b, c: (row(b, c), 1)),
            pl.BlockSpec((q, SSD_BC_DIM), lambda b, c: (row(b, c), 8)),
            pl.BlockSpec((q, SSD_BC_DIM), lambda b, c: (row(b, c), 9)),
            pl.BlockSpec((q, LANE), lambda b, c: (row(b, c), 80)),
            pl.BlockSpec((SSD_CONV, conv_dim), const),
            pl.BlockSpec((1, conv_dim), const),
            pl.BlockSpec((1, LANE), const),
            pl.BlockSpec((1, LANE), const),
            pl.BlockSpec((1, SSD_D_INNER), const),
            pl.BlockSpec((1, SSD_D_INNER), const),
            pl.BlockSpec((LANE, SSD_D_INNER), const),
            pl.BlockSpec((q, q), const),
        ],
        out_specs=pl.BlockSpec((q, SSD_D_INNER), lambda b, c: (row(b, c), 0)),
        out_shape=jax.ShapeDtypeStruct((m, SSD_D_INNER), BF16),
        scratch_shapes=[
            pltpu.VMEM((q + 8, conv_dim), F32),
            pltpu.VMEM((q, SSD_D_INNER), F32),
            pltpu.VMEM((q, SSD_BC_DIM), F32),
            pltpu.VMEM((q, SSD_BC_DIM), F32),
            pltpu.VMEM((q, SSD_D_INNER), F32),
            pltpu.VMEM((q, SSD_D_INNER), F32),
            pltpu.VMEM((q, SSD_D_INNER), F32),
            pltpu.VMEM((8, SSD_D_INNER), F32),
            pltpu.VMEM((SSD_N_GROUPS, q, LANE), F32),
            pltpu.VMEM((LANE // SSD_HPG, SSD_HPG, q), F32),
            pltpu.VMEM((SSD_N_GROUPS, SSD_D_STATE, SSD_GROUP_W), F32),
            pltpu.VMEM((q, SSD_D_INNER), F32),
        ],
        compiler_params=_cparams("parallel", "arbitrary"),
        name="ssd_core",
    )(proj, proj, proj, proj, proj, conv_w, conv_b.reshape(1, conv_dim), dtb, alog, dskip,
      norm_g.reshape(1, SSD_D_INNER), expand, tril)


def _nsa_compress_kernel(t_ref, pos_ref, w1_ref, w2_ref, o_ref, *, nchunk):
    half = NSA_CMP_STRIDE
    dh = NSA_HEAD_DIM
    p_lo = jnp.zeros((nchunk, NSA_CMP_HIDDEN), F32)
    p_hi = jnp.zeros((nchunk, NSA_CMP_HIDDEN), F32)
    for l in range(half):
        rows = t_ref[pl.ds(l, nchunk, stride=half), :]
        a_lo = (rows + pos_ref[l:l + 1, :]).astype(BF16)
        a_hi = (rows + pos_ref[half + l:half + l + 1, :]).astype(BF16)
        p_lo = p_lo + _dot(a_lo, w1_ref[l * dh:(l + 1) * dh, :].astype(BF16))
        p_hi = p_hi + _dot(a_hi, w1_ref[(half + l) * dh:(half + l + 1) * dh, :].astype(BF16))
    pre = p_lo + pltpu.roll(p_hi, nchunk - 1, axis=0)
    o_ref[...] = _dot(_gelu(pre).astype(BF16), w2_ref[...].astype(BF16))


def _nsa_compress(proj, cmp_pos, cmp_w1, cmp_w2, bsz, seq_len):
    nchunk = seq_len // NSA_CMP_STRIDE
    g, dh = NSA_KV_GROUPS, NSA_HEAD_DIM
    col0 = NSA_N_HEADS * dh // dh
    return pl.pallas_call(
        functools.partial(_nsa_compress_kernel, nchunk=nchunk),
        grid=(bsz, 2, g),
        in_specs=[
            pl.BlockSpec((seq_len, dh), lambda b, s, gi: (b, col0 + s * g + gi)),
            pl.BlockSpec((None, NSA_CMP_BLOCK, dh), lambda b, s, gi: (s, 0, 0)),
            pl.BlockSpec((None, NSA_CMP_BLOCK * dh, NSA_CMP_HIDDEN), lambda b, s, gi: (s, 0, 0)),
            pl.BlockSpec((None, NSA_CMP_HIDDEN, dh), lambda b, s, gi: (s, 0, 0)),
        ],
        out_specs=pl.BlockSpec((None, None, None, nchunk, dh), lambda b, s, gi: (b, s, gi, 0, 0)),
        out_shape=jax.ShapeDtypeStruct((bsz, 2, g, nchunk, dh), F32),
        compiler_params=_cparams("parallel", "parallel", "parallel"),
        name="nsa_compress",
    )(proj, cmp_pos, cmp_w1, cmp_w2)


def _nsa_select_kernel(q_ref, kc_ref, vc_ref, cover_ref, ocmp_ref, sel_ref, *, ncmp, nslc):
    tq = NSA_TQ
    dh = NSA_HEAD_DIM
    scale = dh ** -0.5
    nck = kc_ref.shape[0]
    t = pl.program_id(2) * tq + lax.broadcasted_iota(jnp.int32, (tq, 1), 0)
    ci = lax.broadcasted_iota(jnp.int32, (1, nck), 1)
    valid = (ci * NSA_CMP_STRIDE + (NSA_CMP_BLOCK - 1) <= t) & (ci < ncmp)
    kcb = kc_ref[...].astype(BF16)
    vcb = vc_ref[...].astype(BF16)
    psum = jnp.zeros((tq, nck), F32)
    for j in range(NSA_HPG):
        qj = q_ref[:, j * dh:(j + 1) * dh].astype(BF16)
        s = jnp.where(valid, _dot_nt(qj, kcb) * scale, NEG)
        e = jnp.exp(s - jnp.max(s, axis=-1, keepdims=True))
        p = jnp.where(valid, e / jnp.sum(e, axis=-1, keepdims=True), 0.0)
        psum = psum + p
        ocmp_ref[:, j * dh:(j + 1) * dh] = _dot(p.astype(BF16), vcb)
    imp = _dot_f32_lhs(psum, cover_ref[...])
    blk = lax.broadcasted_iota(jnp.int32, (1, nslc), 1)
    cur = t // NSA_SLC_BLOCK
    forced = (blk == 0) | ((blk <= cur) & (blk > cur - 1 - NSA_N_LOCAL))
    future = blk * NSA_SLC_BLOCK > t
    score = jnp.where(forced, BIG, jnp.where(future, NEG, imp))
    rank = jnp.zeros((tq, nslc), F32)
    for j in range(nslc):
        cj = score[:, j:j + 1]
        before = (cj > score) | ((cj == score) & (blk > j))
        rank = rank + jnp.where(before, 1.0, 0.0)
    sel_ref[...] = jnp.where(rank < float(min(NSA_TOPK, nslc)), 1.0, 0.0)


def _nsa_select(proj, kvc, bsz, seq_len):
    tq, dh, g = NSA_TQ, NSA_HEAD_DIM, NSA_KV_GROUPS
    nq = seq_len // tq
    ncmp = (seq_len - NSA_CMP_BLOCK) // NSA_CMP_STRIDE + 1
    nck = seq_len // NSA_CMP_STRIDE
    nslc = seq_len // NSA_SLC_BLOCK
    ci = np.arange(nck)[:, None] * NSA_CMP_STRIDE
    sj = np.arange(nslc)[None, :] * NSA_SLC_BLOCK
    cover = ((ci <= sj + NSA_SLC_BLOCK - 1) & (ci + NSA_CMP_BLOCK - 1 >= sj)).astype(np.float32)
    cover = jnp.asarray(cover, dtype=BF16)
    return pl.pallas_call(
        functools.partial(_nsa_select_kernel, ncmp=ncmp, nslc=nslc),
        grid=(bsz, g, nq),
        in_specs=[
            pl.BlockSpec((tq, NSA_HPG * dh), lambda b, gi, qi: (b * nq + qi, gi)),
            pl.BlockSpec((None, None, None, nck, dh), lambda b, gi, qi: (b, 0, gi, 0, 0)),
            pl.BlockSpec((None, None, None, nck, dh), lambda b, gi, qi: (b, 1, gi, 0, 0)),
            pl.BlockSpec((nck, nslc), lambda b, gi, qi: (0, 0)),
        ],
        out_specs=[
            pl.BlockSpec((tq, NSA_HPG * dh), lambda b, gi, qi: (b * nq + qi, gi)),
            pl.BlockSpec((None, None, tq, nslc), lambda b, gi, qi: (b, gi, qi, 0)),
        ],
        out_shape=[
            jax.ShapeDtypeStruct((bsz * seq_len, NSA_N_HEADS * dh), F32),
            jax.ShapeDtypeStruct((bsz, g, seq_len, nslc), F32),
        ],
        compiler_params=_cparams("parallel", "parallel", "parallel"),
        name="nsa_cmp_select",
    )(proj, kvc, kvc, cover)


def _nsa_attn_kernel(q_ref, ks_ref, vs_ref, kw_ref, vw_ref, sel_ref, expand_ref, ocmp_ref, gl_ref,
                     o_ref, ksb_ref, vsx_ref, kwb_ref, vwx_ref, bias_ref, q4_ref, p_ref, pw_ref,
                     m_ref, alpha_ref, acc_ref):
    tq = NSA_TQ
    dh = NSA_HEAD_DIM
    kt_w = NSA_KEY_TILE
    win_w = NSA_WINDOW + tq
    c = (dh ** -0.5) * math.log2(math.e)
    qi = pl.program_id(2)

    @pl.when(qi == 0)
    def _():
        ones = jnp.ones((ks_ref.shape[0], dh), BF16)
        ksb_ref[...] = ks_ref[...].astype(BF16)
        kwb_ref[...] = kw_ref[...].astype(BF16)
        vsx_ref[:, 0:dh] = vs_ref[...].astype(BF16)
        vsx_ref[:, dh:] = ones
        vwx_ref[:, 0:dh] = vw_ref[...].astype(BF16)
        vwx_ref[:, dh:] = ones

    bias_ref[...] = (_dot(sel_ref[...].astype(BF16), expand_ref[...]) - 1.0) * BIG
    for j in range(NSA_HPG):
        q4_ref[j * tq:(j + 1) * tq, :] = q_ref[:, j * dh:(j + 1) * dh].astype(BF16)
    q4 = q4_ref[...]
    t1 = qi * tq + lax.broadcasted_iota(jnp.int32, (tq, 1), 0)
    m_ref[...] = jnp.full(m_ref.shape, NEG, F32)
    acc_ref[...] = jnp.zeros(acc_ref.shape, F32)

    def slc_tile(kt, carry):
        off = pl.multiple_of(kt * kt_w, kt_w)
        s = _dot_nt(q4, ksb_ref[pl.ds(off, kt_w), :])
        pos = off + lax.broadcasted_iota(jnp.int32, (1, kt_w), 1)
        bias = jnp.where(pos <= t1, bias_ref[:, pl.ds(off, kt_w)], NEG)
        for j in range(NSA_HPG):
            rows = slice(j * tq, (j + 1) * tq)
            sj = s[rows] + bias
            m_prev = m_ref[rows]
            m_new = jnp.maximum(m_prev, jnp.max(sj, axis=-1, keepdims=True))
            alpha_ref[rows] = jnp.exp2((m_prev - m_new) * c)
            m_ref[rows] = m_new
            p_ref[rows] = jnp.exp2((sj - pltpu.repeat(m_new, kt_w // LANE, 1)) * c).astype(BF16)
        pv = _dot(p_ref[...], vsx_ref[pl.ds(off, kt_w), :])
        acc_ref[...] = pltpu.repeat(alpha_ref[...], 2, 1) * acc_ref[...] + pv
        return carry

    lax.fori_loop(0, ((qi + 1) * tq + kt_w - 1) // kt_w, slc_tile, 0)

    w0 = pl.multiple_of(jnp.maximum(qi * tq - NSA_WINDOW, 0), tq)
    sw = _dot_nt(q4, kwb_ref[pl.ds(w0, win_w), :])
    diff = t1 - (w0 + lax.broadcasted_iota(jnp.int32, (1, win_w), 1))
    bias_w = jnp.where((diff >= 0) & (diff < NSA_WINDOW), 0.0, NEG)
    for j in range(NSA_HPG):
        rows = slice(j * tq, (j + 1) * tq)
        sj = sw[rows] + bias_w
        pw_ref[rows] = jnp.exp2((sj - jnp.max(sj, axis=-1, keepdims=True)) * c).astype(BF16)
    ow = _dot(pw_ref[...], vwx_ref[pl.ds(w0, win_w), :])

    gates = _sigmoid(gl_ref[...])
    for j in range(NSA_HPG):
        rows = slice(j * tq, (j + 1) * tq)
        o_slc = acc_ref[rows, 0:dh] / acc_ref[rows, dh:]
        o_win = ow[rows, 0:dh] / ow[rows, dh:]
        o = (gates[:, 3 * j:3 * j + 1] * ocmp_ref[:, j * dh:(j + 1) * dh]
             + gates[:, 3 * j + 1:3 * j + 2] * o_slc
             + gates[:, 3 * j + 2:3 * j + 3] * o_win)
        o_ref[:, j * dh:(j + 1) * dh] = o.astype(o_ref.dtype)


def _nsa_attn(proj, sel, ocmp, gate_logits, bsz, seq_len):
    tq, dh, g = NSA_TQ, NSA_HEAD_DIM, NSA_KV_GROUPS
    assert seq_len % NSA_KEY_TILE == 0 and seq_len >= NSA_WINDOW + tq
    rows = NSA_HPG * tq
    nq = seq_len // tq
    nslc = seq_len // NSA_SLC_BLOCK
    expand = (np.arange(nslc)[:, None] == np.arange(seq_len)[None, :] // NSA_SLC_BLOCK)
    expand = jnp.asarray(expand.astype(np.float32), dtype=BF16)
    kv0 = (NSA_N_HEADS * dh + 2 * NSA_KV_DIM) // dh

    def kv_spec(idx):
        return pl.BlockSpec((seq_len, dh), lambda b, gi, qi: (b, kv0 + idx * g + gi))

    qo_spec = pl.BlockSpec((tq, NSA_HPG * dh), lambda b, gi, qi: (b * nq + qi, gi))
    return pl.pallas_call(
        _nsa_attn_kernel,
        grid=(bsz, g, nq),
        in_specs=[
            qo_spec,
            kv_spec(0), kv_spec(1), kv_spec(2), kv_spec(3),
            pl.BlockSpec((None, None, tq, nslc), lambda b, gi, qi: (b, gi, qi, 0)),
            pl.BlockSpec((nslc, seq_len), lambda b, gi, qi: (0, 0)),
            qo_spec,
            pl.BlockSpec((None, tq, 3 * NSA_HPG), lambda b, gi, qi: (gi, b * nq + qi, 0)),
        ],
        out_specs=qo_spec,
        out_shape=jax.ShapeDtypeStruct((bsz * seq_len, NSA_N_HEADS * dh), BF16),
        scratch_shapes=[
            pltpu.VMEM((seq_len, dh), BF16),
            pltpu.VMEM((seq_len, 2 * dh), BF16),
            pltpu.VMEM((seq_len, dh), BF16),
            pltpu.VMEM((seq_len, 2 * dh), BF16),
            pltpu.VMEM((tq, seq_len), F32),
            pltpu.VMEM((rows, dh), BF16),
            pltpu.VMEM((rows, NSA_KEY_TILE), BF16),
            pltpu.VMEM((rows, NSA_WINDOW + tq), BF16),
            pltpu.VMEM((rows, LANE), F32),
            pltpu.VMEM((rows, LANE), F32),
            pltpu.VMEM((rows, 2 * dh), F32),
        ],
        compiler_params=_cparams("parallel", "parallel", "arbitrary"),
        name="nsa_slc_win_attention",
    )(proj, proj, proj, proj, proj, sel, expand, ocmp, gate_logits)


def _sgu_kernel(u_ref, v_ref, g_ref, ws_ref, bst_ref, o_ref):
    v = v_ref[...]
    mu = jnp.mean(v, axis=-1, keepdims=True)
    vc = v - mu
    vn = (vc * lax.rsqrt(jnp.mean(vc * vc, axis=-1, keepdims=True) + RMS_EPS) * g_ref[...]).astype(BF16)
    q = SGU_CHUNK
    tri = lax.broadcasted_iota(jnp.int32, (q, q), 0) >= lax.broadcasted_iota(jnp.int32, (q, q), 1)
    for g in range(SGU_GROUPS):
        cols = slice(g * SGU_GROUP_DIM, (g + 1) * SGU_GROUP_DIM)
        w_m = jnp.where(tri, ws_ref[g], 0.0).astype(BF16)
        sv = _dot(w_m, vn[:, cols]) + bst_ref[:, g:g + 1]
        o_ref[:, cols] = (u_ref[:, cols] * sv).astype(o_ref.dtype)


def _sgu_core(proj, ln_g, w_spatial, b_spatial):
    m = proj.shape[0]
    q = SGU_CHUNK
    return pl.pallas_call(
        _sgu_kernel,
        grid=(m // q,),
        in_specs=[
            pl.BlockSpec((q, SGU_WIDTH), lambda i: (i, 0)),
            pl.BlockSpec((q, SGU_WIDTH), lambda i: (i, 1)),
            pl.BlockSpec((1, SGU_WIDTH), lambda i: (0, 0)),
            pl.BlockSpec((SGU_GROUPS, q, q), lambda i: (0, 0, 0)),
            pl.BlockSpec((q, SGU_GROUPS), lambda i: (0, 0)),
        ],
        out_specs=pl.BlockSpec((q, SGU_WIDTH), lambda i: (i, 0)),
        out_shape=jax.ShapeDtypeStruct((m, SGU_WIDTH), BF16),
        compiler_params=_cparams("parallel"),
        name="sgu_core",
    )(proj, proj, ln_g.reshape(1, SGU_WIDTH), w_spatial, b_spatial.T)


def _pool_kernel(z_ref, zh_ref, wg_ref, sc_ref, o_ref, zz_ref, *, tm, blocks_per_seq):
    i = pl.program_id(0)
    keep = (i % blocks_per_seq != 0).astype(F32)
    zz_ref[0:POOL_HALO, :] = zh_ref[...] * keep
    zz_ref[POOL_HALO:, :] = z_ref[...]
    t = (i % blocks_per_seq) * tm + lax.broadcasted_iota(jnp.int32, (tm, 1), 0)
    for gi, win in enumerate(POOL_WINDOWS):
        cols = slice(gi * POOL_GROUP_DIM, (gi + 1) * POOL_GROUP_DIM)
        s = zz_ref[POOL_HALO:POOL_HALO + tm, cols]
        for k in range(1, win):
            s = s + zz_ref[POOL_HALO - k:POOL_HALO - k + tm, cols]
        count = jnp.minimum(t + 1, win).astype(F32)
        pooled = s / count - z_ref[:, cols]
        y = _dot(pooled.astype(BF16), wg_ref[gi]) * sc_ref[:, cols]
        o_ref[:, cols] = y.astype(o_ref.dtype)


def _pool_core(z, w_group, scale, seq_len, *, tm=512):
    m, n = z.shape
    tm = min(tm, seq_len)
    halo_blocks = tm // POOL_HALO
    return pl.pallas_call(
        functools.partial(_pool_kernel, tm=tm, blocks_per_seq=seq_len // tm),
        grid=(m // tm,),
        in_specs=[
            pl.BlockSpec((tm, n), lambda i: (i, 0)),
            pl.BlockSpec((POOL_HALO, n), lambda i: (jnp.maximum(i * halo_blocks - 1, 0), 0)),
            pl.BlockSpec(w_group.shape, lambda i: (0, 0, 0)),
            pl.BlockSpec((1, n), lambda i: (0, 0)),
        ],
        out_specs=pl.BlockSpec((tm, n), lambda i: (i, 0)),
        out_shape=jax.ShapeDtypeStruct((m, n), BF16),
        scratch_shapes=[pltpu.VMEM((tm + POOL_HALO, n), F32)],
        compiler_params=_cparams("parallel"),
        name="pool_core",
    )(z, z, w_group, scale.reshape(1, n))


def _pad_cols(w, n):
    return jnp.pad(w, ((0, 0), (0, n - w.shape[1])))


def _ssd_layer(x, g_pre, g_post, w_in, conv_w, conv_b, dt_bias, a_log, d_skip, norm_g, w_out, bsz, seq_len):
    w = _pad_cols(w_in, SSD_IN_PAD).astype(BF16)
    proj = _norm_matmul(x, g_pre, w, jnp.zeros((SSD_IN_PAD,), F32))
    y = _ssd_core(proj, conv_w, conv_b, dt_bias, a_log, d_skip, norm_g, bsz, seq_len)
    return _matmul_post(y, w_out.astype(BF16), g_post, x)


def _nsa_layer(x, g_pre, g_post, w_in, cmp_pos, cmp_w1, cmp_w2, w_out, bsz, seq_len):
    w = _pad_cols(w_in, NSA_IN_PAD).astype(BF16)
    proj = _norm_matmul(x, g_pre, w, jnp.zeros((NSA_IN_PAD,), F32))
    kvc = _nsa_compress(proj, cmp_pos, cmp_w1, cmp_w2, bsz, seq_len)
    ocmp, sel = _nsa_select(proj, kvc, bsz, seq_len)
    ngate = 3 * NSA_N_HEADS
    gl = proj[:, NSA_GATE_COL:NSA_GATE_COL + ngate].reshape(-1, NSA_KV_GROUPS, 3 * NSA_HPG).transpose(1, 0, 2)
    o = _nsa_attn(proj, sel, ocmp, gl, bsz, seq_len)
    return _matmul_post(o, w_out.astype(BF16), g_post, x)


def _sgu_layer(x, g_pre, g_post, w_in, b_in, ln_g, w_spatial, b_spatial, w_out):
    proj = _norm_matmul(x, g_pre, w_in.astype(BF16), b_in, act="gelu")
    y = _sgu_core(proj, ln_g, w_spatial, b_spatial)
    return _matmul_post(y, w_out.astype(BF16), g_post, x)


def _pool_layer(x, g_pre, g_post, w_in, w_group, scale, w_out, seq_len):
    z = _norm_matmul(x, g_pre, w_in.astype(BF16), jnp.zeros((D_MODEL,), F32))
    y = _pool_core(z, w_group.astype(BF16), scale, seq_len)
    return _matmul_post(y, w_out.astype(BF16), g_post, x)


def kernel(x, mem, norm_pre, norm_post, norm_mem, ssd_w_in, ssd_conv_w, ssd_conv_b, ssd_dt_bias, ssd_a_log, ssd_d, ssd_norm_g, ssd_w_out, nsa_w_in, nsa_cmp_pos, nsa_cmp_w1, nsa_cmp_w2, nsa_w_out, sgu_w_in, sgu_b_in, sgu_ln_g, sgu_w_spatial, sgu_b_spatial, sgu_w_out, pool_w_in, pool_w_group, pool_scale, pool_w_out, xa_w_q, xa_w_kv, xa_w_o, ffn_w_up, ffn_conv_w, ffn_conv_b, ffn_w_down):
    bsz, seq_len, d = x.shape
    depth = norm_pre.shape[0]
    xf = x.reshape(bsz * seq_len, d)
    memf = mem.reshape(bsz * mem.shape[1], d)
    for i in range(depth):
        kind, j = i % 4, i // 4
        if kind == 0:
            xf = _ssd_layer(xf, norm_pre[i, 0], norm_post[i, 0], ssd_w_in[j], ssd_conv_w[j], ssd_conv_b[j],
                            ssd_dt_bias[j], ssd_a_log[j], ssd_d[j], ssd_norm_g[j], ssd_w_out[j], bsz, seq_len)
        elif kind == 1:
            xf = _nsa_layer(xf, norm_pre[i, 0], norm_post[i, 0], nsa_w_in[j], nsa_cmp_pos[j], nsa_cmp_w1[j],
                            nsa_cmp_w2[j], nsa_w_out[j], bsz, seq_len)
        elif kind == 2:
            xf = _sgu_layer(xf, norm_pre[i, 0], norm_post[i, 0], sgu_w_in[j], sgu_b_in[j], sgu_ln_g[j],
                            sgu_w_spatial[j], sgu_b_spatial[j], sgu_w_out[j])
        else:
            xf = _pool_layer(xf, norm_pre[i, 0], norm_post[i, 0], pool_w_in[j], pool_w_group[j],
                             pool_scale[j], pool_w_out[j], seq_len)
        kv = _norm_matmul(memf, norm_mem[i], xa_w_kv[i].astype(BF16), jnp.zeros((2 * XA_DIM,), F32),
                          out_dtype=BF16, tm=MEM_LEN)
        xf = _xattn(xf, kv, norm_pre[i, 1], norm_post[i, 1], xa_w_q[i].astype(BF16),
                    xa_w_o[i].astype(BF16), seq_len)
        act = _ffn_up(xf, norm_pre[i, 2], ffn_w_up[i].astype(BF16), ffn_conv_w[i], ffn_conv_b[i], seq_len)
        xf = _matmul_post(act, ffn_w_down[i].astype(BF16), norm_post[i, 2], xf)
    return xf.reshape(bsz, seq_len, d)
```

```python
import functools
import math

import jax
import jax.numpy as jnp
import numpy as np
from jax import lax
from jax.experimental import pallas as pl
from jax.experimental.pallas import tpu as pltpu

F32 = jnp.float32
BF16 = jnp.bfloat16

D_MODEL = 2048
RMS_EPS = 1e-6
NEG = -1e30
BIG = 1e30
MEM_LEN = 256

SSD_D_INNER = 4096
SSD_HEAD_DIM = 64
SSD_N_HEADS = 64
SSD_N_GROUPS = 8
SSD_HPG = 8
SSD_D_STATE = 128
SSD_CONV = 4
SSD_CHUNK = 128
SSD_GROUP_W = SSD_HPG * SSD_HEAD_DIM
SSD_BC_DIM = SSD_N_GROUPS * SSD_D_STATE
SSD_IN_PAD = 10752

NSA_HEAD_DIM = 128
NSA_N_HEADS = 16
NSA_KV_GROUPS = 4
NSA_HPG = 4
NSA_KV_DIM = 512
NSA_CMP_BLOCK = 32
NSA_CMP_STRIDE = 16
NSA_CMP_HIDDEN = 256
NSA_SLC_BLOCK = 64
NSA_TOPK = 16
NSA_N_LOCAL = 2
NSA_WINDOW = 512
NSA_TQ = 128
NSA_KEY_TILE = 512
NSA_IN_PAD = 5632
NSA_GATE_COL = 5120

SGU_CHUNK = 128
SGU_WIDTH = 4096
SGU_GROUPS = 8
SGU_GROUP_DIM = 512

POOL_WINDOWS = (2, 4, 8, 16)
POOL_GROUP_DIM = 512
POOL_HALO = 16

XA_HEADS = 4
XA_HEAD_DIM = 128
XA_DIM = 512

FFN_HIDDEN = 5632
FFN_CONV = 3
FFN_HALO = 16

LANE = 128
VMEM_LIMIT = 56 * 1024 * 1024


def _cparams(*sem):
    return pltpu.CompilerParams(dimension_semantics=sem, vmem_limit_bytes=VMEM_LIMIT)


def _dot(a, b):
    return jnp.dot(a, b, preferred_element_type=F32)


def _dot_nt(a, b):
    return lax.dot_general(a, b, (((1,), (1,)), ((), ())), preferred_element_type=F32)


def _split3(v):
    hi = v.astype(BF16)
    r1 = v - hi.astype(F32)
    mid = r1.astype(BF16)
    lo = (r1 - mid.astype(F32)).astype(BF16)
    return hi, mid, lo


def _dot_f32_lhs(v, e):
    hi, mid, lo = _split3(v)
    return _dot(hi, e) + _dot(mid, e) + _dot(lo, e)


def _dot_2piece_lhs(v, e2):
    hi = v.astype(BF16)
    mid = (v - hi.astype(F32)).astype(BF16)
    return _dot(jnp.concatenate([hi, mid], axis=1), e2)


def _dot_f32_rhs(e, v):
    hi, mid, lo = _split3(v)
    return _dot(e, hi) + _dot(e, mid) + _dot(e, lo)


def _rms(x, g):
    ms = jnp.mean(x * x, axis=-1, keepdims=True)
    return x * lax.rsqrt(ms + RMS_EPS) * g


def _sigmoid(x):
    return 1.0 / (1.0 + jnp.exp(-x))


def _silu(x):
    return x * _sigmoid(x)


def _gelu(x):
    c = math.sqrt(2.0 / math.pi)
    return x * (0.5 * (1.0 + jnp.tanh(c * (x + 0.044715 * (x * x * x)))))


def _softplus(x):
    return jnp.maximum(x, 0.0) + jnp.log1p(jnp.exp(-jnp.abs(x)))


def _norm_mm_kernel(x_ref, g_ref, w_ref, b_ref, o_ref, h_ref, *, act):
    @pl.when(pl.program_id(1) == 0)
    def _():
        h_ref[...] = _rms(x_ref[...], g_ref[...]).astype(BF16)

    y = _dot(h_ref[...], w_ref[...]) + b_ref[...]
    if act == "gelu":
        y = _gelu(y)
    o_ref[...] = y.astype(o_ref.dtype)


def _norm_matmul(x, g, w, b, *, act=None, out_dtype=F32, tm=1024, tn=512):
    m, k = x.shape
    n = w.shape[1]
    tm = min(tm, m)
    assert m % tm == 0 and n % tn == 0, (m, n, tm, tn)
    return pl.pallas_call(
        functools.partial(_norm_mm_kernel, act=act),
        grid=(m // tm, n // tn),
        in_specs=[
            pl.BlockSpec((tm, k), lambda i, j: (i, 0)),
            pl.BlockSpec((1, k), lambda i, j: (0, 0)),
            pl.BlockSpec((k, tn), lambda i, j: (0, j)),
            pl.BlockSpec((1, tn), lambda i, j: (0, j)),
        ],
        out_specs=pl.BlockSpec((tm, tn), lambda i, j: (i, j)),
        out_shape=jax.ShapeDtypeStruct((m, n), out_dtype),
        scratch_shapes=[pltpu.VMEM((tm, k), BF16)],
        compiler_params=_cparams("parallel", "arbitrary"),
        name="norm_matmul",
    )(x, g.reshape(1, k), w, b.reshape(1, n))


def _mm_post_kernel(a_ref, w_ref, g_ref, r_ref, o_ref, *, nj, tn):
    j = pl.program_id(1)
    o_ref[:, pl.ds(pl.multiple_of(j * tn, tn), tn)] = _dot(a_ref[...], w_ref[...])

    @pl.when(j == nj - 1)
    def _():
        o_ref[...] = r_ref[...] + _rms(o_ref[...], g_ref[...])


def _matmul_post(a, w, g, res, *, tm=512, tn=512):
    m, kdim = a.shape
    n = w.shape[1]
    tm = min(tm, m)
    assert m % tm == 0 and n % tn == 0
    nj = n // tn
    return pl.pallas_call(
        functools.partial(_mm_post_kernel, nj=nj, tn=tn),
        grid=(m // tm, nj),
        in_specs=[
            pl.BlockSpec((tm, kdim), lambda i, j: (i, 0)),
            pl.BlockSpec((kdim, tn), lambda i, j: (0, j)),
            pl.BlockSpec((1, n), lambda i, j: (0, 0)),
            pl.BlockSpec((tm, n), lambda i, j: (i, 0)),
        ],
        out_specs=pl.BlockSpec((tm, n), lambda i, j: (i, 0)),
        out_shape=jax.ShapeDtypeStruct((m, n), F32),
        compiler_params=_cparams("parallel", "arbitrary"),
        name="matmul_postnorm_residual",
    )(a, w, g.reshape(1, n), res)


def _ffn_up_kernel(x_ref, xh_ref, g_ref, wg_ref, wv_ref, cwg_ref, cwv_ref, cbg_ref, cbv_ref,
                   o_ref, h_ref, *, tm, blocks_per_seq):
    i = pl.program_id(0)

    @pl.when(pl.program_id(1) == 0)
    def _():
        keep = (i % blocks_per_seq != 0).astype(F32)
        h_ref[0:FFN_HALO, :] = (_rms(xh_ref[...], g_ref[...]) * keep).astype(BF16)
        h_ref[FFN_HALO:, :] = _rms(x_ref[...], g_ref[...]).astype(BF16)

    h = h_ref[...]

    def conv(u, cw_ref, cb_ref):
        y = cb_ref[...] + u[FFN_HALO:] * cw_ref[FFN_CONV - 1:FFN_CONV, :]
        for tap in range(FFN_CONV - 1):
            back = FFN_CONV - 1 - tap
            y = y + pltpu.roll(u, back, axis=0)[FFN_HALO:] * cw_ref[tap:tap + 1, :]
        return y

    gate = conv(_dot(h, wg_ref[...]), cwg_ref, cbg_ref)
    val = conv(_dot(h, wv_ref[...]), cwv_ref, cbv_ref)
    o_ref[...] = (_silu(gate) * val).astype(o_ref.dtype)


def _ffn_up(x, g, w_up, conv_w, conv_b, seq_len, *, tm=1024, tn=512):
    m, k = x.shape
    tm = min(tm, seq_len)
    assert m % tm == 0 and seq_len % tm == 0 and FFN_HIDDEN % tn == 0
    nj = FFN_HIDDEN // tn
    halo_blocks = tm // FFN_HALO
    return pl.pallas_call(
        functools.partial(_ffn_up_kernel, tm=tm, blocks_per_seq=seq_len // tm),
        grid=(m // tm, nj),
        in_specs=[
            pl.BlockSpec((tm, k), lambda i, j: (i, 0)),
            pl.BlockSpec((FFN_HALO, k), lambda i, j: (jnp.maximum(i * halo_blocks - 1, 0), 0)),
            pl.BlockSpec((1, k), lambda i, j: (0, 0)),
            pl.BlockSpec((k, tn), lambda i, j: (0, j)),
            pl.BlockSpec((k, tn), lambda i, j: (0, j + nj)),
            pl.BlockSpec((FFN_CONV, tn), lambda i, j: (0, j)),
            pl.BlockSpec((FFN_CONV, tn), lambda i, j: (0, j + nj)),
            pl.BlockSpec((1, tn), lambda i, j: (0, j)),
            pl.BlockSpec((1, tn), lambda i, j: (0, j + nj)),
        ],
        out_specs=pl.BlockSpec((tm, tn), lambda i, j: (i, j)),
        out_shape=jax.ShapeDtypeStruct((m, FFN_HIDDEN), BF16),
        scratch_shapes=[pltpu.VMEM((tm + FFN_HALO, k), BF16)],
        compiler_params=_cparams("parallel", "arbitrary"),
        name="ffn_up_conv_gate",
    )(x, x, g.reshape(1, k), w_up, w_up, conv_w, conv_w,
      conv_b.reshape(1, -1), conv_b.reshape(1, -1))


def _xattn_kernel(x_ref, kv_ref, gpre_ref, gpost_ref, wq_ref, wo_ref, o_ref, att_ref):
    x = x_ref[...]
    h = _rms(x, gpre_ref[...]).astype(BF16)
    q = _dot(h, wq_ref[...]).astype(BF16)
    scale = XA_HEAD_DIM ** -0.5
    for hd in range(XA_HEADS):
        lo = hd * XA_HEAD_DIM
        s = _dot_nt(q[:, lo:lo + XA_HEAD_DIM], kv_ref[:, lo:lo + XA_HEAD_DIM]) * scale
        e = jnp.exp(s - jnp.max(s, axis=-1, keepdims=True))
        p = e / jnp.sum(e, axis=-1, keepdims=True)
        att_ref[:, lo:lo + XA_HEAD_DIM] = _dot(
            p.astype(BF16), kv_ref[:, XA_DIM + lo:XA_DIM + lo + XA_HEAD_DIM]).astype(BF16)
    a = _dot(att_ref[...], wo_ref[...])
    o_ref[...] = x + _rms(a, gpost_ref[...])


def _xattn(x, kv, g_pre, g_post, w_q, w_o, seq_len, *, tm=512):
    m, k = x.shape
    tm = min(tm, seq_len)
    bps = seq_len // tm
    return pl.pallas_call(
        _xattn_kernel,
        grid=(m // tm,),
        in_specs=[
            pl.BlockSpec((tm, k), lambda i: (i, 0)),
            pl.BlockSpec((MEM_LEN, 2 * XA_DIM), lambda i: (i // bps, 0)),
            pl.BlockSpec((1, k), lambda i: (0, 0)),
            pl.BlockSpec((1, k), lambda i: (0, 0)),
            pl.BlockSpec((k, XA_DIM), lambda i: (0, 0)),
            pl.BlockSpec((XA_DIM, k), lambda i: (0, 0)),
        ],
        out_specs=pl.BlockSpec((tm, k), lambda i: (i, 0)),
        out_shape=jax.ShapeDtypeStruct((m, k), F32),
        scratch_shapes=[pltpu.VMEM((tm, XA_DIM), BF16)],
        compiler_params=_cparams("parallel"),
        name="memory_cross_attention",
    )(x, kv, g_pre.reshape(1, k), g_post.reshape(1, k), w_q, w_o)


def _ssd_kernel(z_ref, xr_ref, br_ref, cr_ref, dtr_ref, cw_ref, cbias_ref, dtb_ref, alog_ref,
                dskip_ref, ng_ref, expand_ref, tril_ref, o_ref,
                xin_ref, xs_ref, bm_ref, cm_ref, wx_ref, expax_ref, cdx_ref,
                acsg_ref, acst_ref, state_ref, y_ref):
    q = SSD_CHUNK
    c = pl.program_id(1)
    halo = 8
    xw = SSD_D_INNER

    @pl.when(c == 0)
    def _():
        xin_ref[0:halo, :] = jnp.zeros((halo, xin_ref.shape[1]), F32)
        state_ref[...] = jnp.zeros(state_ref.shape, F32)

    @pl.when(c > 0)
    def _():
        xin_ref[0:halo, :] = xin_ref[q:q + halo, :]

    xin_ref[halo:, 0:xw] = xr_ref[...]
    xin_ref[halo:, xw:xw + SSD_BC_DIM] = br_ref[...]
    xin_ref[halo:, xw + SSD_BC_DIM:] = cr_ref[...]

    def conv_tile(lo, width):
        slab = xin_ref[:, pl.ds(lo, width)]
        y = cbias_ref[:, pl.ds(lo, width)] + slab[halo:] * cw_ref[SSD_CONV - 1:SSD_CONV, pl.ds(lo, width)]
        for tap in range(SSD_CONV - 1):
            back = SSD_CONV - 1 - tap
            y = y + pltpu.roll(slab, back, axis=0)[halo:] * cw_ref[tap:tap + 1, pl.ds(lo, width)]
        return _silu(y)

    def conv_x(t, carry):
        lo = pl.multiple_of(t * 512, 512)
        xs_ref[:, pl.ds(lo, 512)] = conv_tile(lo, 512)
        return carry

    lax.fori_loop(0, xw // 512, conv_x, 0)
    for t in range(SSD_BC_DIM // 512):
        bm_ref[:, t * 512:(t + 1) * 512] = conv_tile(xw + t * 512, 512)
        cm_ref[:, t * 512:(t + 1) * 512] = conv_tile(xw + SSD_BC_DIM + t * 512, 512)

    dt = _softplus(dtr_ref[...] + dtb_ref[...])
    a = dt * (-jnp.exp(alog_ref[...]))
    a_cs = _dot_f32_rhs(tril_ref[...], a)
    a_last = a_cs[q - 1:q, :]
    expand = expand_ref[...]
    wx_ref[...] = _dot_2piece_lhs(dt * jnp.exp(a_last - a_cs), expand)
    expax_ref[...] = _dot_2piece_lhs(jnp.exp(a_cs), expand)
    cdx_ref[...] = _dot_2piece_lhs(jnp.broadcast_to(jnp.exp(a_last), (8, LANE)), expand)
    acst_ref[...] = (a_cs - jnp.log(dt)).T.reshape(LANE // SSD_HPG, SSD_HPG, q)
    for g in range(SSD_N_GROUPS):
        acsg_ref[g] = a_cs if g == 0 else pltpu.roll(a_cs, LANE - g * SSD_HPG, axis=1)

    row = lax.broadcasted_iota(jnp.int32, (q, q), 0)
    col = lax.broadcasted_iota(jnp.int32, (q, q), 1)
    causal = row >= col
    left_half = col < SSD_HEAD_DIM

    def group_body(g, carry):
        off = pl.multiple_of(g * SSD_GROUP_W, SSD_GROUP_W)
        offn = pl.multiple_of(g * SSD_D_STATE, SSD_D_STATE)
        cm = cm_ref[:, pl.ds(offn, SSD_D_STATE)]
        bm = bm_ref[:, pl.ds(offn, SSD_D_STATE)]
        cmb = cm.astype(BF16)
        cb = _dot_nt(cmb, bm.astype(BF16))
        xs = xs_ref[:, pl.ds(off, SSD_GROUP_W)]
        st = state_ref[g]
        y_off = _dot(cmb, st.astype(BF16)) * expax_ref[:, pl.ds(off, SSD_GROUP_W)]
        s_new = _dot(bm.T.astype(BF16), (xs * wx_ref[:, pl.ds(off, SSD_GROUP_W)]).astype(BF16))
        state_ref[g] = st * cdx_ref[0:1, pl.ds(off, SSD_GROUP_W)] + s_new
        acs_g = acsg_ref[g]
        acst_g = acst_ref[g]
        skip = xs * dskip_ref[:, pl.ds(off, SSD_GROUP_W)]
        for pr in range(SSD_HPG // 2):
            scs = []
            for e in range(2):
                j = 2 * pr + e
                diff = acs_g[:, j:j + 1] - acst_g[j:j + 1, :]
                dec = jnp.where(causal, jnp.exp(diff), 0.0)
                scs.append((cb * dec).astype(BF16))
            xp = xs[:, pr * LANE:(pr + 1) * LANE]
            x0 = jnp.where(left_half, xp, 0.0).astype(BF16)
            x1 = jnp.where(left_half, 0.0, xp).astype(BF16)
            y_diag = _dot(jnp.concatenate(scs, axis=1), jnp.concatenate([x0, x1], axis=0))
            y = y_diag + y_off[:, pr * LANE:(pr + 1) * LANE] + skip[:, pr * LANE:(pr + 1) * LANE]
            y_ref[:, pl.ds(pl.multiple_of(off + pr * LANE, LANE), LANE)] = y
        return carry

    lax.fori_loop(0, SSD_N_GROUPS, group_body, 0)

    gated = y_ref[...] * _silu(z_ref[...])
    o_ref[...] = _rms(gated, ng_ref[...]).astype(o_ref.dtype)


def _ssd_core(proj, conv_w, conv_b, dt_bias, a_log, d_skip, norm_g, bsz, seq_len):
    q = SSD_CHUNK
    nc = seq_len // q
    m = bsz * seq_len
    conv_dim = SSD_D_INNER + 2 * SSD_BC_DIM
    pad = LANE - SSD_N_HEADS
    dtb = jnp.pad(dt_bias, (0, pad)).reshape(1, LANE)
    alog = jnp.pad(a_log, (0, pad)).reshape(1, LANE)
    dskip = jnp.repeat(d_skip, SSD_HEAD_DIM).reshape(1, SSD_D_INNER)
    heads = np.arange(2 * LANE)[:, None] % LANE
    chans = np.arange(SSD_D_INNER)[None, :] // SSD_HEAD_DIM
    expand = jnp.asarray((heads == chans).astype(np.float32), dtype=BF16)
    tril = jnp.asarray(np.tril(np.ones((q, q), np.float32)), dtype=BF16)
    row = lambda b, c: b * nc + c
    const = lambda b, c: (0, 0)
    return pl.pallas_call(
        _ssd_kernel,
        grid=(bsz, nc),
        in_specs=[
            pl.BlockSpec((q, SSD_D_INNER), lambda b, c: (row(b, c), 0)),
            pl.BlockSpec((q, SSD_D_INNER), lambda b, c: (row(b, c), 1)),
            pl.BlockSpec((q, SSD_BC_DIM), lambda b, c: (row(b, c), 8)),
            pl.BlockSpec((q, SSD_BC_DIM), lambda b, c: (row(b, c), 9)),
            pl.BlockSpec((q, LANE), lambda b, c: (row(b, c), 80)),
            pl.BlockSpec((SSD_CONV, conv_dim), const),
            pl.BlockSpec((1, conv_dim), const),
            pl.BlockSpec((1, LANE), const),
            pl.BlockSpec((1, LANE), const),
            pl.BlockSpec((1, SSD_D_INNER), const),
            pl.BlockSpec((1, SSD_D_INNER), const),
            pl.BlockSpec((2 * LANE, SSD_D_INNER), const),
            pl.BlockSpec((q, q), const),
        ],
        out_specs=pl.BlockSpec((q, SSD_D_INNER), lambda b, c: (row(b, c), 0)),
        out_shape=jax.ShapeDtypeStruct((m, SSD_D_INNER), BF16),
        scratch_shapes=[
            pltpu.VMEM((q + 8, conv_dim), F32),
            pltpu.VMEM((q, SSD_D_INNER), F32),
            pltpu.VMEM((q, SSD_BC_DIM), F32),
            pltpu.VMEM((q, SSD_BC_DIM), F32),
            pltpu.VMEM((q, SSD_D_INNER), F32),
            pltpu.VMEM((q, SSD_D_INNER), F32),
            pltpu.VMEM((8, SSD_D_INNER), F32),
            pltpu.VMEM((SSD_N_GROUPS, q, LANE), F32),
            pltpu.VMEM((LANE // SSD_HPG, SSD_HPG, q), F32),
            pltpu.VMEM((SSD_N_GROUPS, SSD_D_STATE, SSD_GROUP_W), F32),
            pltpu.VMEM((q, SSD_D_INNER), F32),
        ],
        compiler_params=_cparams("parallel", "arbitrary"),
        name="ssd_core",
    )(proj, proj, proj, proj, proj, conv_w, conv_b.reshape(1, conv_dim), dtb, alog, dskip,
      norm_g.reshape(1, SSD_D_INNER), expand, tril)


def _nsa_compress_kernel(t_ref, pos_ref, w1_ref, w2_ref, o_ref, *, nchunk):
    half = NSA_CMP_STRIDE
    dh = NSA_HEAD_DIM
    p_lo = jnp.zeros((nchunk, NSA_CMP_HIDDEN), F32)
    p_hi = jnp.zeros((nchunk, NSA_CMP_HIDDEN), F32)
    for l in range(half):
        rows = t_ref[pl.ds(l, nchunk, stride=half), :]
        a_lo = (rows + pos_ref[l:l + 1, :]).astype(BF16)
        a_hi = (rows + pos_ref[half + l:half + l + 1, :]).astype(BF16)
        p_lo = p_lo + _dot(a_lo, w1_ref[l * dh:(l + 1) * dh, :].astype(BF16))
        p_hi = p_hi + _dot(a_hi, w1_ref[(half + l) * dh:(half + l + 1) * dh, :].astype(BF16))
    pre = p_lo + pltpu.roll(p_hi, nchunk - 1, axis=0)
    o_ref[...] = _dot(_gelu(pre).astype(BF16), w2_ref[...].astype(BF16))


def _nsa_compress(proj, cmp_pos, cmp_w1, cmp_w2, bsz, seq_len):
    nchunk = seq_len // NSA_CMP_STRIDE
    g, dh = NSA_KV_GROUPS, NSA_HEAD_DIM
    col0 = NSA_N_HEADS * dh // dh
    return pl.pallas_call(
        functools.partial(_nsa_compress_kernel, nchunk=nchunk),
        grid=(bsz, 2, g),
        in_specs=[
            pl.BlockSpec((seq_len, dh), lambda b, s, gi: (b, col0 + s * g + gi)),
            pl.BlockSpec((None, NSA_CMP_BLOCK, dh), lambda b, s, gi: (s, 0, 0)),
            pl.BlockSpec((None, NSA_CMP_BLOCK * dh, NSA_CMP_HIDDEN), lambda b, s, gi: (s, 0, 0)),
            pl.BlockSpec((None, NSA_CMP_HIDDEN, dh), lambda b, s, gi: (s, 0, 0)),
        ],
        out_specs=pl.BlockSpec((None, None, None, nchunk, dh), lambda b, s, gi: (b, s, gi, 0, 0)),
        out_shape=jax.ShapeDtypeStruct((bsz, 2, g, nchunk, dh), F32),
        compiler_params=_cparams("parallel", "parallel", "parallel"),
        name="nsa_compress",
    )(proj, cmp_pos, cmp_w1, cmp_w2)


def _nsa_select_kernel(q_ref, kc_ref, vc_ref, cover_ref, ocmp_ref, sel_ref, *, ncmp, nslc):
    tq = NSA_TQ
    dh = NSA_HEAD_DIM
    scale = dh ** -0.5
    nck = kc_ref.shape[0]
    t0 = pl.program_id(2) * tq
    t_col = t0 + lax.broadcasted_iota(jnp.int32, (tq, 1), 0)
    ci_row = lax.broadcasted_iota(jnp.int32, (1, nck), 1)
    valid = (ci_row * NSA_CMP_STRIDE + (NSA_CMP_BLOCK - 1) <= t_col) & (ci_row < ncmp)
    t_row = t0 + lax.broadcasted_iota(jnp.int32, (1, tq), 1)
    ci_col = lax.broadcasted_iota(jnp.int32, (nck, 1), 0)
    valid_t = (ci_col * NSA_CMP_STRIDE + (NSA_CMP_BLOCK - 1) <= t_row) & (ci_col < ncmp)
    kcb = kc_ref[...].astype(BF16)
    vcb = vc_ref[...].astype(BF16)
    psum_t = jnp.zeros((nck, tq), F32)
    for j in range(NSA_HPG):
        qj = q_ref[:, j * dh:(j + 1) * dh].astype(BF16)
        s = jnp.where(valid, _dot_nt(qj, kcb) * scale, NEG)
        e = jnp.exp(s - jnp.max(s, axis=-1, keepdims=True))
        p = jnp.where(valid, e / jnp.sum(e, axis=-1, keepdims=True), 0.0)
        ocmp_ref[:, j * dh:(j + 1) * dh] = _dot(p.astype(BF16), vcb)
        st = jnp.where(valid_t, _dot_nt(kcb, qj) * scale, NEG)
        et = jnp.exp(st - jnp.max(st, axis=0, keepdims=True))
        psum_t = psum_t + jnp.where(valid_t, et / jnp.sum(et, axis=0, keepdims=True), 0.0)
    imp_t = _dot_f32_rhs(cover_ref[...], psum_t)
    blk = lax.broadcasted_iota(jnp.int32, (nslc, tq), 0)
    cur = t_row // NSA_SLC_BLOCK
    forced = (blk == 0) | ((blk <= cur) & (blk > cur - 1 - NSA_N_LOCAL))
    future = blk * NSA_SLC_BLOCK > t_row
    score = jnp.where(forced, BIG, jnp.where(future, NEG, imp_t))
    sub = 8
    slabs = [score[r * sub:(r + 1) * sub, :] for r in range(nslc // sub)]
    ranks = [jnp.zeros((sub, tq), F32) for _ in slabs]
    row_in_slab = lax.broadcasted_iota(jnp.int32, (sub, tq), 0)
    for j in range(nslc):
        cj = jnp.broadcast_to(score[j:j + 1, :], (sub, tq))
        for r, s_r in enumerate(slabs):
            if r * sub > j:
                before = cj >= s_r
            elif (r + 1) * sub - 1 < j:
                before = cj > s_r
            else:
                before = (cj > s_r) | ((cj == s_r) & (row_in_slab > j - r * sub))
            ranks[r] = ranks[r] + jnp.where(before, 1.0, 0.0)
    topk = float(min(NSA_TOPK, nslc))
    for r, rank in enumerate(ranks):
        sel_ref[r * sub:(r + 1) * sub, :] = jnp.where(rank < topk, 1.0, 0.0)


def _nsa_select(proj, kvc, bsz, seq_len):
    tq, dh, g = NSA_TQ, NSA_HEAD_DIM, NSA_KV_GROUPS
    nq = seq_len // tq
    ncmp = (seq_len - NSA_CMP_BLOCK) // NSA_CMP_STRIDE + 1
    nck = seq_len // NSA_CMP_STRIDE
    nslc = seq_len // NSA_SLC_BLOCK
    ci = np.arange(nck)[:, None] * NSA_CMP_STRIDE
    sj = np.arange(nslc)[None, :] * NSA_SLC_BLOCK
    cover = ((ci <= sj + NSA_SLC_BLOCK - 1) & (ci + NSA_CMP_BLOCK - 1 >= sj)).astype(np.float32)
    cover = jnp.asarray(cover.T, dtype=BF16)
    return pl.pallas_call(
        functools.partial(_nsa_select_kernel, ncmp=ncmp, nslc=nslc),
        grid=(bsz, g, nq),
        in_specs=[
            pl.BlockSpec((tq, NSA_HPG * dh), lambda b, gi, qi: (b * nq + qi, gi)),
            pl.BlockSpec((None, None, None, nck, dh), lambda b, gi, qi: (b, 0, gi, 0, 0)),
            pl.BlockSpec((None, None, None, nck, dh), lambda b, gi, qi: (b, 1, gi, 0, 0)),
            pl.BlockSpec((nslc, nck), lambda b, gi, qi: (0, 0)),
        ],
        out_specs=[
            pl.BlockSpec((tq, NSA_HPG * dh), lambda b, gi, qi: (b * nq + qi, gi)),
            pl.BlockSpec((None, None, nslc, tq), lambda b, gi, qi: (b, gi, 0, qi)),
        ],
        out_shape=[
            jax.ShapeDtypeStruct((bsz * seq_len, NSA_N_HEADS * dh), F32),
            jax.ShapeDtypeStruct((bsz, g, nslc, seq_len), F32),
        ],
        compiler_params=_cparams("parallel", "parallel", "parallel"),
        name="nsa_cmp_select",
    )(proj, kvc, kvc, cover)


def _nsa_attn_kernel(q_ref, ks_ref, vs_ref, kw_ref, vw_ref, sel_ref, expand_ref, ocmp_ref, gl_ref,
                     o_ref, ksb_ref, vsx_ref, kwb_ref, vwx_ref, bias_ref, q4_ref, p_ref, pw_ref,
                     m_ref, alpha_ref, acc_ref):
    tq = NSA_TQ
    dh = NSA_HEAD_DIM
    kt_w = NSA_KEY_TILE
    win_w = NSA_WINDOW + tq
    c = (dh ** -0.5) * math.log2(math.e)
    qi = pl.program_id(2)

    @pl.when(qi == 0)
    def _():
        ones = jnp.ones((ks_ref.shape[0], dh), BF16)
        ksb_ref[...] = ks_ref[...].astype(BF16)
        kwb_ref[...] = kw_ref[...].astype(BF16)
        vsx_ref[:, 0:dh] = vs_ref[...].astype(BF16)
        vsx_ref[:, dh:] = ones
        vwx_ref[:, 0:dh] = vw_ref[...].astype(BF16)
        vwx_ref[:, dh:] = ones

    bias_ref[...] = (_dot(sel_ref[...].T.astype(BF16), expand_ref[...]) - 1.0) * BIG
    for j in range(NSA_HPG):
        q4_ref[j * tq:(j + 1) * tq, :] = q_ref[:, j * dh:(j + 1) * dh].astype(BF16)
    t1 = qi * tq + lax.broadcasted_iota(jnp.int32, (tq, 1), 0)
    m_ref[...] = jnp.full(m_ref.shape, NEG, F32)
    acc_ref[...] = jnp.zeros(acc_ref.shape, F32)
    half = NSA_HPG // 2 * tq

    def slc_tile(kt, carry):
        off = pl.multiple_of(kt * kt_w, kt_w)
        pos = off + lax.broadcasted_iota(jnp.int32, (1, kt_w), 1)
        bias = jnp.where(pos <= t1, bias_ref[:, pl.ds(off, kt_w)], NEG)
        for hp in range(2):
            pr = slice(hp * half, (hp + 1) * half)
            s = _dot_nt(q4_ref[pr], ksb_ref[pl.ds(off, kt_w), :])
            for e in range(NSA_HPG // 2):
                rows = slice(hp * half + e * tq, hp * half + (e + 1) * tq)
                sj = s[e * tq:(e + 1) * tq] + bias
                m_prev = m_ref[rows]
                m_new = jnp.maximum(m_prev, jnp.max(sj, axis=-1, keepdims=True))
                alpha_ref[rows] = jnp.exp2((m_prev - m_new) * c)
                m_ref[rows] = m_new
                m_wide = jnp.concatenate([m_new] * (kt_w // LANE), axis=1)
                p_ref[rows] = jnp.exp2((sj - m_wide) * c).astype(BF16)
            pv = _dot(p_ref[pr], vsx_ref[pl.ds(off, kt_w), :])
            alpha = alpha_ref[pr]
            acc_ref[pr] = jnp.concatenate([alpha, alpha], axis=1) * acc_ref[pr] + pv
        return carry

    lax.fori_loop(0, ((qi + 1) * tq + kt_w - 1) // kt_w, slc_tile, 0)

    w0 = pl.multiple_of(jnp.maximum(qi * tq - NSA_WINDOW, 0), tq)
    sw = _dot_nt(q4_ref[...], kwb_ref[pl.ds(w0, win_w), :])
    diff = t1 - (w0 + lax.broadcasted_iota(jnp.int32, (1, win_w), 1))
    bias_w = jnp.where((diff >= 0) & (diff < NSA_WINDOW), 0.0, NEG)
    for j in range(NSA_HPG):
        rows = slice(j * tq, (j + 1) * tq)
        sj = sw[rows] + bias_w
        pw_ref[rows] = jnp.exp2((sj - jnp.max(sj, axis=-1, keepdims=True)) * c).astype(BF16)
    ow = _dot(pw_ref[...], vwx_ref[pl.ds(w0, win_w), :])

    gates = _sigmoid(gl_ref[...])
    for j in range(NSA_HPG):
        rows = slice(j * tq, (j + 1) * tq)
        o_slc = acc_ref[rows, 0:dh] / acc_ref[rows, dh:]
        o_win = ow[rows, 0:dh] / ow[rows, dh:]
        o = (gates[:, 3 * j:3 * j + 1] * ocmp_ref[:, j * dh:(j + 1) * dh]
             + gates[:, 3 * j + 1:3 * j + 2] * o_slc
             + gates[:, 3 * j + 2:3 * j + 3] * o_win)
        o_ref[:, j * dh:(j + 1) * dh] = o.astype(o_ref.dtype)


def _nsa_attn(proj, sel, ocmp, gate_logits, bsz, seq_len):
    tq, dh, g = NSA_TQ, NSA_HEAD_DIM, NSA_KV_GROUPS
    assert seq_len % NSA_KEY_TILE == 0 and seq_len >= NSA_WINDOW + tq
    rows = NSA_HPG * tq
    nq = seq_len // tq
    nslc = seq_len // NSA_SLC_BLOCK
    expand = (np.arange(nslc)[:, None] == np.arange(seq_len)[None, :] // NSA_SLC_BLOCK)
    expand = jnp.asarray(expand.astype(np.float32), dtype=BF16)
    kv0 = (NSA_N_HEADS * dh + 2 * NSA_KV_DIM) // dh

    def kv_spec(idx):
        return pl.BlockSpec((seq_len, dh), lambda b, gi, qi: (b, kv0 + idx * g + gi))

    qo_spec = pl.BlockSpec((tq, NSA_HPG * dh), lambda b, gi, qi: (b * nq + qi, gi))
    return pl.pallas_call(
        _nsa_attn_kernel,
        grid=(bsz, g, nq),
        in_specs=[
            qo_spec,
            kv_spec(0), kv_spec(1), kv_spec(2), kv_spec(3),
            pl.BlockSpec((None, None, nslc, tq), lambda b, gi, qi: (b, gi, 0, qi)),
            pl.BlockSpec((nslc, seq_len), lambda b, gi, qi: (0, 0)),
            qo_spec,
            pl.BlockSpec((None, tq, 3 * NSA_HPG), lambda b, gi, qi: (gi, b * nq + qi, 0)),
        ],
        out_specs=qo_spec,
        out_shape=jax.ShapeDtypeStruct((bsz * seq_len, NSA_N_HEADS * dh), BF16),
        scratch_shapes=[
            pltpu.VMEM((seq_len, dh), BF16),
            pltpu.VMEM((seq_len, 2 * dh), BF16),
            pltpu.VMEM((seq_len, dh), BF16),
            pltpu.VMEM((seq_len, 2 * dh), BF16),
            pltpu.VMEM((tq, seq_len), F32),
            pltpu.VMEM((rows, dh), BF16),
            pltpu.VMEM((rows, NSA_KEY_TILE), BF16),
            pltpu.VMEM((rows, NSA_WINDOW + tq), BF16),
            pltpu.VMEM((rows, LANE), F32),
            pltpu.VMEM((rows, LANE), F32),
            pltpu.VMEM((rows, 2 * dh), F32),
        ],
        compiler_params=_cparams("parallel", "parallel", "arbitrary"),
        name="nsa_slc_win_attention",
    )(proj, proj, proj, proj, proj, sel, expand, ocmp, gate_logits)


def _sgu_kernel(u_ref, v_ref, g_ref, ws_ref, bst_ref, o_ref):
    v = v_ref[...]
    mu = jnp.mean(v, axis=-1, keepdims=True)
    vc = v - mu
    vn = (vc * lax.rsqrt(jnp.mean(vc * vc, axis=-1, keepdims=True) + RMS_EPS) * g_ref[...]).astype(BF16)
    q = SGU_CHUNK
    tri = lax.broadcasted_iota(jnp.int32, (q, q), 0) >= lax.broadcasted_iota(jnp.int32, (q, q), 1)
    for g in range(SGU_GROUPS):
        cols = slice(g * SGU_GROUP_DIM, (g + 1) * SGU_GROUP_DIM)
        w_m = jnp.where(tri, ws_ref[g], 0.0).astype(BF16)
        sv = _dot(w_m, vn[:, cols]) + bst_ref[:, g:g + 1]
        o_ref[:, cols] = (u_ref[:, cols] * sv).astype(o_ref.dtype)


def _sgu_core(proj, ln_g, w_spatial, b_spatial):
    m = proj.shape[0]
    q = SGU_CHUNK
    return pl.pallas_call(
        _sgu_kernel,
        grid=(m // q,),
        in_specs=[
            pl.BlockSpec((q, SGU_WIDTH), lambda i: (i, 0)),
            pl.BlockSpec((q, SGU_WIDTH), lambda i: (i, 1)),
            pl.BlockSpec((1, SGU_WIDTH), lambda i: (0, 0)),
            pl.BlockSpec((SGU_GROUPS, q, q), lambda i: (0, 0, 0)),
            pl.BlockSpec((q, SGU_GROUPS), lambda i: (0, 0)),
        ],
        out_specs=pl.BlockSpec((q, SGU_WIDTH), lambda i: (i, 0)),
        out_shape=jax.ShapeDtypeStruct((m, SGU_WIDTH), BF16),
        compiler_params=_cparams("parallel"),
        name="sgu_core",
    )(proj, proj, ln_g.reshape(1, SGU_WIDTH), w_spatial, b_spatial.T)


def _pool_kernel(z_ref, zh_ref, wg_ref, sc_ref, o_ref, zz_ref, *, tm, blocks_per_seq):
    i = pl.program_id(0)
    keep = (i % blocks_per_seq != 0).astype(F32)
    zz_ref[0:POOL_HALO, :] = zh_ref[...] * keep
    zz_ref[POOL_HALO:, :] = z_ref[...]
    t = (i % blocks_per_seq) * tm + lax.broadcasted_iota(jnp.int32, (tm, 1), 0)
    for gi, win in enumerate(POOL_WINDOWS):
        cols = slice(gi * POOL_GROUP_DIM, (gi + 1) * POOL_GROUP_DIM)
        s = zz_ref[POOL_HALO:POOL_HALO + tm, cols]
        for k in range(1, win):
            s = s + zz_ref[POOL_HALO - k:POOL_HALO - k + tm, cols]
        count = jnp.minimum(t + 1, win).astype(F32)
        pooled = s / count - z_ref[:, cols]
        y = _dot(pooled.astype(BF16), wg_ref[gi]) * sc_ref[:, cols]
        o_ref[:, cols] = y.astype(o_ref.dtype)


def _pool_core(z, w_group, scale, seq_len, *, tm=512):
    m, n = z.shape
    tm = min(tm, seq_len)
    halo_blocks = tm // POOL_HALO
    return pl.pallas_call(
        functools.partial(_pool_kernel, tm=tm, blocks_per_seq=seq_len // tm),
        grid=(m // tm,),
        in_specs=[
            pl.BlockSpec((tm, n), lambda i: (i, 0)),
            pl.BlockSpec((POOL_HALO, n), lambda i: (jnp.maximum(i * halo_blocks - 1, 0), 0)),
            pl.BlockSpec(w_group.shape, lambda i: (0, 0, 0)),
            pl.BlockSpec((1, n), lambda i: (0, 0)),
        ],
        out_specs=pl.BlockSpec((tm, n), lambda i: (i, 0)),
        out_shape=jax.ShapeDtypeStruct((m, n), BF16),
        scratch_shapes=[pltpu.VMEM((tm + POOL_HALO, n), F32)],
        compiler_params=_cparams("parallel"),
        name="pool_core",
    )(z, z, w_group, scale.reshape(1, n))


def _pad_cols(w, n):
    return jnp.pad(w, ((0, 0), (0, n - w.shape[1])))


def _ssd_layer(x, g_pre, g_post, w_in, conv_w, conv_b, dt_bias, a_log, d_skip, norm_g, w_out, bsz, seq_len):
    w = _pad_cols(w_in, SSD_IN_PAD).astype(BF16)
    proj = _norm_matmul(x, g_pre, w, jnp.zeros((SSD_IN_PAD,), F32))
    y = _ssd_core(proj, conv_w, conv_b, dt_bias, a_log, d_skip, norm_g, bsz, seq_len)
    return _matmul_post(y, w_out.astype(BF16), g_post, x)


def _nsa_layer(x, g_pre, g_post, w_in, cmp_pos, cmp_w1, cmp_w2, w_out, bsz, seq_len):
    w = _pad_cols(w_in, NSA_IN_PAD).astype(BF16)
    proj = _norm_matmul(x, g_pre, w, jnp.zeros((NSA_IN_PAD,), F32))
    kvc = _nsa_compress(proj, cmp_pos, cmp_w1, cmp_w2, bsz, seq_len)
    ocmp, sel = _nsa_select(proj, kvc, bsz, seq_len)
    ngate = 3 * NSA_N_HEADS
    gl = proj[:, NSA_GATE_COL:NSA_GATE_COL + ngate].reshape(-1, NSA_KV_GROUPS, 3 * NSA_HPG).transpose(1, 0, 2)
    o = _nsa_attn(proj, sel, ocmp, gl, bsz, seq_len)
    return _matmul_post(o, w_out.astype(BF16), g_post, x)


def _sgu_layer(x, g_pre, g_post, w_in, b_in, ln_g, w_spatial, b_spatial, w_out):
    proj = _norm_matmul(x, g_pre, w_in.astype(BF16), b_in, act="gelu")
    y = _sgu_core(proj, ln_g, w_spatial, b_spatial)
    return _matmul_post(y, w_out.astype(BF16), g_post, x)


def _pool_layer(x, g_pre, g_post, w_in, w_group, scale, w_out, seq_len):
    z = _norm_matmul(x, g_pre, w_in.astype(BF16), jnp.zeros((D_MODEL,), F32))
    y = _pool_core(z, w_group.astype(BF16), scale, seq_len)
    return _matmul_post(y, w_out.astype(BF16), g_post, x)


def kernel(x, mem, norm_pre, norm_post, norm_mem, ssd_w_in, ssd_conv_w, ssd_conv_b, ssd_dt_bias, ssd_a_log, ssd_d, ssd_norm_g, ssd_w_out, nsa_w_in, nsa_cmp_pos, nsa_cmp_w1, nsa_cmp_w2, nsa_w_out, sgu_w_in, sgu_b_in, sgu_ln_g, sgu_w_spatial, sgu_b_spatial, sgu_w_out, pool_w_in, pool_w_group, pool_scale, pool_w_out, xa_w_q, xa_w_kv, xa_w_o, ffn_w_up, ffn_conv_w, ffn_conv_b, ffn_w_down):
    bsz, seq_len, d = x.shape
    depth = norm_pre.shape[0]
    xf = x.reshape(bsz * seq_len, d)
    memf = mem.reshape(bsz * mem.shape[1], d)
    for i in range(depth):
        kind, j = i % 4, i // 4
        if kind == 0:
            xf = _ssd_layer(xf, norm_pre[i, 0], norm_post[i, 0], ssd_w_in[j], ssd_conv_w[j], ssd_conv_b[j],
                            ssd_dt_bias[j], ssd_a_log[j], ssd_d[j], ssd_norm_g[j], ssd_w_out[j], bsz, seq_len)
        elif kind == 1:
            xf = _nsa_layer(xf, norm_pre[i, 0], norm_post[i, 0], nsa_w_in[j], nsa_cmp_pos[j], nsa_cmp_w1[j],
                            nsa_cmp_w2[j], nsa_w_out[j], bsz, seq_len)
        elif kind == 2:
            xf = _sgu_layer(xf, norm_pre[i, 0], norm_post[i, 0], sgu_w_in[j], sgu_b_in[j], sgu_ln_g[j],
                            sgu_w_spatial[j], sgu_b_spatial[j], sgu_w_out[j])
        else:
            xf = _pool_layer(xf, norm_pre[i, 0], norm_post[i, 0], pool_w_in[j], pool_w_group[j],
                             pool_scale[j], pool_w_out[j], seq_len)
        kv = _norm_matmul(memf, norm_mem[i], xa_w_kv[i].astype(BF16), jnp.zeros((2 * XA_DIM,), F32),
                          out_dtype=BF16, tm=MEM_LEN)
        xf = _xattn(xf, kv, norm_pre[i, 1], norm_post[i, 1], xa_w_q[i].astype(BF16),
                    xa_w_o[i].astype(BF16), seq_len)
        act = _ffn_up(xf, norm_pre[i, 2], ffn_w_up[i].astype(BF16), ffn_conv_w[i], ffn_conv_b[i], seq_len)
        xf = _matmul_post(act, ffn_w_down[i].astype(BF16), norm_post[i, 2], xf)
    return xf.reshape(bsz, seq_len, d)
```

```python
import functools
import math

import jax
import jax.numpy as jnp
import numpy as np
from jax import lax
from jax.experimental import pallas as pl
from jax.experimental.pallas import tpu as pltpu

F32 = jnp.float32
BF16 = jnp.bfloat16

D_MODEL = 2048
RMS_EPS = 1e-6
NEG = -1e30
BIG = 1e30
MEM_LEN = 256

SSD_D_INNER = 4096
SSD_HEAD_DIM = 64
SSD_N_HEADS = 64
SSD_N_GROUPS = 8
SSD_HPG = 8
SSD_D_STATE = 128
SSD_CONV = 4
SSD_CHUNK = 128
SSD_GROUP_W = SSD_HPG * SSD_HEAD_DIM
SSD_BC_DIM = SSD_N_GROUPS * SSD_D_STATE
SSD_IN_PAD = 10752

NSA_HEAD_DIM = 128
NSA_N_HEADS = 16
NSA_KV_GROUPS = 4
NSA_HPG = 4
NSA_KV_DIM = 512
NSA_CMP_BLOCK = 32
NSA_CMP_STRIDE = 16
NSA_CMP_HIDDEN = 256
NSA_SLC_BLOCK = 64
NSA_TOPK = 16
NSA_N_LOCAL = 2
NSA_WINDOW = 512
NSA_TQ = 256
NSA_KEY_TILE = 1024
NSA_SEL_TQ = 256
NSA_IN_PAD = 5632
NSA_GATE_COL = 5120

SGU_CHUNK = 128
SGU_WIDTH = 4096
SGU_GROUPS = 8
SGU_GROUP_DIM = 512

POOL_WINDOWS = (2, 4, 8, 16)
POOL_GROUP_DIM = 512
POOL_HALO = 16

XA_HEADS = 4
XA_HEAD_DIM = 128
XA_DIM = 512

FFN_HIDDEN = 5632
FFN_CONV = 3
FFN_HALO = 16

LANE = 128
VMEM_LIMIT = 56 * 1024 * 1024


def _cparams(*sem):
    return pltpu.CompilerParams(dimension_semantics=sem, vmem_limit_bytes=VMEM_LIMIT)


def _dot(a, b):
    return jnp.dot(a, b, preferred_element_type=F32)


def _dot_nt(a, b):
    return lax.dot_general(a, b, (((1,), (1,)), ((), ())), preferred_element_type=F32)


def _split3(v):
    hi = v.astype(BF16)
    r1 = v - hi.astype(F32)
    mid = r1.astype(BF16)
    lo = (r1 - mid.astype(F32)).astype(BF16)
    return hi, mid, lo


def _dot_f32_lhs(v, e):
    hi, mid, lo = _split3(v)
    return _dot(hi, e) + _dot(mid, e) + _dot(lo, e)


def _dot_2piece_lhs(v, e2):
    hi = v.astype(BF16)
    mid = (v - hi.astype(F32)).astype(BF16)
    return _dot(jnp.concatenate([hi, mid], axis=1), e2)


def _dot_f32_rhs(e, v):
    hi, mid, lo = _split3(v)
    return _dot(e, hi) + _dot(e, mid) + _dot(e, lo)


def _rms(x, g):
    ms = jnp.mean(x * x, axis=-1, keepdims=True)
    return x * lax.rsqrt(ms + RMS_EPS) * g


def _sigmoid(x):
    return 1.0 / (1.0 + jnp.exp(-x))


def _silu(x):
    return x * _sigmoid(x)


def _gelu(x):
    c = math.sqrt(2.0 / math.pi)
    return x * (0.5 * (1.0 + jnp.tanh(c * (x + 0.044715 * (x * x * x)))))


def _softplus(x):
    return jnp.maximum(x, 0.0) + jnp.log1p(jnp.exp(-jnp.abs(x)))


def _norm_mm_kernel(x_ref, g_ref, w_ref, b_ref, o_ref, h_ref, *, act):
    @pl.when(pl.program_id(1) == 0)
    def _():
        h_ref[...] = _rms(x_ref[...], g_ref[...]).astype(BF16)

    y = _dot(h_ref[...], w_ref[...]) + b_ref[...]
    if act == "gelu":
        y = _gelu(y)
    o_ref[...] = y.astype(o_ref.dtype)


def _norm_matmul(x, g, w, b, *, act=None, out_dtype=F32, tm=1024, tn=512):
    m, k = x.shape
    n = w.shape[1]
    tm = min(tm, m)
    assert m % tm == 0 and n % tn == 0, (m, n, tm, tn)
    return pl.pallas_call(
        functools.partial(_norm_mm_kernel, act=act),
        grid=(m // tm, n // tn),
        in_specs=[
            pl.BlockSpec((tm, k), lambda i, j: (i, 0)),
            pl.BlockSpec((1, k), lambda i, j: (0, 0)),
            pl.BlockSpec((k, tn), lambda i, j: (0, j)),
            pl.BlockSpec((1, tn), lambda i, j: (0, j)),
        ],
        out_specs=pl.BlockSpec((tm, tn), lambda i, j: (i, j)),
        out_shape=jax.ShapeDtypeStruct((m, n), out_dtype),
        scratch_shapes=[pltpu.VMEM((tm, k), BF16)],
        compiler_params=_cparams("parallel", "arbitrary"),
        name="norm_matmul",
    )(x, g.reshape(1, k), w, b.reshape(1, n))


def _mm_post_kernel(a_ref, w_ref, g_ref, r_ref, o_ref, *, nj, tn):
    j = pl.program_id(1)
    o_ref[:, pl.ds(pl.multiple_of(j * tn, tn), tn)] = _dot(a_ref[...], w_ref[...])

    @pl.when(j == nj - 1)
    def _():
        o_ref[...] = r_ref[...] + _rms(o_ref[...], g_ref[...])


def _matmul_post(a, w, g, res, *, tm=512, tn=512):
    m, kdim = a.shape
    n = w.shape[1]
    tm = min(tm, m)
    assert m % tm == 0 and n % tn == 0
    nj = n // tn
    return pl.pallas_call(
        functools.partial(_mm_post_kernel, nj=nj, tn=tn),
        grid=(m // tm, nj),
        in_specs=[
            pl.BlockSpec((tm, kdim), lambda i, j: (i, 0)),
            pl.BlockSpec((kdim, tn), lambda i, j: (0, j)),
            pl.BlockSpec((1, n), lambda i, j: (0, 0)),
            pl.BlockSpec((tm, n), lambda i, j: (i, 0)),
        ],
        out_specs=pl.BlockSpec((tm, n), lambda i, j: (i, 0)),
        out_shape=jax.ShapeDtypeStruct((m, n), F32),
        compiler_params=_cparams("parallel", "arbitrary"),
        name="matmul_postnorm_residual",
    )(a, w, g.reshape(1, n), res)


def _ffn_up_kernel(x_ref, xh_ref, g_ref, wg_ref, wv_ref, cwg_ref, cwv_ref, cbg_ref, cbv_ref,
                   o_ref, h_ref, *, tm, blocks_per_seq):
    i = pl.program_id(0)

    @pl.when(pl.program_id(1) == 0)
    def _():
        keep = (i % blocks_per_seq != 0).astype(F32)
        h_ref[0:FFN_HALO, :] = (_rms(xh_ref[...], g_ref[...]) * keep).astype(BF16)
        h_ref[FFN_HALO:, :] = _rms(x_ref[...], g_ref[...]).astype(BF16)

    h = h_ref[...]

    def conv(u, cw_ref, cb_ref):
        y = cb_ref[...] + u[FFN_HALO:] * cw_ref[FFN_CONV - 1:FFN_CONV, :]
        for tap in range(FFN_CONV - 1):
            back = FFN_CONV - 1 - tap
            y = y + pltpu.roll(u, back, axis=0)[FFN_HALO:] * cw_ref[tap:tap + 1, :]
        return y

    gate = conv(_dot(h, wg_ref[...]), cwg_ref, cbg_ref)
    val = conv(_dot(h, wv_ref[...]), cwv_ref, cbv_ref)
    o_ref[...] = (_silu(gate) * val).astype(o_ref.dtype)


def _ffn_up(x, g, w_up, conv_w, conv_b, seq_len, *, tm=1024, tn=512):
    m, k = x.shape
    tm = min(tm, seq_len)
    assert m % tm == 0 and seq_len % tm == 0 and FFN_HIDDEN % tn == 0
    nj = FFN_HIDDEN // tn
    halo_blocks = tm // FFN_HALO
    return pl.pallas_call(
        functools.partial(_ffn_up_kernel, tm=tm, blocks_per_seq=seq_len // tm),
        grid=(m // tm, nj),
        in_specs=[
            pl.BlockSpec((tm, k), lambda i, j: (i, 0)),
            pl.BlockSpec((FFN_HALO, k), lambda i, j: (jnp.maximum(i * halo_blocks - 1, 0), 0)),
            pl.BlockSpec((1, k), lambda i, j: (0, 0)),
            pl.BlockSpec((k, tn), lambda i, j: (0, j)),
            pl.BlockSpec((k, tn), lambda i, j: (0, j + nj)),
            pl.BlockSpec((FFN_CONV, tn), lambda i, j: (0, j)),
            pl.BlockSpec((FFN_CONV, tn), lambda i, j: (0, j + nj)),
            pl.BlockSpec((1, tn), lambda i, j: (0, j)),
            pl.BlockSpec((1, tn), lambda i, j: (0, j + nj)),
        ],
        out_specs=pl.BlockSpec((tm, tn), lambda i, j: (i, j)),
        out_shape=jax.ShapeDtypeStruct((m, FFN_HIDDEN), BF16),
        scratch_shapes=[pltpu.VMEM((tm + FFN_HALO, k), BF16)],
        compiler_params=_cparams("parallel", "arbitrary"),
        name="ffn_up_conv_gate",
    )(x, x, g.reshape(1, k), w_up, w_up, conv_w, conv_w,
      conv_b.reshape(1, -1), conv_b.reshape(1, -1))


def _xattn_kernel(x_ref, kv_ref, gpre_ref, gpost_ref, wq_ref, wo_ref, o_ref, att_ref):
    x = x_ref[...]
    h = _rms(x, gpre_ref[...]).astype(BF16)
    q = _dot(h, wq_ref[...]).astype(BF16)
    scale = XA_HEAD_DIM ** -0.5
    for hd in range(XA_HEADS):
        lo = hd * XA_HEAD_DIM
        s = _dot_nt(q[:, lo:lo + XA_HEAD_DIM], kv_ref[:, lo:lo + XA_HEAD_DIM]) * scale
        e = jnp.exp(s - jnp.max(s, axis=-1, keepdims=True))
        p = e / jnp.sum(e, axis=-1, keepdims=True)
        att_ref[:, lo:lo + XA_HEAD_DIM] = _dot(
            p.astype(BF16), kv_ref[:, XA_DIM + lo:XA_DIM + lo + XA_HEAD_DIM]).astype(BF16)
    a = _dot(att_ref[...], wo_ref[...])
    o_ref[...] = x + _rms(a, gpost_ref[...])


def _xattn(x, kv, g_pre, g_post, w_q, w_o, seq_len, *, tm=512):
    m, k = x.shape
    tm = min(tm, seq_len)
    bps = seq_len // tm
    return pl.pallas_call(
        _xattn_kernel,
        grid=(m // tm,),
        in_specs=[
            pl.BlockSpec((tm, k), lambda i: (i, 0)),
            pl.BlockSpec((MEM_LEN, 2 * XA_DIM), lambda i: (i // bps, 0)),
            pl.BlockSpec((1, k), lambda i: (0, 0)),
            pl.BlockSpec((1, k), lambda i: (0, 0)),
            pl.BlockSpec((k, XA_DIM), lambda i: (0, 0)),
            pl.BlockSpec((XA_DIM, k), lambda i: (0, 0)),
        ],
        out_specs=pl.BlockSpec((tm, k), lambda i: (i, 0)),
        out_shape=jax.ShapeDtypeStruct((m, k), F32),
        scratch_shapes=[pltpu.VMEM((tm, XA_DIM), BF16)],
        compiler_params=_cparams("parallel"),
        name="memory_cross_attention",
    )(x, kv, g_pre.reshape(1, k), g_post.reshape(1, k), w_q, w_o)


def _ssd_kernel(z_ref, xr_ref, br_ref, cr_ref, dtr_ref, cw_ref, cbias_ref, dtb_ref, alog_ref,
                dskip_ref, ng_ref, expand_ref, tril_ref, o_ref,
                xin_ref, xs_ref, bm_ref, cm_ref, wx_ref, expax_ref, cdx_ref,
                acsg_ref, acst_ref, state_ref, y_ref):
    q = SSD_CHUNK
    c = pl.program_id(1)
    halo = 8
    xw = SSD_D_INNER

    @pl.when(c == 0)
    def _():
        xin_ref[0:halo, :] = jnp.zeros((halo, xin_ref.shape[1]), F32)
        state_ref[...] = jnp.zeros(state_ref.shape, F32)

    @pl.when(c > 0)
    def _():
        xin_ref[0:halo, :] = xin_ref[q:q + halo, :]

    xin_ref[halo:, 0:xw] = xr_ref[...]
    xin_ref[halo:, xw:xw + SSD_BC_DIM] = br_ref[...]
    xin_ref[halo:, xw + SSD_BC_DIM:] = cr_ref[...]

    def conv_tile(lo, width):
        slab = xin_ref[:, pl.ds(lo, width)]
        y = cbias_ref[:, pl.ds(lo, width)] + slab[halo:] * cw_ref[SSD_CONV - 1:SSD_CONV, pl.ds(lo, width)]
        for tap in range(SSD_CONV - 1):
            back = SSD_CONV - 1 - tap
            y = y + pltpu.roll(slab, back, axis=0)[halo:] * cw_ref[tap:tap + 1, pl.ds(lo, width)]
        return _silu(y)

    def conv_x(t, carry):
        lo = pl.multiple_of(t * 512, 512)
        xs_ref[:, pl.ds(lo, 512)] = conv_tile(lo, 512)
        return carry

    lax.fori_loop(0, xw // 512, conv_x, 0)
    for t in range(SSD_BC_DIM // 512):
        bm_ref[:, t * 512:(t + 1) * 512] = conv_tile(xw + t * 512, 512)
        cm_ref[:, t * 512:(t + 1) * 512] = conv_tile(xw + SSD_BC_DIM + t * 512, 512)

    dt = _softplus(dtr_ref[...] + dtb_ref[...])
    a = dt * (-jnp.exp(alog_ref[...]))
    a_cs = _dot_f32_rhs(tril_ref[...], a)
    a_last = a_cs[q - 1:q, :]
    expand = expand_ref[...]
    wx_ref[...] = _dot_2piece_lhs(dt * jnp.exp(a_last - a_cs), expand)
    expax_ref[...] = _dot_2piece_lhs(jnp.exp(a_cs), expand)
    cdx_ref[...] = _dot_2piece_lhs(jnp.broadcast_to(jnp.exp(a_last), (8, LANE)), expand)
    acst_ref[...] = (a_cs - jnp.log(dt)).T.reshape(LANE // SSD_HPG, SSD_HPG, q)
    for g in range(SSD_N_GROUPS):
        acsg_ref[g] = a_cs if g == 0 else pltpu.roll(a_cs, LANE - g * SSD_HPG, axis=1)

    row = lax.broadcasted_iota(jnp.int32, (q, q), 0)
    col = lax.broadcasted_iota(jnp.int32, (q, q), 1)
    causal = row >= col
    left_half = col < SSD_HEAD_DIM

    def group_body(g, carry):
        off = pl.multiple_of(g * SSD_GROUP_W, SSD_GROUP_W)
        offn = pl.multiple_of(g * SSD_D_STATE, SSD_D_STATE)
        cm = cm_ref[:, pl.ds(offn, SSD_D_STATE)]
        bm = bm_ref[:, pl.ds(offn, SSD_D_STATE)]
        cmb = cm.astype(BF16)
        cb = _dot_nt(cmb, bm.astype(BF16))
        xs = xs_ref[:, pl.ds(off, SSD_GROUP_W)]
        st = state_ref[g]
        y_off = _dot(cmb, st.astype(BF16)) * expax_ref[:, pl.ds(off, SSD_GROUP_W)]
        s_new = _dot(bm.T.astype(BF16), (xs * wx_ref[:, pl.ds(off, SSD_GROUP_W)]).astype(BF16))
        state_ref[g] = st * cdx_ref[0:1, pl.ds(off, SSD_GROUP_W)] + s_new
        acs_g = acsg_ref[g]
        acst_g = acst_ref[g]
        skip = xs * dskip_ref[:, pl.ds(off, SSD_GROUP_W)]
        for pr in range(SSD_HPG // 2):
            scs = []
            for e in range(2):
                j = 2 * pr + e
                diff = acs_g[:, j:j + 1] - acst_g[j:j + 1, :]
                dec = jnp.where(causal, jnp.exp(diff), 0.0)
                scs.append((cb * dec).astype(BF16))
            xp = xs[:, pr * LANE:(pr + 1) * LANE]
            x0 = jnp.where(left_half, xp, 0.0).astype(BF16)
            x1 = jnp.where(left_half, 0.0, xp).astype(BF16)
            y_diag = _dot(jnp.concatenate(scs, axis=1), jnp.concatenate([x0, x1], axis=0))
            y = y_diag + y_off[:, pr * LANE:(pr + 1) * LANE] + skip[:, pr * LANE:(pr + 1) * LANE]
            y_ref[:, pl.ds(pl.multiple_of(off + pr * LANE, LANE), LANE)] = y
        return carry

    lax.fori_loop(0, SSD_N_GROUPS, group_body, 0)

    gated = y_ref[...] * _silu(z_ref[...])
    o_ref[...] = _rms(gated, ng_ref[...]).astype(o_ref.dtype)


def _ssd_core(proj, conv_w, conv_b, dt_bias, a_log, d_skip, norm_g, bsz, seq_len):
    q = SSD_CHUNK
    nc = seq_len // q
    m = bsz * seq_len
    conv_dim = SSD_D_INNER + 2 * SSD_BC_DIM
    pad = LANE - SSD_N_HEADS
    dtb = jnp.pad(dt_bias, (0, pad)).reshape(1, LANE)
    alog = jnp.pad(a_log, (0, pad)).reshape(1, LANE)
    dskip = jnp.repeat(d_skip, SSD_HEAD_DIM).reshape(1, SSD_D_INNER)
    heads = np.arange(2 * LANE)[:, None] % LANE
    chans = np.arange(SSD_D_INNER)[None, :] // SSD_HEAD_DIM
    expand = jnp.asarray((heads == chans).astype(np.float32), dtype=BF16)
    tril = jnp.asarray(np.tril(np.ones((q, q), np.float32)), dtype=BF16)
    row = lambda b, c: b * nc + c
    const = lambda b, c: (0, 0)
    return pl.pallas_call(
        _ssd_kernel,
        grid=(bsz, nc),
        in_specs=[
            pl.BlockSpec((q, SSD_D_INNER), lambda b, c: (row(b, c), 0)),
            pl.BlockSpec((q, SSD_D_INNER), lambda b, c: (row(b, c), 1)),
            pl.BlockSpec((q, SSD_BC_DIM), lambda b, c: (row(b, c), 8)),
            pl.BlockSpec((q, SSD_BC_DIM), lambda b, c: (row(b, c), 9)),
            pl.BlockSpec((q, LANE), lambda b, c: (row(b, c), 80)),
            pl.BlockSpec((SSD_CONV, conv_dim), const),
            pl.BlockSpec((1, conv_dim), const),
            pl.BlockSpec((1, LANE), const),
            pl.BlockSpec((1, LANE), const),
            pl.BlockSpec((1, SSD_D_INNER), const),
            pl.BlockSpec((1, SSD_D_INNER), const),
            pl.BlockSpec((2 * LANE, SSD_D_INNER), const),
            pl.BlockSpec((q, q), const),
        ],
        out_specs=pl.BlockSpec((q, SSD_D_INNER), lambda b, c: (row(b, c), 0)),
        out_shape=jax.ShapeDtypeStruct((m, SSD_D_INNER), BF16),
        scratch_shapes=[
            pltpu.VMEM((q + 8, conv_dim), F32),
            pltpu.VMEM((q, SSD_D_INNER), F32),
            pltpu.VMEM((q, SSD_BC_DIM), F32),
            pltpu.VMEM((q, SSD_BC_DIM), F32),
            pltpu.VMEM((q, SSD_D_INNER), F32),
            pltpu.VMEM((q, SSD_D_INNER), F32),
            pltpu.VMEM((8, SSD_D_INNER), F32),
            pltpu.VMEM((SSD_N_GROUPS, q, LANE), F32),
            pltpu.VMEM((LANE // SSD_HPG, SSD_HPG, q), F32),
            pltpu.VMEM((SSD_N_GROUPS, SSD_D_STATE, SSD_GROUP_W), F32),
            pltpu.VMEM((q, SSD_D_INNER), F32),
        ],
        compiler_params=_cparams("parallel", "arbitrary"),
        name="ssd_core",
    )(proj, proj, proj, proj, proj, conv_w, conv_b.reshape(1, conv_dim), dtb, alog, dskip,
      norm_g.reshape(1, SSD_D_INNER), expand, tril)


def _nsa_compress_kernel(t_ref, pos_ref, w1_ref, w2_ref, o_ref, *, nchunk):
    half = NSA_CMP_STRIDE
    dh = NSA_HEAD_DIM
    p_lo = jnp.zeros((nchunk, NSA_CMP_HIDDEN), F32)
    p_hi = jnp.zeros((nchunk, NSA_CMP_HIDDEN), F32)
    for l in range(half):
        rows = t_ref[pl.ds(l, nchunk, stride=half), :]
        a_lo = (rows + pos_ref[l:l + 1, :]).astype(BF16)
        a_hi = (rows + pos_ref[half + l:half + l + 1, :]).astype(BF16)
        p_lo = p_lo + _dot(a_lo, w1_ref[l * dh:(l + 1) * dh, :].astype(BF16))
        p_hi = p_hi + _dot(a_hi, w1_ref[(half + l) * dh:(half + l + 1) * dh, :].astype(BF16))
    pre = p_lo + pltpu.roll(p_hi, nchunk - 1, axis=0)
    o_ref[...] = _dot(_gelu(pre).astype(BF16), w2_ref[...].astype(BF16))


def _nsa_compress(proj, cmp_pos, cmp_w1, cmp_w2, bsz, seq_len):
    nchunk = seq_len // NSA_CMP_STRIDE
    g, dh = NSA_KV_GROUPS, NSA_HEAD_DIM
    col0 = NSA_N_HEADS * dh // dh
    return pl.pallas_call(
        functools.partial(_nsa_compress_kernel, nchunk=nchunk),
        grid=(bsz, 2, g),
        in_specs=[
            pl.BlockSpec((seq_len, dh), lambda b, s, gi: (b, col0 + s * g + gi)),
            pl.BlockSpec((None, NSA_CMP_BLOCK, dh), lambda b, s, gi: (s, 0, 0)),
            pl.BlockSpec((None, NSA_CMP_BLOCK * dh, NSA_CMP_HIDDEN), lambda b, s, gi: (s, 0, 0)),
            pl.BlockSpec((None, NSA_CMP_HIDDEN, dh), lambda b, s, gi: (s, 0, 0)),
        ],
        out_specs=pl.BlockSpec((None, None, None, nchunk, dh), lambda b, s, gi: (b, s, gi, 0, 0)),
        out_shape=jax.ShapeDtypeStruct((bsz, 2, g, nchunk, dh), F32),
        compiler_params=_cparams("parallel", "parallel", "parallel"),
        name="nsa_compress",
    )(proj, cmp_pos, cmp_w1, cmp_w2)


def _nsa_select_kernel(q_ref, kc_ref, vc_ref, cover_ref, ocmp_ref, sel_ref, *, ncmp, nslc):
    tq = NSA_SEL_TQ
    dh = NSA_HEAD_DIM
    scale = dh ** -0.5
    nck = kc_ref.shape[0]
    t0 = pl.program_id(2) * tq
    t_col = t0 + lax.broadcasted_iota(jnp.int32, (tq, 1), 0)
    ci_row = lax.broadcasted_iota(jnp.int32, (1, nck), 1)
    valid = (ci_row * NSA_CMP_STRIDE + (NSA_CMP_BLOCK - 1) <= t_col) & (ci_row < ncmp)
    t_row = t0 + lax.broadcasted_iota(jnp.int32, (1, tq), 1)
    ci_col = lax.broadcasted_iota(jnp.int32, (nck, 1), 0)
    valid_t = (ci_col * NSA_CMP_STRIDE + (NSA_CMP_BLOCK - 1) <= t_row) & (ci_col < ncmp)
    kcb = kc_ref[...].astype(BF16)
    vcb = vc_ref[...].astype(BF16)
    psum_t = jnp.zeros((nck, tq), F32)
    for j in range(NSA_HPG):
        qj = q_ref[:, j * dh:(j + 1) * dh].astype(BF16)
        s = jnp.where(valid, _dot_nt(qj, kcb) * scale, NEG)
        e = jnp.exp(s - jnp.max(s, axis=-1, keepdims=True))
        p = jnp.where(valid, e / jnp.sum(e, axis=-1, keepdims=True), 0.0)
        ocmp_ref[:, j * dh:(j + 1) * dh] = _dot(p.astype(BF16), vcb)
        st = jnp.where(valid_t, _dot_nt(kcb, qj) * scale, NEG)
        et = jnp.exp(st - jnp.max(st, axis=0, keepdims=True))
        psum_t = psum_t + jnp.where(valid_t, et / jnp.sum(et, axis=0, keepdims=True), 0.0)
    imp_t = _dot_f32_rhs(cover_ref[...], psum_t)
    blk = lax.broadcasted_iota(jnp.int32, (nslc, tq), 0)
    cur = t_row // NSA_SLC_BLOCK
    forced = (blk == 0) | ((blk <= cur) & (blk > cur - 1 - NSA_N_LOCAL))
    future = blk * NSA_SLC_BLOCK > t_row
    score = jnp.where(forced, BIG, jnp.where(future, NEG, imp_t))
    sub = 8
    slabs = [score[r * sub:(r + 1) * sub, :] for r in range(nslc // sub)]
    ranks = [jnp.zeros((sub, tq), F32) for _ in slabs]
    row_in_slab = lax.broadcasted_iota(jnp.int32, (sub, tq), 0)
    for j in range(nslc):
        cj = jnp.broadcast_to(score[j:j + 1, :], (sub, tq))
        for r, s_r in enumerate(slabs):
            if r * sub > j:
                before = cj >= s_r
            elif (r + 1) * sub - 1 < j:
                before = cj > s_r
            else:
                before = (cj > s_r) | ((cj == s_r) & (row_in_slab > j - r * sub))
            ranks[r] = ranks[r] + jnp.where(before, 1.0, 0.0)
    topk = float(min(NSA_TOPK, nslc))
    for r, rank in enumerate(ranks):
        sel_ref[r * sub:(r + 1) * sub, :] = jnp.where(rank < topk, 1.0, 0.0)


def _nsa_select(proj, kvc, bsz, seq_len):
    tq, dh, g = NSA_SEL_TQ, NSA_HEAD_DIM, NSA_KV_GROUPS
    nq = seq_len // tq
    ncmp = (seq_len - NSA_CMP_BLOCK) // NSA_CMP_STRIDE + 1
    nck = seq_len // NSA_CMP_STRIDE
    nslc = seq_len // NSA_SLC_BLOCK
    ci = np.arange(nck)[:, None] * NSA_CMP_STRIDE
    sj = np.arange(nslc)[None, :] * NSA_SLC_BLOCK
    cover = ((ci <= sj + NSA_SLC_BLOCK - 1) & (ci + NSA_CMP_BLOCK - 1 >= sj)).astype(np.float32)
    cover = jnp.asarray(cover.T, dtype=BF16)
    return pl.pallas_call(
        functools.partial(_nsa_select_kernel, ncmp=ncmp, nslc=nslc),
        grid=(bsz, g, nq),
        in_specs=[
            pl.BlockSpec((tq, NSA_HPG * dh), lambda b, gi, qi: (b * nq + qi, gi)),
            pl.BlockSpec((None, None, None, nck, dh), lambda b, gi, qi: (b, 0, gi, 0, 0)),
            pl.BlockSpec((None, None, None, nck, dh), lambda b, gi, qi: (b, 1, gi, 0, 0)),
            pl.BlockSpec((nslc, nck), lambda b, gi, qi: (0, 0)),
        ],
        out_specs=[
            pl.BlockSpec((tq, NSA_HPG * dh), lambda b, gi, qi: (b * nq + qi, gi)),
            pl.BlockSpec((None, None, nslc, tq), lambda b, gi, qi: (b, gi, 0, qi)),
        ],
        out_shape=[
            jax.ShapeDtypeStruct((bsz * seq_len, NSA_N_HEADS * dh), F32),
            jax.ShapeDtypeStruct((bsz, g, nslc, seq_len), F32),
        ],
        compiler_params=_cparams("parallel", "parallel", "parallel"),
        name="nsa_cmp_select",
    )(proj, kvc, kvc, cover)


def _nsa_attn_kernel(q_ref, ks_ref, vs_ref, kw_ref, vw_ref, sel_ref, expand_ref, ocmp_ref, gl_ref,
                     o_ref, ksb_ref, vsx_ref, kwb_ref, vwx_ref, bias_ref, q4_ref, pw_ref, m_ref, acc_ref):
    tq = NSA_TQ
    dh = NSA_HEAD_DIM
    kt_w = NSA_KEY_TILE
    win_w = NSA_WINDOW + tq
    c = (dh ** -0.5) * math.log2(math.e)
    qi = pl.program_id(2)

    @pl.when(qi == 0)
    def _():
        ones = jnp.ones((ks_ref.shape[0], dh), BF16)
        for blk in range(ks_ref.shape[0] // LANE):
            cols = slice(blk * LANE, (blk + 1) * LANE)
            ksb_ref[:, cols] = ks_ref[cols, :].T.astype(BF16)
            kwb_ref[:, cols] = kw_ref[cols, :].T.astype(BF16)
        vsx_ref[:, 0:dh] = vs_ref[...].astype(BF16)
        vsx_ref[:, dh:] = ones
        vwx_ref[:, 0:dh] = vw_ref[...].astype(BF16)
        vwx_ref[:, dh:] = ones

    bias_ref[...] = (_dot(sel_ref[...].T.astype(BF16), expand_ref[...]) - 1.0) * BIG
    for j in range(NSA_HPG):
        q4_ref[j * tq:(j + 1) * tq, :] = q_ref[:, j * dh:(j + 1) * dh].astype(BF16)
    t1 = qi * tq + lax.broadcasted_iota(jnp.int32, (tq, 1), 0)
    m_ref[...] = jnp.full(m_ref.shape, NEG, F32)
    acc_ref[...] = jnp.zeros(acc_ref.shape, F32)
    half = NSA_HPG // 2 * tq

    def slc_tile(kt, carry):
        off = pl.multiple_of(kt * kt_w, kt_w)
        pos = off + lax.broadcasted_iota(jnp.int32, (1, kt_w), 1)
        bias = jnp.where(pos <= t1, bias_ref[:, pl.ds(off, kt_w)], NEG)
        k_t = ksb_ref[:, pl.ds(off, kt_w)]
        v_x = vsx_ref[pl.ds(off, kt_w), :]
        pairs = [slice(hp * half, (hp + 1) * half) for hp in range(2)]
        scores = [_dot(q4_ref[pr], k_t) for pr in pairs]
        for hp, pr in enumerate(pairs):
            m_prev = m_ref[pr]
            ps, m_news = [], []
            for e in range(NSA_HPG // 2):
                sj = scores[hp][e * tq:(e + 1) * tq] + bias
                m_new = jnp.maximum(m_prev[e * tq:(e + 1) * tq], jnp.max(sj, axis=-1, keepdims=True))
                m_wide = jnp.concatenate([m_new] * (kt_w // LANE), axis=1)
                ps.append(jnp.exp2((sj - m_wide) * c).astype(BF16))
                m_news.append(m_new)
            m_new = jnp.concatenate(m_news, axis=0)
            alpha = jnp.exp2((m_prev - m_new) * c)
            m_ref[pr] = m_new
            pv = _dot(jnp.concatenate(ps, axis=0), v_x)
            acc_ref[pr] = jnp.concatenate([alpha, alpha], axis=1) * acc_ref[pr] + pv
        return carry

    lax.fori_loop(0, ((qi + 1) * tq + kt_w - 1) // kt_w, slc_tile, 0)

    w0 = pl.multiple_of(jnp.maximum(qi * tq - NSA_WINDOW, 0), tq)
    sw = _dot(q4_ref[...], kwb_ref[:, pl.ds(w0, win_w)])
    diff = t1 - (w0 + lax.broadcasted_iota(jnp.int32, (1, win_w), 1))
    bias_w = jnp.where((diff >= 0) & (diff < NSA_WINDOW), 0.0, NEG)
    for j in range(NSA_HPG):
        rows = slice(j * tq, (j + 1) * tq)
        sj = sw[rows] + bias_w
        pw_ref[rows] = jnp.exp2((sj - jnp.max(sj, axis=-1, keepdims=True)) * c).astype(BF16)
    ow = _dot(pw_ref[...], vwx_ref[pl.ds(w0, win_w), :])

    gates = _sigmoid(gl_ref[...])
    for j in range(NSA_HPG):
        rows = slice(j * tq, (j + 1) * tq)
        o_slc = acc_ref[rows, 0:dh] / acc_ref[rows, dh:]
        o_win = ow[rows, 0:dh] / ow[rows, dh:]
        o = (gates[:, 3 * j:3 * j + 1] * ocmp_ref[:, j * dh:(j + 1) * dh]
             + gates[:, 3 * j + 1:3 * j + 2] * o_slc
             + gates[:, 3 * j + 2:3 * j + 3] * o_win)
        o_ref[:, j * dh:(j + 1) * dh] = o.astype(o_ref.dtype)


def _nsa_attn(proj, sel, ocmp, gate_logits, bsz, seq_len):
    tq, dh, g = NSA_TQ, NSA_HEAD_DIM, NSA_KV_GROUPS
    assert seq_len % NSA_KEY_TILE == 0 and seq_len >= NSA_WINDOW + tq
    rows = NSA_HPG * tq
    nq = seq_len // tq
    nslc = seq_len // NSA_SLC_BLOCK
    expand = (np.arange(nslc)[:, None] == np.arange(seq_len)[None, :] // NSA_SLC_BLOCK)
    expand = jnp.asarray(expand.astype(np.float32), dtype=BF16)
    kv0 = (NSA_N_HEADS * dh + 2 * NSA_KV_DIM) // dh

    def kv_spec(idx):
        return pl.BlockSpec((seq_len, dh), lambda b, gi, qi: (b, kv0 + idx * g + gi))

    qo_spec = pl.BlockSpec((tq, NSA_HPG * dh), lambda b, gi, qi: (b * nq + qi, gi))
    return pl.pallas_call(
        _nsa_attn_kernel,
        grid=(bsz, g, nq),
        in_specs=[
            qo_spec,
            kv_spec(0), kv_spec(1), kv_spec(2), kv_spec(3),
            pl.BlockSpec((None, None, nslc, tq), lambda b, gi, qi: (b, gi, 0, qi)),
            pl.BlockSpec((nslc, seq_len), lambda b, gi, qi: (0, 0)),
            qo_spec,
            pl.BlockSpec((None, tq, 3 * NSA_HPG), lambda b, gi, qi: (gi, b * nq + qi, 0)),
        ],
        out_specs=qo_spec,
        out_shape=jax.ShapeDtypeStruct((bsz * seq_len, NSA_N_HEADS * dh), BF16),
        scratch_shapes=[
            pltpu.VMEM((dh, seq_len), BF16),
            pltpu.VMEM((seq_len, 2 * dh), BF16),
            pltpu.VMEM((dh, seq_len), BF16),
            pltpu.VMEM((seq_len, 2 * dh), BF16),
            pltpu.VMEM((tq, seq_len), F32),
            pltpu.VMEM((rows, dh), BF16),
            pltpu.VMEM((rows, NSA_WINDOW + tq), BF16),
            pltpu.VMEM((rows, LANE), F32),
            pltpu.VMEM((rows, 2 * dh), F32),
        ],
        compiler_params=_cparams("parallel", "parallel", "arbitrary"),
        name="nsa_slc_win_attention",
    )(proj, proj, proj, proj, proj, sel, expand, ocmp, gate_logits)


def _sgu_kernel(u_ref, v_ref, g_ref, ws_ref, bst_ref, o_ref):
    v = v_ref[...]
    mu = jnp.mean(v, axis=-1, keepdims=True)
    vc = v - mu
    vn = (vc * lax.rsqrt(jnp.mean(vc * vc, axis=-1, keepdims=True) + RMS_EPS) * g_ref[...]).astype(BF16)
    q = SGU_CHUNK
    tri = lax.broadcasted_iota(jnp.int32, (q, q), 0) >= lax.broadcasted_iota(jnp.int32, (q, q), 1)
    for g in range(SGU_GROUPS):
        cols = slice(g * SGU_GROUP_DIM, (g + 1) * SGU_GROUP_DIM)
        w_m = jnp.where(tri, ws_ref[g], 0.0).astype(BF16)
        sv = _dot(w_m, vn[:, cols]) + bst_ref[:, g:g + 1]
        o_ref[:, cols] = (u_ref[:, cols] * sv).astype(o_ref.dtype)


def _sgu_core(proj, ln_g, w_spatial, b_spatial):
    m = proj.shape[0]
    q = SGU_CHUNK
    return pl.pallas_call(
        _sgu_kernel,
        grid=(m // q,),
        in_specs=[
            pl.BlockSpec((q, SGU_WIDTH), lambda i: (i, 0)),
            pl.BlockSpec((q, SGU_WIDTH), lambda i: (i, 1)),
            pl.BlockSpec((1, SGU_WIDTH), lambda i: (0, 0)),
            pl.BlockSpec((SGU_GROUPS, q, q), lambda i: (0, 0, 0)),
            pl.BlockSpec((q, SGU_GROUPS), lambda i: (0, 0)),
        ],
        out_specs=pl.BlockSpec((q, SGU_WIDTH), lambda i: (i, 0)),
        out_shape=jax.ShapeDtypeStruct((m, SGU_WIDTH), BF16),
        compiler_params=_cparams("parallel"),
        name="sgu_core",
    )(proj, proj, ln_g.reshape(1, SGU_WIDTH), w_spatial, b_spatial.T)


def _pool_kernel(z_ref, zh_ref, wg_ref, sc_ref, o_ref, zz_ref, *, tm, blocks_per_seq):
    i = pl.program_id(0)
    keep = (i % blocks_per_seq != 0).astype(F32)
    zz_ref[0:POOL_HALO, :] = zh_ref[...] * keep
    zz_ref[POOL_HALO:, :] = z_ref[...]
    t = (i % blocks_per_seq) * tm + lax.broadcasted_iota(jnp.int32, (tm, 1), 0)
    for gi, win in enumerate(POOL_WINDOWS):
        cols = slice(gi * POOL_GROUP_DIM, (gi + 1) * POOL_GROUP_DIM)
        s = zz_ref[POOL_HALO:POOL_HALO + tm, cols]
        for k in range(1, win):
            s = s + zz_ref[POOL_HALO - k:POOL_HALO - k + tm, cols]
        count = jnp.minimum(t + 1, win).astype(F32)
        pooled = s / count - z_ref[:, cols]
        y = _dot(pooled.astype(BF16), wg_ref[gi]) * sc_ref[:, cols]
        o_ref[:, cols] = y.astype(o_ref.dtype)


def _pool_core(z, w_group, scale, seq_len, *, tm=512):
    m, n = z.shape
    tm = min(tm, seq_len)
    halo_blocks = tm // POOL_HALO
    return pl.pallas_call(
        functools.partial(_pool_kernel, tm=tm, blocks_per_seq=seq_len // tm),
        grid=(m // tm,),
        in_specs=[
            pl.BlockSpec((tm, n), lambda i: (i, 0)),
            pl.BlockSpec((POOL_HALO, n), lambda i: (jnp.maximum(i * halo_blocks - 1, 0), 0)),
            pl.BlockSpec(w_group.shape, lambda i: (0, 0, 0)),
            pl.BlockSpec((1, n), lambda i: (0, 0)),
        ],
        out_specs=pl.BlockSpec((tm, n), lambda i: (i, 0)),
        out_shape=jax.ShapeDtypeStruct((m, n), BF16),
        scratch_shapes=[pltpu.VMEM((tm + POOL_HALO, n), F32)],
        compiler_params=_cparams("parallel"),
        name="pool_core",
    )(z, z, w_group, scale.reshape(1, n))


def _pad_cols(w, n):
    return jnp.pad(w, ((0, 0), (0, n - w.shape[1])))


def _ssd_layer(x, g_pre, g_post, w_in, conv_w, conv_b, dt_bias, a_log, d_skip, norm_g, w_out, bsz, seq_len):
    w = _pad_cols(w_in, SSD_IN_PAD).astype(BF16)
    proj = _norm_matmul(x, g_pre, w, jnp.zeros((SSD_IN_PAD,), F32))
    y = _ssd_core(proj, conv_w, conv_b, dt_bias, a_log, d_skip, norm_g, bsz, seq_len)
    return _matmul_post(y, w_out.astype(BF16), g_post, x)


def _nsa_layer(x, g_pre, g_post, w_in, cmp_pos, cmp_w1, cmp_w2, w_out, bsz, seq_len):
    w = _pad_cols(w_in, NSA_IN_PAD).astype(BF16)
    proj = _norm_matmul(x, g_pre, w, jnp.zeros((NSA_IN_PAD,), F32))
    kvc = _nsa_compress(proj, cmp_pos, cmp_w1, cmp_w2, bsz, seq_len)
    ocmp, sel = _nsa_select(proj, kvc, bsz, seq_len)
    ngate = 3 * NSA_N_HEADS
    gl = proj[:, NSA_GATE_COL:NSA_GATE_COL + ngate].reshape(-1, NSA_KV_GROUPS, 3 * NSA_HPG).transpose(1, 0, 2)
    o = _nsa_attn(proj, sel, ocmp, gl, bsz, seq_len)
    return _matmul_post(o, w_out.astype(BF16), g_post, x)


def _sgu_layer(x, g_pre, g_post, w_in, b_in, ln_g, w_spatial, b_spatial, w_out):
    proj = _norm_matmul(x, g_pre, w_in.astype(BF16), b_in, act="gelu")
    y = _sgu_core(proj, ln_g, w_spatial, b_spatial)
    return _matmul_post(y, w_out.astype(BF16), g_post, x)


def _pool_layer(x, g_pre, g_post, w_in, w_group, scale, w_out, seq_len):
    z = _norm_matmul(x, g_pre, w_in.astype(BF16), jnp.zeros((D_MODEL,), F32))
    y = _pool_core(z, w_group.astype(BF16), scale, seq_len)
    return _matmul_post(y, w_out.astype(BF16), g_post, x)


def kernel(x, mem, norm_pre, norm_post, norm_mem, ssd_w_in, ssd_conv_w, ssd_conv_b, ssd_dt_bias, ssd_a_log, ssd_d, ssd_norm_g, ssd_w_out, nsa_w_in, nsa_cmp_pos, nsa_cmp_w1, nsa_cmp_w2, nsa_w_out, sgu_w_in, sgu_b_in, sgu_ln_g, sgu_w_spatial, sgu_b_spatial, sgu_w_out, pool_w_in, pool_w_group, pool_scale, pool_w_out, xa_w_q, xa_w_kv, xa_w_o, ffn_w_up, ffn_conv_w, ffn_conv_b, ffn_w_down):
    bsz, seq_len, d = x.shape
    depth = norm_pre.shape[0]
    xf = x.reshape(bsz * seq_len, d)
    memf = mem.reshape(bsz * mem.shape[1], d)
    for i in range(depth):
        kind, j = i % 4, i // 4
        if kind == 0:
            xf = _ssd_layer(xf, norm_pre[i, 0], norm_post[i, 0], ssd_w_in[j], ssd_conv_w[j], ssd_conv_b[j],
                            ssd_dt_bias[j], ssd_a_log[j], ssd_d[j], ssd_norm_g[j], ssd_w_out[j], bsz, seq_len)
        elif kind == 1:
            xf = _nsa_layer(xf, norm_pre[i, 0], norm_post[i, 0], nsa_w_in[j], nsa_cmp_pos[j], nsa_cmp_w1[j],
                            nsa_cmp_w2[j], nsa_w_out[j], bsz, seq_len)
        elif kind == 2:
            xf = _sgu_layer(xf, norm_pre[i, 0], norm_post[i, 0], sgu_w_in[j], sgu_b_in[j], sgu_ln_g[j],
                            sgu_w_spatial[j], sgu_b_spatial[j], sgu_w_out[j])
        else:
            xf = _pool_layer(xf, norm_pre[i, 0], norm_post[i, 0], pool_w_in[j], pool_w_group[j],
                             pool_scale[j], pool_w_out[j], seq_len)
        kv = _norm_matmul(memf, norm_mem[i], xa_w_kv[i].astype(BF16), jnp.zeros((2 * XA_DIM,), F32),
                          out_dtype=BF16, tm=MEM_LEN)
        xf = _xattn(xf, kv, norm_pre[i, 1], norm_post[i, 1], xa_w_q[i].astype(BF16),
                    xa_w_o[i].astype(BF16), seq_len)
        act = _ffn_up(xf, norm_pre[i, 2], ffn_w_up[i].astype(BF16), ffn_conv_w[i], ffn_conv_b[i], seq_len)
        xf = _matmul_post(act, ffn_w_down[i].astype(BF16), norm_post[i, 2], xf)
    return xf.reshape(bsz, seq_len, d)
```

```python
import functools
import math

import jax
import jax.numpy as jnp
import numpy as np
from jax import lax
from jax.experimental import pallas as pl
from jax.experimental.pallas import tpu as pltpu

F32 = jnp.float32
BF16 = jnp.bfloat16

D_MODEL = 2048
RMS_EPS = 1e-6
NEG = -1e30
BIG = 1e30
MEM_LEN = 256

SSD_D_INNER = 4096
SSD_HEAD_DIM = 64
SSD_N_HEADS = 64
SSD_N_GROUPS = 8
SSD_HPG = 8
SSD_D_STATE = 128
SSD_CONV = 4
SSD_CHUNK = 128
SSD_GROUP_W = SSD_HPG * SSD_HEAD_DIM
SSD_BC_DIM = SSD_N_GROUPS * SSD_D_STATE
SSD_IN_PAD = 10752

NSA_HEAD_DIM = 128
NSA_N_HEADS = 16
NSA_KV_GROUPS = 4
NSA_HPG = 4
NSA_KV_DIM = 512
NSA_CMP_BLOCK = 32
NSA_CMP_STRIDE = 16
NSA_CMP_HIDDEN = 256
NSA_SLC_BLOCK = 64
NSA_TOPK = 16
NSA_N_LOCAL = 2
NSA_WINDOW = 512
NSA_TQ = 256
NSA_KEY_TILE = 1024
NSA_SEL_TQ = 256
NSA_IN_PAD = 5632
NSA_GATE_COL = 5120

SGU_CHUNK = 128
SGU_WIDTH = 4096
SGU_GROUPS = 8
SGU_GROUP_DIM = 512

POOL_WINDOWS = (2, 4, 8, 16)
POOL_GROUP_DIM = 512
POOL_HALO = 16

XA_HEADS = 4
XA_HEAD_DIM = 128
XA_DIM = 512

FFN_HIDDEN = 5632
FFN_CONV = 3
FFN_HALO = 16

LANE = 128
VMEM_LIMIT = 56 * 1024 * 1024


def _cparams(*sem):
    return pltpu.CompilerParams(dimension_semantics=sem, vmem_limit_bytes=VMEM_LIMIT)


def _dot(a, b):
    return jnp.dot(a, b, preferred_element_type=F32)


def _dot_nt(a, b):
    return lax.dot_general(a, b, (((1,), (1,)), ((), ())), preferred_element_type=F32)


def _split3(v):
    hi = v.astype(BF16)
    r1 = v - hi.astype(F32)
    mid = r1.astype(BF16)
    lo = (r1 - mid.astype(F32)).astype(BF16)
    return hi, mid, lo


def _dot_f32_lhs(v, e):
    hi, mid, lo = _split3(v)
    return _dot(hi, e) + _dot(mid, e) + _dot(lo, e)


def _dot_2piece_lhs(v, e2):
    hi = v.astype(BF16)
    mid = (v - hi.astype(F32)).astype(BF16)
    return _dot(jnp.concatenate([hi, mid], axis=1), e2)


def _dot_f32_rhs(e, v):
    hi, mid, lo = _split3(v)
    return _dot(e, hi) + _dot(e, mid) + _dot(e, lo)


def _rms(x, g):
    ms = jnp.mean(x * x, axis=-1, keepdims=True)
    return x * lax.rsqrt(ms + RMS_EPS) * g


def _sigmoid(x):
    return 1.0 / (1.0 + jnp.exp(-x))


def _silu(x):
    return x * _sigmoid(x)


def _gelu(x):
    c = math.sqrt(2.0 / math.pi)
    return x * (0.5 * (1.0 + jnp.tanh(c * (x + 0.044715 * (x * x * x)))))


def _softplus(x):
    return jnp.maximum(x, 0.0) + jnp.log1p(jnp.exp(-jnp.abs(x)))


def _norm_mm_kernel(x_ref, g_ref, w_ref, b_ref, o_ref, h_ref, *, act):
    @pl.when(pl.program_id(1) == 0)
    def _():
        h_ref[...] = _rms(x_ref[...], g_ref[...]).astype(BF16)

    y = _dot(h_ref[...], w_ref[...]) + b_ref[...]
    if act == "gelu":
        y = _gelu(y)
    o_ref[...] = y.astype(o_ref.dtype)


def _norm_matmul(x, g, w, li, b, *, act=None, out_dtype=F32, tm=1024, tn=512):
    m, k = x.shape
    n = w.shape[2]
    tm = min(tm, m)
    assert m % tm == 0 and n % tn == 0, (m, n, tm, tn)
    return pl.pallas_call(
        functools.partial(_norm_mm_kernel, act=act),
        grid=(m // tm, n // tn),
        in_specs=[
            pl.BlockSpec((tm, k), lambda i, j: (i, 0)),
            pl.BlockSpec((1, k), lambda i, j: (0, 0)),
            pl.BlockSpec((None, k, tn), lambda i, j: (li, 0, j)),
            pl.BlockSpec((1, tn), lambda i, j: (0, j)),
        ],
        out_specs=pl.BlockSpec((tm, tn), lambda i, j: (i, j)),
        out_shape=jax.ShapeDtypeStruct((m, n), out_dtype),
        scratch_shapes=[pltpu.VMEM((tm, k), BF16)],
        compiler_params=_cparams("parallel", "arbitrary"),
        name="norm_matmul",
    )(x, g.reshape(1, k), w, b.reshape(1, n))


def _mm_post_kernel(a_ref, w_ref, g_ref, r_ref, o_ref, *, nj, tn):
    j = pl.program_id(1)
    o_ref[:, pl.ds(pl.multiple_of(j * tn, tn), tn)] = _dot(a_ref[...], w_ref[...])

    @pl.when(j == nj - 1)
    def _():
        o_ref[...] = r_ref[...] + _rms(o_ref[...], g_ref[...])


def _matmul_post(a, w, li, g, res, *, tm=512, tn=512):
    m, kdim = a.shape
    n = w.shape[2]
    tm = min(tm, m)
    assert m % tm == 0 and n % tn == 0
    nj = n // tn
    return pl.pallas_call(
        functools.partial(_mm_post_kernel, nj=nj, tn=tn),
        grid=(m // tm, nj),
        in_specs=[
            pl.BlockSpec((tm, kdim), lambda i, j: (i, 0)),
            pl.BlockSpec((None, kdim, tn), lambda i, j: (li, 0, j)),
            pl.BlockSpec((1, n), lambda i, j: (0, 0)),
            pl.BlockSpec((tm, n), lambda i, j: (i, 0)),
        ],
        out_specs=pl.BlockSpec((tm, n), lambda i, j: (i, 0)),
        out_shape=jax.ShapeDtypeStruct((m, n), F32),
        compiler_params=_cparams("parallel", "arbitrary"),
        name="matmul_postnorm_residual",
    )(a, w, g.reshape(1, n), res)


def _ffn_up_kernel(x_ref, xh_ref, g_ref, wg_ref, wv_ref, cwg_ref, cwv_ref, cbg_ref, cbv_ref,
                   o_ref, h_ref, *, tm, blocks_per_seq):
    i = pl.program_id(0)

    @pl.when(pl.program_id(1) == 0)
    def _():
        keep = (i % blocks_per_seq != 0).astype(F32)
        h_ref[0:FFN_HALO, :] = (_rms(xh_ref[...], g_ref[...]) * keep).astype(BF16)
        h_ref[FFN_HALO:, :] = _rms(x_ref[...], g_ref[...]).astype(BF16)

    h = h_ref[...]

    def conv(u, cw_ref, cb_ref):
        y = cb_ref[...] + u[FFN_HALO:] * cw_ref[FFN_CONV - 1:FFN_CONV, :]
        for tap in range(FFN_CONV - 1):
            back = FFN_CONV - 1 - tap
            y = y + pltpu.roll(u, back, axis=0)[FFN_HALO:] * cw_ref[tap:tap + 1, :]
        return y

    gate = conv(_dot(h, wg_ref[...]), cwg_ref, cbg_ref)
    val = conv(_dot(h, wv_ref[...]), cwv_ref, cbv_ref)
    o_ref[...] = (_silu(gate) * val).astype(o_ref.dtype)


def _ffn_up(x, g, w_up, li, conv_w, conv_b, seq_len, *, tm=1024, tn=512):
    m, k = x.shape
    tm = min(tm, seq_len)
    assert m % tm == 0 and seq_len % tm == 0 and FFN_HIDDEN % tn == 0
    nj = FFN_HIDDEN // tn
    halo_blocks = tm // FFN_HALO
    return pl.pallas_call(
        functools.partial(_ffn_up_kernel, tm=tm, blocks_per_seq=seq_len // tm),
        grid=(m // tm, nj),
        in_specs=[
            pl.BlockSpec((tm, k), lambda i, j: (i, 0)),
            pl.BlockSpec((FFN_HALO, k), lambda i, j: (jnp.maximum(i * halo_blocks - 1, 0), 0)),
            pl.BlockSpec((1, k), lambda i, j: (0, 0)),
            pl.BlockSpec((None, k, tn), lambda i, j: (li, 0, j)),
            pl.BlockSpec((None, k, tn), lambda i, j: (li, 0, j + nj)),
            pl.BlockSpec((FFN_CONV, tn), lambda i, j: (0, j)),
            pl.BlockSpec((FFN_CONV, tn), lambda i, j: (0, j + nj)),
            pl.BlockSpec((1, tn), lambda i, j: (0, j)),
            pl.BlockSpec((1, tn), lambda i, j: (0, j + nj)),
        ],
        out_specs=pl.BlockSpec((tm, tn), lambda i, j: (i, j)),
        out_shape=jax.ShapeDtypeStruct((m, FFN_HIDDEN), BF16),
        scratch_shapes=[pltpu.VMEM((tm + FFN_HALO, k), BF16)],
        compiler_params=_cparams("parallel", "arbitrary"),
        name="ffn_up_conv_gate",
    )(x, x, g.reshape(1, k), w_up, w_up, conv_w, conv_w,
      conv_b.reshape(1, -1), conv_b.reshape(1, -1))


def _xattn_kernel(x_ref, kv_ref, gpre_ref, gpost_ref, wq_ref, wo_ref, o_ref, att_ref):
    x = x_ref[...]
    h = _rms(x, gpre_ref[...]).astype(BF16)
    q = _dot(h, wq_ref[...]).astype(BF16)
    scale = XA_HEAD_DIM ** -0.5
    for hd in range(XA_HEADS):
        lo = hd * XA_HEAD_DIM
        s = _dot_nt(q[:, lo:lo + XA_HEAD_DIM], kv_ref[:, lo:lo + XA_HEAD_DIM]) * scale
        e = jnp.exp(s - jnp.max(s, axis=-1, keepdims=True))
        p = e / jnp.sum(e, axis=-1, keepdims=True)
        att_ref[:, lo:lo + XA_HEAD_DIM] = _dot(
            p.astype(BF16), kv_ref[:, XA_DIM + lo:XA_DIM + lo + XA_HEAD_DIM]).astype(BF16)
    a = _dot(att_ref[...], wo_ref[...])
    o_ref[...] = x + _rms(a, gpost_ref[...])


def _xattn(x, kv, g_pre, g_post, w_q, w_o, li, seq_len, *, tm=512):
    m, k = x.shape
    tm = min(tm, seq_len)
    bps = seq_len // tm
    return pl.pallas_call(
        _xattn_kernel,
        grid=(m // tm,),
        in_specs=[
            pl.BlockSpec((tm, k), lambda i: (i, 0)),
            pl.BlockSpec((MEM_LEN, 2 * XA_DIM), lambda i: (i // bps, 0)),
            pl.BlockSpec((1, k), lambda i: (0, 0)),
            pl.BlockSpec((1, k), lambda i: (0, 0)),
            pl.BlockSpec((None, k, XA_DIM), lambda i: (li, 0, 0)),
            pl.BlockSpec((None, XA_DIM, k), lambda i: (li, 0, 0)),
        ],
        out_specs=pl.BlockSpec((tm, k), lambda i: (i, 0)),
        out_shape=jax.ShapeDtypeStruct((m, k), F32),
        scratch_shapes=[pltpu.VMEM((tm, XA_DIM), BF16)],
        compiler_params=_cparams("parallel"),
        name="memory_cross_attention",
    )(x, kv, g_pre.reshape(1, k), g_post.reshape(1, k), w_q, w_o)


def _ssd_kernel(z_ref, xr_ref, br_ref, cr_ref, dtr_ref, cw_ref, cbias_ref, dtb_ref, alog_ref,
                dskip_ref, ng_ref, expand_ref, tril_ref, o_ref,
                xin_ref, xs_ref, bm_ref, cm_ref, wx_ref, expax_ref, cdx_ref,
                acsg_ref, acst_ref, state_ref, y_ref):
    q = SSD_CHUNK
    c = pl.program_id(1)
    halo = 8
    xw = SSD_D_INNER

    @pl.when(c == 0)
    def _():
        xin_ref[0:halo, :] = jnp.zeros((halo, xin_ref.shape[1]), F32)
        state_ref[...] = jnp.zeros(state_ref.shape, F32)

    @pl.when(c > 0)
    def _():
        xin_ref[0:halo, :] = xin_ref[q:q + halo, :]

    xin_ref[halo:, 0:xw] = xr_ref[...]
    xin_ref[halo:, xw:xw + SSD_BC_DIM] = br_ref[...]
    xin_ref[halo:, xw + SSD_BC_DIM:] = cr_ref[...]

    def conv_tile(lo, width):
        slab = xin_ref[:, pl.ds(lo, width)]
        y = cbias_ref[:, pl.ds(lo, width)] + slab[halo:] * cw_ref[SSD_CONV - 1:SSD_CONV, pl.ds(lo, width)]
        for tap in range(SSD_CONV - 1):
            back = SSD_CONV - 1 - tap
            y = y + pltpu.roll(slab, back, axis=0)[halo:] * cw_ref[tap:tap + 1, pl.ds(lo, width)]
        return _silu(y)

    def conv_x(t, carry):
        lo = pl.multiple_of(t * 512, 512)
        xs_ref[:, pl.ds(lo, 512)] = conv_tile(lo, 512)
        return carry

    lax.fori_loop(0, xw // 512, conv_x, 0)
    for t in range(SSD_BC_DIM // 512):
        bm_ref[:, t * 512:(t + 1) * 512] = conv_tile(xw + t * 512, 512)
        cm_ref[:, t * 512:(t + 1) * 512] = conv_tile(xw + SSD_BC_DIM + t * 512, 512)

    dt = _softplus(dtr_ref[...] + dtb_ref[...])
    a = dt * (-jnp.exp(alog_ref[...]))
    a_cs = _dot_f32_rhs(tril_ref[...], a)
    a_last = a_cs[q - 1:q, :]
    expand = expand_ref[...]
    wx_ref[...] = _dot_2piece_lhs(dt * jnp.exp(a_last - a_cs), expand)
    expax_ref[...] = _dot_2piece_lhs(jnp.exp(a_cs), expand)
    cdx_ref[...] = _dot_2piece_lhs(jnp.broadcast_to(jnp.exp(a_last), (8, LANE)), expand)
    acst_ref[...] = (a_cs - jnp.log(dt)).T.reshape(LANE // SSD_HPG, SSD_HPG, q)
    for g in range(SSD_N_GROUPS):
        acsg_ref[g] = a_cs if g == 0 else pltpu.roll(a_cs, LANE - g * SSD_HPG, axis=1)

    row = lax.broadcasted_iota(jnp.int32, (q, q), 0)
    col = lax.broadcasted_iota(jnp.int32, (q, q), 1)
    causal = row >= col
    left_half = col < SSD_HEAD_DIM

    def group_body(g, carry):
        off = pl.multiple_of(g * SSD_GROUP_W, SSD_GROUP_W)
        offn = pl.multiple_of(g * SSD_D_STATE, SSD_D_STATE)
        cm = cm_ref[:, pl.ds(offn, SSD_D_STATE)]
        bm = bm_ref[:, pl.ds(offn, SSD_D_STATE)]
        cmb = cm.astype(BF16)
        cb = _dot_nt(cmb, bm.astype(BF16))
        xs = xs_ref[:, pl.ds(off, SSD_GROUP_W)]
        st = state_ref[g]
        y_off = _dot(cmb, st.astype(BF16)) * expax_ref[:, pl.ds(off, SSD_GROUP_W)]
        s_new = _dot(bm.T.astype(BF16), (xs * wx_ref[:, pl.ds(off, SSD_GROUP_W)]).astype(BF16))
        state_ref[g] = st * cdx_ref[0:1, pl.ds(off, SSD_GROUP_W)] + s_new
        acs_g = acsg_ref[g]
        acst_g = acst_ref[g]
        skip = xs * dskip_ref[:, pl.ds(off, SSD_GROUP_W)]
        for pr in range(SSD_HPG // 2):
            scs = []
            for e in range(2):
                j = 2 * pr + e
                diff = acs_g[:, j:j + 1] - acst_g[j:j + 1, :]
                dec = jnp.where(causal, jnp.exp(diff), 0.0)
                scs.append((cb * dec).astype(BF16))
            xp = xs[:, pr * LANE:(pr + 1) * LANE]
            x0 = jnp.where(left_half, xp, 0.0).astype(BF16)
            x1 = jnp.where(left_half, 0.0, xp).astype(BF16)
            y_diag = _dot(jnp.concatenate(scs, axis=1), jnp.concatenate([x0, x1], axis=0))
            y = y_diag + y_off[:, pr * LANE:(pr + 1) * LANE] + skip[:, pr * LANE:(pr + 1) * LANE]
            y_ref[:, pl.ds(pl.multiple_of(off + pr * LANE, LANE), LANE)] = y
        return carry

    lax.fori_loop(0, SSD_N_GROUPS, group_body, 0, unroll=8)

    gated = y_ref[...] * _silu(z_ref[...])
    o_ref[...] = _rms(gated, ng_ref[...]).astype(o_ref.dtype)


def _ssd_core(proj, conv_w, conv_b, dt_bias, a_log, d_skip, norm_g, bsz, seq_len):
    q = SSD_CHUNK
    nc = seq_len // q
    m = bsz * seq_len
    conv_dim = SSD_D_INNER + 2 * SSD_BC_DIM
    pad = LANE - SSD_N_HEADS
    dtb = jnp.pad(dt_bias, (0, pad)).reshape(1, LANE)
    alog = jnp.pad(a_log, (0, pad)).reshape(1, LANE)
    dskip = jnp.repeat(d_skip, SSD_HEAD_DIM).reshape(1, SSD_D_INNER)
    heads = np.arange(2 * LANE)[:, None] % LANE
    chans = np.arange(SSD_D_INNER)[None, :] // SSD_HEAD_DIM
    expand = jnp.asarray((heads == chans).astype(np.float32), dtype=BF16)
    tril = jnp.asarray(np.tril(np.ones((q, q), np.float32)), dtype=BF16)
    row = lambda b, c: b * nc + c
    const = lambda b, c: (0, 0)
    return pl.pallas_call(
        _ssd_kernel,
        grid=(bsz, nc),
        in_specs=[
            pl.BlockSpec((q, SSD_D_INNER), lambda b, c: (row(b, c), 0)),
            pl.BlockSpec((q, SSD_D_INNER), lambda b, c: (row(b, c), 1)),
            pl.BlockSpec((q, SSD_BC_DIM), lambda b, c: (row(b, c), 8)),
            pl.BlockSpec((q, SSD_BC_DIM), lambda b, c: (row(b, c), 9)),
            pl.BlockSpec((q, LANE), lambda b, c: (row(b, c), 80)),
            pl.BlockSpec((SSD_CONV, conv_dim), const),
            pl.BlockSpec((1, conv_dim), const),
            pl.BlockSpec((1, LANE), const),
            pl.BlockSpec((1, LANE), const),
            pl.BlockSpec((1, SSD_D_INNER), const),
            pl.BlockSpec((1, SSD_D_INNER), const),
            pl.BlockSpec((2 * LANE, SSD_D_INNER), const),
            pl.BlockSpec((q, q), const),
        ],
        out_specs=pl.BlockSpec((q, SSD_D_INNER), lambda b, c: (row(b, c), 0)),
        out_shape=jax.ShapeDtypeStruct((m, SSD_D_INNER), BF16),
        scratch_shapes=[
            pltpu.VMEM((q + 8, conv_dim), F32),
            pltpu.VMEM((q, SSD_D_INNER), F32),
            pltpu.VMEM((q, SSD_BC_DIM), F32),
            pltpu.VMEM((q, SSD_BC_DIM), F32),
            pltpu.VMEM((q, SSD_D_INNER), F32),
            pltpu.VMEM((q, SSD_D_INNER), F32),
            pltpu.VMEM((8, SSD_D_INNER), F32),
            pltpu.VMEM((SSD_N_GROUPS, q, LANE), F32),
            pltpu.VMEM((LANE // SSD_HPG, SSD_HPG, q), F32),
            pltpu.VMEM((SSD_N_GROUPS, SSD_D_STATE, SSD_GROUP_W), F32),
            pltpu.VMEM((q, SSD_D_INNER), F32),
        ],
        compiler_params=_cparams("parallel", "arbitrary"),
        name="ssd_core",
    )(proj, proj, proj, proj, proj, conv_w, conv_b.reshape(1, conv_dim), dtb, alog, dskip,
      norm_g.reshape(1, SSD_D_INNER), expand, tril)


def _nsa_compress_kernel(t_ref, pos_ref, w1_ref, w2_ref, o_ref, *, nchunk):
    half = NSA_CMP_STRIDE
    dh = NSA_HEAD_DIM
    p_lo = jnp.zeros((nchunk, NSA_CMP_HIDDEN), F32)
    p_hi = jnp.zeros((nchunk, NSA_CMP_HIDDEN), F32)
    for l in range(half):
        rows = t_ref[pl.ds(l, nchunk, stride=half), :]
        a_lo = (rows + pos_ref[l:l + 1, :]).astype(BF16)
        a_hi = (rows + pos_ref[half + l:half + l + 1, :]).astype(BF16)
        p_lo = p_lo + _dot(a_lo, w1_ref[l * dh:(l + 1) * dh, :].astype(BF16))
        p_hi = p_hi + _dot(a_hi, w1_ref[(half + l) * dh:(half + l + 1) * dh, :].astype(BF16))
    pre = p_lo + pltpu.roll(p_hi, nchunk - 1, axis=0)
    o_ref[...] = _dot(_gelu(pre).astype(BF16), w2_ref[...].astype(BF16))


def _nsa_compress(proj, cmp_pos, cmp_w1, cmp_w2, bsz, seq_len):
    nchunk = seq_len // NSA_CMP_STRIDE
    g, dh = NSA_KV_GROUPS, NSA_HEAD_DIM
    col0 = NSA_N_HEADS * dh // dh
    return pl.pallas_call(
        functools.partial(_nsa_compress_kernel, nchunk=nchunk),
        grid=(bsz, 2, g),
        in_specs=[
            pl.BlockSpec((seq_len, dh), lambda b, s, gi: (b, col0 + s * g + gi)),
            pl.BlockSpec((None, NSA_CMP_BLOCK, dh), lambda b, s, gi: (s, 0, 0)),
            pl.BlockSpec((None, NSA_CMP_BLOCK * dh, NSA_CMP_HIDDEN), lambda b, s, gi: (s, 0, 0)),
            pl.BlockSpec((None, NSA_CMP_HIDDEN, dh), lambda b, s, gi: (s, 0, 0)),
        ],
        out_specs=pl.BlockSpec((None, None, None, nchunk, dh), lambda b, s, gi: (b, s, gi, 0, 0)),
        out_shape=jax.ShapeDtypeStruct((bsz, 2, g, nchunk, dh), F32),
        compiler_params=_cparams("parallel", "parallel", "parallel"),
        name="nsa_compress",
    )(proj, cmp_pos, cmp_w1, cmp_w2)


def _nsa_select_kernel(q_ref, kc_ref, vc_ref, cover_ref, ocmp_ref, sel_ref, *, ncmp, nslc):
    tq = NSA_SEL_TQ
    dh = NSA_HEAD_DIM
    scale = dh ** -0.5
    nck = kc_ref.shape[0]
    t0 = pl.program_id(2) * tq
    t_col = t0 + lax.broadcasted_iota(jnp.int32, (tq, 1), 0)
    ci_row = lax.broadcasted_iota(jnp.int32, (1, nck), 1)
    valid = (ci_row * NSA_CMP_STRIDE + (NSA_CMP_BLOCK - 1) <= t_col) & (ci_row < ncmp)
    t_row = t0 + lax.broadcasted_iota(jnp.int32, (1, tq), 1)
    ci_col = lax.broadcasted_iota(jnp.int32, (nck, 1), 0)
    valid_t = (ci_col * NSA_CMP_STRIDE + (NSA_CMP_BLOCK - 1) <= t_row) & (ci_col < ncmp)
    kcb = kc_ref[...].astype(BF16)
    vcb = vc_ref[...].astype(BF16)
    psum_t = jnp.zeros((nck, tq), F32)
    for j in range(NSA_HPG):
        qj = q_ref[:, j * dh:(j + 1) * dh].astype(BF16)
        s = jnp.where(valid, _dot_nt(qj, kcb) * scale, NEG)
        e = jnp.exp(s - jnp.max(s, axis=-1, keepdims=True))
        p = jnp.where(valid, e / jnp.sum(e, axis=-1, keepdims=True), 0.0)
        ocmp_ref[:, j * dh:(j + 1) * dh] = _dot(p.astype(BF16), vcb)
        st = jnp.where(valid_t, _dot_nt(kcb, qj) * scale, NEG)
        et = jnp.exp(st - jnp.max(st, axis=0, keepdims=True))
        psum_t = psum_t + jnp.where(valid_t, et / jnp.sum(et, axis=0, keepdims=True), 0.0)
    imp_t = _dot_f32_rhs(cover_ref[...], psum_t)
    blk = lax.broadcasted_iota(jnp.int32, (nslc, tq), 0)
    cur = t_row // NSA_SLC_BLOCK
    forced = (blk == 0) | ((blk <= cur) & (blk > cur - 1 - NSA_N_LOCAL))
    future = blk * NSA_SLC_BLOCK > t_row
    score = jnp.where(forced, BIG, jnp.where(future, NEG, imp_t))
    sub = 8
    slabs = [score[r * sub:(r + 1) * sub, :] for r in range(nslc // sub)]
    ranks = [jnp.zeros((sub, tq), F32) for _ in slabs]
    row_in_slab = lax.broadcasted_iota(jnp.int32, (sub, tq), 0)
    for j in range(nslc):
        cj = jnp.broadcast_to(score[j:j + 1, :], (sub, tq))
        for r, s_r in enumerate(slabs):
            if r * sub > j:
                before = cj >= s_r
            elif (r + 1) * sub - 1 < j:
                before = cj > s_r
            else:
                before = (cj > s_r) | ((cj == s_r) & (row_in_slab > j - r * sub))
            ranks[r] = ranks[r] + jnp.where(before, 1.0, 0.0)
    topk = float(min(NSA_TOPK, nslc))
    for r, rank in enumerate(ranks):
        sel_ref[r * sub:(r + 1) * sub, :] = jnp.where(rank < topk, 1.0, 0.0)


def _nsa_select(proj, kvc, bsz, seq_len):
    tq, dh, g = NSA_SEL_TQ, NSA_HEAD_DIM, NSA_KV_GROUPS
    nq = seq_len // tq
    ncmp = (seq_len - NSA_CMP_BLOCK) // NSA_CMP_STRIDE + 1
    nck = seq_len // NSA_CMP_STRIDE
    nslc = seq_len // NSA_SLC_BLOCK
    ci = np.arange(nck)[:, None] * NSA_CMP_STRIDE
    sj = np.arange(nslc)[None, :] * NSA_SLC_BLOCK
    cover = ((ci <= sj + NSA_SLC_BLOCK - 1) & (ci + NSA_CMP_BLOCK - 1 >= sj)).astype(np.float32)
    cover = jnp.asarray(cover.T, dtype=BF16)
    return pl.pallas_call(
        functools.partial(_nsa_select_kernel, ncmp=ncmp, nslc=nslc),
        grid=(bsz, g, nq),
        in_specs=[
            pl.BlockSpec((tq, NSA_HPG * dh), lambda b, gi, qi: (b * nq + qi, gi)),
            pl.BlockSpec((None, None, None, nck, dh), lambda b, gi, qi: (b, 0, gi, 0, 0)),
            pl.BlockSpec((None, None, None, nck, dh), lambda b, gi, qi: (b, 1, gi, 0, 0)),
            pl.BlockSpec((nslc, nck), lambda b, gi, qi: (0, 0)),
        ],
        out_specs=[
            pl.BlockSpec((tq, NSA_HPG * dh), lambda b, gi, qi: (b * nq + qi, gi)),
            pl.BlockSpec((None, None, nslc, tq), lambda b, gi, qi: (b, gi, 0, qi)),
        ],
        out_shape=[
            jax.ShapeDtypeStruct((bsz * seq_len, NSA_N_HEADS * dh), F32),
            jax.ShapeDtypeStruct((bsz, g, nslc, seq_len), F32),
        ],
        compiler_params=_cparams("parallel", "parallel", "parallel"),
        name="nsa_cmp_select",
    )(proj, kvc, kvc, cover)


def _nsa_attn_kernel(q_ref, ks_ref, vs_ref, kw_ref, vw_ref, sel_ref, expand_ref, ocmp_ref, gl_ref,
                     o_ref, ksb_ref, vsx_ref, kwb_ref, vwx_ref, bias_ref, q4_ref, pw_ref, m_ref, acc_ref):
    tq = NSA_TQ
    dh = NSA_HEAD_DIM
    kt_w = NSA_KEY_TILE
    win_w = NSA_WINDOW + tq
    c = (dh ** -0.5) * math.log2(math.e)
    qi = pl.program_id(2)

    @pl.when(qi == 0)
    def _():
        ones = jnp.ones((ks_ref.shape[0], dh), BF16)
        for blk in range(ks_ref.shape[0] // LANE):
            cols = slice(blk * LANE, (blk + 1) * LANE)
            ksb_ref[:, cols] = ks_ref[cols, :].T.astype(BF16)
            kwb_ref[:, cols] = kw_ref[cols, :].T.astype(BF16)
        vsx_ref[:, 0:dh] = vs_ref[...].astype(BF16)
        vsx_ref[:, dh:] = ones
        vwx_ref[:, 0:dh] = vw_ref[...].astype(BF16)
        vwx_ref[:, dh:] = ones

    bias_ref[...] = (_dot(sel_ref[...].T.astype(BF16), expand_ref[...]) - 1.0) * BIG
    for j in range(NSA_HPG):
        q4_ref[j * tq:(j + 1) * tq, :] = q_ref[:, j * dh:(j + 1) * dh].astype(BF16)
    t1 = qi * tq + lax.broadcasted_iota(jnp.int32, (tq, 1), 0)
    m_ref[...] = jnp.full(m_ref.shape, NEG, F32)
    acc_ref[...] = jnp.zeros(acc_ref.shape, F32)
    half = NSA_HPG // 2 * tq

    def slc_tile(kt, carry):
        off = pl.multiple_of(kt * kt_w, kt_w)
        pos = off + lax.broadcasted_iota(jnp.int32, (1, kt_w), 1)
        bias = jnp.where(pos <= t1, bias_ref[:, pl.ds(off, kt_w)], NEG)
        k_t = ksb_ref[:, pl.ds(off, kt_w)]
        v_x = vsx_ref[pl.ds(off, kt_w), :]
        pairs = [slice(hp * half, (hp + 1) * half) for hp in range(2)]
        scores = [_dot(q4_ref[pr], k_t) for pr in pairs]
        for hp, pr in enumerate(pairs):
            m_prev = m_ref[pr]
            ps, m_news = [], []
            for e in range(NSA_HPG // 2):
                sj = scores[hp][e * tq:(e + 1) * tq] + bias
                m_new = jnp.maximum(m_prev[e * tq:(e + 1) * tq], jnp.max(sj, axis=-1, keepdims=True))
                m_wide = jnp.concatenate([m_new] * (kt_w // LANE), axis=1)
                ps.append(jnp.exp2((sj - m_wide) * c).astype(BF16))
                m_news.append(m_new)
            m_new = jnp.concatenate(m_news, axis=0)
            alpha = jnp.exp2((m_prev - m_new) * c)
            m_ref[pr] = m_new
            pv = _dot(jnp.concatenate(ps, axis=0), v_x)
            acc_ref[pr] = jnp.concatenate([alpha, alpha], axis=1) * acc_ref[pr] + pv
        return carry

    lax.fori_loop(0, ((qi + 1) * tq + kt_w - 1) // kt_w, slc_tile, 0)

    w0 = pl.multiple_of(jnp.maximum(qi * tq - NSA_WINDOW, 0), tq)
    sw = _dot(q4_ref[...], kwb_ref[:, pl.ds(w0, win_w)])
    diff = t1 - (w0 + lax.broadcasted_iota(jnp.int32, (1, win_w), 1))
    bias_w = jnp.where((diff >= 0) & (diff < NSA_WINDOW), 0.0, NEG)
    for j in range(NSA_HPG):
        rows = slice(j * tq, (j + 1) * tq)
        sj = sw[rows] + bias_w
        pw_ref[rows] = jnp.exp2((sj - jnp.max(sj, axis=-1, keepdims=True)) * c).astype(BF16)
    ow = _dot(pw_ref[...], vwx_ref[pl.ds(w0, win_w), :])

    gates = _sigmoid(gl_ref[...])
    for j in range(NSA_HPG):
        rows = slice(j * tq, (j + 1) * tq)
        o_slc = acc_ref[rows, 0:dh] / acc_ref[rows, dh:]
        o_win = ow[rows, 0:dh] / ow[rows, dh:]
        o = (gates[:, 3 * j:3 * j + 1] * ocmp_ref[:, j * dh:(j + 1) * dh]
             + gates[:, 3 * j + 1:3 * j + 2] * o_slc
             + gates[:, 3 * j + 2:3 * j + 3] * o_win)
        o_ref[:, j * dh:(j + 1) * dh] = o.astype(o_ref.dtype)


def _nsa_attn(proj, sel, ocmp, gate_logits, bsz, seq_len):
    tq, dh, g = NSA_TQ, NSA_HEAD_DIM, NSA_KV_GROUPS
    assert seq_len % NSA_KEY_TILE == 0 and seq_len >= NSA_WINDOW + tq
    rows = NSA_HPG * tq
    nq = seq_len // tq
    nslc = seq_len // NSA_SLC_BLOCK
    expand = (np.arange(nslc)[:, None] == np.arange(seq_len)[None, :] // NSA_SLC_BLOCK)
    expand = jnp.asarray(expand.astype(np.float32), dtype=BF16)
    kv0 = (NSA_N_HEADS * dh + 2 * NSA_KV_DIM) // dh

    def kv_spec(idx):
        return pl.BlockSpec((seq_len, dh), lambda b, gi, qi: (b, kv0 + idx * g + gi))

    qo_spec = pl.BlockSpec((tq, NSA_HPG * dh), lambda b, gi, qi: (b * nq + qi, gi))
    return pl.pallas_call(
        _nsa_attn_kernel,
        grid=(bsz, g, nq),
        in_specs=[
            qo_spec,
            kv_spec(0), kv_spec(1), kv_spec(2), kv_spec(3),
            pl.BlockSpec((None, None, nslc, tq), lambda b, gi, qi: (b, gi, 0, qi)),
            pl.BlockSpec((nslc, seq_len), lambda b, gi, qi: (0, 0)),
            qo_spec,
            pl.BlockSpec((None, tq, 3 * NSA_HPG), lambda b, gi, qi: (gi, b * nq + qi, 0)),
        ],
        out_specs=qo_spec,
        out_shape=jax.ShapeDtypeStruct((bsz * seq_len, NSA_N_HEADS * dh), BF16),
        scratch_shapes=[
            pltpu.VMEM((dh, seq_len), BF16),
            pltpu.VMEM((seq_len, 2 * dh), BF16),
            pltpu.VMEM((dh, seq_len), BF16),
            pltpu.VMEM((seq_len, 2 * dh), BF16),
            pltpu.VMEM((tq, seq_len), F32),
            pltpu.VMEM((rows, dh), BF16),
            pltpu.VMEM((rows, NSA_WINDOW + tq), BF16),
            pltpu.VMEM((rows, LANE), F32),
            pltpu.VMEM((rows, 2 * dh), F32),
        ],
        compiler_params=_cparams("parallel", "parallel", "arbitrary"),
        name="nsa_slc_win_attention",
    )(proj, proj, proj, proj, proj, sel, expand, ocmp, gate_logits)


def _sgu_kernel(u_ref, v_ref, g_ref, ws_ref, bst_ref, o_ref):
    v = v_ref[...].astype(F32)
    mu = jnp.mean(v, axis=-1, keepdims=True)
    vc = v - mu
    vn = (vc * lax.rsqrt(jnp.mean(vc * vc, axis=-1, keepdims=True) + RMS_EPS) * g_ref[...]).astype(BF16)
    q = SGU_CHUNK
    tri = lax.broadcasted_iota(jnp.int32, (q, q), 0) >= lax.broadcasted_iota(jnp.int32, (q, q), 1)
    for g in range(SGU_GROUPS):
        cols = slice(g * SGU_GROUP_DIM, (g + 1) * SGU_GROUP_DIM)
        w_m = jnp.where(tri, ws_ref[g], 0.0).astype(BF16)
        sv = _dot(w_m, vn[:, cols]) + bst_ref[:, g:g + 1]
        o_ref[:, cols] = (u_ref[:, cols].astype(F32) * sv).astype(o_ref.dtype)


def _sgu_core(proj, ln_g, w_spatial, b_spatial):
    m = proj.shape[0]
    q = SGU_CHUNK
    return pl.pallas_call(
        _sgu_kernel,
        grid=(m // q,),
        in_specs=[
            pl.BlockSpec((q, SGU_WIDTH), lambda i: (i, 0)),
            pl.BlockSpec((q, SGU_WIDTH), lambda i: (i, 1)),
            pl.BlockSpec((1, SGU_WIDTH), lambda i: (0, 0)),
            pl.BlockSpec((SGU_GROUPS, q, q), lambda i: (0, 0, 0)),
            pl.BlockSpec((q, SGU_GROUPS), lambda i: (0, 0)),
        ],
        out_specs=pl.BlockSpec((q, SGU_WIDTH), lambda i: (i, 0)),
        out_shape=jax.ShapeDtypeStruct((m, SGU_WIDTH), BF16),
        compiler_params=_cparams("parallel"),
        name="sgu_core",
    )(proj, proj, ln_g.reshape(1, SGU_WIDTH), w_spatial, b_spatial.T)


def _pool_kernel(z_ref, zh_ref, wg_ref, sc_ref, o_ref, zz_ref, *, tm, blocks_per_seq):
    i = pl.program_id(0)
    keep = (i % blocks_per_seq != 0).astype(F32)
    zz_ref[0:POOL_HALO, :] = zh_ref[...] * keep
    zz_ref[POOL_HALO:, :] = z_ref[...]
    t = (i % blocks_per_seq) * tm + lax.broadcasted_iota(jnp.int32, (tm, 1), 0)
    for gi, win in enumerate(POOL_WINDOWS):
        cols = slice(gi * POOL_GROUP_DIM, (gi + 1) * POOL_GROUP_DIM)
        s = zz_ref[POOL_HALO:POOL_HALO + tm, cols]
        for k in range(1, win):
            s = s + zz_ref[POOL_HALO - k:POOL_HALO - k + tm, cols]
        count = jnp.minimum(t + 1, win).astype(F32)
        pooled = s / count - z_ref[:, cols]
        y = _dot(pooled.astype(BF16), wg_ref[gi]) * sc_ref[:, cols]
        o_ref[:, cols] = y.astype(o_ref.dtype)


def _pool_core(z, w_group, scale, seq_len, *, tm=512):
    m, n = z.shape
    tm = min(tm, seq_len)
    halo_blocks = tm // POOL_HALO
    return pl.pallas_call(
        functools.partial(_pool_kernel, tm=tm, blocks_per_seq=seq_len // tm),
        grid=(m // tm,),
        in_specs=[
            pl.BlockSpec((tm, n), lambda i: (i, 0)),
            pl.BlockSpec((POOL_HALO, n), lambda i: (jnp.maximum(i * halo_blocks - 1, 0), 0)),
            pl.BlockSpec(w_group.shape, lambda i: (0, 0, 0)),
            pl.BlockSpec((1, n), lambda i: (0, 0)),
        ],
        out_specs=pl.BlockSpec((tm, n), lambda i: (i, 0)),
        out_shape=jax.ShapeDtypeStruct((m, n), BF16),
        scratch_shapes=[pltpu.VMEM((tm + POOL_HALO, n), F32)],
        compiler_params=_cparams("parallel"),
        name="pool_core",
    )(z, z, w_group, scale.reshape(1, n))


def _pad_cols(w, n):
    return jnp.pad(w, ((0, 0), (0, 0), (0, n - w.shape[2])))


def _ssd_layer(x, g_pre, g_post, w_in, j, conv_w, conv_b, dt_bias, a_log, d_skip, norm_g, w_out, bsz, seq_len):
    proj = _norm_matmul(x, g_pre, w_in, j, jnp.zeros((SSD_IN_PAD,), F32))
    y = _ssd_core(proj, conv_w, conv_b, dt_bias, a_log, d_skip, norm_g, bsz, seq_len)
    return _matmul_post(y, w_out, j, g_post, x)


def _nsa_layer(x, g_pre, g_post, w_in, j, cmp_pos, cmp_w1, cmp_w2, w_out, bsz, seq_len):
    proj = _norm_matmul(x, g_pre, w_in, j, jnp.zeros((NSA_IN_PAD,), F32))
    kvc = _nsa_compress(proj, cmp_pos, cmp_w1, cmp_w2, bsz, seq_len)
    ocmp, sel = _nsa_select(proj, kvc, bsz, seq_len)
    ngate = 3 * NSA_N_HEADS
    gl = proj[:, NSA_GATE_COL:NSA_GATE_COL + ngate].reshape(-1, NSA_KV_GROUPS, 3 * NSA_HPG).transpose(1, 0, 2)
    o = _nsa_attn(proj, sel, ocmp, gl, bsz, seq_len)
    return _matmul_post(o, w_out, j, g_post, x)


def _sgu_layer(x, g_pre, g_post, w_in, j, b_in, ln_g, w_spatial, b_spatial, w_out):
    proj = _norm_matmul(x, g_pre, w_in, j, b_in, act="gelu", out_dtype=BF16)
    y = _sgu_core(proj, ln_g, w_spatial, b_spatial)
    return _matmul_post(y, w_out, j, g_post, x)


def _pool_layer(x, g_pre, g_post, w_in, j, w_group, scale, w_out, seq_len):
    z = _norm_matmul(x, g_pre, w_in, j, jnp.zeros((D_MODEL,), F32))
    y = _pool_core(z, w_group.astype(BF16), scale, seq_len)
    return _matmul_post(y, w_out, j, g_post, x)


def kernel(x, mem, norm_pre, norm_post, norm_mem, ssd_w_in, ssd_conv_w, ssd_conv_b, ssd_dt_bias, ssd_a_log, ssd_d, ssd_norm_g, ssd_w_out, nsa_w_in, nsa_cmp_pos, nsa_cmp_w1, nsa_cmp_w2, nsa_w_out, sgu_w_in, sgu_b_in, sgu_ln_g, sgu_w_spatial, sgu_b_spatial, sgu_w_out, pool_w_in, pool_w_group, pool_scale, pool_w_out, xa_w_q, xa_w_kv, xa_w_o, ffn_w_up, ffn_conv_w, ffn_conv_b, ffn_w_down):
    bsz, seq_len, d = x.shape
    depth = norm_pre.shape[0]
    xf = x.reshape(bsz * seq_len, d)
    memf = mem.reshape(bsz * mem.shape[1], d)
    ssd_in, ssd_out = _pad_cols(ssd_w_in, SSD_IN_PAD).astype(BF16), ssd_w_out.astype(BF16)
    nsa_in, nsa_out = _pad_cols(nsa_w_in, NSA_IN_PAD).astype(BF16), nsa_w_out.astype(BF16)
    sgu_in, sgu_out = sgu_w_in.astype(BF16), sgu_w_out.astype(BF16)
    pool_in, pool_out = pool_w_in.astype(BF16), pool_w_out.astype(BF16)
    xa_q, xa_kv, xa_o = xa_w_q.astype(BF16), xa_w_kv.astype(BF16), xa_w_o.astype(BF16)
    ffn_up, ffn_down = ffn_w_up.astype(BF16), ffn_w_down.astype(BF16)
    for i in range(depth):
        kind, j = i % 4, i // 4
        if kind == 0:
            xf = _ssd_layer(xf, norm_pre[i, 0], norm_post[i, 0], ssd_in, j, ssd_conv_w[j], ssd_conv_b[j],
                            ssd_dt_bias[j], ssd_a_log[j], ssd_d[j], ssd_norm_g[j], ssd_out, bsz, seq_len)
        elif kind == 1:
            xf = _nsa_layer(xf, norm_pre[i, 0], norm_post[i, 0], nsa_in, j, nsa_cmp_pos[j], nsa_cmp_w1[j],
                            nsa_cmp_w2[j], nsa_out, bsz, seq_len)
        elif kind == 2:
            xf = _sgu_layer(xf, norm_pre[i, 0], norm_post[i, 0], sgu_in, j, sgu_b_in[j], sgu_ln_g[j],
                            sgu_w_spatial[j], sgu_b_spatial[j], sgu_out)
        else:
            xf = _pool_layer(xf, norm_pre[i, 0], norm_post[i, 0], pool_in, j, pool_w_group[j],
                             pool_scale[j], pool_out, seq_len)
        kv = _norm_matmul(memf, norm_mem[i], xa_kv, i, jnp.zeros((2 * XA_DIM,), F32),
                          out_dtype=BF16, tm=MEM_LEN)
        xf = _xattn(xf, kv, norm_pre[i, 1], norm_post[i, 1], xa_q, xa_o, i, seq_len)
        act = _ffn_up(xf, norm_pre[i, 2], ffn_up, i, ffn_conv_w[i], ffn_conv_b[i], seq_len)
        xf = _matmul_post(act, ffn_down, i, norm_post[i, 2], xf)
    return xf.reshape(bsz, seq_len, d)
```

```python
import functools
import math

import jax
import jax.numpy as jnp
import numpy as np
from jax import lax
from jax.experimental import pallas as pl
from jax.experimental.pallas import tpu as pltpu

F32 = jnp.float32
BF16 = jnp.bfloat16

D_MODEL = 2048
RMS_EPS = 1e-6
NEG = -1e30
BIG = 1e30
MEM_LEN = 256

SSD_D_INNER = 4096
SSD_HEAD_DIM = 64
SSD_N_HEADS = 64
SSD_N_GROUPS = 8
SSD_HPG = 8
SSD_D_STATE = 128
SSD_CONV = 4
SSD_CHUNK = 128
SSD_GROUP_W = SSD_HPG * SSD_HEAD_DIM
SSD_BC_DIM = SSD_N_GROUPS * SSD_D_STATE
SSD_IN_PAD = 10752

NSA_HEAD_DIM = 128
NSA_N_HEADS = 16
NSA_KV_GROUPS = 4
NSA_HPG = 4
NSA_KV_DIM = 512
NSA_CMP_BLOCK = 32
NSA_CMP_STRIDE = 16
NSA_CMP_HIDDEN = 256
NSA_SLC_BLOCK = 64
NSA_TOPK = 16
NSA_N_LOCAL = 2
NSA_WINDOW = 512
NSA_TQ = 256
NSA_KEY_TILE = 1024
NSA_SEL_TQ = 256
NSA_IN_PAD = 5632
NSA_GATE_COL = 5120

SGU_CHUNK = 128
SGU_WIDTH = 4096
SGU_GROUPS = 8
SGU_GROUP_DIM = 512

POOL_WINDOWS = (2, 4, 8, 16)
POOL_GROUP_DIM = 512
POOL_HALO = 16

XA_HEADS = 4
XA_HEAD_DIM = 128
XA_DIM = 512

FFN_HIDDEN = 5632
FFN_CONV = 3
FFN_HALO = 16

LANE = 128
VMEM_LIMIT = 56 * 1024 * 1024


def _cparams(*sem):
    return pltpu.CompilerParams(dimension_semantics=sem, vmem_limit_bytes=VMEM_LIMIT)


def _dot(a, b):
    return jnp.dot(a, b, preferred_element_type=F32)


def _dot_nt(a, b):
    return lax.dot_general(a, b, (((1,), (1,)), ((), ())), preferred_element_type=F32)


def _split3(v):
    hi = v.astype(BF16)
    r1 = v - hi.astype(F32)
    mid = r1.astype(BF16)
    lo = (r1 - mid.astype(F32)).astype(BF16)
    return hi, mid, lo


def _dot_f32_lhs(v, e):
    hi, mid, lo = _split3(v)
    return _dot(hi, e) + _dot(mid, e) + _dot(lo, e)


def _dot_2piece_lhs(v, e2):
    hi = v.astype(BF16)
    mid = (v - hi.astype(F32)).astype(BF16)
    return _dot(jnp.concatenate([hi, mid], axis=1), e2)


def _dot_f32_rhs(e, v):
    hi, mid, lo = _split3(v)
    return _dot(e, hi) + _dot(e, mid) + _dot(e, lo)


def _rms(x, g):
    ms = jnp.mean(x * x, axis=-1, keepdims=True)
    return x * lax.rsqrt(ms + RMS_EPS) * g


def _sigmoid(x):
    return 1.0 / (1.0 + jnp.exp(-x))


def _silu(x):
    return x * _sigmoid(x)


def _gelu(x):
    c = math.sqrt(2.0 / math.pi)
    return x * (0.5 * (1.0 + jnp.tanh(c * (x + 0.044715 * (x * x * x)))))


def _softplus(x):
    return jnp.maximum(x, 0.0) + jnp.log1p(jnp.exp(-jnp.abs(x)))


def _norm_mm_kernel(x_ref, g_ref, w_ref, b_ref, o_ref, h_ref, *, act):
    @pl.when(pl.program_id(1) == 0)
    def _():
        h_ref[...] = _rms(x_ref[...], g_ref[...]).astype(BF16)

    y = _dot(h_ref[...], w_ref[...]) + b_ref[...]
    if act == "gelu":
        y = _gelu(y)
    o_ref[...] = y.astype(o_ref.dtype)


def _norm_matmul(x, g, w, li, b, *, act=None, out_dtype=F32, tm=1024, tn=512):
    m, k = x.shape
    n = w.shape[2]
    tm = min(tm, m)
    assert m % tm == 0 and n % tn == 0, (m, n, tm, tn)
    return pl.pallas_call(
        functools.partial(_norm_mm_kernel, act=act),
        grid=(m // tm, n // tn),
        in_specs=[
            pl.BlockSpec((tm, k), lambda i, j: (i, 0)),
            pl.BlockSpec((1, k), lambda i, j: (0, 0)),
            pl.BlockSpec((None, k, tn), lambda i, j: (li, 0, j)),
            pl.BlockSpec((1, tn), lambda i, j: (0, j)),
        ],
        out_specs=pl.BlockSpec((tm, tn), lambda i, j: (i, j)),
        out_shape=jax.ShapeDtypeStruct((m, n), out_dtype),
        scratch_shapes=[pltpu.VMEM((tm, k), BF16)],
        compiler_params=_cparams("parallel", "arbitrary"),
        name="norm_matmul",
    )(x, g.reshape(1, k), w, b.reshape(1, n))


def _mm_post_kernel(a_ref, w_ref, g_ref, r_ref, o_ref, ssq_ref, *, nj, tn):
    j = pl.program_id(1)
    y = _dot(a_ref[...], w_ref[...])
    o_ref[:, pl.ds(pl.multiple_of(j * tn, tn), tn)] = y
    part = jnp.sum(y * y, axis=-1, keepdims=True)

    @pl.when(j == 0)
    def _():
        ssq_ref[...] = jnp.broadcast_to(part, ssq_ref.shape)

    @pl.when(j > 0)
    def _():
        ssq_ref[...] += jnp.broadcast_to(part, ssq_ref.shape)

    @pl.when(j == nj - 1)
    def _():
        inv = lax.rsqrt(ssq_ref[:, 0:1] * (1.0 / o_ref.shape[1]) + RMS_EPS)
        o_ref[...] = r_ref[...] + o_ref[...] * inv * g_ref[...]


def _matmul_post(a, w, li, g, res, *, tm=512, tn=512):
    m, kdim = a.shape
    n = w.shape[2]
    tm = min(tm, m)
    assert m % tm == 0 and n % tn == 0
    nj = n // tn
    return pl.pallas_call(
        functools.partial(_mm_post_kernel, nj=nj, tn=tn),
        grid=(m // tm, nj),
        in_specs=[
            pl.BlockSpec((tm, kdim), lambda i, j: (i, 0)),
            pl.BlockSpec((None, kdim, tn), lambda i, j: (li, 0, j)),
            pl.BlockSpec((1, n), lambda i, j: (0, 0)),
            pl.BlockSpec((tm, n), lambda i, j: (i, 0)),
        ],
        out_specs=pl.BlockSpec((tm, n), lambda i, j: (i, 0)),
        out_shape=jax.ShapeDtypeStruct((m, n), F32),
        scratch_shapes=[pltpu.VMEM((tm, LANE), F32)],
        compiler_params=_cparams("parallel", "arbitrary"),
        name="matmul_postnorm_residual",
    )(a, w, g.reshape(1, n), res)


def _ffn_up_kernel(x_ref, xh_ref, g_ref, wg_ref, wv_ref, cwg_ref, cwv_ref, cbg_ref, cbv_ref,
                   o_ref, h_ref, *, tm, blocks_per_seq):
    i = pl.program_id(0)

    @pl.when(pl.program_id(1) == 0)
    def _():
        keep = (i % blocks_per_seq != 0).astype(F32)
        h_ref[0:FFN_HALO, :] = (_rms(xh_ref[...], g_ref[...]) * keep).astype(BF16)
        h_ref[FFN_HALO:, :] = _rms(x_ref[...], g_ref[...]).astype(BF16)

    h = h_ref[...]

    def conv(u, cw_ref, cb_ref):
        y = cb_ref[...] + u[FFN_HALO:] * cw_ref[FFN_CONV - 1:FFN_CONV, :]
        for tap in range(FFN_CONV - 1):
            back = FFN_CONV - 1 - tap
            y = y + pltpu.roll(u, back, axis=0)[FFN_HALO:] * cw_ref[tap:tap + 1, :]
        return y

    gate = conv(_dot(h, wg_ref[...]), cwg_ref, cbg_ref)
    val = conv(_dot(h, wv_ref[...]), cwv_ref, cbv_ref)
    o_ref[...] = (_silu(gate) * val).astype(o_ref.dtype)


def _ffn_up(x, g, w_up, li, conv_w, conv_b, seq_len, *, tm=1024, tn=512):
    m, k = x.shape
    tm = min(tm, seq_len)
    assert m % tm == 0 and seq_len % tm == 0 and FFN_HIDDEN % tn == 0
    nj = FFN_HIDDEN // tn
    halo_blocks = tm // FFN_HALO
    return pl.pallas_call(
        functools.partial(_ffn_up_kernel, tm=tm, blocks_per_seq=seq_len // tm),
        grid=(m // tm, nj),
        in_specs=[
            pl.BlockSpec((tm, k), lambda i, j: (i, 0)),
            pl.BlockSpec((FFN_HALO, k), lambda i, j: (jnp.maximum(i * halo_blocks - 1, 0), 0)),
            pl.BlockSpec((1, k), lambda i, j: (0, 0)),
            pl.BlockSpec((None, k, tn), lambda i, j: (li, 0, j)),
            pl.BlockSpec((None, k, tn), lambda i, j: (li, 0, j + nj)),
            pl.BlockSpec((FFN_CONV, tn), lambda i, j: (0, j)),
            pl.BlockSpec((FFN_CONV, tn), lambda i, j: (0, j + nj)),
            pl.BlockSpec((1, tn), lambda i, j: (0, j)),
            pl.BlockSpec((1, tn), lambda i, j: (0, j + nj)),
        ],
        out_specs=pl.BlockSpec((tm, tn), lambda i, j: (i, j)),
        out_shape=jax.ShapeDtypeStruct((m, FFN_HIDDEN), BF16),
        scratch_shapes=[pltpu.VMEM((tm + FFN_HALO, k), BF16)],
        compiler_params=_cparams("parallel", "arbitrary"),
        name="ffn_up_conv_gate",
    )(x, x, g.reshape(1, k), w_up, w_up, conv_w, conv_w,
      conv_b.reshape(1, -1), conv_b.reshape(1, -1))


def _xattn_kernel(x_ref, kv_ref, gpre_ref, gpost_ref, wq_ref, wo_ref, o_ref, att_ref):
    x = x_ref[...]
    h = _rms(x, gpre_ref[...]).astype(BF16)
    q = _dot(h, wq_ref[...]).astype(BF16)
    scale = XA_HEAD_DIM ** -0.5
    for hd in range(XA_HEADS):
        lo = hd * XA_HEAD_DIM
        s = _dot_nt(q[:, lo:lo + XA_HEAD_DIM], kv_ref[:, lo:lo + XA_HEAD_DIM]) * scale
        e = jnp.exp(s - jnp.max(s, axis=-1, keepdims=True))
        p = e / jnp.sum(e, axis=-1, keepdims=True)
        att_ref[:, lo:lo + XA_HEAD_DIM] = _dot(
            p.astype(BF16), kv_ref[:, XA_DIM + lo:XA_DIM + lo + XA_HEAD_DIM]).astype(BF16)
    a = _dot(att_ref[...], wo_ref[...])
    o_ref[...] = x + _rms(a, gpost_ref[...])


def _xattn(x, kv, g_pre, g_post, w_q, w_o, li, seq_len, *, tm=512):
    m, k = x.shape
    tm = min(tm, seq_len)
    bps = seq_len // tm
    return pl.pallas_call(
        _xattn_kernel,
        grid=(m // tm,),
        in_specs=[
            pl.BlockSpec((tm, k), lambda i: (i, 0)),
            pl.BlockSpec((MEM_LEN, 2 * XA_DIM), lambda i: (i // bps, 0)),
            pl.BlockSpec((1, k), lambda i: (0, 0)),
            pl.BlockSpec((1, k), lambda i: (0, 0)),
            pl.BlockSpec((None, k, XA_DIM), lambda i: (li, 0, 0)),
            pl.BlockSpec((None, XA_DIM, k), lambda i: (li, 0, 0)),
        ],
        out_specs=pl.BlockSpec((tm, k), lambda i: (i, 0)),
        out_shape=jax.ShapeDtypeStruct((m, k), F32),
        scratch_shapes=[pltpu.VMEM((tm, XA_DIM), BF16)],
        compiler_params=_cparams("parallel"),
        name="memory_cross_attention",
    )(x, kv, g_pre.reshape(1, k), g_post.reshape(1, k), w_q, w_o)


def _ssd_kernel(z_ref, xr_ref, br_ref, cr_ref, dtr_ref, cw_ref, cbias_ref, dtb_ref, alog_ref,
                dskip_ref, ng_ref, expand_ref, tril_ref, o_ref,
                xin_ref, xs_ref, bm_ref, cm_ref, wx_ref, expax_ref, cdx_ref,
                acsg_ref, acst_ref, state_ref, y_ref):
    q = SSD_CHUNK
    c = pl.program_id(1)
    halo = 8
    xw = SSD_D_INNER

    @pl.when(c == 0)
    def _():
        xin_ref[0:halo, :] = jnp.zeros((halo, xin_ref.shape[1]), F32)
        state_ref[...] = jnp.zeros(state_ref.shape, F32)

    @pl.when(c > 0)
    def _():
        xin_ref[0:halo, :] = xin_ref[q:q + halo, :]

    xin_ref[halo:, 0:xw] = xr_ref[...]
    xin_ref[halo:, xw:xw + SSD_BC_DIM] = br_ref[...]
    xin_ref[halo:, xw + SSD_BC_DIM:] = cr_ref[...]

    def conv_tile(lo, width):
        slab = xin_ref[:, pl.ds(lo, width)]
        y = cbias_ref[:, pl.ds(lo, width)] + slab[halo:] * cw_ref[SSD_CONV - 1:SSD_CONV, pl.ds(lo, width)]
        for tap in range(SSD_CONV - 1):
            back = SSD_CONV - 1 - tap
            y = y + pltpu.roll(slab, back, axis=0)[halo:] * cw_ref[tap:tap + 1, pl.ds(lo, width)]
        return _silu(y)

    def conv_x(t, carry):
        lo = pl.multiple_of(t * 512, 512)
        xs_ref[:, pl.ds(lo, 512)] = conv_tile(lo, 512)
        return carry

    lax.fori_loop(0, xw // 512, conv_x, 0)
    for t in range(SSD_BC_DIM // 512):
        bm_ref[:, t * 512:(t + 1) * 512] = conv_tile(xw + t * 512, 512)
        cm_ref[:, t * 512:(t + 1) * 512] = conv_tile(xw + SSD_BC_DIM + t * 512, 512)

    dt = _softplus(dtr_ref[...] + dtb_ref[...])
    a = dt * (-jnp.exp(alog_ref[...]))
    a_cs = _dot_f32_rhs(tril_ref[...], a)
    a_last = a_cs[q - 1:q, :]
    expand = expand_ref[...]
    wx_ref[...] = _dot_2piece_lhs(dt * jnp.exp(a_last - a_cs), expand)
    expax_ref[...] = _dot_2piece_lhs(jnp.exp(a_cs), expand)
    cdx_ref[...] = _dot_2piece_lhs(jnp.broadcast_to(jnp.exp(a_last), (8, LANE)), expand)
    acst_ref[...] = (a_cs - jnp.log(dt)).T.reshape(LANE // SSD_HPG, SSD_HPG, q)
    for g in range(SSD_N_GROUPS):
        acsg_ref[g] = a_cs if g == 0 else pltpu.roll(a_cs, LANE - g * SSD_HPG, axis=1)

    row = lax.broadcasted_iota(jnp.int32, (q, q), 0)
    col = lax.broadcasted_iota(jnp.int32, (q, q), 1)
    causal = row >= col
    left_half = col < SSD_HEAD_DIM

    def group_body(g, carry):
        off = pl.multiple_of(g * SSD_GROUP_W, SSD_GROUP_W)
        offn = pl.multiple_of(g * SSD_D_STATE, SSD_D_STATE)
        cm = cm_ref[:, pl.ds(offn, SSD_D_STATE)]
        bm = bm_ref[:, pl.ds(offn, SSD_D_STATE)]
        cmb = cm.astype(BF16)
        cb = _dot_nt(cmb, bm.astype(BF16))
        xs = xs_ref[:, pl.ds(off, SSD_GROUP_W)]
        st = state_ref[g]
        y_off = _dot(cmb, st.astype(BF16)) * expax_ref[:, pl.ds(off, SSD_GROUP_W)]
        s_new = _dot(bm.T.astype(BF16), (xs * wx_ref[:, pl.ds(off, SSD_GROUP_W)]).astype(BF16))
        state_ref[g] = st * cdx_ref[0:1, pl.ds(off, SSD_GROUP_W)] + s_new
        acs_g = acsg_ref[g]
        acst_g = acst_ref[g]
        skip = xs * dskip_ref[:, pl.ds(off, SSD_GROUP_W)]
        for pr in range(SSD_HPG // 2):
            scs = []
            for e in range(2):
                j = 2 * pr + e
                diff = acs_g[:, j:j + 1] - acst_g[j:j + 1, :]
                dec = jnp.where(causal, jnp.exp(diff), 0.0)
                scs.append((cb * dec).astype(BF16))
            xp = xs[:, pr * LANE:(pr + 1) * LANE]
            x0 = jnp.where(left_half, xp, 0.0).astype(BF16)
            x1 = jnp.where(left_half, 0.0, xp).astype(BF16)
            y_diag = _dot(jnp.concatenate(scs, axis=1), jnp.concatenate([x0, x1], axis=0))
            y = y_diag + y_off[:, pr * LANE:(pr + 1) * LANE] + skip[:, pr * LANE:(pr + 1) * LANE]
            y_ref[:, pl.ds(pl.multiple_of(off + pr * LANE, LANE), LANE)] = y
        return carry

    lax.fori_loop(0, SSD_N_GROUPS, group_body, 0, unroll=8)

    gated = y_ref[...] * _silu(z_ref[...])
    o_ref[...] = _rms(gated, ng_ref[...]).astype(o_ref.dtype)


def _ssd_core(proj, conv_w, conv_b, dt_bias, a_log, d_skip, norm_g, bsz, seq_len):
    q = SSD_CHUNK
    nc = seq_len // q
    m = bsz * seq_len
    conv_dim = SSD_D_INNER + 2 * SSD_BC_DIM
    pad = LANE - SSD_N_HEADS
    dtb = jnp.pad(dt_bias, (0, pad)).reshape(1, LANE)
    alog = jnp.pad(a_log, (0, pad)).reshape(1, LANE)
    dskip = jnp.repeat(d_skip, SSD_HEAD_DIM).reshape(1, SSD_D_INNER)
    heads = np.arange(2 * LANE)[:, None] % LANE
    chans = np.arange(SSD_D_INNER)[None, :] // SSD_HEAD_DIM
    expand = jnp.asarray((heads == chans).astype(np.float32), dtype=BF16)
    tril = jnp.asarray(np.tril(np.ones((q, q), np.float32)), dtype=BF16)
    row = lambda b, c: b * nc + c
    const = lambda b, c: (0, 0)
    return pl.pallas_call(
        _ssd_kernel,
        grid=(bsz, nc),
        in_specs=[
            pl.BlockSpec((q, SSD_D_INNER), lambda b, c: (row(b, c), 0)),
            pl.BlockSpec((q, SSD_D_INNER), lambda b, c: (row(b, c), 1)),
            pl.BlockSpec((q, SSD_BC_DIM), lambda b, c: (row(b, c), 8)),
            pl.BlockSpec((q, SSD_BC_DIM), lambda b, c: (row(b, c), 9)),
            pl.BlockSpec((q, LANE), lambda b, c: (row(b, c), 80)),
            pl.BlockSpec((SSD_CONV, conv_dim), const),
            pl.BlockSpec((1, conv_dim), const),
            pl.BlockSpec((1, LANE), const),
            pl.BlockSpec((1, LANE), const),
            pl.BlockSpec((1, SSD_D_INNER), const),
            pl.BlockSpec((1, SSD_D_INNER), const),
            pl.BlockSpec((2 * LANE, SSD_D_INNER), const),
            pl.BlockSpec((q, q), const),
        ],
        out_specs=pl.BlockSpec((q, SSD_D_INNER), lambda b, c: (row(b, c), 0)),
        out_shape=jax.ShapeDtypeStruct((m, SSD_D_INNER), BF16),
        scratch_shapes=[
            pltpu.VMEM((q + 8, conv_dim), F32),
            pltpu.VMEM((q, SSD_D_INNER), F32),
            pltpu.VMEM((q, SSD_BC_DIM), F32),
            pltpu.VMEM((q, SSD_BC_DIM), F32),
            pltpu.VMEM((q, SSD_D_INNER), F32),
            pltpu.VMEM((q, SSD_D_INNER), F32),
            pltpu.VMEM((8, SSD_D_INNER), F32),
            pltpu.VMEM((SSD_N_GROUPS, q, LANE), F32),
            pltpu.VMEM((LANE // SSD_HPG, SSD_HPG, q), F32),
            pltpu.VMEM((SSD_N_GROUPS, SSD_D_STATE, SSD_GROUP_W), F32),
            pltpu.VMEM((q, SSD_D_INNER), F32),
        ],
        compiler_params=_cparams("parallel", "arbitrary"),
        name="ssd_core",
    )(proj, proj, proj, proj, proj, conv_w, conv_b.reshape(1, conv_dim), dtb, alog, dskip,
      norm_g.reshape(1, SSD_D_INNER), expand, tril)


def _nsa_compress_kernel(t_ref, pos_ref, w1_ref, w2_ref, o_ref, *, nchunk):
    half = NSA_CMP_STRIDE
    dh = NSA_HEAD_DIM
    p_lo = jnp.zeros((nchunk, NSA_CMP_HIDDEN), F32)
    p_hi = jnp.zeros((nchunk, NSA_CMP_HIDDEN), F32)
    for l in range(half):
        rows = t_ref[pl.ds(l, nchunk, stride=half), :]
        a_lo = (rows + pos_ref[l:l + 1, :]).astype(BF16)
        a_hi = (rows + pos_ref[half + l:half + l + 1, :]).astype(BF16)
        p_lo = p_lo + _dot(a_lo, w1_ref[l * dh:(l + 1) * dh, :].astype(BF16))
        p_hi = p_hi + _dot(a_hi, w1_ref[(half + l) * dh:(half + l + 1) * dh, :].astype(BF16))
    pre = p_lo + pltpu.roll(p_hi, nchunk - 1, axis=0)
    o_ref[...] = _dot(_gelu(pre).astype(BF16), w2_ref[...].astype(BF16))


def _nsa_compress(proj, cmp_pos, cmp_w1, cmp_w2, bsz, seq_len):
    nchunk = seq_len // NSA_CMP_STRIDE
    g, dh = NSA_KV_GROUPS, NSA_HEAD_DIM
    col0 = NSA_N_HEADS * dh // dh
    return pl.pallas_call(
        functools.partial(_nsa_compress_kernel, nchunk=nchunk),
        grid=(bsz, 2, g),
        in_specs=[
            pl.BlockSpec((seq_len, dh), lambda b, s, gi: (b, col0 + s * g + gi)),
            pl.BlockSpec((None, NSA_CMP_BLOCK, dh), lambda b, s, gi: (s, 0, 0)),
            pl.BlockSpec((None, NSA_CMP_BLOCK * dh, NSA_CMP_HIDDEN), lambda b, s, gi: (s, 0, 0)),
            pl.BlockSpec((None, NSA_CMP_HIDDEN, dh), lambda b, s, gi: (s, 0, 0)),
        ],
        out_specs=pl.BlockSpec((None, None, None, nchunk, dh), lambda b, s, gi: (b, s, gi, 0, 0)),
        out_shape=jax.ShapeDtypeStruct((bsz, 2, g, nchunk, dh), F32),
        compiler_params=_cparams("parallel", "parallel", "parallel"),
        name="nsa_compress",
    )(proj, cmp_pos, cmp_w1, cmp_w2)


def _nsa_select_kernel(q_ref, kc_ref, vc_ref, cover_ref, ocmp_ref, sel_ref, *, ncmp, nslc):
    tq = NSA_SEL_TQ
    dh = NSA_HEAD_DIM
    scale = dh ** -0.5
    nck = kc_ref.shape[0]
    t0 = pl.program_id(2) * tq
    t_col = t0 + lax.broadcasted_iota(jnp.int32, (tq, 1), 0)
    ci_row = lax.broadcasted_iota(jnp.int32, (1, nck), 1)
    valid = (ci_row * NSA_CMP_STRIDE + (NSA_CMP_BLOCK - 1) <= t_col) & (ci_row < ncmp)
    t_row = t0 + lax.broadcasted_iota(jnp.int32, (1, tq), 1)
    ci_col = lax.broadcasted_iota(jnp.int32, (nck, 1), 0)
    valid_t = (ci_col * NSA_CMP_STRIDE + (NSA_CMP_BLOCK - 1) <= t_row) & (ci_col < ncmp)
    c = scale * math.log2(math.e)
    bias = jnp.where(valid, 0.0, NEG)
    bias_t = jnp.where(valid_t, 0.0, NEG)
    some = jnp.where(t_col >= NSA_CMP_BLOCK - 1, 1.0, 0.0)
    some_t = jnp.where(t_row >= NSA_CMP_BLOCK - 1, 1.0, 0.0)
    kcb = kc_ref[...].astype(BF16)
    vcb = vc_ref[...].astype(BF16)
    psum_t = jnp.zeros((nck, tq), F32)
    for j in range(NSA_HPG):
        qj = q_ref[:, j * dh:(j + 1) * dh].astype(BF16)
        s = _dot_nt(qj, kcb) * c + bias
        e = jnp.exp2(s - jnp.max(s, axis=-1, keepdims=True))
        p = e * (some / jnp.sum(e, axis=-1, keepdims=True))
        ocmp_ref[:, j * dh:(j + 1) * dh] = _dot(p.astype(BF16), vcb)
        st = _dot_nt(kcb, qj) * c + bias_t
        et = jnp.exp2(st - jnp.max(st, axis=0, keepdims=True))
        psum_t = psum_t + et * (some_t / jnp.sum(et, axis=0, keepdims=True))
    imp_t = _dot_f32_rhs(cover_ref[...], psum_t)
    blk = lax.broadcasted_iota(jnp.int32, (nslc, tq), 0)
    cur = t_row // NSA_SLC_BLOCK
    forced = (blk == 0) | ((blk <= cur) & (blk > cur - 1 - NSA_N_LOCAL))
    future = blk * NSA_SLC_BLOCK > t_row
    score = jnp.where(forced, BIG, jnp.where(future, NEG, imp_t))
    sub = 8
    slabs = [score[r * sub:(r + 1) * sub, :] for r in range(nslc // sub)]
    ranks = [jnp.zeros((sub, tq), F32) for _ in slabs]
    row_in_slab = lax.broadcasted_iota(jnp.int32, (sub, tq), 0)
    for j in range(nslc):
        cj = jnp.broadcast_to(score[j:j + 1, :], (sub, tq))
        for r, s_r in enumerate(slabs):
            if r * sub > j:
                before = cj >= s_r
            elif (r + 1) * sub - 1 < j:
                before = cj > s_r
            else:
                before = (cj > s_r) | ((cj == s_r) & (row_in_slab > j - r * sub))
            ranks[r] = ranks[r] + jnp.where(before, 1.0, 0.0)
    topk = float(min(NSA_TOPK, nslc))
    for r, rank in enumerate(ranks):
        sel_ref[r * sub:(r + 1) * sub, :] = jnp.where(rank < topk, 1.0, 0.0)


def _nsa_select(proj, kvc, bsz, seq_len):
    tq, dh, g = NSA_SEL_TQ, NSA_HEAD_DIM, NSA_KV_GROUPS
    nq = seq_len // tq
    ncmp = (seq_len - NSA_CMP_BLOCK) // NSA_CMP_STRIDE + 1
    nck = seq_len // NSA_CMP_STRIDE
    nslc = seq_len // NSA_SLC_BLOCK
    ci = np.arange(nck)[:, None] * NSA_CMP_STRIDE
    sj = np.arange(nslc)[None, :] * NSA_SLC_BLOCK
    cover = ((ci <= sj + NSA_SLC_BLOCK - 1) & (ci + NSA_CMP_BLOCK - 1 >= sj)).astype(np.float32)
    cover = jnp.asarray(cover.T, dtype=BF16)
    return pl.pallas_call(
        functools.partial(_nsa_select_kernel, ncmp=ncmp, nslc=nslc),
        grid=(bsz, g, nq),
        in_specs=[
            pl.BlockSpec((tq, NSA_HPG * dh), lambda b, gi, qi: (b * nq + qi, gi)),
            pl.BlockSpec((None, None, None, nck, dh), lambda b, gi, qi: (b, 0, gi, 0, 0)),
            pl.BlockSpec((None, None, None, nck, dh), lambda b, gi, qi: (b, 1, gi, 0, 0)),
            pl.BlockSpec((nslc, nck), lambda b, gi, qi: (0, 0)),
        ],
        out_specs=[
            pl.BlockSpec((tq, NSA_HPG * dh), lambda b, gi, qi: (b * nq + qi, gi)),
            pl.BlockSpec((None, None, nslc, tq), lambda b, gi, qi: (b, gi, 0, qi)),
        ],
        out_shape=[
            jax.ShapeDtypeStruct((bsz * seq_len, NSA_N_HEADS * dh), F32),
            jax.ShapeDtypeStruct((bsz, g, nslc, seq_len), F32),
        ],
        compiler_params=_cparams("parallel", "parallel", "parallel"),
        name="nsa_cmp_select",
    )(proj, kvc, kvc, cover)


def _nsa_attn_kernel(q_ref, ks_ref, vs_ref, kw_ref, vw_ref, sel_ref, expand_ref, ocmp_ref, gl_ref,
                     o_ref, ksb_ref, vsx_ref, kwb_ref, vwx_ref, selb_ref, q4_ref, pw_ref, m_ref, acc_ref):
    tq = NSA_TQ
    dh = NSA_HEAD_DIM
    kt_w = NSA_KEY_TILE
    win_w = NSA_WINDOW + tq
    c = (dh ** -0.5) * math.log2(math.e)
    qi = pl.program_id(2)

    @pl.when(qi == 0)
    def _():
        ones = jnp.ones((ks_ref.shape[0], dh), BF16)
        for blk in range(ks_ref.shape[0] // LANE):
            cols = slice(blk * LANE, (blk + 1) * LANE)
            ksb_ref[:, cols] = ks_ref[cols, :].T.astype(BF16)
            kwb_ref[:, cols] = kw_ref[cols, :].T.astype(BF16)
        vsx_ref[:, 0:dh] = vs_ref[...].astype(BF16)
        vsx_ref[:, dh:] = ones
        vwx_ref[:, 0:dh] = vw_ref[...].astype(BF16)
        vwx_ref[:, dh:] = ones

    selb_ref[...] = sel_ref[...].T.astype(BF16)
    for j in range(NSA_HPG):
        q4_ref[j * tq:(j + 1) * tq, :] = q_ref[:, j * dh:(j + 1) * dh].astype(BF16)
    t1 = qi * tq + lax.broadcasted_iota(jnp.int32, (tq, 1), 0)
    m_ref[...] = jnp.full(m_ref.shape, NEG, F32)
    acc_ref[...] = jnp.zeros(acc_ref.shape, F32)
    half = NSA_HPG // 2 * tq

    def slc_tile(kt, carry):
        off = pl.multiple_of(kt * kt_w, kt_w)
        pos = off + lax.broadcasted_iota(jnp.int32, (1, kt_w), 1)
        chosen = _dot(selb_ref[...], expand_ref[:, pl.ds(off, kt_w)])
        bias = jnp.where(pos <= t1, (chosen - 1.0) * BIG, NEG)
        k_t = ksb_ref[:, pl.ds(off, kt_w)]
        v_x = vsx_ref[pl.ds(off, kt_w), :]
        pairs = [slice(hp * half, (hp + 1) * half) for hp in range(2)]
        scores = [_dot(q4_ref[pr], k_t) for pr in pairs]
        for hp, pr in enumerate(pairs):
            m_prev = m_ref[pr]
            ps, m_news = [], []
            for e in range(NSA_HPG // 2):
                sj = scores[hp][e * tq:(e + 1) * tq] + bias
                m_new = jnp.maximum(m_prev[e * tq:(e + 1) * tq], jnp.max(sj, axis=-1, keepdims=True))
                m_wide = jnp.concatenate([m_new] * (kt_w // LANE), axis=1)
                ps.append(jnp.exp2((sj - m_wide) * c).astype(BF16))
                m_news.append(m_new)
            m_new = jnp.concatenate(m_news, axis=0)
            alpha = jnp.exp2((m_prev - m_new) * c)
            m_ref[pr] = m_new
            pv = _dot(jnp.concatenate(ps, axis=0), v_x)
            acc_ref[pr] = jnp.concatenate([alpha, alpha], axis=1) * acc_ref[pr] + pv
        return carry

    lax.fori_loop(0, ((qi + 1) * tq + kt_w - 1) // kt_w, slc_tile, 0)

    w0 = pl.multiple_of(jnp.maximum(qi * tq - NSA_WINDOW, 0), tq)
    sw = _dot(q4_ref[...], kwb_ref[:, pl.ds(w0, win_w)])
    diff = t1 - (w0 + lax.broadcasted_iota(jnp.int32, (1, win_w), 1))
    bias_w = jnp.where((diff >= 0) & (diff < NSA_WINDOW), 0.0, NEG)
    for j in range(NSA_HPG):
        rows = slice(j * tq, (j + 1) * tq)
        sj = sw[rows] + bias_w
        pw_ref[rows] = jnp.exp2((sj - jnp.max(sj, axis=-1, keepdims=True)) * c).astype(BF16)
    ow = _dot(pw_ref[...], vwx_ref[pl.ds(w0, win_w), :])

    gates = _sigmoid(gl_ref[...])
    for j in range(NSA_HPG):
        rows = slice(j * tq, (j + 1) * tq)
        o_slc = acc_ref[rows, 0:dh] / acc_ref[rows, dh:]
        o_win = ow[rows, 0:dh] / ow[rows, dh:]
        o = (gates[:, 3 * j:3 * j + 1] * ocmp_ref[:, j * dh:(j + 1) * dh]
             + gates[:, 3 * j + 1:3 * j + 2] * o_slc
             + gates[:, 3 * j + 2:3 * j + 3] * o_win)
        o_ref[:, j * dh:(j + 1) * dh] = o.astype(o_ref.dtype)


def _nsa_attn(proj, sel, ocmp, gate_logits, bsz, seq_len):
    tq, dh, g = NSA_TQ, NSA_HEAD_DIM, NSA_KV_GROUPS
    assert seq_len % NSA_KEY_TILE == 0 and seq_len >= NSA_WINDOW + tq
    rows = NSA_HPG * tq
    nq = seq_len // tq
    nslc = seq_len // NSA_SLC_BLOCK
    expand = (np.arange(nslc)[:, None] == np.arange(seq_len)[None, :] // NSA_SLC_BLOCK)
    expand = jnp.asarray(expand.astype(np.float32), dtype=BF16)
    kv0 = (NSA_N_HEADS * dh + 2 * NSA_KV_DIM) // dh

    def kv_spec(idx):
        return pl.BlockSpec((seq_len, dh), lambda b, gi, qi: (b, kv0 + idx * g + gi))

    qo_spec = pl.BlockSpec((tq, NSA_HPG * dh), lambda b, gi, qi: (b * nq + qi, gi))
    return pl.pallas_call(
        _nsa_attn_kernel,
        grid=(bsz, g, nq),
        in_specs=[
            qo_spec,
            kv_spec(0), kv_spec(1), kv_spec(2), kv_spec(3),
            pl.BlockSpec((None, None, nslc, tq), lambda b, gi, qi: (b, gi, 0, qi)),
            pl.BlockSpec((nslc, seq_len), lambda b, gi, qi: (0, 0)),
            qo_spec,
            pl.BlockSpec((None, tq, 3 * NSA_HPG), lambda b, gi, qi: (gi, b * nq + qi, 0)),
        ],
        out_specs=qo_spec,
        out_shape=jax.ShapeDtypeStruct((bsz * seq_len, NSA_N_HEADS * dh), BF16),
        scratch_shapes=[
            pltpu.VMEM((dh, seq_len), BF16),
            pltpu.VMEM((seq_len, 2 * dh), BF16),
            pltpu.VMEM((dh, seq_len), BF16),
            pltpu.VMEM((seq_len, 2 * dh), BF16),
            pltpu.VMEM((tq, nslc), BF16),
            pltpu.VMEM((rows, dh), BF16),
            pltpu.VMEM((rows, NSA_WINDOW + tq), BF16),
            pltpu.VMEM((rows, LANE), F32),
            pltpu.VMEM((rows, 2 * dh), F32),
        ],
        compiler_params=_cparams("parallel", "parallel", "arbitrary"),
        name="nsa_slc_win_attention",
    )(proj, proj, proj, proj, proj, sel, expand, ocmp, gate_logits)


def _sgu_kernel(u_ref, v_ref, g_ref, ws_ref, bst_ref, o_ref):
    v = v_ref[...].astype(F32)
    mu = jnp.mean(v, axis=-1, keepdims=True)
    vc = v - mu
    vn = (vc * lax.rsqrt(jnp.mean(vc * vc, axis=-1, keepdims=True) + RMS_EPS) * g_ref[...]).astype(BF16)
    q = SGU_CHUNK
    tri = lax.broadcasted_iota(jnp.int32, (q, q), 0) >= lax.broadcasted_iota(jnp.int32, (q, q), 1)
    for g in range(SGU_GROUPS):
        cols = slice(g * SGU_GROUP_DIM, (g + 1) * SGU_GROUP_DIM)
        w_m = jnp.where(tri, ws_ref[g], 0.0).astype(BF16)
        sv = _dot(w_m, vn[:, cols]) + bst_ref[:, g:g + 1]
        o_ref[:, cols] = (u_ref[:, cols].astype(F32) * sv).astype(o_ref.dtype)


def _sgu_core(proj, ln_g, w_spatial, b_spatial):
    m = proj.shape[0]
    q = SGU_CHUNK
    return pl.pallas_call(
        _sgu_kernel,
        grid=(m // q,),
        in_specs=[
            pl.BlockSpec((q, SGU_WIDTH), lambda i: (i, 0)),
            pl.BlockSpec((q, SGU_WIDTH), lambda i: (i, 1)),
            pl.BlockSpec((1, SGU_WIDTH), lambda i: (0, 0)),
            pl.BlockSpec((SGU_GROUPS, q, q), lambda i: (0, 0, 0)),
            pl.BlockSpec((q, SGU_GROUPS), lambda i: (0, 0)),
        ],
        out_specs=pl.BlockSpec((q, SGU_WIDTH), lambda i: (i, 0)),
        out_shape=jax.ShapeDtypeStruct((m, SGU_WIDTH), BF16),
        compiler_params=_cparams("parallel"),
        name="sgu_core",
    )(proj, proj, ln_g.reshape(1, SGU_WIDTH), w_spatial, b_spatial.T)


def _pool_kernel(z_ref, zh_ref, wg_ref, sc_ref, o_ref, zz_ref, *, tm, blocks_per_seq):
    i = pl.program_id(0)
    keep = (i % blocks_per_seq != 0).astype(F32)
    zz_ref[0:POOL_HALO, :] = zh_ref[...] * keep
    zz_ref[POOL_HALO:, :] = z_ref[...]
    t = (i % blocks_per_seq) * tm + lax.broadcasted_iota(jnp.int32, (tm, 1), 0)
    for gi, win in enumerate(POOL_WINDOWS):
        cols = slice(gi * POOL_GROUP_DIM, (gi + 1) * POOL_GROUP_DIM)
        s = zz_ref[POOL_HALO:POOL_HALO + tm, cols]
        for k in range(1, win):
            s = s + zz_ref[POOL_HALO - k:POOL_HALO - k + tm, cols]
        count = jnp.minimum(t + 1, win).astype(F32)
        pooled = s / count - z_ref[:, cols]
        y = _dot(pooled.astype(BF16), wg_ref[gi]) * sc_ref[:, cols]
        o_ref[:, cols] = y.astype(o_ref.dtype)


def _pool_core(z, w_group, scale, seq_len, *, tm=512):
    m, n = z.shape
    tm = min(tm, seq_len)
    halo_blocks = tm // POOL_HALO
    return pl.pallas_call(
        functools.partial(_pool_kernel, tm=tm, blocks_per_seq=seq_len // tm),
        grid=(m // tm,),
        in_specs=[
            pl.BlockSpec((tm, n), lambda i: (i, 0)),
            pl.BlockSpec((POOL_HALO, n), lambda i: (jnp.maximum(i * halo_blocks - 1, 0), 0)),
            pl.BlockSpec(w_group.shape, lambda i: (0, 0, 0)),
            pl.BlockSpec((1, n), lambda i: (0, 0)),
        ],
        out_specs=pl.BlockSpec((tm, n), lambda i: (i, 0)),
        out_shape=jax.ShapeDtypeStruct((m, n), BF16),
        scratch_shapes=[pltpu.VMEM((tm + POOL_HALO, n), F32)],
        compiler_params=_cparams("parallel"),
        name="pool_core",
    )(z, z, w_group, scale.reshape(1, n))


def _pad_cols(w, n):
    return jnp.pad(w, ((0, 0), (0, 0), (0, n - w.shape[2])))


def _ssd_layer(x, g_pre, g_post, w_in, j, conv_w, conv_b, dt_bias, a_log, d_skip, norm_g, w_out, bsz, seq_len):
    proj = _norm_matmul(x, g_pre, w_in, j, jnp.zeros((SSD_IN_PAD,), F32))
    y = _ssd_core(proj, conv_w, conv_b, dt_bias, a_log, d_skip, norm_g, bsz, seq_len)
    return _matmul_post(y, w_out, j, g_post, x)


def _nsa_layer(x, g_pre, g_post, w_in, j, cmp_pos, cmp_w1, cmp_w2, w_out, bsz, seq_len):
    proj = _norm_matmul(x, g_pre, w_in, j, jnp.zeros((NSA_IN_PAD,), F32))
    kvc = _nsa_compress(proj, cmp_pos, cmp_w1, cmp_w2, bsz, seq_len)
    ocmp, sel = _nsa_select(proj, kvc, bsz, seq_len)
    ngate = 3 * NSA_N_HEADS
    gl = proj[:, NSA_GATE_COL:NSA_GATE_COL + ngate].reshape(-1, NSA_KV_GROUPS, 3 * NSA_HPG).transpose(1, 0, 2)
    o = _nsa_attn(proj, sel, ocmp, gl, bsz, seq_len)
    return _matmul_post(o, w_out, j, g_post, x)


def _sgu_layer(x, g_pre, g_post, w_in, j, b_in, ln_g, w_spatial, b_spatial, w_out):
    proj = _norm_matmul(x, g_pre, w_in, j, b_in, act="gelu", out_dtype=BF16)
    y = _sgu_core(proj, ln_g, w_spatial, b_spatial)
    return _matmul_post(y, w_out, j, g_post, x)


def _pool_layer(x, g_pre, g_post, w_in, j, w_group, scale, w_out, seq_len):
    z = _norm_matmul(x, g_pre, w_in, j, jnp.zeros((D_MODEL,), F32))
    y = _pool_core(z, w_group.astype(BF16), scale, seq_len)
    return _matmul_post(y, w_out, j, g_post, x)


def kernel(x, mem, norm_pre, norm_post, norm_mem, ssd_w_in, ssd_conv_w, ssd_conv_b, ssd_dt_bias, ssd_a_log, ssd_d, ssd_norm_g, ssd_w_out, nsa_w_in, nsa_cmp_pos, nsa_cmp_w1, nsa_cmp_w2, nsa_w_out, sgu_w_in, sgu_b_in, sgu_ln_g, sgu_w_spatial, sgu_b_spatial, sgu_w_out, pool_w_in, pool_w_group, pool_scale, pool_w_out, xa_w_q, xa_w_kv, xa_w_o, ffn_w_up, ffn_conv_w, ffn_conv_b, ffn_w_down):
    bsz, seq_len, d = x.shape
    depth = norm_pre.shape[0]
    xf = x.reshape(bsz * seq_len, d)
    memf = mem.reshape(bsz * mem.shape[1], d)
    ssd_in, ssd_out = _pad_cols(ssd_w_in, SSD_IN_PAD).astype(BF16), ssd_w_out.astype(BF16)
    nsa_in, nsa_out = _pad_cols(nsa_w_in, NSA_IN_PAD).astype(BF16), nsa_w_out.astype(BF16)
    sgu_in, sgu_out = sgu_w_in.astype(BF16), sgu_w_out.astype(BF16)
    pool_in, pool_out = pool_w_in.astype(BF16), pool_w_out.astype(BF16)
    xa_q, xa_kv, xa_o = xa_w_q.astype(BF16), xa_w_kv.astype(BF16), xa_w_o.astype(BF16)
    ffn_up, ffn_down = ffn_w_up.astype(BF16), ffn_w_down.astype(BF16)
    for i in range(depth):
        kind, j = i % 4, i // 4
        if kind == 0:
            xf = _ssd_layer(xf, norm_pre[i, 0], norm_post[i, 0], ssd_in, j, ssd_conv_w[j], ssd_conv_b[j],
                            ssd_dt_bias[j], ssd_a_log[j], ssd_d[j], ssd_norm_g[j], ssd_out, bsz, seq_len)
        elif kind == 1:
            xf = _nsa_layer(xf, norm_pre[i, 0], norm_post[i, 0], nsa_in, j, nsa_cmp_pos[j], nsa_cmp_w1[j],
                            nsa_cmp_w2[j], nsa_out, bsz, seq_len)
        elif kind == 2:
            xf = _sgu_layer(xf, norm_pre[i, 0], norm_post[i, 0], sgu_in, j, sgu_b_in[j], sgu_ln_g[j],
                            sgu_w_spatial[j], sgu_b_spatial[j], sgu_out)
        else:
            xf = _pool_layer(xf, norm_pre[i, 0], norm_post[i, 0], pool_in, j, pool_w_group[j],
                             pool_scale[j], pool_out, seq_len)
        kv = _norm_matmul(memf, norm_mem[i], xa_kv, i, jnp.zeros((2 * XA_DIM,), F32),
                          out_dtype=BF16, tm=MEM_LEN)
        xf = _xattn(xf, kv, norm_pre[i, 1], norm_post[i, 1], xa_q, xa_o, i, seq_len)
        act = _ffn_up(xf, norm_pre[i, 2], ffn_up, i, ffn_conv_w[i], ffn_conv_b[i], seq_len)
        xf = _matmul_post(act, ffn_down, i, norm_post[i, 2], xf)
    return xf.reshape(bsz, seq_len, d)
```

```python
import functools
import math

import jax
import jax.numpy as jnp
import numpy as np
from jax import lax
from jax.experimental import pallas as pl
from jax.experimental.pallas import tpu as pltpu

F32 = jnp.float32
BF16 = jnp.bfloat16

D_MODEL = 2048
RMS_EPS = 1e-6
NEG = -1e30
BIG = 1e30
MEM_LEN = 256

SSD_D_INNER = 4096
SSD_HEAD_DIM = 64
SSD_N_HEADS = 64
SSD_N_GROUPS = 8
SSD_HPG = 8
SSD_D_STATE = 128
SSD_CONV = 4
SSD_HALO = 16
SSD_CHUNK = 128
SSD_GROUP_W = SSD_HPG * SSD_HEAD_DIM
SSD_BC_DIM = SSD_N_GROUPS * SSD_D_STATE
SSD_IN_PAD = 10752

NSA_HEAD_DIM = 128
NSA_N_HEADS = 16
NSA_KV_GROUPS = 4
NSA_HPG = 4
NSA_KV_DIM = 512
NSA_CMP_BLOCK = 32
NSA_CMP_STRIDE = 16
NSA_CMP_HIDDEN = 256
NSA_SLC_BLOCK = 64
NSA_TOPK = 16
NSA_N_LOCAL = 2
NSA_WINDOW = 512
NSA_TQ = 256
NSA_KEY_TILE = 1024
NSA_SEL_TQ = 256
NSA_IN_PAD = 5632
NSA_GATE_COL = 5120

SGU_CHUNK = 128
SGU_WIDTH = 4096
SGU_GROUPS = 8
SGU_GROUP_DIM = 512

POOL_WINDOWS = (2, 4, 8, 16)
POOL_GROUP_DIM = 512
POOL_HALO = 16

XA_HEADS = 4
XA_HEAD_DIM = 128
XA_DIM = 512

FFN_HIDDEN = 5632
FFN_CONV = 3
FFN_HALO = 16

LANE = 128
VMEM_LIMIT = 56 * 1024 * 1024


def _cparams(*sem):
    return pltpu.CompilerParams(dimension_semantics=sem, vmem_limit_bytes=VMEM_LIMIT)


def _dot(a, b):
    return jnp.dot(a, b, preferred_element_type=F32)


def _dot_nt(a, b):
    return lax.dot_general(a, b, (((1,), (1,)), ((), ())), preferred_element_type=F32)


def _split3(v):
    hi = v.astype(BF16)
    r1 = v - hi.astype(F32)
    mid = r1.astype(BF16)
    lo = (r1 - mid.astype(F32)).astype(BF16)
    return hi, mid, lo


def _dot_f32_lhs(v, e):
    hi, mid, lo = _split3(v)
    return _dot(hi, e) + _dot(mid, e) + _dot(lo, e)


def _dot_2piece_lhs(v, e2):
    hi = v.astype(BF16)
    mid = (v - hi.astype(F32)).astype(BF16)
    return _dot(jnp.concatenate([hi, mid], axis=1), e2)


def _dot_f32_rhs(e, v):
    hi, mid, lo = _split3(v)
    return _dot(e, hi) + _dot(e, mid) + _dot(e, lo)


def _rms(x, g):
    ms = jnp.mean(x * x, axis=-1, keepdims=True)
    return x * lax.rsqrt(ms + RMS_EPS) * g


def _sigmoid(x):
    return 1.0 / (1.0 + jnp.exp(-x))


def _silu(x):
    return x * _sigmoid(x)


def _gelu(x):
    c = math.sqrt(2.0 / math.pi)
    return x * (0.5 * (1.0 + jnp.tanh(c * (x + 0.044715 * (x * x * x)))))


def _softplus(x):
    return jnp.maximum(x, 0.0) + jnp.log1p(jnp.exp(-jnp.abs(x)))


def _norm_mm_kernel(x_ref, g_ref, w_ref, b_ref, o_ref, h_ref, *, act):
    @pl.when(pl.program_id(1) == 0)
    def _():
        h_ref[...] = _rms(x_ref[...], g_ref[...]).astype(BF16)

    y = _dot(h_ref[...], w_ref[...]) + b_ref[...]
    if act == "gelu":
        y = _gelu(y)
    o_ref[...] = y.astype(o_ref.dtype)


def _norm_matmul(x, g, w, li, b, *, act=None, out_dtype=F32, tm=1024, tn=512):
    m, k = x.shape
    n = w.shape[2]
    tm = min(tm, m)
    assert m % tm == 0 and n % tn == 0, (m, n, tm, tn)
    return pl.pallas_call(
        functools.partial(_norm_mm_kernel, act=act),
        grid=(m // tm, n // tn),
        in_specs=[
            pl.BlockSpec((tm, k), lambda i, j: (i, 0)),
            pl.BlockSpec((1, k), lambda i, j: (0, 0)),
            pl.BlockSpec((None, k, tn), lambda i, j: (li, 0, j)),
            pl.BlockSpec((1, tn), lambda i, j: (0, j)),
        ],
        out_specs=pl.BlockSpec((tm, tn), lambda i, j: (i, j)),
        out_shape=jax.ShapeDtypeStruct((m, n), out_dtype),
        scratch_shapes=[pltpu.VMEM((tm, k), BF16)],
        compiler_params=_cparams("parallel", "arbitrary"),
        name="norm_matmul",
    )(x, g.reshape(1, k), w, b.reshape(1, n))


def _mm_post_kernel(a_ref, w_ref, g_ref, r_ref, o_ref, *, nj, tn):
    j = pl.program_id(1)
    o_ref[:, pl.ds(pl.multiple_of(j * tn, tn), tn)] = _dot(a_ref[...], w_ref[...])

    @pl.when(j == nj - 1)
    def _():
        o_ref[...] = r_ref[...] + _rms(o_ref[...], g_ref[...])


def _matmul_post(a, w, li, g, res, *, tm=512, tn=512):
    m, kdim = a.shape
    n = w.shape[2]
    tm = min(tm, m)
    assert m % tm == 0 and n % tn == 0
    nj = n // tn
    return pl.pallas_call(
        functools.partial(_mm_post_kernel, nj=nj, tn=tn),
        grid=(m // tm, nj),
        in_specs=[
            pl.BlockSpec((tm, kdim), lambda i, j: (i, 0)),
            pl.BlockSpec((None, kdim, tn), lambda i, j: (li, 0, j)),
            pl.BlockSpec((1, n), lambda i, j: (0, 0)),
            pl.BlockSpec((tm, n), lambda i, j: (i, 0)),
        ],
        out_specs=pl.BlockSpec((tm, n), lambda i, j: (i, 0)),
        out_shape=jax.ShapeDtypeStruct((m, n), F32),
        compiler_params=_cparams("parallel", "arbitrary"),
        name="matmul_postnorm_residual",
    )(a, w, g.reshape(1, n), res)


def _ffn_up_kernel(x_ref, xh_ref, g_ref, wg_ref, wv_ref, cwg_ref, cwv_ref, cbg_ref, cbv_ref,
                   o_ref, h_ref, *, tm, blocks_per_seq):
    i = pl.program_id(0)

    @pl.when(pl.program_id(1) == 0)
    def _():
        keep = (i % blocks_per_seq != 0).astype(F32)
        h_ref[0:FFN_HALO, :] = (_rms(xh_ref[...], g_ref[...]) * keep).astype(BF16)
        h_ref[FFN_HALO:, :] = _rms(x_ref[...], g_ref[...]).astype(BF16)

    h = h_ref[...]

    def conv(u, cw_ref, cb_ref):
        y = cb_ref[...] + u[FFN_HALO:] * cw_ref[FFN_CONV - 1:FFN_CONV, :]
        for tap in range(FFN_CONV - 1):
            back = FFN_CONV - 1 - tap
            y = y + pltpu.roll(u, back, axis=0)[FFN_HALO:] * cw_ref[tap:tap + 1, :]
        return y

    gate = conv(_dot(h, wg_ref[...]), cwg_ref, cbg_ref)
    val = conv(_dot(h, wv_ref[...]), cwv_ref, cbv_ref)
    o_ref[...] = (_silu(gate) * val).astype(o_ref.dtype)


def _ffn_up(x, g, w_up, li, conv_w, conv_b, seq_len, *, tm=1024, tn=512):
    m, k = x.shape
    tm = min(tm, seq_len)
    assert m % tm == 0 and seq_len % tm == 0 and FFN_HIDDEN % tn == 0
    nj = FFN_HIDDEN // tn
    halo_blocks = tm // FFN_HALO
    return pl.pallas_call(
        functools.partial(_ffn_up_kernel, tm=tm, blocks_per_seq=seq_len // tm),
        grid=(m // tm, nj),
        in_specs=[
            pl.BlockSpec((tm, k), lambda i, j: (i, 0)),
            pl.BlockSpec((FFN_HALO, k), lambda i, j: (jnp.maximum(i * halo_blocks - 1, 0), 0)),
            pl.BlockSpec((1, k), lambda i, j: (0, 0)),
            pl.BlockSpec((None, k, tn), lambda i, j: (li, 0, j)),
            pl.BlockSpec((None, k, tn), lambda i, j: (li, 0, j + nj)),
            pl.BlockSpec((FFN_CONV, tn), lambda i, j: (0, j)),
            pl.BlockSpec((FFN_CONV, tn), lambda i, j: (0, j + nj)),
            pl.BlockSpec((1, tn), lambda i, j: (0, j)),
            pl.BlockSpec((1, tn), lambda i, j: (0, j + nj)),
        ],
        out_specs=pl.BlockSpec((tm, tn), lambda i, j: (i, j)),
        out_shape=jax.ShapeDtypeStruct((m, FFN_HIDDEN), BF16),
        scratch_shapes=[pltpu.VMEM((tm + FFN_HALO, k), BF16)],
        compiler_params=_cparams("parallel", "arbitrary"),
        name="ffn_up_conv_gate",
    )(x, x, g.reshape(1, k), w_up, w_up, conv_w, conv_w,
      conv_b.reshape(1, -1), conv_b.reshape(1, -1))


def _xattn_kernel(x_ref, kv_ref, gpre_ref, gpost_ref, wq_ref, wo_ref, o_ref, att_ref):
    x = x_ref[...]
    h = _rms(x, gpre_ref[...]).astype(BF16)
    q = _dot(h, wq_ref[...]).astype(BF16)
    scale = XA_HEAD_DIM ** -0.5
    for hd in range(XA_HEADS):
        lo = hd * XA_HEAD_DIM
        s = _dot_nt(q[:, lo:lo + XA_HEAD_DIM], kv_ref[:, lo:lo + XA_HEAD_DIM]) * scale
        e = jnp.exp(s - jnp.max(s, axis=-1, keepdims=True))
        p = e / jnp.sum(e, axis=-1, keepdims=True)
        att_ref[:, lo:lo + XA_HEAD_DIM] = _dot(
            p.astype(BF16), kv_ref[:, XA_DIM + lo:XA_DIM + lo + XA_HEAD_DIM]).astype(BF16)
    a = _dot(att_ref[...], wo_ref[...])
    o_ref[...] = x + _rms(a, gpost_ref[...])


def _xattn(x, kv, g_pre, g_post, w_q, w_o, li, seq_len, *, tm=512):
    m, k = x.shape
    tm = min(tm, seq_len)
    bps = seq_len // tm
    return pl.pallas_call(
        _xattn_kernel,
        grid=(m // tm,),
        in_specs=[
            pl.BlockSpec((tm, k), lambda i: (i, 0)),
            pl.BlockSpec((MEM_LEN, 2 * XA_DIM), lambda i: (i // bps, 0)),
            pl.BlockSpec((1, k), lambda i: (0, 0)),
            pl.BlockSpec((1, k), lambda i: (0, 0)),
            pl.BlockSpec((None, k, XA_DIM), lambda i: (li, 0, 0)),
            pl.BlockSpec((None, XA_DIM, k), lambda i: (li, 0, 0)),
        ],
        out_specs=pl.BlockSpec((tm, k), lambda i: (i, 0)),
        out_shape=jax.ShapeDtypeStruct((m, k), F32),
        scratch_shapes=[pltpu.VMEM((tm, XA_DIM), BF16)],
        compiler_params=_cparams("parallel"),
        name="memory_cross_attention",
    )(x, kv, g_pre.reshape(1, k), g_post.reshape(1, k), w_q, w_o)


def _ssd_in_kernel(x_ref, xh_ref, g_ref, w_ref, cw_ref, cb_ref, o_ref, h_ref, *, blocks_per_seq, tn):
    i = pl.program_id(0)
    j = pl.program_id(1)
    z_tiles = SSD_D_INNER // tn
    conv_tiles = (SSD_D_INNER + 2 * SSD_BC_DIM) // tn

    @pl.when(j == 0)
    def _():
        keep = (i % blocks_per_seq != 0).astype(F32)
        h_ref[0:SSD_HALO, :] = (_rms(xh_ref[...], g_ref[...]) * keep).astype(BF16)
        h_ref[SSD_HALO:, :] = _rms(x_ref[...], g_ref[...]).astype(BF16)

    @pl.when(j < z_tiles)
    def _():
        o_ref[...] = _silu(_dot(h_ref[SSD_HALO:, :], w_ref[...]))

    @pl.when((j >= z_tiles) & (j < z_tiles + conv_tiles))
    def _():
        u = _dot(h_ref[...], w_ref[...])
        y = cb_ref[...] + u[SSD_HALO:] * cw_ref[SSD_CONV - 1:SSD_CONV, :]
        for tap in range(SSD_CONV - 1):
            back = SSD_CONV - 1 - tap
            y = y + pltpu.roll(u, back, axis=0)[SSD_HALO:] * cw_ref[tap:tap + 1, :]
        o_ref[...] = _silu(y)

    @pl.when(j >= z_tiles + conv_tiles)
    def _():
        o_ref[...] = _dot(h_ref[SSD_HALO:, :], w_ref[...])


def _ssd_in_proj(x, g, w_in, li, conv_w, conv_b, seq_len, *, tm=1024, tn=512):
    m, k = x.shape
    n = w_in.shape[2]
    tm = min(tm, seq_len)
    assert m % tm == 0 and seq_len % tm == 0 and n % tn == 0
    z_tiles = SSD_D_INNER // tn
    conv_tiles = (SSD_D_INNER + 2 * SSD_BC_DIM) // tn
    halo_blocks = tm // SSD_HALO
    conv_tile = lambda j: jnp.clip(j - z_tiles, 0, conv_tiles - 1)
    return pl.pallas_call(
        functools.partial(_ssd_in_kernel, blocks_per_seq=seq_len // tm, tn=tn),
        grid=(m // tm, n // tn),
        in_specs=[
            pl.BlockSpec((tm, k), lambda i, j: (i, 0)),
            pl.BlockSpec((SSD_HALO, k), lambda i, j: (jnp.maximum(i * halo_blocks - 1, 0), 0)),
            pl.BlockSpec((1, k), lambda i, j: (0, 0)),
            pl.BlockSpec((None, k, tn), lambda i, j: (li, 0, j)),
            pl.BlockSpec((SSD_CONV, tn), lambda i, j: (0, conv_tile(j))),
            pl.BlockSpec((1, tn), lambda i, j: (0, conv_tile(j))),
        ],
        out_specs=pl.BlockSpec((tm, tn), lambda i, j: (i, j)),
        out_shape=jax.ShapeDtypeStruct((m, n), F32),
        scratch_shapes=[pltpu.VMEM((tm + SSD_HALO, k), BF16)],
        compiler_params=_cparams("parallel", "arbitrary"),
        name="ssd_in_proj_conv",
    )(x, x, g.reshape(1, k), w_in, conv_w, conv_b.reshape(1, -1))


def _ssd_kernel(zs_ref, xs_ref, bm_ref, cm_ref, dtr_ref, dtb_ref, alog_ref,
                dskip_ref, ng_ref, expand_ref, tril_ref, o_ref,
                wx_ref, expax_ref, cdx_ref, acsg_ref, acst_ref, state_ref, y_ref):
    q = SSD_CHUNK
    c = pl.program_id(1)

    @pl.when(c == 0)
    def _():
        state_ref[...] = jnp.zeros(state_ref.shape, F32)

    dt = _softplus(dtr_ref[...] + dtb_ref[...])
    a = dt * (-jnp.exp(alog_ref[...]))
    a_cs = _dot_f32_rhs(tril_ref[...], a)
    a_last = a_cs[q - 1:q, :]
    expand = expand_ref[...]
    wx_ref[...] = _dot_2piece_lhs(dt * jnp.exp(a_last - a_cs), expand)
    expax_ref[...] = _dot_2piece_lhs(jnp.exp(a_cs), expand)
    cdx_ref[...] = _dot_2piece_lhs(jnp.broadcast_to(jnp.exp(a_last), (8, LANE)), expand)
    acst_ref[...] = (a_cs - jnp.log(dt)).T.reshape(LANE // SSD_HPG, SSD_HPG, q)
    for g in range(SSD_N_GROUPS):
        acsg_ref[g] = a_cs if g == 0 else pltpu.roll(a_cs, LANE - g * SSD_HPG, axis=1)

    row = lax.broadcasted_iota(jnp.int32, (q, q), 0)
    col = lax.broadcasted_iota(jnp.int32, (q, q), 1)
    causal = row >= col
    left_half = col < SSD_HEAD_DIM

    def group_body(g, carry):
        off = pl.multiple_of(g * SSD_GROUP_W, SSD_GROUP_W)
        offn = pl.multiple_of(g * SSD_D_STATE, SSD_D_STATE)
        cm = cm_ref[:, pl.ds(offn, SSD_D_STATE)]
        bm = bm_ref[:, pl.ds(offn, SSD_D_STATE)]
        cmb = cm.astype(BF16)
        cb = _dot_nt(cmb, bm.astype(BF16))
        xs = xs_ref[:, pl.ds(off, SSD_GROUP_W)]
        st = state_ref[g]
        y_off = _dot(cmb, st.astype(BF16)) * expax_ref[:, pl.ds(off, SSD_GROUP_W)]
        s_new = _dot(bm.T.astype(BF16), (xs * wx_ref[:, pl.ds(off, SSD_GROUP_W)]).astype(BF16))
        state_ref[g] = st * cdx_ref[0:1, pl.ds(off, SSD_GROUP_W)] + s_new
        acs_g = acsg_ref[g]
        acst_g = acst_ref[g]
        skip = xs * dskip_ref[:, pl.ds(off, SSD_GROUP_W)]
        for pr in range(SSD_HPG // 2):
            scs = []
            for e in range(2):
                j = 2 * pr + e
                diff = acs_g[:, j:j + 1] - acst_g[j:j + 1, :]
                dec = jnp.where(causal, jnp.exp(diff), 0.0)
                scs.append((cb * dec).astype(BF16))
            xp = xs[:, pr * LANE:(pr + 1) * LANE]
            x0 = jnp.where(left_half, xp, 0.0).astype(BF16)
            x1 = jnp.where(left_half, 0.0, xp).astype(BF16)
            y_diag = _dot(jnp.concatenate(scs, axis=1), jnp.concatenate([x0, x1], axis=0))
            y = y_diag + y_off[:, pr * LANE:(pr + 1) * LANE] + skip[:, pr * LANE:(pr + 1) * LANE]
            y_ref[:, pl.ds(pl.multiple_of(off + pr * LANE, LANE), LANE)] = y
        return carry

    lax.fori_loop(0, SSD_N_GROUPS, group_body, 0, unroll=8)

    gated = y_ref[...] * zs_ref[...]
    o_ref[...] = _rms(gated, ng_ref[...]).astype(o_ref.dtype)


def _ssd_core(proj, dt_bias, a_log, d_skip, norm_g, bsz, seq_len):
    q = SSD_CHUNK
    nc = seq_len // q
    m = bsz * seq_len
    pad = LANE - SSD_N_HEADS
    dtb = jnp.pad(dt_bias, (0, pad)).reshape(1, LANE)
    alog = jnp.pad(a_log, (0, pad)).reshape(1, LANE)
    dskip = jnp.repeat(d_skip, SSD_HEAD_DIM).reshape(1, SSD_D_INNER)
    heads = np.arange(2 * LANE)[:, None] % LANE
    chans = np.arange(SSD_D_INNER)[None, :] // SSD_HEAD_DIM
    expand = jnp.asarray((heads == chans).astype(np.float32), dtype=BF16)
    tril = jnp.asarray(np.tril(np.ones((q, q), np.float32)), dtype=BF16)
    row = lambda b, c: b * nc + c
    const = lambda b, c: (0, 0)
    return pl.pallas_call(
        _ssd_kernel,
        grid=(bsz, nc),
        in_specs=[
            pl.BlockSpec((q, SSD_D_INNER), lambda b, c: (row(b, c), 0)),
            pl.BlockSpec((q, SSD_D_INNER), lambda b, c: (row(b, c), 1)),
            pl.BlockSpec((q, SSD_BC_DIM), lambda b, c: (row(b, c), 8)),
            pl.BlockSpec((q, SSD_BC_DIM), lambda b, c: (row(b, c), 9)),
            pl.BlockSpec((q, LANE), lambda b, c: (row(b, c), 80)),
            pl.BlockSpec((1, LANE), const),
            pl.BlockSpec((1, LANE), const),
            pl.BlockSpec((1, SSD_D_INNER), const),
            pl.BlockSpec((1, SSD_D_INNER), const),
            pl.BlockSpec((2 * LANE, SSD_D_INNER), const),
            pl.BlockSpec((q, q), const),
        ],
        out_specs=pl.BlockSpec((q, SSD_D_INNER), lambda b, c: (row(b, c), 0)),
        out_shape=jax.ShapeDtypeStruct((m, SSD_D_INNER), BF16),
        scratch_shapes=[
            pltpu.VMEM((q, SSD_D_INNER), F32),
            pltpu.VMEM((q, SSD_D_INNER), F32),
            pltpu.VMEM((8, SSD_D_INNER), F32),
            pltpu.VMEM((SSD_N_GROUPS, q, LANE), F32),
            pltpu.VMEM((LANE // SSD_HPG, SSD_HPG, q), F32),
            pltpu.VMEM((SSD_N_GROUPS, SSD_D_STATE, SSD_GROUP_W), F32),
            pltpu.VMEM((q, SSD_D_INNER), F32),
        ],
        compiler_params=_cparams("parallel", "arbitrary"),
        name="ssd_core",
    )(proj, proj, proj, proj, proj, dtb, alog, dskip, norm_g.reshape(1, SSD_D_INNER), expand, tril)


def _nsa_compress_kernel(t_ref, pos_ref, w1_ref, w2_ref, o_ref, *, nchunk):
    half = NSA_CMP_STRIDE
    dh = NSA_HEAD_DIM
    p_lo = jnp.zeros((nchunk, NSA_CMP_HIDDEN), F32)
    p_hi = jnp.zeros((nchunk, NSA_CMP_HIDDEN), F32)
    for l in range(half):
        rows = t_ref[pl.ds(l, nchunk, stride=half), :]
        a_lo = (rows + pos_ref[l:l + 1, :]).astype(BF16)
        a_hi = (rows + pos_ref[half + l:half + l + 1, :]).astype(BF16)
        p_lo = p_lo + _dot(a_lo, w1_ref[l * dh:(l + 1) * dh, :].astype(BF16))
        p_hi = p_hi + _dot(a_hi, w1_ref[(half + l) * dh:(half + l + 1) * dh, :].astype(BF16))
    pre = p_lo + pltpu.roll(p_hi, nchunk - 1, axis=0)
    o_ref[...] = _dot(_gelu(pre).astype(BF16), w2_ref[...].astype(BF16))


def _nsa_compress(proj, cmp_pos, cmp_w1, cmp_w2, bsz, seq_len):
    nchunk = seq_len // NSA_CMP_STRIDE
    g, dh = NSA_KV_GROUPS, NSA_HEAD_DIM
    col0 = NSA_N_HEADS * dh // dh
    return pl.pallas_call(
        functools.partial(_nsa_compress_kernel, nchunk=nchunk),
        grid=(bsz, 2, g),
        in_specs=[
            pl.BlockSpec((seq_len, dh), lambda b, s, gi: (b, col0 + s * g + gi)),
            pl.BlockSpec((None, NSA_CMP_BLOCK, dh), lambda b, s, gi: (s, 0, 0)),
            pl.BlockSpec((None, NSA_CMP_BLOCK * dh, NSA_CMP_HIDDEN), lambda b, s, gi: (s, 0, 0)),
            pl.BlockSpec((None, NSA_CMP_HIDDEN, dh), lambda b, s, gi: (s, 0, 0)),
        ],
        out_specs=pl.BlockSpec((None, None, None, nchunk, dh), lambda b, s, gi: (b, s, gi, 0, 0)),
        out_shape=jax.ShapeDtypeStruct((bsz, 2, g, nchunk, dh), F32),
        compiler_params=_cparams("parallel", "parallel", "parallel"),
        name="nsa_compress",
    )(proj, cmp_pos, cmp_w1, cmp_w2)


def _nsa_select_kernel(q_ref, kc_ref, vc_ref, cover_ref, ocmp_ref, sel_ref, *, ncmp, nslc):
    tq = NSA_SEL_TQ
    dh = NSA_HEAD_DIM
    scale = dh ** -0.5
    nck = kc_ref.shape[0]
    t0 = pl.program_id(2) * tq
    t_col = t0 + lax.broadcasted_iota(jnp.int32, (tq, 1), 0)
    ci_row = lax.broadcasted_iota(jnp.int32, (1, nck), 1)
    valid = (ci_row * NSA_CMP_STRIDE + (NSA_CMP_BLOCK - 1) <= t_col) & (ci_row < ncmp)
    t_row = t0 + lax.broadcasted_iota(jnp.int32, (1, tq), 1)
    ci_col = lax.broadcasted_iota(jnp.int32, (nck, 1), 0)
    valid_t = (ci_col * NSA_CMP_STRIDE + (NSA_CMP_BLOCK - 1) <= t_row) & (ci_col < ncmp)
    c = scale * math.log2(math.e)
    bias = jnp.where(valid, 0.0, NEG)
    bias_t = jnp.where(valid_t, 0.0, NEG)
    some = jnp.where(t_col >= NSA_CMP_BLOCK - 1, 1.0, 0.0)
    some_t = jnp.where(t_row >= NSA_CMP_BLOCK - 1, 1.0, 0.0)
    kcb = kc_ref[...].astype(BF16)
    vcb = vc_ref[...].astype(BF16)
    psum_t = jnp.zeros((nck, tq), F32)
    for j in range(NSA_HPG):
        qj = q_ref[:, j * dh:(j + 1) * dh].astype(BF16)
        s = _dot_nt(qj, kcb) * c + bias
        e = jnp.exp2(s - jnp.max(s, axis=-1, keepdims=True))
        p = e * (some / jnp.sum(e, axis=-1, keepdims=True))
        ocmp_ref[:, j * dh:(j + 1) * dh] = _dot(p.astype(BF16), vcb)
        st = _dot_nt(kcb, qj) * c + bias_t
        et = jnp.exp2(st - jnp.max(st, axis=0, keepdims=True))
        psum_t = psum_t + et * (some_t / jnp.sum(et, axis=0, keepdims=True))
    imp_t = _dot_f32_rhs(cover_ref[...], psum_t)
    blk = lax.broadcasted_iota(jnp.int32, (nslc, tq), 0)
    cur = t_row // NSA_SLC_BLOCK
    forced = (blk == 0) | ((blk <= cur) & (blk > cur - 1 - NSA_N_LOCAL))
    future = blk * NSA_SLC_BLOCK > t_row
    score = jnp.where(forced, BIG, jnp.where(future, NEG, imp_t))
    sub = 8
    slabs = [score[r * sub:(r + 1) * sub, :] for r in range(nslc // sub)]
    ranks = [jnp.zeros((sub, tq), F32) for _ in slabs]
    row_in_slab = lax.broadcasted_iota(jnp.int32, (sub, tq), 0)
    for j in range(nslc):
        cj = jnp.broadcast_to(score[j:j + 1, :], (sub, tq))
        for r, s_r in enumerate(slabs):
            if r * sub > j:
                before = cj >= s_r
            elif (r + 1) * sub - 1 < j:
                before = cj > s_r
            else:
                before = (cj > s_r) | ((cj == s_r) & (row_in_slab > j - r * sub))
            ranks[r] = ranks[r] + jnp.where(before, 1.0, 0.0)
    topk = float(min(NSA_TOPK, nslc))
    for r, rank in enumerate(ranks):
        sel_ref[r * sub:(r + 1) * sub, :] = jnp.where(rank < topk, 1.0, 0.0)


def _nsa_select(proj, kvc, bsz, seq_len):
    tq, dh, g = NSA_SEL_TQ, NSA_HEAD_DIM, NSA_KV_GROUPS
    nq = seq_len // tq
    ncmp = (seq_len - NSA_CMP_BLOCK) // NSA_CMP_STRIDE + 1
    nck = seq_len // NSA_CMP_STRIDE
    nslc = seq_len // NSA_SLC_BLOCK
    ci = np.arange(nck)[:, None] * NSA_CMP_STRIDE
    sj = np.arange(nslc)[None, :] * NSA_SLC_BLOCK
    cover = ((ci <= sj + NSA_SLC_BLOCK - 1) & (ci + NSA_CMP_BLOCK - 1 >= sj)).astype(np.float32)
    cover = jnp.asarray(cover.T, dtype=BF16)
    return pl.pallas_call(
        functools.partial(_nsa_select_kernel, ncmp=ncmp, nslc=nslc),
        grid=(bsz, g, nq),
        in_specs=[
            pl.BlockSpec((tq, NSA_HPG * dh), lambda b, gi, qi: (b * nq + qi, gi)),
            pl.BlockSpec((None, None, None, nck, dh), lambda b, gi, qi: (b, 0, gi, 0, 0)),
            pl.BlockSpec((None, None, None, nck, dh), lambda b, gi, qi: (b, 1, gi, 0, 0)),
            pl.BlockSpec((nslc, nck), lambda b, gi, qi: (0, 0)),
        ],
        out_specs=[
            pl.BlockSpec((tq, NSA_HPG * dh), lambda b, gi, qi: (b * nq + qi, gi)),
            pl.BlockSpec((None, None, nslc, tq), lambda b, gi, qi: (b, gi, 0, qi)),
        ],
        out_shape=[
            jax.ShapeDtypeStruct((bsz * seq_len, NSA_N_HEADS * dh), F32),
            jax.ShapeDtypeStruct((bsz, g, nslc, seq_len), F32),
        ],
        compiler_params=_cparams("parallel", "parallel", "parallel"),
        name="nsa_cmp_select",
    )(proj, kvc, kvc, cover)


def _nsa_attn_kernel(q_ref, ks_ref, vs_ref, kw_ref, vw_ref, sel_ref, expand_ref, ocmp_ref, gl_ref,
                     o_ref, ksb_ref, vsx_ref, kwb_ref, vwx_ref, selb_ref, q4_ref, pw_ref, m_ref, acc_ref):
    tq = NSA_TQ
    dh = NSA_HEAD_DIM
    kt_w = NSA_KEY_TILE
    win_w = NSA_WINDOW + tq
    c = (dh ** -0.5) * math.log2(math.e)
    qi = pl.program_id(2)

    @pl.when(qi == 0)
    def _():
        ones = jnp.ones((ks_ref.shape[0], dh), BF16)
        for blk in range(ks_ref.shape[0] // LANE):
            cols = slice(blk * LANE, (blk + 1) * LANE)
            ksb_ref[:, cols] = ks_ref[cols, :].T.astype(BF16)
            kwb_ref[:, cols] = kw_ref[cols, :].T.astype(BF16)
        vsx_ref[:, 0:dh] = vs_ref[...].astype(BF16)
        vsx_ref[:, dh:] = ones
        vwx_ref[:, 0:dh] = vw_ref[...].astype(BF16)
        vwx_ref[:, dh:] = ones

    selb_ref[...] = sel_ref[...].T.astype(BF16)
    for j in range(NSA_HPG):
        q4_ref[j * tq:(j + 1) * tq, :] = q_ref[:, j * dh:(j + 1) * dh].astype(BF16)
    t1 = qi * tq + lax.broadcasted_iota(jnp.int32, (tq, 1), 0)
    m_ref[...] = jnp.full(m_ref.shape, NEG, F32)
    acc_ref[...] = jnp.zeros(acc_ref.shape, F32)
    half = NSA_HPG // 2 * tq

    def slc_tile(off, width, on_diagonal):
        chosen = _dot(selb_ref[...], expand_ref[:, pl.ds(off, width)])
        bias = (chosen - 1.0) * BIG
        if on_diagonal:
            pos = off + lax.broadcasted_iota(jnp.int32, (1, width), 1)
            bias = jnp.where(pos <= t1, bias, NEG)
        k_t = ksb_ref[:, pl.ds(off, width)]
        v_x = vsx_ref[pl.ds(off, width), :]
        pairs = [slice(hp * half, (hp + 1) * half) for hp in range(2)]
        scores = [_dot(q4_ref[pr], k_t) for pr in pairs]
        for hp, pr in enumerate(pairs):
            m_prev = m_ref[pr]
            ps, m_news = [], []
            for e in range(NSA_HPG // 2):
                sj = scores[hp][e * tq:(e + 1) * tq] + bias
                m_new = jnp.maximum(m_prev[e * tq:(e + 1) * tq], jnp.max(sj, axis=-1, keepdims=True))
                m_wide = jnp.concatenate([m_new] * (width // LANE), axis=1)
                ps.append(jnp.exp2((sj - m_wide) * c).astype(BF16))
                m_news.append(m_new)
            m_new = jnp.concatenate(m_news, axis=0)
            alpha = jnp.exp2((m_prev - m_new) * c)
            m_ref[pr] = m_new
            pv = _dot(jnp.concatenate(ps, axis=0), v_x)
            acc_ref[pr] = jnp.concatenate([alpha, alpha], axis=1) * acc_ref[pr] + pv

    n_wide = (qi * tq) // kt_w

    def wide_step(kt, carry):
        slc_tile(pl.multiple_of(kt * kt_w, kt_w), kt_w, False)
        return carry

    def narrow_step(s, carry):
        slc_tile(pl.multiple_of(n_wide * kt_w + s * tq, tq), tq, True)
        return carry

    lax.fori_loop(0, n_wide, wide_step, 0)
    lax.fori_loop(0, qi + 1 - n_wide * (kt_w // tq), narrow_step, 0)

    w0 = pl.multiple_of(jnp.maximum(qi * tq - NSA_WINDOW, 0), tq)
    sw = _dot(q4_ref[...], kwb_ref[:, pl.ds(w0, win_w)])
    diff = t1 - (w0 + lax.broadcasted_iota(jnp.int32, (1, win_w), 1))
    bias_w = jnp.where((diff >= 0) & (diff < NSA_WINDOW), 0.0, NEG)
    for j in range(NSA_HPG):
        rows = slice(j * tq, (j + 1) * tq)
        sj = sw[rows] + bias_w
        pw_ref[rows] = jnp.exp2((sj - jnp.max(sj, axis=-1, keepdims=True)) * c).astype(BF16)
    ow = _dot(pw_ref[...], vwx_ref[pl.ds(w0, win_w), :])

    gates = _sigmoid(gl_ref[...])
    for j in range(NSA_HPG):
        rows = slice(j * tq, (j + 1) * tq)
        o_slc = acc_ref[rows, 0:dh] / acc_ref[rows, dh:]
        o_win = ow[rows, 0:dh] / ow[rows, dh:]
        o = (gates[:, 3 * j:3 * j + 1] * ocmp_ref[:, j * dh:(j + 1) * dh]
             + gates[:, 3 * j + 1:3 * j + 2] * o_slc
             + gates[:, 3 * j + 2:3 * j + 3] * o_win)
        o_ref[:, j * dh:(j + 1) * dh] = o.astype(o_ref.dtype)


def _nsa_attn(proj, sel, ocmp, gate_logits, bsz, seq_len):
    tq, dh, g = NSA_TQ, NSA_HEAD_DIM, NSA_KV_GROUPS
    assert seq_len % NSA_KEY_TILE == 0 and seq_len >= NSA_WINDOW + tq
    rows = NSA_HPG * tq
    nq = seq_len // tq
    nslc = seq_len // NSA_SLC_BLOCK
    expand = (np.arange(nslc)[:, None] == np.arange(seq_len)[None, :] // NSA_SLC_BLOCK)
    expand = jnp.asarray(expand.astype(np.float32), dtype=BF16)
    kv0 = (NSA_N_HEADS * dh + 2 * NSA_KV_DIM) // dh

    def kv_spec(idx):
        return pl.BlockSpec((seq_len, dh), lambda b, gi, qi: (b, kv0 + idx * g + gi))

    qo_spec = pl.BlockSpec((tq, NSA_HPG * dh), lambda b, gi, qi: (b * nq + qi, gi))
    return pl.pallas_call(
        _nsa_attn_kernel,
        grid=(bsz, g, nq),
        in_specs=[
            qo_spec,
            kv_spec(0), kv_spec(1), kv_spec(2), kv_spec(3),
            pl.BlockSpec((None, None, nslc, tq), lambda b, gi, qi: (b, gi, 0, qi)),
            pl.BlockSpec((nslc, seq_len), lambda b, gi, qi: (0, 0)),
            qo_spec,
            pl.BlockSpec((None, tq, 3 * NSA_HPG), lambda b, gi, qi: (gi, b * nq + qi, 0)),
        ],
        out_specs=qo_spec,
        out_shape=jax.ShapeDtypeStruct((bsz * seq_len, NSA_N_HEADS * dh), BF16),
        scratch_shapes=[
            pltpu.VMEM((dh, seq_len), BF16),
            pltpu.VMEM((seq_len, 2 * dh), BF16),
            pltpu.VMEM((dh, seq_len), BF16),
            pltpu.VMEM((seq_len, 2 * dh), BF16),
            pltpu.VMEM((tq, nslc), BF16),
            pltpu.VMEM((rows, dh), BF16),
            pltpu.VMEM((rows, NSA_WINDOW + tq), BF16),
            pltpu.VMEM((rows, LANE), F32),
            pltpu.VMEM((rows, 2 * dh), F32),
        ],
        compiler_params=_cparams("parallel", "parallel", "arbitrary"),
        name="nsa_slc_win_attention",
    )(proj, proj, proj, proj, proj, sel, expand, ocmp, gate_logits)


def _sgu_kernel(u_ref, v_ref, g_ref, ws_ref, bst_ref, o_ref):
    v = v_ref[...].astype(F32)
    mu = jnp.mean(v, axis=-1, keepdims=True)
    vc = v - mu
    vn = (vc * lax.rsqrt(jnp.mean(vc * vc, axis=-1, keepdims=True) + RMS_EPS) * g_ref[...]).astype(BF16)
    q = SGU_CHUNK
    tri = lax.broadcasted_iota(jnp.int32, (q, q), 0) >= lax.broadcasted_iota(jnp.int32, (q, q), 1)
    for g in range(SGU_GROUPS):
        cols = slice(g * SGU_GROUP_DIM, (g + 1) * SGU_GROUP_DIM)
        w_m = jnp.where(tri, ws_ref[g], 0.0).astype(BF16)
        sv = _dot(w_m, vn[:, cols]) + bst_ref[:, g:g + 1]
        o_ref[:, cols] = (u_ref[:, cols].astype(F32) * sv).astype(o_ref.dtype)


def _sgu_core(proj, ln_g, w_spatial, b_spatial):
    m = proj.shape[0]
    q = SGU_CHUNK
    return pl.pallas_call(
        _sgu_kernel,
        grid=(m // q,),
        in_specs=[
            pl.BlockSpec((q, SGU_WIDTH), lambda i: (i, 0)),
            pl.BlockSpec((q, SGU_WIDTH), lambda i: (i, 1)),
            pl.BlockSpec((1, SGU_WIDTH), lambda i: (0, 0)),
            pl.BlockSpec((SGU_GROUPS, q, q), lambda i: (0, 0, 0)),
            pl.BlockSpec((q, SGU_GROUPS), lambda i: (0, 0)),
        ],
        out_specs=pl.BlockSpec((q, SGU_WIDTH), lambda i: (i, 0)),
        out_shape=jax.ShapeDtypeStruct((m, SGU_WIDTH), BF16),
        compiler_params=_cparams("parallel"),
        name="sgu_core",
    )(proj, proj, ln_g.reshape(1, SGU_WIDTH), w_spatial, b_spatial.T)


def _pool_kernel(z_ref, zh_ref, wg_ref, sc_ref, o_ref, zz_ref, *, tm, blocks_per_seq):
    i = pl.program_id(0)
    keep = (i % blocks_per_seq != 0).astype(F32)
    zz_ref[0:POOL_HALO, :] = zh_ref[...] * keep
    zz_ref[POOL_HALO:, :] = z_ref[...]
    t = (i % blocks_per_seq) * tm + lax.broadcasted_iota(jnp.int32, (tm, 1), 0)
    for gi, win in enumerate(POOL_WINDOWS):
        cols = slice(gi * POOL_GROUP_DIM, (gi + 1) * POOL_GROUP_DIM)
        s = zz_ref[POOL_HALO:POOL_HALO + tm, cols]
        for k in range(1, win):
            s = s + zz_ref[POOL_HALO - k:POOL_HALO - k + tm, cols]
        count = jnp.minimum(t + 1, win).astype(F32)
        pooled = s / count - z_ref[:, cols]
        y = _dot(pooled.astype(BF16), wg_ref[gi]) * sc_ref[:, cols]
        o_ref[:, cols] = y.astype(o_ref.dtype)


def _pool_core(z, w_group, scale, seq_len, *, tm=512):
    m, n = z.shape
    tm = min(tm, seq_len)
    halo_blocks = tm // POOL_HALO
    return pl.pallas_call(
        functools.partial(_pool_kernel, tm=tm, blocks_per_seq=seq_len // tm),
        grid=(m // tm,),
        in_specs=[
            pl.BlockSpec((tm, n), lambda i: (i, 0)),
            pl.BlockSpec((POOL_HALO, n), lambda i: (jnp.maximum(i * halo_blocks - 1, 0), 0)),
            pl.BlockSpec(w_group.shape, lambda i: (0, 0, 0)),
            pl.BlockSpec((1, n), lambda i: (0, 0)),
        ],
        out_specs=pl.BlockSpec((tm, n), lambda i: (i, 0)),
        out_shape=jax.ShapeDtypeStruct((m, n), BF16),
        scratch_shapes=[pltpu.VMEM((tm + POOL_HALO, n), F32)],
        compiler_params=_cparams("parallel"),
        name="pool_core",
    )(z, z, w_group, scale.reshape(1, n))


def _pad_cols(w, n):
    return jnp.pad(w, ((0, 0), (0, 0), (0, n - w.shape[2])))


def _ssd_layer(x, g_pre, g_post, w_in, j, conv_w, conv_b, dt_bias, a_log, d_skip, norm_g, w_out, bsz, seq_len):
    proj = _ssd_in_proj(x, g_pre, w_in, j, conv_w, conv_b, seq_len)
    y = _ssd_core(proj, dt_bias, a_log, d_skip, norm_g, bsz, seq_len)
    return _matmul_post(y, w_out, j, g_post, x)


def _nsa_layer(x, g_pre, g_post, w_in, j, cmp_pos, cmp_w1, cmp_w2, w_out, bsz, seq_len):
    proj = _norm_matmul(x, g_pre, w_in, j, jnp.zeros((NSA_IN_PAD,), F32))
    kvc = _nsa_compress(proj, cmp_pos, cmp_w1, cmp_w2, bsz, seq_len)
    ocmp, sel = _nsa_select(proj, kvc, bsz, seq_len)
    ngate = 3 * NSA_N_HEADS
    gl = proj[:, NSA_GATE_COL:NSA_GATE_COL + ngate].reshape(-1, NSA_KV_GROUPS, 3 * NSA_HPG).transpose(1, 0, 2)
    o = _nsa_attn(proj, sel, ocmp, gl, bsz, seq_len)
    return _matmul_post(o, w_out, j, g_post, x)


def _sgu_layer(x, g_pre, g_post, w_in, j, b_in, ln_g, w_spatial, b_spatial, w_out):
    proj = _norm_matmul(x, g_pre, w_in, j, b_in, act="gelu", out_dtype=BF16)
    y = _sgu_core(proj, ln_g, w_spatial, b_spatial)
    return _matmul_post(y, w_out, j, g_post, x)


def _pool_layer(x, g_pre, g_post, w_in, j, w_group, scale, w_out, seq_len):
    z = _norm_matmul(x, g_pre, w_in, j, jnp.zeros((D_MODEL,), F32))
    y = _pool_core(z, w_group.astype(BF16), scale, seq_len)
    return _matmul_post(y, w_out, j, g_post, x)


def kernel(x, mem, norm_pre, norm_post, norm_mem, ssd_w_in, ssd_conv_w, ssd_conv_b, ssd_dt_bias, ssd_a_log, ssd_d, ssd_norm_g, ssd_w_out, nsa_w_in, nsa_cmp_pos, nsa_cmp_w1, nsa_cmp_w2, nsa_w_out, sgu_w_in, sgu_b_in, sgu_ln_g, sgu_w_spatial, sgu_b_spatial, sgu_w_out, pool_w_in, pool_w_group, pool_scale, pool_w_out, xa_w_q, xa_w_kv, xa_w_o, ffn_w_up, ffn_conv_w, ffn_conv_b, ffn_w_down):
    bsz, seq_len, d = x.shape
    depth = norm_pre.shape[0]
    xf = x.reshape(bsz * seq_len, d)
    memf = mem.reshape(bsz * mem.shape[1], d)
    ssd_in, ssd_out = _pad_cols(ssd_w_in, SSD_IN_PAD).astype(BF16), ssd_w_out.astype(BF16)
    nsa_in, nsa_out = _pad_cols(nsa_w_in, NSA_IN_PAD).astype(BF16), nsa_w_out.astype(BF16)
    sgu_in, sgu_out = sgu_w_in.astype(BF16), sgu_w_out.astype(BF16)
    pool_in, pool_out = pool_w_in.astype(BF16), pool_w_out.astype(BF16)
    xa_q, xa_kv, xa_o = xa_w_q.astype(BF16), xa_w_kv.astype(BF16), xa_w_o.astype(BF16)
    ffn_up, ffn_down = ffn_w_up.astype(BF16), ffn_w_down.astype(BF16)
    for i in range(depth):
        kind, j = i % 4, i // 4
        if kind == 0:
            xf = _ssd_layer(xf, norm_pre[i, 0], norm_post[i, 0], ssd_in, j, ssd_conv_w[j], ssd_conv_b[j],
                            ssd_dt_bias[j], ssd_a_log[j], ssd_d[j], ssd_norm_g[j], ssd_out, bsz, seq_len)
        elif kind == 1:
            xf = _nsa_layer(xf, norm_pre[i, 0], norm_post[i, 0], nsa_in, j, nsa_cmp_pos[j], nsa_cmp_w1[j],
                            nsa_cmp_w2[j], nsa_out, bsz, seq_len)
        elif kind == 2:
            xf = _sgu_layer(xf, norm_pre[i, 0], norm_post[i, 0], sgu_in, j, sgu_b_in[j], sgu_ln_g[j],
                            sgu_w_spatial[j], sgu_b_spatial[j], sgu_out)
        else:
            xf = _pool_layer(xf, norm_pre[i, 0], norm_post[i, 0], pool_in, j, pool_w_group[j],
                             pool_scale[j], pool_out, seq_len)
        kv = _norm_matmul(memf, norm_mem[i], xa_kv, i, jnp.zeros((2 * XA_DIM,), F32),
                          out_dtype=BF16, tm=MEM_LEN)
        xf = _xattn(xf, kv, norm_pre[i, 1], norm_post[i, 1], xa_q, xa_o, i, seq_len)
        act = _ffn_up(xf, norm_pre[i, 2], ffn_up, i, ffn_conv_w[i], ffn_conv_b[i], seq_len)
        xf = _matmul_post(act, ffn_down, i, norm_post[i, 2], xf)
    return xf.reshape(bsz, seq_len, d)
```

```python
import functools
import math

import jax
import jax.numpy as jnp
import numpy as np
from jax import lax
from jax.experimental import pallas as pl
from jax.experimental.pallas import tpu as pltpu

F32 = jnp.float32
BF16 = jnp.bfloat16

D_MODEL = 2048
RMS_EPS = 1e-6
NEG = -1e30
BIG = 1e30
MEM_LEN = 256

SSD_D_INNER = 4096
SSD_HEAD_DIM = 64
SSD_N_HEADS = 64
SSD_N_GROUPS = 8
SSD_HPG = 8
SSD_D_STATE = 128
SSD_CONV = 4
SSD_HALO = 16
SSD_CHUNK = 128
SSD_GROUP_W = SSD_HPG * SSD_HEAD_DIM
SSD_BC_DIM = SSD_N_GROUPS * SSD_D_STATE
SSD_IN_PAD = 10752

NSA_HEAD_DIM = 128
NSA_N_HEADS = 16
NSA_KV_GROUPS = 4
NSA_HPG = 4
NSA_KV_DIM = 512
NSA_CMP_BLOCK = 32
NSA_CMP_STRIDE = 16
NSA_CMP_HIDDEN = 256
NSA_SLC_BLOCK = 64
NSA_TOPK = 16
NSA_N_LOCAL = 2
NSA_WINDOW = 512
NSA_TQ = 256
NSA_KEY_TILE = 1024
NSA_SEL_TQ = 256
NSA_IN_PAD = 5632
NSA_GATE_COL = 5120

SGU_CHUNK = 128
SGU_WIDTH = 4096
SGU_GROUPS = 8
SGU_GROUP_DIM = 512

POOL_WINDOWS = (2, 4, 8, 16)
POOL_GROUP_DIM = 512
POOL_HALO = 16

XA_HEADS = 4
XA_HEAD_DIM = 128
XA_DIM = 512

FFN_HIDDEN = 5632
FFN_CONV = 3
FFN_HALO = 16

LANE = 128
BF16_SUBLANES = 16
VMEM_LIMIT = 56 * 1024 * 1024


def _cparams(*sem):
    return pltpu.CompilerParams(dimension_semantics=sem, vmem_limit_bytes=VMEM_LIMIT)


def _dot(a, b):
    return jnp.dot(a, b, preferred_element_type=F32)


def _dot_nt(a, b):
    return lax.dot_general(a, b, (((1,), (1,)), ((), ())), preferred_element_type=F32)


def _split3(v):
    hi = v.astype(BF16)
    r1 = v - hi.astype(F32)
    mid = r1.astype(BF16)
    lo = (r1 - mid.astype(F32)).astype(BF16)
    return hi, mid, lo


def _dot_f32_lhs(v, e):
    hi, mid, lo = _split3(v)
    return _dot(hi, e) + _dot(mid, e) + _dot(lo, e)


def _dot_2piece_lhs(v, e2):
    hi = v.astype(BF16)
    mid = (v - hi.astype(F32)).astype(BF16)
    return _dot(jnp.concatenate([hi, mid], axis=1), e2)


def _dot_f32_rhs(e, v):
    hi, mid, lo = _split3(v)
    return _dot(e, hi) + _dot(e, mid) + _dot(e, lo)


def _rms(x, g):
    ms = jnp.mean(x * x, axis=-1, keepdims=True)
    return x * lax.rsqrt(ms + RMS_EPS) * g


def _sigmoid(x):
    return 1.0 / (1.0 + jnp.exp(-x))


def _silu(x):
    return x * _sigmoid(x)


def _gelu(x):
    c = math.sqrt(2.0 / math.pi)
    return x * (0.5 * (1.0 + jnp.tanh(c * (x + 0.044715 * (x * x * x)))))


def _softplus(x):
    return jnp.maximum(x, 0.0) + jnp.log1p(jnp.exp(-jnp.abs(x)))


def _norm_mm_kernel(x_ref, g_ref, w_ref, b_ref, o_ref, h_ref, *, act):
    @pl.when(pl.program_id(1) == 0)
    def _():
        h_ref[...] = _rms(x_ref[...], g_ref[...]).astype(BF16)

    y = _dot(h_ref[...], w_ref[...]) + b_ref[...]
    if act == "gelu":
        y = _gelu(y)
    o_ref[...] = y.astype(o_ref.dtype)


def _norm_matmul(x, g, w, li, b, *, act=None, out_dtype=F32, tm=1024, tn=512):
    m, k = x.shape
    n = w.shape[2]
    tm = min(tm, m)
    assert m % tm == 0 and n % tn == 0, (m, n, tm, tn)
    return pl.pallas_call(
        functools.partial(_norm_mm_kernel, act=act),
        grid=(m // tm, n // tn),
        in_specs=[
            pl.BlockSpec((tm, k), lambda i, j: (i, 0)),
            pl.BlockSpec((1, k), lambda i, j: (0, 0)),
            pl.BlockSpec((None, k, tn), lambda i, j: (li, 0, j)),
            pl.BlockSpec((1, tn), lambda i, j: (0, j)),
        ],
        out_specs=pl.BlockSpec((tm, tn), lambda i, j: (i, j)),
        out_shape=jax.ShapeDtypeStruct((m, n), out_dtype),
        scratch_shapes=[pltpu.VMEM((tm, k), BF16)],
        compiler_params=_cparams("parallel", "arbitrary"),
        name="norm_matmul",
    )(x, g.reshape(1, k), w, b.reshape(1, n))


def _mm_post_kernel(a_ref, w_ref, g_ref, r_ref, o_ref, *, nj, tn):
    j = pl.program_id(1)
    o_ref[:, pl.ds(pl.multiple_of(j * tn, tn), tn)] = _dot(a_ref[...], w_ref[...])

    @pl.when(j == nj - 1)
    def _():
        o_ref[...] = r_ref[...] + _rms(o_ref[...], g_ref[...])


def _matmul_post(a, w, li, g, res, *, tm=512, tn=512):
    m, kdim = a.shape
    n = w.shape[2]
    tm = min(tm, m)
    assert m % tm == 0 and n % tn == 0
    nj = n // tn
    return pl.pallas_call(
        functools.partial(_mm_post_kernel, nj=nj, tn=tn),
        grid=(m // tm, nj),
        in_specs=[
            pl.BlockSpec((tm, kdim), lambda i, j: (i, 0)),
            pl.BlockSpec((None, kdim, tn), lambda i, j: (li, 0, j)),
            pl.BlockSpec((1, n), lambda i, j: (0, 0)),
            pl.BlockSpec((tm, n), lambda i, j: (i, 0)),
        ],
        out_specs=pl.BlockSpec((tm, n), lambda i, j: (i, 0)),
        out_shape=jax.ShapeDtypeStruct((m, n), F32),
        compiler_params=_cparams("parallel", "arbitrary"),
        name="matmul_postnorm_residual",
    )(a, w, g.reshape(1, n), res)


def _ffn_up_kernel(x_ref, xh_ref, g_ref, wg_ref, wv_ref, cwg_ref, cwv_ref, cbg_ref, cbv_ref,
                   o_ref, h_ref, *, tm, blocks_per_seq):
    i = pl.program_id(0)

    @pl.when(pl.program_id(1) == 0)
    def _():
        keep = (i % blocks_per_seq != 0).astype(F32)
        h_ref[0:FFN_HALO, :] = (_rms(xh_ref[...], g_ref[...]) * keep).astype(BF16)
        h_ref[FFN_HALO:, :] = _rms(x_ref[...], g_ref[...]).astype(BF16)

    h = h_ref[...]

    def conv(u, cw_ref, cb_ref):
        y = cb_ref[...] + u[FFN_HALO:] * cw_ref[FFN_CONV - 1:FFN_CONV, :]
        for tap in range(FFN_CONV - 1):
            back = FFN_CONV - 1 - tap
            y = y + pltpu.roll(u, back, axis=0)[FFN_HALO:] * cw_ref[tap:tap + 1, :]
        return y

    gate = conv(_dot(h, wg_ref[...]), cwg_ref, cbg_ref)
    val = conv(_dot(h, wv_ref[...]), cwv_ref, cbv_ref)
    o_ref[...] = (_silu(gate) * val).astype(o_ref.dtype)


def _ffn_up(x, g, w_up, li, conv_w, conv_b, seq_len, *, tm=1024, tn=512):
    m, k = x.shape
    tm = min(tm, seq_len)
    assert m % tm == 0 and seq_len % tm == 0 and FFN_HIDDEN % tn == 0
    nj = FFN_HIDDEN // tn
    halo_blocks = tm // FFN_HALO
    return pl.pallas_call(
        functools.partial(_ffn_up_kernel, tm=tm, blocks_per_seq=seq_len // tm),
        grid=(m // tm, nj),
        in_specs=[
            pl.BlockSpec((tm, k), lambda i, j: (i, 0)),
            pl.BlockSpec((FFN_HALO, k), lambda i, j: (jnp.maximum(i * halo_blocks - 1, 0), 0)),
            pl.BlockSpec((1, k), lambda i, j: (0, 0)),
            pl.BlockSpec((None, k, tn), lambda i, j: (li, 0, j)),
            pl.BlockSpec((None, k, tn), lambda i, j: (li, 0, j + nj)),
            pl.BlockSpec((FFN_CONV, tn), lambda i, j: (0, j)),
            pl.BlockSpec((FFN_CONV, tn), lambda i, j: (0, j + nj)),
            pl.BlockSpec((1, tn), lambda i, j: (0, j)),
            pl.BlockSpec((1, tn), lambda i, j: (0, j + nj)),
        ],
        out_specs=pl.BlockSpec((tm, tn), lambda i, j: (i, j)),
        out_shape=jax.ShapeDtypeStruct((m, FFN_HIDDEN), BF16),
        scratch_shapes=[pltpu.VMEM((tm + FFN_HALO, k), BF16)],
        compiler_params=_cparams("parallel", "arbitrary"),
        name="ffn_up_conv_gate",
    )(x, x, g.reshape(1, k), w_up, w_up, conv_w, conv_w,
      conv_b.reshape(1, -1), conv_b.reshape(1, -1))


def _xattn_kernel(x_ref, kv_ref, gpre_ref, gpost_ref, wq_ref, wo_ref, o_ref, att_ref):
    x = x_ref[...]
    h = _rms(x, gpre_ref[...]).astype(BF16)
    q = _dot(h, wq_ref[...]).astype(BF16)
    scale = XA_HEAD_DIM ** -0.5
    for hd in range(XA_HEADS):
        lo = hd * XA_HEAD_DIM
        s = _dot_nt(q[:, lo:lo + XA_HEAD_DIM], kv_ref[:, lo:lo + XA_HEAD_DIM]) * scale
        e = jnp.exp(s - jnp.max(s, axis=-1, keepdims=True))
        p = e / jnp.sum(e, axis=-1, keepdims=True)
        att_ref[:, lo:lo + XA_HEAD_DIM] = _dot(
            p.astype(BF16), kv_ref[:, XA_DIM + lo:XA_DIM + lo + XA_HEAD_DIM]).astype(BF16)
    a = _dot(att_ref[...], wo_ref[...])
    o_ref[...] = x + _rms(a, gpost_ref[...])


def _xattn(x, kv, g_pre, g_post, w_q, w_o, li, seq_len, *, tm=512):
    m, k = x.shape
    tm = min(tm, seq_len)
    bps = seq_len // tm
    return pl.pallas_call(
        _xattn_kernel,
        grid=(m // tm,),
        in_specs=[
            pl.BlockSpec((tm, k), lambda i: (i, 0)),
            pl.BlockSpec((MEM_LEN, 2 * XA_DIM), lambda i: (i // bps, 0)),
            pl.BlockSpec((1, k), lambda i: (0, 0)),
            pl.BlockSpec((1, k), lambda i: (0, 0)),
            pl.BlockSpec((None, k, XA_DIM), lambda i: (li, 0, 0)),
            pl.BlockSpec((None, XA_DIM, k), lambda i: (li, 0, 0)),
        ],
        out_specs=pl.BlockSpec((tm, k), lambda i: (i, 0)),
        out_shape=jax.ShapeDtypeStruct((m, k), F32),
        scratch_shapes=[pltpu.VMEM((tm, XA_DIM), BF16)],
        compiler_params=_cparams("parallel"),
        name="memory_cross_attention",
    )(x, kv, g_pre.reshape(1, k), g_post.reshape(1, k), w_q, w_o)


def _ssd_in_kernel(x_ref, xh_ref, g_ref, w_ref, cw_ref, cb_ref, *rest, blocks_per_seq, tn, n_riders):
    rider_in = rest[:n_riders]
    o_ref = rest[n_riders]
    rider_out = rest[n_riders + 1:2 * n_riders + 1]
    h_ref = rest[2 * n_riders + 1]
    i = pl.program_id(0)
    j = pl.program_id(1)
    z_tiles = SSD_D_INNER // tn
    conv_tiles = (SSD_D_INNER + 2 * SSD_BC_DIM) // tn
    def cast_riders():
        for src, dst in zip(rider_in, rider_out):
            dst[...] = src[...].astype(dst.dtype)

    @pl.when(j == 0)
    def _():
        keep = (i % blocks_per_seq != 0).astype(F32)
        h_ref[0:SSD_HALO, :] = (_rms(xh_ref[...], g_ref[...]) * keep).astype(BF16)
        h_ref[SSD_HALO:, :] = _rms(x_ref[...], g_ref[...]).astype(BF16)

    @pl.when(j < z_tiles)
    def _():
        cast_riders()
        o_ref[...] = _silu(_dot(h_ref[SSD_HALO:, :], w_ref[...]))

    @pl.when((j >= z_tiles) & (j < z_tiles + conv_tiles))
    def _():
        cast_riders()
        u = _dot(h_ref[...], w_ref[...])
        y = cb_ref[...] + u[SSD_HALO:] * cw_ref[SSD_CONV - 1:SSD_CONV, :]
        for tap in range(SSD_CONV - 1):
            back = SSD_CONV - 1 - tap
            y = y + pltpu.roll(u, back, axis=0)[SSD_HALO:] * cw_ref[tap:tap + 1, :]
        o_ref[...] = _silu(y)

    @pl.when(j >= z_tiles + conv_tiles)
    def _():
        cast_riders()
        o_ref[...] = _dot(h_ref[SSD_HALO:, :], w_ref[...])


def _rider_rows(rows, steps):
    r = BF16_SUBLANES
    while rows % r or rows // r > steps:
        r += BF16_SUBLANES
        assert r <= rows, (rows, steps)
    return r


def _ssd_in_proj(x, g, w_in, li, conv_w, conv_b, seq_len, riders=(), *, tm=1024, tn=512):
    m, k = x.shape
    n = w_in.shape[2]
    tm = min(tm, seq_len)
    assert m % tm == 0 and seq_len % tm == 0 and n % tn == 0
    z_tiles = SSD_D_INNER // tn
    conv_tiles = (SSD_D_INNER + 2 * SSD_BC_DIM) // tn
    halo_blocks = tm // SSD_HALO
    conv_tile = lambda j: jnp.clip(j - z_tiles, 0, conv_tiles - 1)
    nj = n // tn
    steps = (m // tm) * nj
    flat = [r.reshape(-1, r.shape[-1]) for r in riders]
    rider_specs = []
    for r in flat:
        rows = _rider_rows(r.shape[0], steps)
        last = r.shape[0] // rows - 1
        rider_specs.append(pl.BlockSpec(
            (rows, r.shape[1]), lambda i, j, last=last: (jnp.minimum(i * nj + j, last), 0)))
    outs = pl.pallas_call(
        functools.partial(_ssd_in_kernel, blocks_per_seq=seq_len // tm, tn=tn, n_riders=len(flat)),
        grid=(m // tm, nj),
        in_specs=[
            pl.BlockSpec((tm, k), lambda i, j: (i, 0)),
            pl.BlockSpec((SSD_HALO, k), lambda i, j: (jnp.maximum(i * halo_blocks - 1, 0), 0)),
            pl.BlockSpec((1, k), lambda i, j: (0, 0)),
            pl.BlockSpec((None, k, tn), lambda i, j: (li, 0, j)),
            pl.BlockSpec((SSD_CONV, tn), lambda i, j: (0, conv_tile(j))),
            pl.BlockSpec((1, tn), lambda i, j: (0, conv_tile(j))),
        ] + rider_specs,
        out_specs=[pl.BlockSpec((tm, tn), lambda i, j: (i, j))] + rider_specs,
        out_shape=[jax.ShapeDtypeStruct((m, n), F32)]
        + [jax.ShapeDtypeStruct(r.shape, BF16) for r in flat],
        scratch_shapes=[pltpu.VMEM((tm + SSD_HALO, k), BF16)],
        compiler_params=_cparams("arbitrary", "arbitrary"),
        name="ssd_in_proj_conv",
    )(x, x, g.reshape(1, k), w_in, conv_w, conv_b.reshape(1, -1), *flat)
    return outs[0], [o.reshape(r.shape) for o, r in zip(outs[1:], riders)]


def _ssd_kernel(zs_ref, xs_ref, bm_ref, cm_ref, dtr_ref, dtb_ref, alog_ref,
                dskip_ref, ng_ref, expand_ref, tril_ref, o_ref,
                wx_ref, expax_ref, cdx_ref, acsg_ref, acst_ref, state_ref, y_ref):
    q = SSD_CHUNK
    c = pl.program_id(1)

    @pl.when(c == 0)
    def _():
        state_ref[...] = jnp.zeros(state_ref.shape, F32)

    dt = _softplus(dtr_ref[...] + dtb_ref[...])
    a = dt * (-jnp.exp(alog_ref[...]))
    a_cs = _dot_f32_rhs(tril_ref[...], a)
    a_last = a_cs[q - 1:q, :]
    expand = expand_ref[...]
    wx_ref[...] = _dot_2piece_lhs(dt * jnp.exp(a_last - a_cs), expand)
    expax_ref[...] = _dot_2piece_lhs(jnp.exp(a_cs), expand)
    cdx_ref[...] = _dot_2piece_lhs(jnp.broadcast_to(jnp.exp(a_last), (8, LANE)), expand)
    acst_ref[...] = (a_cs - jnp.log(dt)).T.reshape(LANE // SSD_HPG, SSD_HPG, q)
    for g in range(SSD_N_GROUPS):
        acsg_ref[g] = a_cs if g == 0 else pltpu.roll(a_cs, LANE - g * SSD_HPG, axis=1)

    row = lax.broadcasted_iota(jnp.int32, (q, q), 0)
    col = lax.broadcasted_iota(jnp.int32, (q, q), 1)
    causal = row >= col
    left_half = col < SSD_HEAD_DIM

    def group_body(g, carry):
        off = pl.multiple_of(g * SSD_GROUP_W, SSD_GROUP_W)
        offn = pl.multiple_of(g * SSD_D_STATE, SSD_D_STATE)
        cm = cm_ref[:, pl.ds(offn, SSD_D_STATE)]
        bm = bm_ref[:, pl.ds(offn, SSD_D_STATE)]
        cmb = cm.astype(BF16)
        cb = _dot_nt(cmb, bm.astype(BF16))
        xs = xs_ref[:, pl.ds(off, SSD_GROUP_W)]
        st = state_ref[g]
        y_off = _dot(cmb, st.astype(BF16)) * expax_ref[:, pl.ds(off, SSD_GROUP_W)]
        s_new = _dot(bm.T.astype(BF16), (xs * wx_ref[:, pl.ds(off, SSD_GROUP_W)]).astype(BF16))
        state_ref[g] = st * cdx_ref[0:1, pl.ds(off, SSD_GROUP_W)] + s_new
        acs_g = acsg_ref[g]
        acst_g = acst_ref[g]
        skip = xs * dskip_ref[:, pl.ds(off, SSD_GROUP_W)]
        for pr in range(SSD_HPG // 2):
            scs = []
            for e in range(2):
                j = 2 * pr + e
                diff = acs_g[:, j:j + 1] - acst_g[j:j + 1, :]
                dec = jnp.where(causal, jnp.exp(diff), 0.0)
                scs.append((cb * dec).astype(BF16))
            xp = xs[:, pr * LANE:(pr + 1) * LANE]
            x0 = jnp.where(left_half, xp, 0.0).astype(BF16)
            x1 = jnp.where(left_half, 0.0, xp).astype(BF16)
            y_diag = _dot(jnp.concatenate(scs, axis=1), jnp.concatenate([x0, x1], axis=0))
            y = y_diag + y_off[:, pr * LANE:(pr + 1) * LANE] + skip[:, pr * LANE:(pr + 1) * LANE]
            y_ref[:, pl.ds(pl.multiple_of(off + pr * LANE, LANE), LANE)] = y
        return carry

    lax.fori_loop(0, SSD_N_GROUPS, group_body, 0, unroll=8)

    gated = y_ref[...] * zs_ref[...]
    o_ref[...] = _rms(gated, ng_ref[...]).astype(o_ref.dtype)


def _ssd_core(proj, dt_bias, a_log, d_skip, norm_g, bsz, seq_len):
    q = SSD_CHUNK
    nc = seq_len // q
    m = bsz * seq_len
    pad = LANE - SSD_N_HEADS
    dtb = jnp.pad(dt_bias, (0, pad)).reshape(1, LANE)
    alog = jnp.pad(a_log, (0, pad)).reshape(1, LANE)
    dskip = jnp.repeat(d_skip, SSD_HEAD_DIM).reshape(1, SSD_D_INNER)
    heads = np.arange(2 * LANE)[:, None] % LANE
    chans = np.arange(SSD_D_INNER)[None, :] // SSD_HEAD_DIM
    expand = jnp.asarray((heads == chans).astype(np.float32), dtype=BF16)
    tril = jnp.asarray(np.tril(np.ones((q, q), np.float32)), dtype=BF16)
    row = lambda b, c: b * nc + c
    const = lambda b, c: (0, 0)
    return pl.pallas_call(
        _ssd_kernel,
        grid=(bsz, nc),
        in_specs=[
            pl.BlockSpec((q, SSD_D_INNER), lambda b, c: (row(b, c), 0)),
            pl.BlockSpec((q, SSD_D_INNER), lambda b, c: (row(b, c), 1)),
            pl.BlockSpec((q, SSD_BC_DIM), lambda b, c: (row(b, c), 8)),
            pl.BlockSpec((q, SSD_BC_DIM), lambda b, c: (row(b, c), 9)),
            pl.BlockSpec((q, LANE), lambda b, c: (row(b, c), 80)),
            pl.BlockSpec((1, LANE), const),
            pl.BlockSpec((1, LANE), const),
            pl.BlockSpec((1, SSD_D_INNER), const),
            pl.BlockSpec((1, SSD_D_INNER), const),
            pl.BlockSpec((2 * LANE, SSD_D_INNER), const),
            pl.BlockSpec((q, q), const),
        ],
        out_specs=pl.BlockSpec((q, SSD_D_INNER), lambda b, c: (row(b, c), 0)),
        out_shape=jax.ShapeDtypeStruct((m, SSD_D_INNER), BF16),
        scratch_shapes=[
            pltpu.VMEM((q, SSD_D_INNER), F32),
            pltpu.VMEM((q, SSD_D_INNER), F32),
            pltpu.VMEM((8, SSD_D_INNER), F32),
            pltpu.VMEM((SSD_N_GROUPS, q, LANE), F32),
            pltpu.VMEM((LANE // SSD_HPG, SSD_HPG, q), F32),
            pltpu.VMEM((SSD_N_GROUPS, SSD_D_STATE, SSD_GROUP_W), F32),
            pltpu.VMEM((q, SSD_D_INNER), F32),
        ],
        compiler_params=_cparams("parallel", "arbitrary"),
        name="ssd_core",
    )(proj, proj, proj, proj, proj, dtb, alog, dskip, norm_g.reshape(1, SSD_D_INNER), expand, tril)


def _nsa_compress_kernel(t_ref, pos_ref, w1_ref, w2_ref, o_ref, *, nchunk):
    half = NSA_CMP_STRIDE
    dh = NSA_HEAD_DIM
    p_lo = jnp.zeros((nchunk, NSA_CMP_HIDDEN), F32)
    p_hi = jnp.zeros((nchunk, NSA_CMP_HIDDEN), F32)
    for l in range(half):
        rows = t_ref[pl.ds(l, nchunk, stride=half), :]
        a_lo = (rows + pos_ref[l:l + 1, :]).astype(BF16)
        a_hi = (rows + pos_ref[half + l:half + l + 1, :]).astype(BF16)
        p_lo = p_lo + _dot(a_lo, w1_ref[l * dh:(l + 1) * dh, :].astype(BF16))
        p_hi = p_hi + _dot(a_hi, w1_ref[(half + l) * dh:(half + l + 1) * dh, :].astype(BF16))
    pre = p_lo + pltpu.roll(p_hi, nchunk - 1, axis=0)
    o_ref[...] = _dot(_gelu(pre).astype(BF16), w2_ref[...].astype(BF16))


def _nsa_compress(proj, cmp_pos, cmp_w1, cmp_w2, bsz, seq_len):
    nchunk = seq_len // NSA_CMP_STRIDE
    g, dh = NSA_KV_GROUPS, NSA_HEAD_DIM
    col0 = NSA_N_HEADS * dh // dh
    return pl.pallas_call(
        functools.partial(_nsa_compress_kernel, nchunk=nchunk),
        grid=(bsz, 2, g),
        in_specs=[
            pl.BlockSpec((seq_len, dh), lambda b, s, gi: (b, col0 + s * g + gi)),
            pl.BlockSpec((None, NSA_CMP_BLOCK, dh), lambda b, s, gi: (s, 0, 0)),
            pl.BlockSpec((None, NSA_CMP_BLOCK * dh, NSA_CMP_HIDDEN), lambda b, s, gi: (s, 0, 0)),
            pl.BlockSpec((None, NSA_CMP_HIDDEN, dh), lambda b, s, gi: (s, 0, 0)),
        ],
        out_specs=pl.BlockSpec((None, None, None, nchunk, dh), lambda b, s, gi: (b, s, gi, 0, 0)),
        out_shape=jax.ShapeDtypeStruct((bsz, 2, g, nchunk, dh), F32),
        compiler_params=_cparams("parallel", "parallel", "parallel"),
        name="nsa_compress",
    )(proj, cmp_pos, cmp_w1, cmp_w2)


def _nsa_select_kernel(q_ref, kc_ref, vc_ref, cover_ref, ocmp_ref, sel_ref, *, ncmp, nslc):
    tq = NSA_SEL_TQ
    dh = NSA_HEAD_DIM
    scale = dh ** -0.5
    nck = kc_ref.shape[0]
    t0 = pl.program_id(2) * tq
    t_col = t0 + lax.broadcasted_iota(jnp.int32, (tq, 1), 0)
    ci_row = lax.broadcasted_iota(jnp.int32, (1, nck), 1)
    valid = (ci_row * NSA_CMP_STRIDE + (NSA_CMP_BLOCK - 1) <= t_col) & (ci_row < ncmp)
    t_row = t0 + lax.broadcasted_iota(jnp.int32, (1, tq), 1)
    ci_col = lax.broadcasted_iota(jnp.int32, (nck, 1), 0)
    valid_t = (ci_col * NSA_CMP_STRIDE + (NSA_CMP_BLOCK - 1) <= t_row) & (ci_col < ncmp)
    c = scale * math.log2(math.e)
    bias = jnp.where(valid, 0.0, NEG)
    bias_t = jnp.where(valid_t, 0.0, NEG)
    some = jnp.where(t_col >= NSA_CMP_BLOCK - 1, 1.0, 0.0)
    some_t = jnp.where(t_row >= NSA_CMP_BLOCK - 1, 1.0, 0.0)
    kcb = kc_ref[...].astype(BF16)
    vcb = vc_ref[...].astype(BF16)
    psum_t = jnp.zeros((nck, tq), F32)
    for j in range(NSA_HPG):
        qj = q_ref[:, j * dh:(j + 1) * dh].astype(BF16)
        s = _dot_nt(qj, kcb) * c + bias
        e = jnp.exp2(s - jnp.max(s, axis=-1, keepdims=True))
        p = e * (some / jnp.sum(e, axis=-1, keepdims=True))
        ocmp_ref[:, j * dh:(j + 1) * dh] = _dot(p.astype(BF16), vcb)
        st = _dot_nt(kcb, qj) * c + bias_t
        et = jnp.exp2(st - jnp.max(st, axis=0, keepdims=True))
        psum_t = psum_t + et * (some_t / jnp.sum(et, axis=0, keepdims=True))
    imp_t = _dot_f32_rhs(cover_ref[...], psum_t)
    blk = lax.broadcasted_iota(jnp.int32, (nslc, tq), 0)
    cur = t_row // NSA_SLC_BLOCK
    forced = (blk == 0) | ((blk <= cur) & (blk > cur - 1 - NSA_N_LOCAL))
    future = blk * NSA_SLC_BLOCK > t_row
    score = jnp.where(forced, BIG, jnp.where(future, NEG, imp_t))
    sub = 8
    slabs = [score[r * sub:(r + 1) * sub, :] for r in range(nslc // sub)]
    ranks = [jnp.zeros((sub, tq), F32) for _ in slabs]
    row_in_slab = lax.broadcasted_iota(jnp.int32, (sub, tq), 0)
    for j in range(nslc):
        cj = jnp.broadcast_to(score[j:j + 1, :], (sub, tq))
        for r, s_r in enumerate(slabs):
            if r * sub > j:
                before = cj >= s_r
            elif (r + 1) * sub - 1 < j:
                before = cj > s_r
            else:
                before = (cj > s_r) | ((cj == s_r) & (row_in_slab > j - r * sub))
            ranks[r] = ranks[r] + jnp.where(before, 1.0, 0.0)
    topk = float(min(NSA_TOPK, nslc))
    for r, rank in enumerate(ranks):
        sel_ref[r * sub:(r + 1) * sub, :] = jnp.where(rank < topk, 1.0, 0.0)


def _nsa_select(proj, kvc, bsz, seq_len):
    tq, dh, g = NSA_SEL_TQ, NSA_HEAD_DIM, NSA_KV_GROUPS
    nq = seq_len // tq
    ncmp = (seq_len - NSA_CMP_BLOCK) // NSA_CMP_STRIDE + 1
    nck = seq_len // NSA_CMP_STRIDE
    nslc = seq_len // NSA_SLC_BLOCK
    ci = np.arange(nck)[:, None] * NSA_CMP_STRIDE
    sj = np.arange(nslc)[None, :] * NSA_SLC_BLOCK
    cover = ((ci <= sj + NSA_SLC_BLOCK - 1) & (ci + NSA_CMP_BLOCK - 1 >= sj)).astype(np.float32)
    cover = jnp.asarray(cover.T, dtype=BF16)
    return pl.pallas_call(
        functools.partial(_nsa_select_kernel, ncmp=ncmp, nslc=nslc),
        grid=(bsz, g, nq),
        in_specs=[
            pl.BlockSpec((tq, NSA_HPG * dh), lambda b, gi, qi: (b * nq + qi, gi)),
            pl.BlockSpec((None, None, None, nck, dh), lambda b, gi, qi: (b, 0, gi, 0, 0)),
            pl.BlockSpec((None, None, None, nck, dh), lambda b, gi, qi: (b, 1, gi, 0, 0)),
            pl.BlockSpec((nslc, nck), lambda b, gi, qi: (0, 0)),
        ],
        out_specs=[
            pl.BlockSpec((tq, NSA_HPG * dh), lambda b, gi, qi: (b * nq + qi, gi)),
            pl.BlockSpec((None, None, nslc, tq), lambda b, gi, qi: (b, gi, 0, qi)),
        ],
        out_shape=[
            jax.ShapeDtypeStruct((bsz * seq_len, NSA_N_HEADS * dh), F32),
            jax.ShapeDtypeStruct((bsz, g, nslc, seq_len), F32),
        ],
        compiler_params=_cparams("parallel", "parallel", "parallel"),
        name="nsa_cmp_select",
    )(proj, kvc, kvc, cover)


def _nsa_attn_kernel(q_ref, ks_ref, vs_ref, kw_ref, vw_ref, sel_ref, expand_ref, ocmp_ref, gl_ref,
                     o_ref, ksb_ref, vsx_ref, kwb_ref, vwx_ref, selb_ref, q4_ref, pw_ref, m_ref, acc_ref):
    tq = NSA_TQ
    dh = NSA_HEAD_DIM
    kt_w = NSA_KEY_TILE
    win_w = NSA_WINDOW + tq
    c = (dh ** -0.5) * math.log2(math.e)
    qi = pl.program_id(2)

    @pl.when(qi == 0)
    def _():
        ones = jnp.ones((ks_ref.shape[0], dh), BF16)
        for blk in range(ks_ref.shape[0] // LANE):
            cols = slice(blk * LANE, (blk + 1) * LANE)
            ksb_ref[:, cols] = ks_ref[cols, :].T.astype(BF16)
            kwb_ref[:, cols] = kw_ref[cols, :].T.astype(BF16)
        vsx_ref[:, 0:dh] = vs_ref[...].astype(BF16)
        vsx_ref[:, dh:] = ones
        vwx_ref[:, 0:dh] = vw_ref[...].astype(BF16)
        vwx_ref[:, dh:] = ones

    selb_ref[...] = sel_ref[...].T.astype(BF16)
    for j in range(NSA_HPG):
        q4_ref[j * tq:(j + 1) * tq, :] = q_ref[:, j * dh:(j + 1) * dh].astype(BF16)
    t1 = qi * tq + lax.broadcasted_iota(jnp.int32, (tq, 1), 0)
    m_ref[...] = jnp.full(m_ref.shape, NEG, F32)
    acc_ref[...] = jnp.zeros(acc_ref.shape, F32)
    half = NSA_HPG // 2 * tq

    def slc_tile(off, width, on_diagonal):
        chosen = _dot(selb_ref[...], expand_ref[:, pl.ds(off, width)])
        bias = (chosen - 1.0) * BIG
        if on_diagonal:
            pos = off + lax.broadcasted_iota(jnp.int32, (1, width), 1)
            bias = jnp.where(pos <= t1, bias, NEG)
        k_t = ksb_ref[:, pl.ds(off, width)]
        v_x = vsx_ref[pl.ds(off, width), :]
        pairs = [slice(hp * half, (hp + 1) * half) for hp in range(2)]
        scores = [_dot(q4_ref[pr], k_t) for pr in pairs]
        for hp, pr in enumerate(pairs):
            m_prev = m_ref[pr]
            ps, m_news = [], []
            for e in range(NSA_HPG // 2):
                sj = scores[hp][e * tq:(e + 1) * tq] + bias
                m_new = jnp.maximum(m_prev[e * tq:(e + 1) * tq], jnp.max(sj, axis=-1, keepdims=True))
                m_wide = jnp.concatenate([m_new] * (width // LANE), axis=1)
                ps.append(jnp.exp2((sj - m_wide) * c).astype(BF16))
                m_news.append(m_new)
            m_new = jnp.concatenate(m_news, axis=0)
            alpha = jnp.exp2((m_prev - m_new) * c)
            m_ref[pr] = m_new
            pv = _dot(jnp.concatenate(ps, axis=0), v_x)
            acc_ref[pr] = jnp.concatenate([alpha, alpha], axis=1) * acc_ref[pr] + pv

    n_wide = (qi * tq) // kt_w

    def wide_step(kt, carry):
        slc_tile(pl.multiple_of(kt * kt_w, kt_w), kt_w, False)
        return carry

    def narrow_step(s, carry):
        slc_tile(pl.multiple_of(n_wide * kt_w + s * tq, tq), tq, True)
        return carry

    lax.fori_loop(0, n_wide, wide_step, 0)
    lax.fori_loop(0, qi + 1 - n_wide * (kt_w // tq), narrow_step, 0)

    w0 = pl.multiple_of(jnp.maximum(qi * tq - NSA_WINDOW, 0), tq)
    sw = _dot(q4_ref[...], kwb_ref[:, pl.ds(w0, win_w)])
    diff = t1 - (w0 + lax.broadcasted_iota(jnp.int32, (1, win_w), 1))
    bias_w = jnp.where((diff >= 0) & (diff < NSA_WINDOW), 0.0, NEG)
    for j in range(NSA_HPG):
        rows = slice(j * tq, (j + 1) * tq)
        sj = sw[rows] + bias_w
        pw_ref[rows] = jnp.exp2((sj - jnp.max(sj, axis=-1, keepdims=True)) * c).astype(BF16)
    ow = _dot(pw_ref[...], vwx_ref[pl.ds(w0, win_w), :])

    gates = _sigmoid(gl_ref[...])
    for j in range(NSA_HPG):
        rows = slice(j * tq, (j + 1) * tq)
        o_slc = acc_ref[rows, 0:dh] / acc_ref[rows, dh:]
        o_win = ow[rows, 0:dh] / ow[rows, dh:]
        o = (gates[:, 3 * j:3 * j + 1] * ocmp_ref[:, j * dh:(j + 1) * dh]
             + gates[:, 3 * j + 1:3 * j + 2] * o_slc
             + gates[:, 3 * j + 2:3 * j + 3] * o_win)
        o_ref[:, j * dh:(j + 1) * dh] = o.astype(o_ref.dtype)


def _nsa_attn(proj, sel, ocmp, gate_logits, bsz, seq_len):
    tq, dh, g = NSA_TQ, NSA_HEAD_DIM, NSA_KV_GROUPS
    assert seq_len % NSA_KEY_TILE == 0 and seq_len >= NSA_WINDOW + tq
    rows = NSA_HPG * tq
    nq = seq_len // tq
    nslc = seq_len // NSA_SLC_BLOCK
    expand = (np.arange(nslc)[:, None] == np.arange(seq_len)[None, :] // NSA_SLC_BLOCK)
    expand = jnp.asarray(expand.astype(np.float32), dtype=BF16)
    kv0 = (NSA_N_HEADS * dh + 2 * NSA_KV_DIM) // dh

    def kv_spec(idx):
        return pl.BlockSpec((seq_len, dh), lambda b, gi, qi: (b, kv0 + idx * g + gi))

    qo_spec = pl.BlockSpec((tq, NSA_HPG * dh), lambda b, gi, qi: (b * nq + qi, gi))
    return pl.pallas_call(
        _nsa_attn_kernel,
        grid=(bsz, g, nq),
        in_specs=[
            qo_spec,
            kv_spec(0), kv_spec(1), kv_spec(2), kv_spec(3),
            pl.BlockSpec((None, None, nslc, tq), lambda b, gi, qi: (b, gi, 0, qi)),
            pl.BlockSpec((nslc, seq_len), lambda b, gi, qi: (0, 0)),
            qo_spec,
            pl.BlockSpec((None, tq, 3 * NSA_HPG), lambda b, gi, qi: (gi, b * nq + qi, 0)),
        ],
        out_specs=qo_spec,
        out_shape=jax.ShapeDtypeStruct((bsz * seq_len, NSA_N_HEADS * dh), BF16),
        scratch_shapes=[
            pltpu.VMEM((dh, seq_len), BF16),
            pltpu.VMEM((seq_len, 2 * dh), BF16),
            pltpu.VMEM((dh, seq_len), BF16),
            pltpu.VMEM((seq_len, 2 * dh), BF16),
            pltpu.VMEM((tq, nslc), BF16),
            pltpu.VMEM((rows, dh), BF16),
            pltpu.VMEM((rows, NSA_WINDOW + tq), BF16),
            pltpu.VMEM((rows, LANE), F32),
            pltpu.VMEM((rows, 2 * dh), F32),
        ],
        compiler_params=_cparams("parallel", "parallel", "arbitrary"),
        name="nsa_slc_win_attention",
    )(proj, proj, proj, proj, proj, sel, expand, ocmp, gate_logits)


def _sgu_kernel(u_ref, v_ref, g_ref, ws_ref, bst_ref, o_ref):
    v = v_ref[...].astype(F32)
    mu = jnp.mean(v, axis=-1, keepdims=True)
    vc = v - mu
    vn = (vc * lax.rsqrt(jnp.mean(vc * vc, axis=-1, keepdims=True) + RMS_EPS) * g_ref[...]).astype(BF16)
    q = SGU_CHUNK
    tri = lax.broadcasted_iota(jnp.int32, (q, q), 0) >= lax.broadcasted_iota(jnp.int32, (q, q), 1)
    for g in range(SGU_GROUPS):
        cols = slice(g * SGU_GROUP_DIM, (g + 1) * SGU_GROUP_DIM)
        w_m = jnp.where(tri, ws_ref[g], 0.0).astype(BF16)
        sv = _dot(w_m, vn[:, cols]) + bst_ref[:, g:g + 1]
        o_ref[:, cols] = (u_ref[:, cols].astype(F32) * sv).astype(o_ref.dtype)


def _sgu_core(proj, ln_g, w_spatial, b_spatial):
    m = proj.shape[0]
    q = SGU_CHUNK
    return pl.pallas_call(
        _sgu_kernel,
        grid=(m // q,),
        in_specs=[
            pl.BlockSpec((q, SGU_WIDTH), lambda i: (i, 0)),
            pl.BlockSpec((q, SGU_WIDTH), lambda i: (i, 1)),
            pl.BlockSpec((1, SGU_WIDTH), lambda i: (0, 0)),
            pl.BlockSpec((SGU_GROUPS, q, q), lambda i: (0, 0, 0)),
            pl.BlockSpec((q, SGU_GROUPS), lambda i: (0, 0)),
        ],
        out_specs=pl.BlockSpec((q, SGU_WIDTH), lambda i: (i, 0)),
        out_shape=jax.ShapeDtypeStruct((m, SGU_WIDTH), BF16),
        compiler_params=_cparams("parallel"),
        name="sgu_core",
    )(proj, proj, ln_g.reshape(1, SGU_WIDTH), w_spatial, b_spatial.T)


def _pool_kernel(z_ref, zh_ref, wg_ref, sc_ref, o_ref, zz_ref, *, tm, blocks_per_seq):
    i = pl.program_id(0)
    keep = (i % blocks_per_seq != 0).astype(F32)
    zz_ref[0:POOL_HALO, :] = zh_ref[...] * keep
    zz_ref[POOL_HALO:, :] = z_ref[...]
    t = (i % blocks_per_seq) * tm + lax.broadcasted_iota(jnp.int32, (tm, 1), 0)
    for gi, win in enumerate(POOL_WINDOWS):
        cols = slice(gi * POOL_GROUP_DIM, (gi + 1) * POOL_GROUP_DIM)
        s = zz_ref[POOL_HALO:POOL_HALO + tm, cols]
        for k in range(1, win):
            s = s + zz_ref[POOL_HALO - k:POOL_HALO - k + tm, cols]
        count = jnp.minimum(t + 1, win).astype(F32)
        pooled = s / count - z_ref[:, cols]
        y = _dot(pooled.astype(BF16), wg_ref[gi]) * sc_ref[:, cols]
        o_ref[:, cols] = y.astype(o_ref.dtype)


def _pool_core(z, w_group, scale, seq_len, *, tm=512):
    m, n = z.shape
    tm = min(tm, seq_len)
    halo_blocks = tm // POOL_HALO
    return pl.pallas_call(
        functools.partial(_pool_kernel, tm=tm, blocks_per_seq=seq_len // tm),
        grid=(m // tm,),
        in_specs=[
            pl.BlockSpec((tm, n), lambda i: (i, 0)),
            pl.BlockSpec((POOL_HALO, n), lambda i: (jnp.maximum(i * halo_blocks - 1, 0), 0)),
            pl.BlockSpec(w_group.shape, lambda i: (0, 0, 0)),
            pl.BlockSpec((1, n), lambda i: (0, 0)),
        ],
        out_specs=pl.BlockSpec((tm, n), lambda i: (i, 0)),
        out_shape=jax.ShapeDtypeStruct((m, n), BF16),
        scratch_shapes=[pltpu.VMEM((tm + POOL_HALO, n), F32)],
        compiler_params=_cparams("parallel"),
        name="pool_core",
    )(z, z, w_group, scale.reshape(1, n))


def _pad_cols(w, n):
    return jnp.pad(w, ((0, 0), (0, 0), (0, n - w.shape[2])))


def _ssd_layer(x, g_pre, g_post, w_in, j, conv_w, conv_b, dt_bias, a_log, d_skip, norm_g, w_out, bsz, seq_len):
    proj, _ = _ssd_in_proj(x, g_pre, w_in, j, conv_w, conv_b, seq_len)
    y = _ssd_core(proj, dt_bias, a_log, d_skip, norm_g, bsz, seq_len)
    return _matmul_post(y, w_out, j, g_post, x)


def _first_ssd_layer(x, g_pre, g_post, w_in, conv_w, conv_b, dt_bias, a_log, d_skip, norm_g, f32_weights,
                     bsz, seq_len):
    proj, cast = _ssd_in_proj(x, g_pre, w_in, 0, conv_w, conv_b, seq_len, tuple(f32_weights.values()))
    wb = dict(zip(f32_weights, cast))
    y = _ssd_core(proj, dt_bias, a_log, d_skip, norm_g, bsz, seq_len)
    return _matmul_post(y, wb["ssd_out"], 0, g_post, x), wb


def _nsa_layer(x, g_pre, g_post, w_in, j, cmp_pos, cmp_w1, cmp_w2, w_out, bsz, seq_len):
    proj = _norm_matmul(x, g_pre, w_in, j, jnp.zeros((NSA_IN_PAD,), F32))
    kvc = _nsa_compress(proj, cmp_pos, cmp_w1, cmp_w2, bsz, seq_len)
    ocmp, sel = _nsa_select(proj, kvc, bsz, seq_len)
    ngate = 3 * NSA_N_HEADS
    gl = proj[:, NSA_GATE_COL:NSA_GATE_COL + ngate].reshape(-1, NSA_KV_GROUPS, 3 * NSA_HPG).transpose(1, 0, 2)
    o = _nsa_attn(proj, sel, ocmp, gl, bsz, seq_len)
    return _matmul_post(o, w_out, j, g_post, x)


def _sgu_layer(x, g_pre, g_post, w_in, j, b_in, ln_g, w_spatial, b_spatial, w_out):
    proj = _norm_matmul(x, g_pre, w_in, j, b_in, act="gelu", out_dtype=BF16)
    y = _sgu_core(proj, ln_g, w_spatial, b_spatial)
    return _matmul_post(y, w_out, j, g_post, x)


def _pool_layer(x, g_pre, g_post, w_in, j, w_group, scale, w_out, seq_len):
    z = _norm_matmul(x, g_pre, w_in, j, jnp.zeros((D_MODEL,), F32))
    y = _pool_core(z, w_group.astype(BF16), scale, seq_len)
    return _matmul_post(y, w_out, j, g_post, x)


def kernel(x, mem, norm_pre, norm_post, norm_mem, ssd_w_in, ssd_conv_w, ssd_conv_b, ssd_dt_bias, ssd_a_log, ssd_d, ssd_norm_g, ssd_w_out, nsa_w_in, nsa_cmp_pos, nsa_cmp_w1, nsa_cmp_w2, nsa_w_out, sgu_w_in, sgu_b_in, sgu_ln_g, sgu_w_spatial, sgu_b_spatial, sgu_w_out, pool_w_in, pool_w_group, pool_scale, pool_w_out, xa_w_q, xa_w_kv, xa_w_o, ffn_w_up, ffn_conv_w, ffn_conv_b, ffn_w_down):
    bsz, seq_len, d = x.shape
    depth = norm_pre.shape[0]
    xf = x.reshape(bsz * seq_len, d)
    memf = mem.reshape(bsz * mem.shape[1], d)
    ssd_in = _pad_cols(ssd_w_in, SSD_IN_PAD).astype(BF16)
    nsa_in = _pad_cols(nsa_w_in, NSA_IN_PAD).astype(BF16)
    f32_weights = dict(ssd_out=ssd_w_out, nsa_out=nsa_w_out, sgu_in=sgu_w_in, sgu_out=sgu_w_out,
                       pool_in=pool_w_in, pool_out=pool_w_out, xa_q=xa_w_q, xa_kv=xa_w_kv, xa_o=xa_w_o,
                       ffn_up=ffn_w_up, ffn_down=ffn_w_down)
    wb = None
    for i in range(depth):
        kind, j = i % 4, i // 4
        if i == 0:
            xf, wb = _first_ssd_layer(xf, norm_pre[i, 0], norm_post[i, 0], ssd_in, ssd_conv_w[j], ssd_conv_b[j],
                                      ssd_dt_bias[j], ssd_a_log[j], ssd_d[j], ssd_norm_g[j], f32_weights,
                                      bsz, seq_len)
        elif kind == 0:
            xf = _ssd_layer(xf, norm_pre[i, 0], norm_post[i, 0], ssd_in, j, ssd_conv_w[j], ssd_conv_b[j],
                            ssd_dt_bias[j], ssd_a_log[j], ssd_d[j], ssd_norm_g[j], wb["ssd_out"], bsz, seq_len)
        elif kind == 1:
            xf = _nsa_layer(xf, norm_pre[i, 0], norm_post[i, 0], nsa_in, j, nsa_cmp_pos[j], nsa_cmp_w1[j],
                            nsa_cmp_w2[j], wb["nsa_out"], bsz, seq_len)
        elif kind == 2:
            xf = _sgu_layer(xf, norm_pre[i, 0], norm_post[i, 0], wb["sgu_in"], j, sgu_b_in[j], sgu_ln_g[j],
                            sgu_w_spatial[j], sgu_b_spatial[j], wb["sgu_out"])
        else:
            xf = _pool_layer(xf, norm_pre[i, 0], norm_post[i, 0], wb["pool_in"], j, pool_w_group[j],
                             pool_scale[j], wb["pool_out"], seq_len)
        kv = _norm_matmul(memf, norm_mem[i], wb["xa_kv"], i, jnp.zeros((2 * XA_DIM,), F32),
                          out_dtype=BF16, tm=MEM_LEN)
        xf = _xattn(xf, kv, norm_pre[i, 1], norm_post[i, 1], wb["xa_q"], wb["xa_o"], i, seq_len)
        act = _ffn_up(xf, norm_pre[i, 2], wb["ffn_up"], i, ffn_conv_w[i], ffn_conv_b[i], seq_len)
        xf = _matmul_post(act, wb["ffn_down"], i, norm_post[i, 2], xf)
    return xf.reshape(bsz, seq_len, d)
```

```python
import functools
import math

import jax
import jax.numpy as jnp
import numpy as np
from jax import lax
from jax.experimental import pallas as pl
from jax.experimental.pallas import tpu as pltpu

F32 = jnp.float32
BF16 = jnp.bfloat16

D_MODEL = 2048
RMS_EPS = 1e-6
NEG = -1e30
BIG = 1e30
MEM_LEN = 256

SSD_D_INNER = 4096
SSD_HEAD_DIM = 64
SSD_N_HEADS = 64
SSD_N_GROUPS = 8
SSD_HPG = 8
SSD_D_STATE = 128
SSD_CONV = 4
SSD_HALO = 16
SSD_CHUNK = 128
SSD_GROUP_W = SSD_HPG * SSD_HEAD_DIM
SSD_BC_DIM = SSD_N_GROUPS * SSD_D_STATE
SSD_IN_PAD = 10752

NSA_HEAD_DIM = 128
NSA_N_HEADS = 16
NSA_KV_GROUPS = 4
NSA_HPG = 4
NSA_KV_DIM = 512
NSA_CMP_BLOCK = 32
NSA_CMP_STRIDE = 16
NSA_CMP_HIDDEN = 256
NSA_SLC_BLOCK = 64
NSA_TOPK = 16
NSA_N_LOCAL = 2
NSA_WINDOW = 512
NSA_TQ = 256
NSA_KEY_TILE = 1024
NSA_SEL_TQ = 256
NSA_IN_PAD = 5632
NSA_GATE_COL = 5120

SGU_CHUNK = 128
SGU_WIDTH = 4096
SGU_GROUPS = 8
SGU_GROUP_DIM = 512

POOL_WINDOWS = (2, 4, 8, 16)
POOL_GROUP_DIM = 512
POOL_HALO = 16

XA_HEADS = 4
XA_HEAD_DIM = 128
XA_DIM = 512

FFN_HIDDEN = 5632
FFN_CONV = 3
FFN_HALO = 16

LANE = 128
BF16_SUBLANES = 16
VMEM_LIMIT = 56 * 1024 * 1024


def _cparams(*sem):
    return pltpu.CompilerParams(dimension_semantics=sem, vmem_limit_bytes=VMEM_LIMIT)


def _dot(a, b):
    return jnp.dot(a, b, preferred_element_type=F32)


def _dot_nt(a, b):
    return lax.dot_general(a, b, (((1,), (1,)), ((), ())), preferred_element_type=F32)


def _split3(v):
    hi = v.astype(BF16)
    r1 = v - hi.astype(F32)
    mid = r1.astype(BF16)
    lo = (r1 - mid.astype(F32)).astype(BF16)
    return hi, mid, lo


def _dot_f32_lhs(v, e):
    hi, mid, lo = _split3(v)
    return _dot(hi, e) + _dot(mid, e) + _dot(lo, e)


def _dot_2piece_lhs(v, e2):
    hi = v.astype(BF16)
    mid = (v - hi.astype(F32)).astype(BF16)
    return _dot(jnp.concatenate([hi, mid], axis=1), e2)


def _dot_f32_rhs(e, v):
    hi, mid, lo = _split3(v)
    return _dot(e, hi) + _dot(e, mid) + _dot(e, lo)


def _rms(x, g):
    ms = jnp.mean(x * x, axis=-1, keepdims=True)
    return x * lax.rsqrt(ms + RMS_EPS) * g


def _sigmoid(x):
    return 1.0 / (1.0 + jnp.exp(-x))


def _silu(x):
    return x * _sigmoid(x)


def _gelu(x):
    c = math.sqrt(2.0 / math.pi)
    return x * (0.5 * (1.0 + jnp.tanh(c * (x + 0.044715 * (x * x * x)))))


def _softplus(x):
    return jnp.maximum(x, 0.0) + jnp.log1p(jnp.exp(-jnp.abs(x)))


def _norm_mm_kernel(x_ref, g_ref, w_ref, b_ref, o_ref, h_ref, *, act):
    @pl.when(pl.program_id(1) == 0)
    def _():
        h_ref[...] = _rms(x_ref[...], g_ref[...]).astype(BF16)

    y = _dot(h_ref[...], w_ref[...]) + b_ref[...]
    if act == "gelu":
        y = _gelu(y)
    o_ref[...] = y.astype(o_ref.dtype)


def _norm_matmul(x, g, w, li, b, *, act=None, out_dtype=F32, tm=1024, tn=512):
    m, k = x.shape
    n = w.shape[2]
    tm = min(tm, m)
    assert m % tm == 0 and n % tn == 0, (m, n, tm, tn)
    return pl.pallas_call(
        functools.partial(_norm_mm_kernel, act=act),
        grid=(m // tm, n // tn),
        in_specs=[
            pl.BlockSpec((tm, k), lambda i, j: (i, 0)),
            pl.BlockSpec((1, k), lambda i, j: (0, 0)),
            pl.BlockSpec((None, k, tn), lambda i, j: (li, 0, j)),
            pl.BlockSpec((1, tn), lambda i, j: (0, j)),
        ],
        out_specs=pl.BlockSpec((tm, tn), lambda i, j: (i, j)),
        out_shape=jax.ShapeDtypeStruct((m, n), out_dtype),
        scratch_shapes=[pltpu.VMEM((tm, k), BF16)],
        compiler_params=_cparams("parallel", "arbitrary"),
        name="norm_matmul",
    )(x, g.reshape(1, k), w, b.reshape(1, n))


def _mm_post_kernel(a_ref, w_ref, g_ref, r_ref, o_ref, *, nj, tn):
    j = pl.program_id(1)
    o_ref[:, pl.ds(pl.multiple_of(j * tn, tn), tn)] = _dot(a_ref[...], w_ref[...])

    @pl.when(j == nj - 1)
    def _():
        o_ref[...] = r_ref[...] + _rms(o_ref[...], g_ref[...])


def _matmul_post(a, w, li, g, res, *, tm=512, tn=512):
    m, kdim = a.shape
    n = w.shape[2]
    tm = min(tm, m)
    assert m % tm == 0 and n % tn == 0
    nj = n // tn
    return pl.pallas_call(
        functools.partial(_mm_post_kernel, nj=nj, tn=tn),
        grid=(m // tm, nj),
        in_specs=[
            pl.BlockSpec((tm, kdim), lambda i, j: (i, 0)),
            pl.BlockSpec((None, kdim, tn), lambda i, j: (li, 0, j)),
            pl.BlockSpec((1, n), lambda i, j: (0, 0)),
            pl.BlockSpec((tm, n), lambda i, j: (i, 0)),
        ],
        out_specs=pl.BlockSpec((tm, n), lambda i, j: (i, 0)),
        out_shape=jax.ShapeDtypeStruct((m, n), F32),
        compiler_params=_cparams("parallel", "arbitrary"),
        name="matmul_postnorm_residual",
    )(a, w, g.reshape(1, n), res)


def _ffn_up_kernel(x_ref, xh_ref, g_ref, wg_ref, wv_ref, cwg_ref, cwv_ref, cbg_ref, cbv_ref,
                   o_ref, h_ref, *, tm, blocks_per_seq):
    i = pl.program_id(0)

    @pl.when(pl.program_id(1) == 0)
    def _():
        keep = (i % blocks_per_seq != 0).astype(F32)
        h_ref[0:FFN_HALO, :] = (_rms(xh_ref[...], g_ref[...]) * keep).astype(BF16)
        h_ref[FFN_HALO:, :] = _rms(x_ref[...], g_ref[...]).astype(BF16)

    h = h_ref[...]

    def conv(u, cw_ref, cb_ref):
        y = cb_ref[...] + u[FFN_HALO:] * cw_ref[FFN_CONV - 1:FFN_CONV, :]
        for tap in range(FFN_CONV - 1):
            back = FFN_CONV - 1 - tap
            y = y + pltpu.roll(u, back, axis=0)[FFN_HALO:] * cw_ref[tap:tap + 1, :]
        return y

    gate = conv(_dot(h, wg_ref[...]), cwg_ref, cbg_ref)
    val = conv(_dot(h, wv_ref[...]), cwv_ref, cbv_ref)
    o_ref[...] = (_silu(gate) * val).astype(o_ref.dtype)


def _ffn_up(x, g, w_up, li, conv_w, conv_b, seq_len, *, tm=1024, tn=512):
    m, k = x.shape
    tm = min(tm, seq_len)
    assert m % tm == 0 and seq_len % tm == 0 and FFN_HIDDEN % tn == 0
    nj = FFN_HIDDEN // tn
    halo_blocks = tm // FFN_HALO
    return pl.pallas_call(
        functools.partial(_ffn_up_kernel, tm=tm, blocks_per_seq=seq_len // tm),
        grid=(m // tm, nj),
        in_specs=[
            pl.BlockSpec((tm, k), lambda i, j: (i, 0)),
            pl.BlockSpec((FFN_HALO, k), lambda i, j: (jnp.maximum(i * halo_blocks - 1, 0), 0)),
            pl.BlockSpec((1, k), lambda i, j: (0, 0)),
            pl.BlockSpec((None, k, tn), lambda i, j: (li, 0, j)),
            pl.BlockSpec((None, k, tn), lambda i, j: (li, 0, j + nj)),
            pl.BlockSpec((FFN_CONV, tn), lambda i, j: (0, j)),
            pl.BlockSpec((FFN_CONV, tn), lambda i, j: (0, j + nj)),
            pl.BlockSpec((1, tn), lambda i, j: (0, j)),
            pl.BlockSpec((1, tn), lambda i, j: (0, j + nj)),
        ],
        out_specs=pl.BlockSpec((tm, tn), lambda i, j: (i, j)),
        out_shape=jax.ShapeDtypeStruct((m, FFN_HIDDEN), BF16),
        scratch_shapes=[pltpu.VMEM((tm + FFN_HALO, k), BF16)],
        compiler_params=_cparams("parallel", "arbitrary"),
        name="ffn_up_conv_gate",
    )(x, x, g.reshape(1, k), w_up, w_up, conv_w, conv_w,
      conv_b.reshape(1, -1), conv_b.reshape(1, -1))


def _xattn_kernel(x_ref, kv_ref, gpre_ref, gpost_ref, wq_ref, wo_ref, o_ref, att_ref):
    x = x_ref[...]
    h = _rms(x, gpre_ref[...]).astype(BF16)
    c = (XA_HEAD_DIM ** -0.5) * math.log2(math.e)
    q = (_dot(h, wq_ref[...]) * c).astype(BF16)
    for hd in range(XA_HEADS):
        lo = hd * XA_HEAD_DIM
        s = _dot_nt(q[:, lo:lo + XA_HEAD_DIM], kv_ref[:, lo:lo + XA_HEAD_DIM])
        e = jnp.exp2(s - jnp.max(s, axis=-1, keepdims=True))
        pv = _dot(e.astype(BF16), kv_ref[:, XA_DIM + lo:XA_DIM + lo + XA_HEAD_DIM])
        att_ref[:, lo:lo + XA_HEAD_DIM] = (pv / jnp.sum(e, axis=-1, keepdims=True)).astype(BF16)
    a = _dot(att_ref[...], wo_ref[...])
    o_ref[...] = x + _rms(a, gpost_ref[...])


def _xattn(x, kv, g_pre, g_post, w_q, w_o, li, seq_len, *, tm=512):
    m, k = x.shape
    tm = min(tm, seq_len)
    bps = seq_len // tm
    return pl.pallas_call(
        _xattn_kernel,
        grid=(m // tm,),
        in_specs=[
            pl.BlockSpec((tm, k), lambda i: (i, 0)),
            pl.BlockSpec((MEM_LEN, 2 * XA_DIM), lambda i: (i // bps, 0)),
            pl.BlockSpec((1, k), lambda i: (0, 0)),
            pl.BlockSpec((1, k), lambda i: (0, 0)),
            pl.BlockSpec((None, k, XA_DIM), lambda i: (li, 0, 0)),
            pl.BlockSpec((None, XA_DIM, k), lambda i: (li, 0, 0)),
        ],
        out_specs=pl.BlockSpec((tm, k), lambda i: (i, 0)),
        out_shape=jax.ShapeDtypeStruct((m, k), F32),
        scratch_shapes=[pltpu.VMEM((tm, XA_DIM), BF16)],
        compiler_params=_cparams("parallel"),
        name="memory_cross_attention",
    )(x, kv, g_pre.reshape(1, k), g_post.reshape(1, k), w_q, w_o)


def _ssd_in_kernel(x_ref, xh_ref, g_ref, w_ref, cw_ref, cb_ref, *rest, blocks_per_seq, tn, n_riders):
    rider_in = rest[:n_riders]
    o_ref = rest[n_riders]
    rider_out = rest[n_riders + 1:2 * n_riders + 1]
    h_ref = rest[2 * n_riders + 1]
    i = pl.program_id(0)
    j = pl.program_id(1)
    z_tiles = SSD_D_INNER // tn
    conv_tiles = (SSD_D_INNER + 2 * SSD_BC_DIM) // tn
    def cast_riders():
        for src, dst in zip(rider_in, rider_out):
            dst[...] = src[...].astype(dst.dtype)

    @pl.when(j == 0)
    def _():
        keep = (i % blocks_per_seq != 0).astype(F32)
        h_ref[0:SSD_HALO, :] = (_rms(xh_ref[...], g_ref[...]) * keep).astype(BF16)
        h_ref[SSD_HALO:, :] = _rms(x_ref[...], g_ref[...]).astype(BF16)

    @pl.when(j < z_tiles)
    def _():
        cast_riders()
        o_ref[...] = _silu(_dot(h_ref[SSD_HALO:, :], w_ref[...]))

    @pl.when((j >= z_tiles) & (j < z_tiles + conv_tiles))
    def _():
        cast_riders()
        u = _dot(h_ref[...], w_ref[...])
        y = cb_ref[...] + u[SSD_HALO:] * cw_ref[SSD_CONV - 1:SSD_CONV, :]
        for tap in range(SSD_CONV - 1):
            back = SSD_CONV - 1 - tap
            y = y + pltpu.roll(u, back, axis=0)[SSD_HALO:] * cw_ref[tap:tap + 1, :]
        o_ref[...] = _silu(y)

    @pl.when(j >= z_tiles + conv_tiles)
    def _():
        cast_riders()
        o_ref[...] = _dot(h_ref[SSD_HALO:, :], w_ref[...])


def _rider_rows(rows, steps):
    r = BF16_SUBLANES
    while rows % r or rows // r > steps:
        r += BF16_SUBLANES
        assert r <= rows, (rows, steps)
    return r


def _ssd_in_proj(x, g, w_in, li, conv_w, conv_b, seq_len, riders=(), *, tm=1024, tn=512):
    m, k = x.shape
    n = w_in.shape[2]
    tm = min(tm, seq_len)
    assert m % tm == 0 and seq_len % tm == 0 and n % tn == 0
    z_tiles = SSD_D_INNER // tn
    conv_tiles = (SSD_D_INNER + 2 * SSD_BC_DIM) // tn
    halo_blocks = tm // SSD_HALO
    conv_tile = lambda j: jnp.clip(j - z_tiles, 0, conv_tiles - 1)
    nj = n // tn
    steps = (m // tm) * nj
    flat = [r.reshape(-1, r.shape[-1]) for r in riders]
    rider_specs = []
    for r in flat:
        rows = _rider_rows(r.shape[0], steps)
        last = r.shape[0] // rows - 1
        rider_specs.append(pl.BlockSpec(
            (rows, r.shape[1]), lambda i, j, last=last: (jnp.minimum(i * nj + j, last), 0)))
    outs = pl.pallas_call(
        functools.partial(_ssd_in_kernel, blocks_per_seq=seq_len // tm, tn=tn, n_riders=len(flat)),
        grid=(m // tm, nj),
        in_specs=[
            pl.BlockSpec((tm, k), lambda i, j: (i, 0)),
            pl.BlockSpec((SSD_HALO, k), lambda i, j: (jnp.maximum(i * halo_blocks - 1, 0), 0)),
            pl.BlockSpec((1, k), lambda i, j: (0, 0)),
            pl.BlockSpec((None, k, tn), lambda i, j: (li, 0, j)),
            pl.BlockSpec((SSD_CONV, tn), lambda i, j: (0, conv_tile(j))),
            pl.BlockSpec((1, tn), lambda i, j: (0, conv_tile(j))),
        ] + rider_specs,
        out_specs=[pl.BlockSpec((tm, tn), lambda i, j: (i, j))] + rider_specs,
        out_shape=[jax.ShapeDtypeStruct((m, n), F32)]
        + [jax.ShapeDtypeStruct(r.shape, BF16) for r in flat],
        scratch_shapes=[pltpu.VMEM((tm + SSD_HALO, k), BF16)],
        compiler_params=_cparams("arbitrary", "arbitrary"),
        name="ssd_in_proj_conv",
    )(x, x, g.reshape(1, k), w_in, conv_w, conv_b.reshape(1, -1), *flat)
    return outs[0], [o.reshape(r.shape) for o, r in zip(outs[1:], riders)]


def _ssd_kernel(zs_ref, xs_ref, bm_ref, cm_ref, dtr_ref, dtb_ref, alog_ref,
                dskip_ref, ng_ref, expand_ref, tril_ref, o_ref,
                wx_ref, expax_ref, cdx_ref, acsg_ref, acst_ref, state_ref, y_ref):
    q = SSD_CHUNK
    c = pl.program_id(1)

    @pl.when(c == 0)
    def _():
        state_ref[...] = jnp.zeros(state_ref.shape, F32)

    dt = _softplus(dtr_ref[...] + dtb_ref[...])
    a = dt * (-jnp.exp(alog_ref[...]))
    a_cs = _dot_f32_rhs(tril_ref[...], a)
    a_last = a_cs[q - 1:q, :]
    expand = expand_ref[...]
    wx_ref[...] = _dot_2piece_lhs(dt * jnp.exp(a_last - a_cs), expand)
    expax_ref[...] = _dot_2piece_lhs(jnp.exp(a_cs), expand)
    cdx_ref[...] = _dot_2piece_lhs(jnp.broadcast_to(jnp.exp(a_last), (8, LANE)), expand)
    acst_ref[...] = (a_cs - jnp.log(dt)).T.reshape(LANE // SSD_HPG, SSD_HPG, q)
    for g in range(SSD_N_GROUPS):
        acsg_ref[g] = a_cs if g == 0 else pltpu.roll(a_cs, LANE - g * SSD_HPG, axis=1)

    row = lax.broadcasted_iota(jnp.int32, (q, q), 0)
    col = lax.broadcasted_iota(jnp.int32, (q, q), 1)
    causal = row >= col
    left_half = col < SSD_HEAD_DIM

    def group_body(g, carry):
        off = pl.multiple_of(g * SSD_GROUP_W, SSD_GROUP_W)
        offn = pl.multiple_of(g * SSD_D_STATE, SSD_D_STATE)
        cm = cm_ref[:, pl.ds(offn, SSD_D_STATE)]
        bm = bm_ref[:, pl.ds(offn, SSD_D_STATE)]
        cmb = cm.astype(BF16)
        cb = _dot_nt(cmb, bm.astype(BF16))
        xs = xs_ref[:, pl.ds(off, SSD_GROUP_W)]
        st = state_ref[g]
        y_off = _dot(cmb, st.astype(BF16)) * expax_ref[:, pl.ds(off, SSD_GROUP_W)]
        s_new = _dot(bm.T.astype(BF16), (xs * wx_ref[:, pl.ds(off, SSD_GROUP_W)]).astype(BF16))
        state_ref[g] = st * cdx_ref[0:1, pl.ds(off, SSD_GROUP_W)] + s_new
        acs_g = acsg_ref[g]
        acst_g = acst_ref[g]
        skip = xs * dskip_ref[:, pl.ds(off, SSD_GROUP_W)]
        for pr in range(SSD_HPG // 2):
            scs = []
            for e in range(2):
                j = 2 * pr + e
                diff = acs_g[:, j:j + 1] - acst_g[j:j + 1, :]
                dec = jnp.where(causal, jnp.exp(diff), 0.0)
                scs.append((cb * dec).astype(BF16))
            xp = xs[:, pr * LANE:(pr + 1) * LANE]
            x0 = jnp.where(left_half, xp, 0.0).astype(BF16)
            x1 = jnp.where(left_half, 0.0, xp).astype(BF16)
            y_diag = _dot(jnp.concatenate(scs, axis=1), jnp.concatenate([x0, x1], axis=0))
            y = y_diag + y_off[:, pr * LANE:(pr + 1) * LANE] + skip[:, pr * LANE:(pr + 1) * LANE]
            y_ref[:, pl.ds(pl.multiple_of(off + pr * LANE, LANE), LANE)] = y
        return carry

    lax.fori_loop(0, SSD_N_GROUPS, group_body, 0, unroll=8)

    gated = y_ref[...] * zs_ref[...]
    o_ref[...] = _rms(gated, ng_ref[...]).astype(o_ref.dtype)


def _ssd_core(proj, dt_bias, a_log, d_skip, norm_g, bsz, seq_len):
    q = SSD_CHUNK
    nc = seq_len // q
    m = bsz * seq_len
    pad = LANE - SSD_N_HEADS
    dtb = jnp.pad(dt_bias, (0, pad)).reshape(1, LANE)
    alog = jnp.pad(a_log, (0, pad)).reshape(1, LANE)
    dskip = jnp.repeat(d_skip, SSD_HEAD_DIM).reshape(1, SSD_D_INNER)
    heads = np.arange(2 * LANE)[:, None] % LANE
    chans = np.arange(SSD_D_INNER)[None, :] // SSD_HEAD_DIM
    expand = jnp.asarray((heads == chans).astype(np.float32), dtype=BF16)
    tril = jnp.asarray(np.tril(np.ones((q, q), np.float32)), dtype=BF16)
    row = lambda b, c: b * nc + c
    const = lambda b, c: (0, 0)
    return pl.pallas_call(
        _ssd_kernel,
        grid=(bsz, nc),
        in_specs=[
            pl.BlockSpec((q, SSD_D_INNER), lambda b, c: (row(b, c), 0)),
            pl.BlockSpec((q, SSD_D_INNER), lambda b, c: (row(b, c), 1)),
            pl.BlockSpec((q, SSD_BC_DIM), lambda b, c: (row(b, c), 8)),
            pl.BlockSpec((q, SSD_BC_DIM), lambda b, c: (row(b, c), 9)),
            pl.BlockSpec((q, LANE), lambda b, c: (row(b, c), 80)),
            pl.BlockSpec((1, LANE), const),
            pl.BlockSpec((1, LANE), const),
            pl.BlockSpec((1, SSD_D_INNER), const),
            pl.BlockSpec((1, SSD_D_INNER), const),
            pl.BlockSpec((2 * LANE, SSD_D_INNER), const),
            pl.BlockSpec((q, q), const),
        ],
        out_specs=pl.BlockSpec((q, SSD_D_INNER), lambda b, c: (row(b, c), 0)),
        out_shape=jax.ShapeDtypeStruct((m, SSD_D_INNER), BF16),
        scratch_shapes=[
            pltpu.VMEM((q, SSD_D_INNER), F32),
            pltpu.VMEM((q, SSD_D_INNER), F32),
            pltpu.VMEM((8, SSD_D_INNER), F32),
            pltpu.VMEM((SSD_N_GROUPS, q, LANE), F32),
            pltpu.VMEM((LANE // SSD_HPG, SSD_HPG, q), F32),
            pltpu.VMEM((SSD_N_GROUPS, SSD_D_STATE, SSD_GROUP_W), F32),
            pltpu.VMEM((q, SSD_D_INNER), F32),
        ],
        compiler_params=_cparams("parallel", "arbitrary"),
        name="ssd_core",
    )(proj, proj, proj, proj, proj, dtb, alog, dskip, norm_g.reshape(1, SSD_D_INNER), expand, tril)


def _nsa_compress_kernel(t_ref, pos_ref, w1_ref, w2_ref, o_ref, *, nchunk):
    half = NSA_CMP_STRIDE
    dh = NSA_HEAD_DIM
    p_lo = jnp.zeros((nchunk, NSA_CMP_HIDDEN), F32)
    p_hi = jnp.zeros((nchunk, NSA_CMP_HIDDEN), F32)
    for l in range(half):
        rows = t_ref[pl.ds(l, nchunk, stride=half), :]
        a_lo = (rows + pos_ref[l:l + 1, :]).astype(BF16)
        a_hi = (rows + pos_ref[half + l:half + l + 1, :]).astype(BF16)
        p_lo = p_lo + _dot(a_lo, w1_ref[l * dh:(l + 1) * dh, :].astype(BF16))
        p_hi = p_hi + _dot(a_hi, w1_ref[(half + l) * dh:(half + l + 1) * dh, :].astype(BF16))
    pre = p_lo + pltpu.roll(p_hi, nchunk - 1, axis=0)
    o_ref[...] = _dot(_gelu(pre).astype(BF16), w2_ref[...].astype(BF16))


def _nsa_compress(proj, cmp_pos, cmp_w1, cmp_w2, bsz, seq_len):
    nchunk = seq_len // NSA_CMP_STRIDE
    g, dh = NSA_KV_GROUPS, NSA_HEAD_DIM
    col0 = NSA_N_HEADS * dh // dh
    return pl.pallas_call(
        functools.partial(_nsa_compress_kernel, nchunk=nchunk),
        grid=(bsz, 2, g),
        in_specs=[
            pl.BlockSpec((seq_len, dh), lambda b, s, gi: (b, col0 + s * g + gi)),
            pl.BlockSpec((None, NSA_CMP_BLOCK, dh), lambda b, s, gi: (s, 0, 0)),
            pl.BlockSpec((None, NSA_CMP_BLOCK * dh, NSA_CMP_HIDDEN), lambda b, s, gi: (s, 0, 0)),
            pl.BlockSpec((None, NSA_CMP_HIDDEN, dh), lambda b, s, gi: (s, 0, 0)),
        ],
        out_specs=pl.BlockSpec((None, None, None, nchunk, dh), lambda b, s, gi: (b, s, gi, 0, 0)),
        out_shape=jax.ShapeDtypeStruct((bsz, 2, g, nchunk, dh), F32),
        compiler_params=_cparams("parallel", "parallel", "parallel"),
        name="nsa_compress",
    )(proj, cmp_pos, cmp_w1, cmp_w2)


def _nsa_select_kernel(q_ref, kc_ref, vc_ref, cover_ref, ocmp_ref, sel_ref, *, ncmp, nslc):
    tq = NSA_SEL_TQ
    dh = NSA_HEAD_DIM
    scale = dh ** -0.5
    nck = kc_ref.shape[0]
    t0 = pl.program_id(2) * tq
    t_col = t0 + lax.broadcasted_iota(jnp.int32, (tq, 1), 0)
    ci_row = lax.broadcasted_iota(jnp.int32, (1, nck), 1)
    valid = (ci_row * NSA_CMP_STRIDE + (NSA_CMP_BLOCK - 1) <= t_col) & (ci_row < ncmp)
    t_row = t0 + lax.broadcasted_iota(jnp.int32, (1, tq), 1)
    ci_col = lax.broadcasted_iota(jnp.int32, (nck, 1), 0)
    valid_t = (ci_col * NSA_CMP_STRIDE + (NSA_CMP_BLOCK - 1) <= t_row) & (ci_col < ncmp)
    c = scale * math.log2(math.e)
    bias = jnp.where(valid, 0.0, NEG)
    bias_t = jnp.where(valid_t, 0.0, NEG)
    some = jnp.where(t_col >= NSA_CMP_BLOCK - 1, 1.0, 0.0)
    some_t = jnp.where(t_row >= NSA_CMP_BLOCK - 1, 1.0, 0.0)
    kcb = kc_ref[...].astype(BF16)
    vcb = vc_ref[...].astype(BF16)
    psum_t = jnp.zeros((nck, tq), F32)
    for j in range(NSA_HPG):
        qj = q_ref[:, j * dh:(j + 1) * dh].astype(BF16)
        s = _dot_nt(qj, kcb) * c + bias
        e = jnp.exp2(s - jnp.max(s, axis=-1, keepdims=True))
        p = e * (some / jnp.sum(e, axis=-1, keepdims=True))
        ocmp_ref[:, j * dh:(j + 1) * dh] = _dot(p.astype(BF16), vcb)
        st = _dot_nt(kcb, qj) * c + bias_t
        et = jnp.exp2(st - jnp.max(st, axis=0, keepdims=True))
        psum_t = psum_t + et * (some_t / jnp.sum(et, axis=0, keepdims=True))
    imp_t = _dot_f32_rhs(cover_ref[...], psum_t)
    blk = lax.broadcasted_iota(jnp.int32, (nslc, tq), 0)
    cur = t_row // NSA_SLC_BLOCK
    forced = (blk == 0) | ((blk <= cur) & (blk > cur - 1 - NSA_N_LOCAL))
    future = blk * NSA_SLC_BLOCK > t_row
    score = jnp.where(forced, BIG, jnp.where(future, NEG, imp_t))
    sub = 8
    slabs = [score[r * sub:(r + 1) * sub, :] for r in range(nslc // sub)]
    ranks = [jnp.zeros((sub, tq), F32) for _ in slabs]
    row_in_slab = lax.broadcasted_iota(jnp.int32, (sub, tq), 0)
    for j in range(nslc):
        cj = jnp.broadcast_to(score[j:j + 1, :], (sub, tq))
        for r, s_r in enumerate(slabs):
            if r * sub > j:
                before = cj >= s_r
            elif (r + 1) * sub - 1 < j:
                before = cj > s_r
            else:
                before = (cj > s_r) | ((cj == s_r) & (row_in_slab > j - r * sub))
            ranks[r] = ranks[r] + jnp.where(before, 1.0, 0.0)
    topk = float(min(NSA_TOPK, nslc))
    for r, rank in enumerate(ranks):
        sel_ref[r * sub:(r + 1) * sub, :] = jnp.where(rank < topk, 1.0, 0.0)


def _nsa_select(proj, kvc, bsz, seq_len):
    tq, dh, g = NSA_SEL_TQ, NSA_HEAD_DIM, NSA_KV_GROUPS
    nq = seq_len // tq
    ncmp = (seq_len - NSA_CMP_BLOCK) // NSA_CMP_STRIDE + 1
    nck = seq_len // NSA_CMP_STRIDE
    nslc = seq_len // NSA_SLC_BLOCK
    ci = np.arange(nck)[:, None] * NSA_CMP_STRIDE
    sj = np.arange(nslc)[None, :] * NSA_SLC_BLOCK
    cover = ((ci <= sj + NSA_SLC_BLOCK - 1) & (ci + NSA_CMP_BLOCK - 1 >= sj)).astype(np.float32)
    cover = jnp.asarray(cover.T, dtype=BF16)
    return pl.pallas_call(
        functools.partial(_nsa_select_kernel, ncmp=ncmp, nslc=nslc),
        grid=(bsz, g, nq),
        in_specs=[
            pl.BlockSpec((tq, NSA_HPG * dh), lambda b, gi, qi: (b * nq + qi, gi)),
            pl.BlockSpec((None, None, None, nck, dh), lambda b, gi, qi: (b, 0, gi, 0, 0)),
            pl.BlockSpec((None, None, None, nck, dh), lambda b, gi, qi: (b, 1, gi, 0, 0)),
            pl.BlockSpec((nslc, nck), lambda b, gi, qi: (0, 0)),
        ],
        out_specs=[
            pl.BlockSpec((tq, NSA_HPG * dh), lambda b, gi, qi: (b * nq + qi, gi)),
            pl.BlockSpec((None, None, nslc, tq), lambda b, gi, qi: (b, gi, 0, qi)),
        ],
        out_shape=[
            jax.ShapeDtypeStruct((bsz * seq_len, NSA_N_HEADS * dh), F32),
            jax.ShapeDtypeStruct((bsz, g, nslc, seq_len), F32),
        ],
        compiler_params=_cparams("parallel", "parallel", "parallel"),
        name="nsa_cmp_select",
    )(proj, kvc, kvc, cover)


def _nsa_attn_kernel(q_ref, ks_ref, vs_ref, kw_ref, vw_ref, sel_ref, expand_ref, ocmp_ref, gl_ref,
                     o_ref, ksb_ref, vsx_ref, kwb_ref, vwx_ref, selb_ref, q4_ref, pw_ref, m_ref, acc_ref):
    tq = NSA_TQ
    dh = NSA_HEAD_DIM
    kt_w = NSA_KEY_TILE
    win_w = NSA_WINDOW + tq
    c = (dh ** -0.5) * math.log2(math.e)
    qi = pl.program_id(2)

    @pl.when(qi == 0)
    def _():
        ones = jnp.ones((ks_ref.shape[0], dh), BF16)
        for blk in range(ks_ref.shape[0] // LANE):
            cols = slice(blk * LANE, (blk + 1) * LANE)
            ksb_ref[:, cols] = ks_ref[cols, :].T.astype(BF16)
            kwb_ref[:, cols] = kw_ref[cols, :].T.astype(BF16)
        vsx_ref[:, 0:dh] = vs_ref[...].astype(BF16)
        vsx_ref[:, dh:] = ones
        vwx_ref[:, 0:dh] = vw_ref[...].astype(BF16)
        vwx_ref[:, dh:] = ones

    selb_ref[...] = sel_ref[...].T.astype(BF16)
    for j in range(NSA_HPG):
        q4_ref[j * tq:(j + 1) * tq, :] = (q_ref[:, j * dh:(j + 1) * dh] * c).astype(BF16)
    t1 = qi * tq + lax.broadcasted_iota(jnp.int32, (tq, 1), 0)
    m_ref[...] = jnp.full(m_ref.shape, NEG, F32)
    acc_ref[...] = jnp.zeros(acc_ref.shape, F32)
    half = NSA_HPG // 2 * tq

    def slc_tile(off, width, on_diagonal):
        chosen = _dot(selb_ref[...], expand_ref[:, pl.ds(off, width)])
        bias = (chosen - 1.0) * BIG
        if on_diagonal:
            pos = off + lax.broadcasted_iota(jnp.int32, (1, width), 1)
            bias = jnp.where(pos <= t1, bias, NEG)
        k_t = ksb_ref[:, pl.ds(off, width)]
        v_x = vsx_ref[pl.ds(off, width), :]
        pairs = [slice(hp * half, (hp + 1) * half) for hp in range(2)]
        scores = [_dot(q4_ref[pr], k_t) for pr in pairs]
        for hp, pr in enumerate(pairs):
            m_prev = m_ref[pr]
            ps, m_news = [], []
            for e in range(NSA_HPG // 2):
                sj = scores[hp][e * tq:(e + 1) * tq] + bias
                m_new = jnp.maximum(m_prev[e * tq:(e + 1) * tq], jnp.max(sj, axis=-1, keepdims=True))
                m_wide = jnp.concatenate([m_new] * (width // LANE), axis=1)
                ps.append(jnp.exp2(sj - m_wide).astype(BF16))
                m_news.append(m_new)
            m_new = jnp.concatenate(m_news, axis=0)
            alpha = jnp.exp2(m_prev - m_new)
            m_ref[pr] = m_new
            pv = _dot(jnp.concatenate(ps, axis=0), v_x)
            acc_ref[pr] = jnp.concatenate([alpha, alpha], axis=1) * acc_ref[pr] + pv

    n_wide = (qi * tq) // kt_w

    def wide_step(kt, carry):
        slc_tile(pl.multiple_of(kt * kt_w, kt_w), kt_w, False)
        return carry

    def narrow_step(s, carry):
        slc_tile(pl.multiple_of(n_wide * kt_w + s * tq, tq), tq, True)
        return carry

    lax.fori_loop(0, n_wide, wide_step, 0)
    lax.fori_loop(0, qi + 1 - n_wide * (kt_w // tq), narrow_step, 0)

    w0 = pl.multiple_of(jnp.maximum(qi * tq - NSA_WINDOW, 0), tq)
    sw = _dot(q4_ref[...], kwb_ref[:, pl.ds(w0, win_w)])
    diff = t1 - (w0 + lax.broadcasted_iota(jnp.int32, (1, win_w), 1))
    bias_w = jnp.where((diff >= 0) & (diff < NSA_WINDOW), 0.0, NEG)
    for j in range(NSA_HPG):
        rows = slice(j * tq, (j + 1) * tq)
        sj = sw[rows] + bias_w
        pw_ref[rows] = jnp.exp2(sj - jnp.max(sj, axis=-1, keepdims=True)).astype(BF16)
    ow = _dot(pw_ref[...], vwx_ref[pl.ds(w0, win_w), :])

    gates = _sigmoid(gl_ref[...])
    for j in range(NSA_HPG):
        rows = slice(j * tq, (j + 1) * tq)
        o_slc = acc_ref[rows, 0:dh] / acc_ref[rows, dh:]
        o_win = ow[rows, 0:dh] / ow[rows, dh:]
        o = (gates[:, 3 * j:3 * j + 1] * ocmp_ref[:, j * dh:(j + 1) * dh]
             + gates[:, 3 * j + 1:3 * j + 2] * o_slc
             + gates[:, 3 * j + 2:3 * j + 3] * o_win)
        o_ref[:, j * dh:(j + 1) * dh] = o.astype(o_ref.dtype)


def _nsa_attn(proj, sel, ocmp, gate_logits, bsz, seq_len):
    tq, dh, g = NSA_TQ, NSA_HEAD_DIM, NSA_KV_GROUPS
    assert seq_len % NSA_KEY_TILE == 0 and seq_len >= NSA_WINDOW + tq
    rows = NSA_HPG * tq
    nq = seq_len // tq
    nslc = seq_len // NSA_SLC_BLOCK
    expand = (np.arange(nslc)[:, None] == np.arange(seq_len)[None, :] // NSA_SLC_BLOCK)
    expand = jnp.asarray(expand.astype(np.float32), dtype=BF16)
    kv0 = (NSA_N_HEADS * dh + 2 * NSA_KV_DIM) // dh

    def kv_spec(idx):
        return pl.BlockSpec((seq_len, dh), lambda b, gi, qi: (b, kv0 + idx * g + gi))

    qo_spec = pl.BlockSpec((tq, NSA_HPG * dh), lambda b, gi, qi: (b * nq + qi, gi))
    return pl.pallas_call(
        _nsa_attn_kernel,
        grid=(bsz, g, nq),
        in_specs=[
            qo_spec,
            kv_spec(0), kv_spec(1), kv_spec(2), kv_spec(3),
            pl.BlockSpec((None, None, nslc, tq), lambda b, gi, qi: (b, gi, 0, qi)),
            pl.BlockSpec((nslc, seq_len), lambda b, gi, qi: (0, 0)),
            qo_spec,
            pl.BlockSpec((None, tq, 3 * NSA_HPG), lambda b, gi, qi: (gi, b * nq + qi, 0)),
        ],
        out_specs=qo_spec,
        out_shape=jax.ShapeDtypeStruct((bsz * seq_len, NSA_N_HEADS * dh), BF16),
        scratch_shapes=[
            pltpu.VMEM((dh, seq_len), BF16),
            pltpu.VMEM((seq_len, 2 * dh), BF16),
            pltpu.VMEM((dh, seq_len), BF16),
            pltpu.VMEM((seq_len, 2 * dh), BF16),
            pltpu.VMEM((tq, nslc), BF16),
            pltpu.VMEM((rows, dh), BF16),
            pltpu.VMEM((rows, NSA_WINDOW + tq), BF16),
            pltpu.VMEM((rows, LANE), F32),
            pltpu.VMEM((rows, 2 * dh), F32),
        ],
        compiler_params=_cparams("parallel", "parallel", "arbitrary"),
        name="nsa_slc_win_attention",
    )(proj, proj, proj, proj, proj, sel, expand, ocmp, gate_logits)


def _sgu_kernel(u_ref, v_ref, g_ref, ws_ref, bst_ref, o_ref):
    v = v_ref[...].astype(F32)
    mu = jnp.mean(v, axis=-1, keepdims=True)
    vc = v - mu
    vn = (vc * lax.rsqrt(jnp.mean(vc * vc, axis=-1, keepdims=True) + RMS_EPS) * g_ref[...]).astype(BF16)
    q = SGU_CHUNK
    tri = lax.broadcasted_iota(jnp.int32, (q, q), 0) >= lax.broadcasted_iota(jnp.int32, (q, q), 1)
    for g in range(SGU_GROUPS):
        cols = slice(g * SGU_GROUP_DIM, (g + 1) * SGU_GROUP_DIM)
        w_m = jnp.where(tri, ws_ref[g], 0.0).astype(BF16)
        sv = _dot(w_m, vn[:, cols]) + bst_ref[:, g:g + 1]
        o_ref[:, cols] = (u_ref[:, cols].astype(F32) * sv).astype(o_ref.dtype)


def _sgu_core(proj, ln_g, w_spatial, b_spatial):
    m = proj.shape[0]
    q = SGU_CHUNK
    return pl.pallas_call(
        _sgu_kernel,
        grid=(m // q,),
        in_specs=[
            pl.BlockSpec((q, SGU_WIDTH), lambda i: (i, 0)),
            pl.BlockSpec((q, SGU_WIDTH), lambda i: (i, 1)),
            pl.BlockSpec((1, SGU_WIDTH), lambda i: (0, 0)),
            pl.BlockSpec((SGU_GROUPS, q, q), lambda i: (0, 0, 0)),
            pl.BlockSpec((q, SGU_GROUPS), lambda i: (0, 0)),
        ],
        out_specs=pl.BlockSpec((q, SGU_WIDTH), lambda i: (i, 0)),
        out_shape=jax.ShapeDtypeStruct((m, SGU_WIDTH), BF16),
        compiler_params=_cparams("parallel"),
        name="sgu_core",
    )(proj, proj, ln_g.reshape(1, SGU_WIDTH), w_spatial, b_spatial.T)


def _pool_kernel(z_ref, zh_ref, wg_ref, sc_ref, o_ref, zz_ref, *, tm, blocks_per_seq):
    i = pl.program_id(0)
    keep = (i % blocks_per_seq != 0).astype(F32)
    zz_ref[0:POOL_HALO, :] = zh_ref[...] * keep
    zz_ref[POOL_HALO:, :] = z_ref[...]
    t = (i % blocks_per_seq) * tm + lax.broadcasted_iota(jnp.int32, (tm, 1), 0)
    for gi, win in enumerate(POOL_WINDOWS):
        cols = slice(gi * POOL_GROUP_DIM, (gi + 1) * POOL_GROUP_DIM)
        s = zz_ref[POOL_HALO:POOL_HALO + tm, cols]
        for k in range(1, win):
            s = s + zz_ref[POOL_HALO - k:POOL_HALO - k + tm, cols]
        count = jnp.minimum(t + 1, win).astype(F32)
        pooled = s / count - z_ref[:, cols]
        y = _dot(pooled.astype(BF16), wg_ref[gi]) * sc_ref[:, cols]
        o_ref[:, cols] = y.astype(o_ref.dtype)


def _pool_core(z, w_group, scale, seq_len, *, tm=512):
    m, n = z.shape
    tm = min(tm, seq_len)
    halo_blocks = tm // POOL_HALO
    return pl.pallas_call(
        functools.partial(_pool_kernel, tm=tm, blocks_per_seq=seq_len // tm),
        grid=(m // tm,),
        in_specs=[
            pl.BlockSpec((tm, n), lambda i: (i, 0)),
            pl.BlockSpec((POOL_HALO, n), lambda i: (jnp.maximum(i * halo_blocks - 1, 0), 0)),
            pl.BlockSpec(w_group.shape, lambda i: (0, 0, 0)),
            pl.BlockSpec((1, n), lambda i: (0, 0)),
        ],
        out_specs=pl.BlockSpec((tm, n), lambda i: (i, 0)),
        out_shape=jax.ShapeDtypeStruct((m, n), BF16),
        scratch_shapes=[pltpu.VMEM((tm + POOL_HALO, n), F32)],
        compiler_params=_cparams("parallel"),
        name="pool_core",
    )(z, z, w_group, scale.reshape(1, n))


def _pad_cols(w, n):
    return jnp.pad(w, ((0, 0), (0, 0), (0, n - w.shape[2])))


def _ssd_layer(x, g_pre, g_post, w_in, j, conv_w, conv_b, dt_bias, a_log, d_skip, norm_g, w_out, bsz, seq_len):
    proj, _ = _ssd_in_proj(x, g_pre, w_in, j, conv_w, conv_b, seq_len)
    y = _ssd_core(proj, dt_bias, a_log, d_skip, norm_g, bsz, seq_len)
    return _matmul_post(y, w_out, j, g_post, x)


def _first_ssd_layer(x, g_pre, g_post, w_in, conv_w, conv_b, dt_bias, a_log, d_skip, norm_g, f32_weights,
                     bsz, seq_len):
    proj, cast = _ssd_in_proj(x, g_pre, w_in, 0, conv_w, conv_b, seq_len, tuple(f32_weights.values()))
    wb = dict(zip(f32_weights, cast))
    y = _ssd_core(proj, dt_bias, a_log, d_skip, norm_g, bsz, seq_len)
    return _matmul_post(y, wb["ssd_out"], 0, g_post, x), wb


def _nsa_layer(x, g_pre, g_post, w_in, j, cmp_pos, cmp_w1, cmp_w2, w_out, bsz, seq_len):
    proj = _norm_matmul(x, g_pre, w_in, j, jnp.zeros((NSA_IN_PAD,), F32))
    kvc = _nsa_compress(proj, cmp_pos, cmp_w1, cmp_w2, bsz, seq_len)
    ocmp, sel = _nsa_select(proj, kvc, bsz, seq_len)
    ngate = 3 * NSA_N_HEADS
    gl = proj[:, NSA_GATE_COL:NSA_GATE_COL + ngate].reshape(-1, NSA_KV_GROUPS, 3 * NSA_HPG).transpose(1, 0, 2)
    o = _nsa_attn(proj, sel, ocmp, gl, bsz, seq_len)
    return _matmul_post(o, w_out, j, g_post, x)


def _sgu_layer(x, g_pre, g_post, w_in, j, b_in, ln_g, w_spatial, b_spatial, w_out):
    proj = _norm_matmul(x, g_pre, w_in, j, b_in, act="gelu", out_dtype=BF16)
    y = _sgu_core(proj, ln_g, w_spatial, b_spatial)
    return _matmul_post(y, w_out, j, g_post, x)


def _pool_layer(x, g_pre, g_post, w_in, j, w_group, scale, w_out, seq_len):
    z = _norm_matmul(x, g_pre, w_in, j, jnp.zeros((D_MODEL,), F32))
    y = _pool_core(z, w_group.astype(BF16), scale, seq_len)
    return _matmul_post(y, w_out, j, g_post, x)


def kernel(x, mem, norm_pre, norm_post, norm_mem, ssd_w_in, ssd_conv_w, ssd_conv_b, ssd_dt_bias, ssd_a_log, ssd_d, ssd_norm_g, ssd_w_out, nsa_w_in, nsa_cmp_pos, nsa_cmp_w1, nsa_cmp_w2, nsa_w_out, sgu_w_in, sgu_b_in, sgu_ln_g, sgu_w_spatial, sgu_b_spatial, sgu_w_out, pool_w_in, pool_w_group, pool_scale, pool_w_out, xa_w_q, xa_w_kv, xa_w_o, ffn_w_up, ffn_conv_w, ffn_conv_b, ffn_w_down):
    bsz, seq_len, d = x.shape
    depth = norm_pre.shape[0]
    xf = x.reshape(bsz * seq_len, d)
    memf = mem.reshape(bsz * mem.shape[1], d)
    ssd_in = _pad_cols(ssd_w_in, SSD_IN_PAD).astype(BF16)
    nsa_in = _pad_cols(nsa_w_in, NSA_IN_PAD).astype(BF16)
    f32_weights = dict(ssd_out=ssd_w_out, nsa_out=nsa_w_out, sgu_in=sgu_w_in, sgu_out=sgu_w_out,
                       pool_in=pool_w_in, pool_out=pool_w_out, xa_q=xa_w_q, xa_kv=xa_w_kv, xa_o=xa_w_o,
                       ffn_up=ffn_w_up, ffn_down=ffn_w_down)
    wb = None
    for i in range(depth):
        kind, j = i % 4, i // 4
        if i == 0:
            xf, wb = _first_ssd_layer(xf, norm_pre[i, 0], norm_post[i, 0], ssd_in, ssd_conv_w[j], ssd_conv_b[j],
                                      ssd_dt_bias[j], ssd_a_log[j], ssd_d[j], ssd_norm_g[j], f32_weights,
                                      bsz, seq_len)
        elif kind == 0:
            xf = _ssd_layer(xf, norm_pre[i, 0], norm_post[i, 0], ssd_in, j, ssd_conv_w[j], ssd_conv_b[j],
                            ssd_dt_bias[j], ssd_a_log[j], ssd_d[j], ssd_norm_g[j], wb["ssd_out"], bsz, seq_len)
        elif kind == 1:
            xf = _nsa_layer(xf, norm_pre[i, 0], norm_post[i, 0], nsa_in, j, nsa_cmp_pos[j], nsa_cmp_w1[j],
                            nsa_cmp_w2[j], wb["nsa_out"], bsz, seq_len)
        elif kind == 2:
            xf = _sgu_layer(xf, norm_pre[i, 0], norm_post[i, 0], wb["sgu_in"], j, sgu_b_in[j], sgu_ln_g[j],
                            sgu_w_spatial[j], sgu_b_spatial[j], wb["sgu_out"])
        else:
            xf = _pool_layer(xf, norm_pre[i, 0], norm_post[i, 0], wb["pool_in"], j, pool_w_group[j],
                             pool_scale[j], wb["pool_out"], seq_len)
        kv = _norm_matmul(memf, norm_mem[i], wb["xa_kv"], i, jnp.zeros((2 * XA_DIM,), F32),
                          out_dtype=BF16, tm=MEM_LEN)
        xf = _xattn(xf, kv, norm_pre[i, 1], norm_post[i, 1], wb["xa_q"], wb["xa_o"], i, seq_len)
        act = _ffn_up(xf, norm_pre[i, 2], wb["ffn_up"], i, ffn_conv_w[i], ffn_conv_b[i], seq_len)
        xf = _matmul_post(act, wb["ffn_down"], i, norm_post[i, 2], xf)
    return xf.reshape(bsz, seq_len, d)
```

```python
import functools
import math

import jax
import jax.numpy as jnp
import numpy as np
from jax import lax
from jax.experimental import pallas as pl
from jax.experimental.pallas import tpu as pltpu

F32 = jnp.float32
BF16 = jnp.bfloat16

D_MODEL = 2048
RMS_EPS = 1e-6
NEG = -1e30
BIG = 1e30
MEM_LEN = 256

SSD_D_INNER = 4096
SSD_HEAD_DIM = 64
SSD_N_HEADS = 64
SSD_N_GROUPS = 8
SSD_HPG = 8
SSD_D_STATE = 128
SSD_CONV = 4
SSD_CHUNK = 128
SSD_GROUP_W = SSD_HPG * SSD_HEAD_DIM
SSD_BC_DIM = SSD_N_GROUPS * SSD_D_STATE
SSD_IN_PAD = 10752

NSA_HEAD_DIM = 128
NSA_N_HEADS = 16
NSA_KV_GROUPS = 4
NSA_HPG = 4
NSA_KV_DIM = 512
NSA_CMP_BLOCK = 32
NSA_CMP_STRIDE = 16
NSA_CMP_HIDDEN = 256
NSA_SLC_BLOCK = 64
NSA_TOPK = 16
NSA_N_LOCAL = 2
NSA_WINDOW = 512
NSA_TQ = 256
NSA_KEY_TILE = 1024
NSA_SEL_TQ = 256
NSA_IN_PAD = 5632
NSA_GATE_COL = 5120

SGU_CHUNK = 128
SGU_WIDTH = 4096
SGU_GROUPS = 8
SGU_GROUP_DIM = 512

POOL_WINDOWS = (2, 4, 8, 16)
POOL_GROUP_DIM = 512
POOL_HALO = max(POOL_WINDOWS)

XA_HEADS = 4
XA_HEAD_DIM = 128
XA_DIM = 512

FFN_HIDDEN = 5632
FFN_CONV = 3

LANE = 128
BF16_SUBLANES = 16
SSD_HALO = BF16_SUBLANES
FFN_HALO = BF16_SUBLANES
VMEM_LIMIT = 56 * 1024 * 1024


def _cparams(*sem):
    return pltpu.CompilerParams(dimension_semantics=sem, vmem_limit_bytes=VMEM_LIMIT)


def _dot(a, b):
    return jnp.dot(a, b, preferred_element_type=F32)


def _dot_nt(a, b):
    return lax.dot_general(a, b, (((1,), (1,)), ((), ())), preferred_element_type=F32)


def _split3(v):
    hi = v.astype(BF16)
    r1 = v - hi.astype(F32)
    mid = r1.astype(BF16)
    lo = (r1 - mid.astype(F32)).astype(BF16)
    return hi, mid, lo


def _dot_2piece_lhs(v, e2):
    hi = v.astype(BF16)
    mid = (v - hi.astype(F32)).astype(BF16)
    return _dot(jnp.concatenate([hi, mid], axis=1), e2)


def _dot_f32_rhs(e, v):
    hi, mid, lo = _split3(v)
    return _dot(e, hi) + _dot(e, mid) + _dot(e, lo)


def _rms(x, g):
    ms = jnp.mean(x * x, axis=-1, keepdims=True)
    return x * lax.rsqrt(ms + RMS_EPS) * g


def _sigmoid(x):
    return 1.0 / (1.0 + jnp.exp(-x))


def _silu(x):
    h = 0.5 * x
    return h + h * jnp.tanh(h)


def _gelu(x):
    c = math.sqrt(2.0 / math.pi)
    return x * (0.5 * (1.0 + jnp.tanh(c * (x + 0.044715 * (x * x * x)))))


def _softplus(x):
    return jnp.maximum(x, 0.0) + jnp.log1p(jnp.exp(-jnp.abs(x)))


def _norm_mm_kernel(x_ref, g_ref, w_ref, b_ref, o_ref, h_ref, *, act):
    @pl.when(pl.program_id(1) == 0)
    def _():
        h_ref[...] = _rms(x_ref[...], g_ref[...]).astype(BF16)

    y = _dot(h_ref[...], w_ref[...]) + b_ref[...]
    if act == "gelu":
        y = _gelu(y)
    o_ref[...] = y.astype(o_ref.dtype)


def _norm_matmul(x, g, w, li, b, *, act=None, out_dtype=F32, tm=1024, tn=512):
    m, k = x.shape
    n = w.shape[2]
    tm = min(tm, m)
    assert m % tm == 0 and n % tn == 0, (m, n, tm, tn)
    return pl.pallas_call(
        functools.partial(_norm_mm_kernel, act=act),
        grid=(m // tm, n // tn),
        in_specs=[
            pl.BlockSpec((tm, k), lambda i, j: (i, 0)),
            pl.BlockSpec((1, k), lambda i, j: (0, 0)),
            pl.BlockSpec((None, k, tn), lambda i, j: (li, 0, j)),
            pl.BlockSpec((1, tn), lambda i, j: (0, j)),
        ],
        out_specs=pl.BlockSpec((tm, tn), lambda i, j: (i, j)),
        out_shape=jax.ShapeDtypeStruct((m, n), out_dtype),
        scratch_shapes=[pltpu.VMEM((tm, k), BF16)],
        compiler_params=_cparams("parallel", "arbitrary"),
        name="norm_matmul",
    )(x, g.reshape(1, k), w, b.reshape(1, n))


def _mm_post_kernel(a_ref, w_ref, g_ref, r_ref, o_ref, *, nj, tn):
    j = pl.program_id(1)
    o_ref[:, pl.ds(pl.multiple_of(j * tn, tn), tn)] = _dot(a_ref[...], w_ref[...])

    @pl.when(j == nj - 1)
    def _():
        o_ref[...] = r_ref[...] + _rms(o_ref[...], g_ref[...])


def _matmul_post(a, w, li, g, res, *, tm=512, tn=512):
    m, kdim = a.shape
    n = w.shape[2]
    tm = min(tm, m)
    assert m % tm == 0 and n % tn == 0
    nj = n // tn
    return pl.pallas_call(
        functools.partial(_mm_post_kernel, nj=nj, tn=tn),
        grid=(m // tm, nj),
        in_specs=[
            pl.BlockSpec((tm, kdim), lambda i, j: (i, 0)),
            pl.BlockSpec((None, kdim, tn), lambda i, j: (li, 0, j)),
            pl.BlockSpec((1, n), lambda i, j: (0, 0)),
            pl.BlockSpec((tm, n), lambda i, j: (i, 0)),
        ],
        out_specs=pl.BlockSpec((tm, n), lambda i, j: (i, 0)),
        out_shape=jax.ShapeDtypeStruct((m, n), F32),
        compiler_params=_cparams("parallel", "arbitrary"),
        name="matmul_postnorm_residual",
    )(a, w, g.reshape(1, n), res)


def _ffn_up_kernel(x_ref, xh_ref, g_ref, wg_ref, wv_ref, cwg_ref, cwv_ref, cbg_ref, cbv_ref,
                   o_ref, h_ref, *, tm, blocks_per_seq):
    i = pl.program_id(0)

    @pl.when(pl.program_id(1) == 0)
    def _():
        keep = (i % blocks_per_seq != 0).astype(F32)
        h_ref[0:FFN_HALO, :] = (_rms(xh_ref[...], g_ref[...]) * keep).astype(BF16)
        h_ref[FFN_HALO:, :] = _rms(x_ref[...], g_ref[...]).astype(BF16)

    h = h_ref[...]

    def conv(u, cw_ref, cb_ref):
        y = cb_ref[...] + u[FFN_HALO:] * cw_ref[FFN_CONV - 1:FFN_CONV, :]
        for tap in range(FFN_CONV - 1):
            back = FFN_CONV - 1 - tap
            y = y + pltpu.roll(u, back, axis=0)[FFN_HALO:] * cw_ref[tap:tap + 1, :]
        return y

    gate = conv(_dot(h, wg_ref[...]), cwg_ref, cbg_ref)
    val = conv(_dot(h, wv_ref[...]), cwv_ref, cbv_ref)
    o_ref[...] = (_silu(gate) * val).astype(o_ref.dtype)


def _ffn_up(x, g, w_up, li, conv_w, conv_b, seq_len, *, tm=1024, tn=512):
    m, k = x.shape
    tm = min(tm, seq_len)
    assert m % tm == 0 and seq_len % tm == 0 and FFN_HIDDEN % tn == 0
    nj = FFN_HIDDEN // tn
    halo_blocks = tm // FFN_HALO
    return pl.pallas_call(
        functools.partial(_ffn_up_kernel, tm=tm, blocks_per_seq=seq_len // tm),
        grid=(m // tm, nj),
        in_specs=[
            pl.BlockSpec((tm, k), lambda i, j: (i, 0)),
            pl.BlockSpec((FFN_HALO, k), lambda i, j: (jnp.maximum(i * halo_blocks - 1, 0), 0)),
            pl.BlockSpec((1, k), lambda i, j: (0, 0)),
            pl.BlockSpec((None, k, tn), lambda i, j: (li, 0, j)),
            pl.BlockSpec((None, k, tn), lambda i, j: (li, 0, j + nj)),
            pl.BlockSpec((FFN_CONV, tn), lambda i, j: (0, j)),
            pl.BlockSpec((FFN_CONV, tn), lambda i, j: (0, j + nj)),
            pl.BlockSpec((1, tn), lambda i, j: (0, j)),
            pl.BlockSpec((1, tn), lambda i, j: (0, j + nj)),
        ],
        out_specs=pl.BlockSpec((tm, tn), lambda i, j: (i, j)),
        out_shape=jax.ShapeDtypeStruct((m, FFN_HIDDEN), BF16),
        scratch_shapes=[pltpu.VMEM((tm + FFN_HALO, k), BF16)],
        compiler_params=_cparams("parallel", "arbitrary"),
        name="ffn_up_conv_gate",
    )(x, x, g.reshape(1, k), w_up, w_up, conv_w, conv_w,
      conv_b.reshape(1, -1), conv_b.reshape(1, -1))


def _xattn_kernel(x_ref, kv_ref, gpre_ref, gpost_ref, wq_ref, wo_ref, o_ref, att_ref):
    x = x_ref[...]
    h = _rms(x, gpre_ref[...]).astype(BF16)
    c = (XA_HEAD_DIM ** -0.5) * math.log2(math.e)
    q = (_dot(h, wq_ref[...]) * c).astype(BF16)
    for hd in range(XA_HEADS):
        lo = hd * XA_HEAD_DIM
        s = _dot_nt(q[:, lo:lo + XA_HEAD_DIM], kv_ref[:, lo:lo + XA_HEAD_DIM])
        e = jnp.exp2(s - jnp.max(s, axis=-1, keepdims=True))
        pv = _dot(e.astype(BF16), kv_ref[:, XA_DIM + lo:XA_DIM + lo + XA_HEAD_DIM])
        att_ref[:, lo:lo + XA_HEAD_DIM] = (pv / jnp.sum(e, axis=-1, keepdims=True)).astype(BF16)
    a = _dot(att_ref[...], wo_ref[...])
    o_ref[...] = x + _rms(a, gpost_ref[...])


def _xattn(x, kv, g_pre, g_post, w_q, w_o, li, seq_len, *, tm=512):
    m, k = x.shape
    tm = min(tm, seq_len)
    bps = seq_len // tm
    return pl.pallas_call(
        _xattn_kernel,
        grid=(m // tm,),
        in_specs=[
            pl.BlockSpec((tm, k), lambda i: (i, 0)),
            pl.BlockSpec((MEM_LEN, 2 * XA_DIM), lambda i: (i // bps, 0)),
            pl.BlockSpec((1, k), lambda i: (0, 0)),
            pl.BlockSpec((1, k), lambda i: (0, 0)),
            pl.BlockSpec((None, k, XA_DIM), lambda i: (li, 0, 0)),
            pl.BlockSpec((None, XA_DIM, k), lambda i: (li, 0, 0)),
        ],
        out_specs=pl.BlockSpec((tm, k), lambda i: (i, 0)),
        out_shape=jax.ShapeDtypeStruct((m, k), F32),
        scratch_shapes=[pltpu.VMEM((tm, XA_DIM), BF16)],
        compiler_params=_cparams("parallel"),
        name="memory_cross_attention",
    )(x, kv, g_pre.reshape(1, k), g_post.reshape(1, k), w_q, w_o)


def _ssd_in_kernel(x_ref, xh_ref, g_ref, w_ref, cw_ref, cb_ref, *rest, blocks_per_seq, tn, n_riders):
    rider_in = rest[:n_riders]
    o_ref = rest[n_riders]
    rider_out = rest[n_riders + 1:2 * n_riders + 1]
    h_ref = rest[2 * n_riders + 1]
    i = pl.program_id(0)
    j = pl.program_id(1)
    z_tiles = SSD_D_INNER // tn
    conv_tiles = (SSD_D_INNER + 2 * SSD_BC_DIM) // tn
    def cast_riders():
        for src, dst in zip(rider_in, rider_out):
            dst[...] = src[...].astype(dst.dtype)

    @pl.when(j == 0)
    def _():
        keep = (i % blocks_per_seq != 0).astype(F32)
        h_ref[0:SSD_HALO, :] = (_rms(xh_ref[...], g_ref[...]) * keep).astype(BF16)
        h_ref[SSD_HALO:, :] = _rms(x_ref[...], g_ref[...]).astype(BF16)

    @pl.when(j < z_tiles)
    def _():
        cast_riders()
        o_ref[...] = _silu(_dot(h_ref[SSD_HALO:, :], w_ref[...]))

    @pl.when((j >= z_tiles) & (j < z_tiles + conv_tiles))
    def _():
        cast_riders()
        u = _dot(h_ref[...], w_ref[...])
        y = cb_ref[...] + u[SSD_HALO:] * cw_ref[SSD_CONV - 1:SSD_CONV, :]
        for tap in range(SSD_CONV - 1):
            back = SSD_CONV - 1 - tap
            y = y + pltpu.roll(u, back, axis=0)[SSD_HALO:] * cw_ref[tap:tap + 1, :]
        o_ref[...] = _silu(y)

    @pl.when(j >= z_tiles + conv_tiles)
    def _():
        cast_riders()
        o_ref[...] = _dot(h_ref[SSD_HALO:, :], w_ref[...])


def _rider_rows(rows, steps):
    r = BF16_SUBLANES
    while rows % r or rows // r > steps:
        r += BF16_SUBLANES
        assert r <= rows, (rows, steps)
    return r


def _ssd_in_proj(x, g, w_in, li, conv_w, conv_b, seq_len, riders=(), *, tm=1024, tn=512):
    m, k = x.shape
    n = w_in.shape[2]
    tm = min(tm, seq_len)
    assert m % tm == 0 and seq_len % tm == 0 and n % tn == 0
    z_tiles = SSD_D_INNER // tn
    conv_tiles = (SSD_D_INNER + 2 * SSD_BC_DIM) // tn
    halo_blocks = tm // SSD_HALO
    conv_tile = lambda j: jnp.clip(j - z_tiles, 0, conv_tiles - 1)
    nj = n // tn
    steps = (m // tm) * nj
    flat = [r.reshape(-1, r.shape[-1]) for r in riders]
    rider_specs = []
    for r in flat:
        rows = _rider_rows(r.shape[0], steps)
        last = r.shape[0] // rows - 1
        rider_specs.append(pl.BlockSpec(
            (rows, r.shape[1]), lambda i, j, last=last: (jnp.minimum(i * nj + j, last), 0)))
    outs = pl.pallas_call(
        functools.partial(_ssd_in_kernel, blocks_per_seq=seq_len // tm, tn=tn, n_riders=len(flat)),
        grid=(m // tm, nj),
        in_specs=[
            pl.BlockSpec((tm, k), lambda i, j: (i, 0)),
            pl.BlockSpec((SSD_HALO, k), lambda i, j: (jnp.maximum(i * halo_blocks - 1, 0), 0)),
            pl.BlockSpec((1, k), lambda i, j: (0, 0)),
            pl.BlockSpec((None, k, tn), lambda i, j: (li, 0, j)),
            pl.BlockSpec((SSD_CONV, tn), lambda i, j: (0, conv_tile(j))),
            pl.BlockSpec((1, tn), lambda i, j: (0, conv_tile(j))),
        ] + rider_specs,
        out_specs=[pl.BlockSpec((tm, tn), lambda i, j: (i, j))] + rider_specs,
        out_shape=[jax.ShapeDtypeStruct((m, n), F32)]
        + [jax.ShapeDtypeStruct(r.shape, BF16) for r in flat],
        scratch_shapes=[pltpu.VMEM((tm + SSD_HALO, k), BF16)],
        compiler_params=_cparams("arbitrary", "arbitrary"),
        name="ssd_in_proj_conv",
    )(x, x, g.reshape(1, k), w_in, conv_w, conv_b.reshape(1, -1), *flat)
    return outs[0], [o.reshape(r.shape) for o, r in zip(outs[1:], riders)]


def _ssd_kernel(zs_ref, xs_ref, bm_ref, cm_ref, dtr_ref, dtb_ref, alog_ref,
                dskip_ref, ng_ref, expand_ref, tril_ref, o_ref,
                wx_ref, expax_ref, cdx_ref, acsg_ref, acst_ref, state_ref, y_ref):
    q = SSD_CHUNK
    c = pl.program_id(1)

    @pl.when(c == 0)
    def _():
        state_ref[...] = jnp.zeros(state_ref.shape, F32)

    dt = _softplus(dtr_ref[...] + dtb_ref[...])
    a = dt * (-jnp.exp(alog_ref[...]))
    a_cs = _dot_f32_rhs(tril_ref[...], a)
    a_last = a_cs[q - 1:q, :]
    expand = expand_ref[...]
    wx_ref[...] = _dot_2piece_lhs(dt * jnp.exp(a_last - a_cs), expand)
    expax_ref[...] = _dot_2piece_lhs(jnp.exp(a_cs), expand)
    cdx_ref[...] = _dot_2piece_lhs(jnp.broadcast_to(jnp.exp(a_last), (8, LANE)), expand)
    acst_ref[...] = (a_cs - jnp.log(dt)).T.reshape(LANE // SSD_HPG, SSD_HPG, q)
    for g in range(SSD_N_GROUPS):
        acsg_ref[g] = a_cs if g == 0 else pltpu.roll(a_cs, LANE - g * SSD_HPG, axis=1)

    row = lax.broadcasted_iota(jnp.int32, (q, q), 0)
    col = lax.broadcasted_iota(jnp.int32, (q, q), 1)
    causal = row >= col
    left_half = col < SSD_HEAD_DIM

    def group_body(g, carry):
        off = pl.multiple_of(g * SSD_GROUP_W, SSD_GROUP_W)
        offn = pl.multiple_of(g * SSD_D_STATE, SSD_D_STATE)
        cm = cm_ref[:, pl.ds(offn, SSD_D_STATE)]
        bm = bm_ref[:, pl.ds(offn, SSD_D_STATE)]
        cmb = cm.astype(BF16)
        cb = _dot_nt(cmb, bm.astype(BF16))
        xs = xs_ref[:, pl.ds(off, SSD_GROUP_W)]
        st = state_ref[g]
        y_off = _dot(cmb, st.astype(BF16)) * expax_ref[:, pl.ds(off, SSD_GROUP_W)]
        s_new = _dot(bm.T.astype(BF16), (xs * wx_ref[:, pl.ds(off, SSD_GROUP_W)]).astype(BF16))
        state_ref[g] = st * cdx_ref[0:1, pl.ds(off, SSD_GROUP_W)] + s_new
        acs_g = acsg_ref[g]
        acst_g = acst_ref[g]
        skip = xs * dskip_ref[:, pl.ds(off, SSD_GROUP_W)]
        for pr in range(SSD_HPG // 2):
            scs = []
            for e in range(2):
                j = 2 * pr + e
                diff = acs_g[:, j:j + 1] - acst_g[j:j + 1, :]
                dec = jnp.where(causal, jnp.exp(diff), 0.0)
                scs.append((cb * dec).astype(BF16))
            xp = xs[:, pr * LANE:(pr + 1) * LANE]
            x0 = jnp.where(left_half, xp, 0.0).astype(BF16)
            x1 = jnp.where(left_half, 0.0, xp).astype(BF16)
            y_diag = _dot(jnp.concatenate(scs, axis=1), jnp.concatenate([x0, x1], axis=0))
            y = y_diag + y_off[:, pr * LANE:(pr + 1) * LANE] + skip[:, pr * LANE:(pr + 1) * LANE]
            y_ref[:, pl.ds(pl.multiple_of(off + pr * LANE, LANE), LANE)] = y
        return carry

    lax.fori_loop(0, SSD_N_GROUPS, group_body, 0, unroll=8)

    gated = y_ref[...] * zs_ref[...]
    o_ref[...] = _rms(gated, ng_ref[...]).astype(o_ref.dtype)


def _ssd_core(proj, dt_bias, a_log, d_skip, norm_g, bsz, seq_len):
    q = SSD_CHUNK
    nc = seq_len // q
    m = bsz * seq_len
    pad = LANE - SSD_N_HEADS
    dtb = jnp.pad(dt_bias, (0, pad)).reshape(1, LANE)
    alog = jnp.pad(a_log, (0, pad)).reshape(1, LANE)
    dskip = jnp.repeat(d_skip, SSD_HEAD_DIM).reshape(1, SSD_D_INNER)
    heads = np.arange(2 * LANE)[:, None] % LANE
    chans = np.arange(SSD_D_INNER)[None, :] // SSD_HEAD_DIM
    expand = jnp.asarray((heads == chans).astype(np.float32), dtype=BF16)
    tril = jnp.asarray(np.tril(np.ones((q, q), np.float32)), dtype=BF16)
    row = lambda b, c: b * nc + c
    const = lambda b, c: (0, 0)
    return pl.pallas_call(
        _ssd_kernel,
        grid=(bsz, nc),
        in_specs=[
            pl.BlockSpec((q, SSD_D_INNER), lambda b, c: (row(b, c), 0)),
            pl.BlockSpec((q, SSD_D_INNER), lambda b, c: (row(b, c), 1)),
            pl.BlockSpec((q, SSD_BC_DIM), lambda b, c: (row(b, c), 8)),
            pl.BlockSpec((q, SSD_BC_DIM), lambda b, c: (row(b, c), 9)),
            pl.BlockSpec((q, LANE), lambda b, c: (row(b, c), 80)),
            pl.BlockSpec((1, LANE), const),
            pl.BlockSpec((1, LANE), const),
            pl.BlockSpec((1, SSD_D_INNER), const),
            pl.BlockSpec((1, SSD_D_INNER), const),
            pl.BlockSpec((2 * LANE, SSD_D_INNER), const),
            pl.BlockSpec((q, q), const),
        ],
        out_specs=pl.BlockSpec((q, SSD_D_INNER), lambda b, c: (row(b, c), 0)),
        out_shape=jax.ShapeDtypeStruct((m, SSD_D_INNER), BF16),
        scratch_shapes=[
            pltpu.VMEM((q, SSD_D_INNER), F32),
            pltpu.VMEM((q, SSD_D_INNER), F32),
            pltpu.VMEM((8, SSD_D_INNER), F32),
            pltpu.VMEM((SSD_N_GROUPS, q, LANE), F32),
            pltpu.VMEM((LANE // SSD_HPG, SSD_HPG, q), F32),
            pltpu.VMEM((SSD_N_GROUPS, SSD_D_STATE, SSD_GROUP_W), F32),
            pltpu.VMEM((q, SSD_D_INNER), F32),
        ],
        compiler_params=_cparams("parallel", "arbitrary"),
        name="ssd_core",
    )(proj, proj, proj, proj, proj, dtb, alog, dskip, norm_g.reshape(1, SSD_D_INNER), expand, tril)


def _nsa_compress_kernel(t_ref, pos_ref, w1_ref, w2_ref, o_ref, *, nchunk):
    half = NSA_CMP_STRIDE
    dh = NSA_HEAD_DIM
    p_lo = jnp.zeros((nchunk, NSA_CMP_HIDDEN), F32)
    p_hi = jnp.zeros((nchunk, NSA_CMP_HIDDEN), F32)
    for l in range(half):
        rows = t_ref[pl.ds(l, nchunk, stride=half), :]
        a_lo = (rows + pos_ref[l:l + 1, :]).astype(BF16)
        a_hi = (rows + pos_ref[half + l:half + l + 1, :]).astype(BF16)
        p_lo = p_lo + _dot(a_lo, w1_ref[l * dh:(l + 1) * dh, :].astype(BF16))
        p_hi = p_hi + _dot(a_hi, w1_ref[(half + l) * dh:(half + l + 1) * dh, :].astype(BF16))
    pre = p_lo + pltpu.roll(p_hi, nchunk - 1, axis=0)
    o_ref[...] = _dot(_gelu(pre).astype(BF16), w2_ref[...].astype(BF16))


def _nsa_compress(proj, cmp_pos, cmp_w1, cmp_w2, bsz, seq_len):
    nchunk = seq_len // NSA_CMP_STRIDE
    g, dh = NSA_KV_GROUPS, NSA_HEAD_DIM
    col0 = NSA_N_HEADS * dh // dh
    return pl.pallas_call(
        functools.partial(_nsa_compress_kernel, nchunk=nchunk),
        grid=(bsz, 2, g),
        in_specs=[
            pl.BlockSpec((seq_len, dh), lambda b, s, gi: (b, col0 + s * g + gi)),
            pl.BlockSpec((None, NSA_CMP_BLOCK, dh), lambda b, s, gi: (s, 0, 0)),
            pl.BlockSpec((None, NSA_CMP_BLOCK * dh, NSA_CMP_HIDDEN), lambda b, s, gi: (s, 0, 0)),
            pl.BlockSpec((None, NSA_CMP_HIDDEN, dh), lambda b, s, gi: (s, 0, 0)),
        ],
        out_specs=pl.BlockSpec((None, None, None, nchunk, dh), lambda b, s, gi: (b, s, gi, 0, 0)),
        out_shape=jax.ShapeDtypeStruct((bsz, 2, g, nchunk, dh), F32),
        compiler_params=_cparams("parallel", "parallel", "parallel"),
        name="nsa_compress",
    )(proj, cmp_pos, cmp_w1, cmp_w2)


def _nsa_select_kernel(q_ref, kc_ref, vc_ref, cover_ref, ocmp_ref, sel_ref, *, ncmp, nslc):
    tq = NSA_SEL_TQ
    dh = NSA_HEAD_DIM
    scale = dh ** -0.5
    nck = kc_ref.shape[0]
    t0 = pl.program_id(2) * tq
    t_col = t0 + lax.broadcasted_iota(jnp.int32, (tq, 1), 0)
    ci_row = lax.broadcasted_iota(jnp.int32, (1, nck), 1)
    valid = (ci_row * NSA_CMP_STRIDE + (NSA_CMP_BLOCK - 1) <= t_col) & (ci_row < ncmp)
    t_row = t0 + lax.broadcasted_iota(jnp.int32, (1, tq), 1)
    ci_col = lax.broadcasted_iota(jnp.int32, (nck, 1), 0)
    valid_t = (ci_col * NSA_CMP_STRIDE + (NSA_CMP_BLOCK - 1) <= t_row) & (ci_col < ncmp)
    c = scale * math.log2(math.e)
    bias = jnp.where(valid, 0.0, NEG)
    bias_t = jnp.where(valid_t, 0.0, NEG)
    some = jnp.where(t_col >= NSA_CMP_BLOCK - 1, 1.0, 0.0)
    some_t = jnp.where(t_row >= NSA_CMP_BLOCK - 1, 1.0, 0.0)
    kcb = kc_ref[...].astype(BF16)
    vcb = vc_ref[...].astype(BF16)
    psum_t = jnp.zeros((nck, tq), F32)
    for j in range(NSA_HPG):
        qj = q_ref[:, j * dh:(j + 1) * dh].astype(BF16)
        s = _dot_nt(qj, kcb) * c + bias
        e = jnp.exp2(s - jnp.max(s, axis=-1, keepdims=True))
        p = e * (some / jnp.sum(e, axis=-1, keepdims=True))
        ocmp_ref[:, j * dh:(j + 1) * dh] = _dot(p.astype(BF16), vcb)
        st = _dot_nt(kcb, qj) * c + bias_t
        et = jnp.exp2(st - jnp.max(st, axis=0, keepdims=True))
        psum_t = psum_t + et * (some_t / jnp.sum(et, axis=0, keepdims=True))
    imp_t = _dot_f32_rhs(cover_ref[...], psum_t)
    blk = lax.broadcasted_iota(jnp.int32, (nslc, tq), 0)
    cur = t_row // NSA_SLC_BLOCK
    forced = (blk == 0) | ((blk <= cur) & (blk > cur - 1 - NSA_N_LOCAL))
    future = blk * NSA_SLC_BLOCK > t_row
    score = jnp.where(forced, BIG, jnp.where(future, NEG, imp_t))
    sub = 8
    slabs = [score[r * sub:(r + 1) * sub, :] for r in range(nslc // sub)]
    ranks = [jnp.zeros((sub, tq), F32) for _ in slabs]
    row_in_slab = lax.broadcasted_iota(jnp.int32, (sub, tq), 0)
    for j in range(nslc):
        cj = jnp.broadcast_to(score[j:j + 1, :], (sub, tq))
        for r, s_r in enumerate(slabs):
            if r * sub > j:
                before = cj >= s_r
            elif (r + 1) * sub - 1 < j:
                before = cj > s_r
            else:
                before = (cj > s_r) | ((cj == s_r) & (row_in_slab > j - r * sub))
            ranks[r] = ranks[r] + jnp.where(before, 1.0, 0.0)
    topk = float(min(NSA_TOPK, nslc))
    for r, rank in enumerate(ranks):
        sel_ref[r * sub:(r + 1) * sub, :] = jnp.where(rank < topk, 1.0, 0.0)


def _nsa_select(proj, kvc, bsz, seq_len):
    tq, dh, g = NSA_SEL_TQ, NSA_HEAD_DIM, NSA_KV_GROUPS
    nq = seq_len // tq
    ncmp = (seq_len - NSA_CMP_BLOCK) // NSA_CMP_STRIDE + 1
    nck = seq_len // NSA_CMP_STRIDE
    nslc = seq_len // NSA_SLC_BLOCK
    ci = np.arange(nck)[:, None] * NSA_CMP_STRIDE
    sj = np.arange(nslc)[None, :] * NSA_SLC_BLOCK
    cover = ((ci <= sj + NSA_SLC_BLOCK - 1) & (ci + NSA_CMP_BLOCK - 1 >= sj)).astype(np.float32)
    cover = jnp.asarray(cover.T, dtype=BF16)
    return pl.pallas_call(
        functools.partial(_nsa_select_kernel, ncmp=ncmp, nslc=nslc),
        grid=(bsz, g, nq),
        in_specs=[
            pl.BlockSpec((tq, NSA_HPG * dh), lambda b, gi, qi: (b * nq + qi, gi)),
            pl.BlockSpec((None, None, None, nck, dh), lambda b, gi, qi: (b, 0, gi, 0, 0)),
            pl.BlockSpec((None, None, None, nck, dh), lambda b, gi, qi: (b, 1, gi, 0, 0)),
            pl.BlockSpec((nslc, nck), lambda b, gi, qi: (0, 0)),
        ],
        out_specs=[
            pl.BlockSpec((tq, NSA_HPG * dh), lambda b, gi, qi: (b * nq + qi, gi)),
            pl.BlockSpec((None, None, nslc, tq), lambda b, gi, qi: (b, gi, 0, qi)),
        ],
        out_shape=[
            jax.ShapeDtypeStruct((bsz * seq_len, NSA_N_HEADS * dh), F32),
            jax.ShapeDtypeStruct((bsz, g, nslc, seq_len), F32),
        ],
        compiler_params=_cparams("parallel", "parallel", "parallel"),
        name="nsa_cmp_select",
    )(proj, kvc, kvc, cover)


def _nsa_attn_kernel(q_ref, ks_ref, vs_ref, kw_ref, vw_ref, sel_ref, expand_ref, ocmp_ref, gl_ref,
                     o_ref, ksb_ref, vsx_ref, kwb_ref, vwx_ref, selb_ref, q4_ref, pw_ref, m_ref, acc_ref):
    tq = NSA_TQ
    dh = NSA_HEAD_DIM
    kt_w = NSA_KEY_TILE
    win_w = NSA_WINDOW + tq
    c = (dh ** -0.5) * math.log2(math.e)
    qi = pl.program_id(2)

    @pl.when(qi == 0)
    def _():
        ones = jnp.ones((ks_ref.shape[0], dh), BF16)
        for blk in range(ks_ref.shape[0] // LANE):
            cols = slice(blk * LANE, (blk + 1) * LANE)
            ksb_ref[:, cols] = ks_ref[cols, :].T.astype(BF16)
            kwb_ref[:, cols] = kw_ref[cols, :].T.astype(BF16)
        vsx_ref[:, 0:dh] = vs_ref[...].astype(BF16)
        vsx_ref[:, dh:] = ones
        vwx_ref[:, 0:dh] = vw_ref[...].astype(BF16)
        vwx_ref[:, dh:] = ones

    selb_ref[...] = sel_ref[...].T.astype(BF16)
    for j in range(NSA_HPG):
        q4_ref[j * tq:(j + 1) * tq, :] = (q_ref[:, j * dh:(j + 1) * dh] * c).astype(BF16)
    t1 = qi * tq + lax.broadcasted_iota(jnp.int32, (tq, 1), 0)
    m_ref[...] = jnp.full(m_ref.shape, NEG, F32)
    acc_ref[...] = jnp.zeros(acc_ref.shape, F32)
    half = NSA_HPG // 2 * tq

    def slc_tile(off, width, on_diagonal):
        chosen = _dot(selb_ref[...], expand_ref[:, pl.ds(off, width)])
        bias = (chosen - 1.0) * BIG
        if on_diagonal:
            pos = off + lax.broadcasted_iota(jnp.int32, (1, width), 1)
            bias = jnp.where(pos <= t1, bias, NEG)
        k_t = ksb_ref[:, pl.ds(off, width)]
        v_x = vsx_ref[pl.ds(off, width), :]
        pairs = [slice(hp * half, (hp + 1) * half) for hp in range(2)]
        scores = [_dot(q4_ref[pr], k_t) for pr in pairs]
        for hp, pr in enumerate(pairs):
            m_prev = m_ref[pr]
            ps, m_news = [], []
            for e in range(NSA_HPG // 2):
                sj = scores[hp][e * tq:(e + 1) * tq] + bias
                m_new = jnp.maximum(m_prev[e * tq:(e + 1) * tq], jnp.max(sj, axis=-1, keepdims=True))
                m_wide = jnp.concatenate([m_new] * (width // LANE), axis=1)
                ps.append(jnp.exp2(sj - m_wide).astype(BF16))
                m_news.append(m_new)
            m_new = jnp.concatenate(m_news, axis=0)
            alpha = jnp.exp2(m_prev - m_new)
            m_ref[pr] = m_new
            pv = _dot(jnp.concatenate(ps, axis=0), v_x)
            acc_ref[pr] = jnp.concatenate([alpha, alpha], axis=1) * acc_ref[pr] + pv

    n_wide = (qi * tq) // kt_w

    def wide_step(kt, carry):
        slc_tile(pl.multiple_of(kt * kt_w, kt_w), kt_w, False)
        return carry

    def narrow_step(s, carry):
        slc_tile(pl.multiple_of(n_wide * kt_w + s * tq, tq), tq, True)
        return carry

    lax.fori_loop(0, n_wide, wide_step, 0)
    lax.fori_loop(0, qi + 1 - n_wide * (kt_w // tq), narrow_step, 0)

    w0 = pl.multiple_of(jnp.maximum(qi * tq - NSA_WINDOW, 0), tq)
    sw = _dot(q4_ref[...], kwb_ref[:, pl.ds(w0, win_w)])
    diff = t1 - (w0 + lax.broadcasted_iota(jnp.int32, (1, win_w), 1))
    bias_w = jnp.where((diff >= 0) & (diff < NSA_WINDOW), 0.0, NEG)
    for j in range(NSA_HPG):
        rows = slice(j * tq, (j + 1) * tq)
        sj = sw[rows] + bias_w
        pw_ref[rows] = jnp.exp2(sj - jnp.max(sj, axis=-1, keepdims=True)).astype(BF16)
    ow = _dot(pw_ref[...], vwx_ref[pl.ds(w0, win_w), :])

    gates = _sigmoid(gl_ref[...])
    for j in range(NSA_HPG):
        rows = slice(j * tq, (j + 1) * tq)
        o_slc = acc_ref[rows, 0:dh] / acc_ref[rows, dh:]
        o_win = ow[rows, 0:dh] / ow[rows, dh:]
        o = (gates[:, 3 * j:3 * j + 1] * ocmp_ref[:, j * dh:(j + 1) * dh]
             + gates[:, 3 * j + 1:3 * j + 2] * o_slc
             + gates[:, 3 * j + 2:3 * j + 3] * o_win)
        o_ref[:, j * dh:(j + 1) * dh] = o.astype(o_ref.dtype)


def _nsa_attn(proj, sel, ocmp, gate_logits, bsz, seq_len):
    tq, dh, g = NSA_TQ, NSA_HEAD_DIM, NSA_KV_GROUPS
    assert seq_len % NSA_KEY_TILE == 0 and seq_len >= NSA_WINDOW + tq
    rows = NSA_HPG * tq
    nq = seq_len // tq
    nslc = seq_len // NSA_SLC_BLOCK
    expand = (np.arange(nslc)[:, None] == np.arange(seq_len)[None, :] // NSA_SLC_BLOCK)
    expand = jnp.asarray(expand.astype(np.float32), dtype=BF16)
    kv0 = (NSA_N_HEADS * dh + 2 * NSA_KV_DIM) // dh

    def kv_spec(idx):
        return pl.BlockSpec((seq_len, dh), lambda b, gi, qi: (b, kv0 + idx * g + gi))

    qo_spec = pl.BlockSpec((tq, NSA_HPG * dh), lambda b, gi, qi: (b * nq + qi, gi))
    return pl.pallas_call(
        _nsa_attn_kernel,
        grid=(bsz, g, nq),
        in_specs=[
            qo_spec,
            kv_spec(0), kv_spec(1), kv_spec(2), kv_spec(3),
            pl.BlockSpec((None, None, nslc, tq), lambda b, gi, qi: (b, gi, 0, qi)),
            pl.BlockSpec((nslc, seq_len), lambda b, gi, qi: (0, 0)),
            qo_spec,
            pl.BlockSpec((None, tq, 3 * NSA_HPG), lambda b, gi, qi: (gi, b * nq + qi, 0)),
        ],
        out_specs=qo_spec,
        out_shape=jax.ShapeDtypeStruct((bsz * seq_len, NSA_N_HEADS * dh), BF16),
        scratch_shapes=[
            pltpu.VMEM((dh, seq_len), BF16),
            pltpu.VMEM((seq_len, 2 * dh), BF16),
            pltpu.VMEM((dh, seq_len), BF16),
            pltpu.VMEM((seq_len, 2 * dh), BF16),
            pltpu.VMEM((tq, nslc), BF16),
            pltpu.VMEM((rows, dh), BF16),
            pltpu.VMEM((rows, NSA_WINDOW + tq), BF16),
            pltpu.VMEM((rows, LANE), F32),
            pltpu.VMEM((rows, 2 * dh), F32),
        ],
        compiler_params=_cparams("parallel", "parallel", "arbitrary"),
        name="nsa_slc_win_attention",
    )(proj, proj, proj, proj, proj, sel, expand, ocmp, gate_logits)


def _sgu_kernel(u_ref, v_ref, g_ref, ws_ref, bst_ref, o_ref):
    v = v_ref[...].astype(F32)
    mu = jnp.mean(v, axis=-1, keepdims=True)
    vc = v - mu
    vn = (vc * lax.rsqrt(jnp.mean(vc * vc, axis=-1, keepdims=True) + RMS_EPS) * g_ref[...]).astype(BF16)
    q = SGU_CHUNK
    tri = lax.broadcasted_iota(jnp.int32, (q, q), 0) >= lax.broadcasted_iota(jnp.int32, (q, q), 1)
    for g in range(SGU_GROUPS):
        cols = slice(g * SGU_GROUP_DIM, (g + 1) * SGU_GROUP_DIM)
        w_m = jnp.where(tri, ws_ref[g], 0.0).astype(BF16)
        sv = _dot(w_m, vn[:, cols]) + bst_ref[:, g:g + 1]
        o_ref[:, cols] = (u_ref[:, cols].astype(F32) * sv).astype(o_ref.dtype)


def _sgu_core(proj, ln_g, w_spatial, b_spatial):
    m = proj.shape[0]
    q = SGU_CHUNK
    return pl.pallas_call(
        _sgu_kernel,
        grid=(m // q,),
        in_specs=[
            pl.BlockSpec((q, SGU_WIDTH), lambda i: (i, 0)),
            pl.BlockSpec((q, SGU_WIDTH), lambda i: (i, 1)),
            pl.BlockSpec((1, SGU_WIDTH), lambda i: (0, 0)),
            pl.BlockSpec((SGU_GROUPS, q, q), lambda i: (0, 0, 0)),
            pl.BlockSpec((q, SGU_GROUPS), lambda i: (0, 0)),
        ],
        out_specs=pl.BlockSpec((q, SGU_WIDTH), lambda i: (i, 0)),
        out_shape=jax.ShapeDtypeStruct((m, SGU_WIDTH), BF16),
        compiler_params=_cparams("parallel"),
        name="sgu_core",
    )(proj, proj, ln_g.reshape(1, SGU_WIDTH), w_spatial, b_spatial.T)


def _pool_kernel(z_ref, zh_ref, wg_ref, sc_ref, o_ref, zz_ref, *, tm, blocks_per_seq):
    i = pl.program_id(0)
    keep = (i % blocks_per_seq != 0).astype(F32)
    zz_ref[0:POOL_HALO, :] = zh_ref[...] * keep
    zz_ref[POOL_HALO:, :] = z_ref[...]
    t = (i % blocks_per_seq) * tm + lax.broadcasted_iota(jnp.int32, (tm, 1), 0)
    for gi, win in enumerate(POOL_WINDOWS):
        cols = slice(gi * POOL_GROUP_DIM, (gi + 1) * POOL_GROUP_DIM)
        s = zz_ref[POOL_HALO:POOL_HALO + tm, cols]
        for k in range(1, win):
            s = s + zz_ref[POOL_HALO - k:POOL_HALO - k + tm, cols]
        count = jnp.minimum(t + 1, win).astype(F32)
        pooled = s / count - z_ref[:, cols]
        y = _dot(pooled.astype(BF16), wg_ref[gi]) * sc_ref[:, cols]
        o_ref[:, cols] = y.astype(o_ref.dtype)


def _pool_core(z, w_group, scale, seq_len, *, tm=512):
    m, n = z.shape
    tm = min(tm, seq_len)
    halo_blocks = tm // POOL_HALO
    return pl.pallas_call(
        functools.partial(_pool_kernel, tm=tm, blocks_per_seq=seq_len // tm),
        grid=(m // tm,),
        in_specs=[
            pl.BlockSpec((tm, n), lambda i: (i, 0)),
            pl.BlockSpec((POOL_HALO, n), lambda i: (jnp.maximum(i * halo_blocks - 1, 0), 0)),
            pl.BlockSpec(w_group.shape, lambda i: (0, 0, 0)),
            pl.BlockSpec((1, n), lambda i: (0, 0)),
        ],
        out_specs=pl.BlockSpec((tm, n), lambda i: (i, 0)),
        out_shape=jax.ShapeDtypeStruct((m, n), BF16),
        scratch_shapes=[pltpu.VMEM((tm + POOL_HALO, n), F32)],
        compiler_params=_cparams("parallel"),
        name="pool_core",
    )(z, z, w_group, scale.reshape(1, n))


def _pad_cols(w, n):
    return jnp.pad(w, ((0, 0), (0, 0), (0, n - w.shape[2])))


def _ssd_layer(x, g_pre, g_post, w_in, j, conv_w, conv_b, dt_bias, a_log, d_skip, norm_g, w_out, bsz, seq_len):
    proj, _ = _ssd_in_proj(x, g_pre, w_in, j, conv_w, conv_b, seq_len)
    y = _ssd_core(proj, dt_bias, a_log, d_skip, norm_g, bsz, seq_len)
    return _matmul_post(y, w_out, j, g_post, x)


def _first_ssd_layer(x, g_pre, g_post, w_in, conv_w, conv_b, dt_bias, a_log, d_skip, norm_g, f32_weights,
                     bsz, seq_len):
    proj, cast = _ssd_in_proj(x, g_pre, w_in, 0, conv_w, conv_b, seq_len, tuple(f32_weights.values()))
    wb = dict(zip(f32_weights, cast))
    y = _ssd_core(proj, dt_bias, a_log, d_skip, norm_g, bsz, seq_len)
    return _matmul_post(y, wb["ssd_out"], 0, g_post, x), wb


def _nsa_layer(x, g_pre, g_post, w_in, j, cmp_pos, cmp_w1, cmp_w2, w_out, bsz, seq_len):
    proj = _norm_matmul(x, g_pre, w_in, j, jnp.zeros((NSA_IN_PAD,), F32))
    kvc = _nsa_compress(proj, cmp_pos, cmp_w1, cmp_w2, bsz, seq_len)
    ocmp, sel = _nsa_select(proj, kvc, bsz, seq_len)
    ngate = 3 * NSA_N_HEADS
    gl = proj[:, NSA_GATE_COL:NSA_GATE_COL + ngate].reshape(-1, NSA_KV_GROUPS, 3 * NSA_HPG).transpose(1, 0, 2)
    o = _nsa_attn(proj, sel, ocmp, gl, bsz, seq_len)
    return _matmul_post(o, w_out, j, g_post, x)


def _sgu_layer(x, g_pre, g_post, w_in, j, b_in, ln_g, w_spatial, b_spatial, w_out):
    proj = _norm_matmul(x, g_pre, w_in, j, b_in, act="gelu", out_dtype=BF16)
    y = _sgu_core(proj, ln_g, w_spatial, b_spatial)
    return _matmul_post(y, w_out, j, g_post, x)


def _pool_layer(x, g_pre, g_post, w_in, j, w_group, scale, w_out, seq_len):
    z = _norm_matmul(x, g_pre, w_in, j, jnp.zeros((D_MODEL,), F32))
    y = _pool_core(z, w_group.astype(BF16), scale, seq_len)
    return _matmul_post(y, w_out, j, g_post, x)


def kernel(x, mem, norm_pre, norm_post, norm_mem, ssd_w_in, ssd_conv_w, ssd_conv_b, ssd_dt_bias, ssd_a_log, ssd_d, ssd_norm_g, ssd_w_out, nsa_w_in, nsa_cmp_pos, nsa_cmp_w1, nsa_cmp_w2, nsa_w_out, sgu_w_in, sgu_b_in, sgu_ln_g, sgu_w_spatial, sgu_b_spatial, sgu_w_out, pool_w_in, pool_w_group, pool_scale, pool_w_out, xa_w_q, xa_w_kv, xa_w_o, ffn_w_up, ffn_conv_w, ffn_conv_b, ffn_w_down):
    bsz, seq_len, d = x.shape
    depth = norm_pre.shape[0]
    xf = x.reshape(bsz * seq_len, d)
    memf = mem.reshape(bsz * mem.shape[1], d)
    ssd_in = _pad_cols(ssd_w_in, SSD_IN_PAD).astype(BF16)
    nsa_in = _pad_cols(nsa_w_in, NSA_IN_PAD).astype(BF16)
    f32_weights = dict(ssd_out=ssd_w_out, nsa_out=nsa_w_out, sgu_in=sgu_w_in, sgu_out=sgu_w_out,
                       pool_in=pool_w_in, pool_out=pool_w_out, xa_q=xa_w_q, xa_kv=xa_w_kv, xa_o=xa_w_o,
                       ffn_up=ffn_w_up, ffn_down=ffn_w_down)
    wb = None
    for i in range(depth):
        kind, j = i % 4, i // 4
        if i == 0:
            xf, wb = _first_ssd_layer(xf, norm_pre[i, 0], norm_post[i, 0], ssd_in, ssd_conv_w[j], ssd_conv_b[j],
                                      ssd_dt_bias[j], ssd_a_log[j], ssd_d[j], ssd_norm_g[j], f32_weights,
                                      bsz, seq_len)
        elif kind == 0:
            xf = _ssd_layer(xf, norm_pre[i, 0], norm_post[i, 0], ssd_in, j, ssd_conv_w[j], ssd_conv_b[j],
                            ssd_dt_bias[j], ssd_a_log[j], ssd_d[j], ssd_norm_g[j], wb["ssd_out"], bsz, seq_len)
        elif kind == 1:
            xf = _nsa_layer(xf, norm_pre[i, 0], norm_post[i, 0], nsa_in, j, nsa_cmp_pos[j], nsa_cmp_w1[j],
                            nsa_cmp_w2[j], wb["nsa_out"], bsz, seq_len)
        elif kind == 2:
            xf = _sgu_layer(xf, norm_pre[i, 0], norm_post[i, 0], wb["sgu_in"], j, sgu_b_in[j], sgu_ln_g[j],
                            sgu_w_spatial[j], sgu_b_spatial[j], wb["sgu_out"])
        else:
            xf = _pool_layer(xf, norm_pre[i, 0], norm_post[i, 0], wb["pool_in"], j, pool_w_group[j],
                             pool_scale[j], wb["pool_out"], seq_len)
        kv = _norm_matmul(memf, norm_mem[i], wb["xa_kv"], i, jnp.zeros((2 * XA_DIM,), F32),
                          out_dtype=BF16, tm=MEM_LEN)
        xf = _xattn(xf, kv, norm_pre[i, 1], norm_post[i, 1], wb["xa_q"], wb["xa_o"], i, seq_len)
        act = _ffn_up(xf, norm_pre[i, 2], wb["ffn_up"], i, ffn_conv_w[i], ffn_conv_b[i], seq_len)
        xf = _matmul_post(act, wb["ffn_down"], i, norm_post[i, 2], xf)
    return xf.reshape(bsz, seq_len, d)
```

```python
import functools
import math

import jax
import jax.numpy as jnp
import numpy as np
from jax import lax
from jax.experimental import pallas as pl
from jax.experimental.pallas import tpu as pltpu

F32 = jnp.float32
BF16 = jnp.bfloat16

D_MODEL = 2048
RMS_EPS = 1e-6
NEG = -1e30
BIG = 1e30
MEM_LEN = 256

SSD_D_INNER = 4096
SSD_HEAD_DIM = 64
SSD_N_HEADS = 64
SSD_N_GROUPS = 8
SSD_HPG = 8
SSD_D_STATE = 128
SSD_CONV = 4
SSD_CHUNK = 128
SSD_GROUP_W = SSD_HPG * SSD_HEAD_DIM
SSD_BC_DIM = SSD_N_GROUPS * SSD_D_STATE
SSD_IN_PAD = 10752

NSA_HEAD_DIM = 128
NSA_N_HEADS = 16
NSA_KV_GROUPS = 4
NSA_HPG = 4
NSA_KV_DIM = 512
NSA_CMP_BLOCK = 32
NSA_CMP_STRIDE = 16
NSA_CMP_HIDDEN = 256
NSA_SLC_BLOCK = 64
NSA_TOPK = 16
NSA_N_LOCAL = 2
NSA_WINDOW = 512
NSA_TQ = 256
NSA_KEY_TILE = 1024
NSA_SEL_TQ = 512
NSA_IN_PAD = 5632
NSA_GATE_COL = 5120

SGU_CHUNK = 128
SGU_WIDTH = 4096
SGU_GROUPS = 8
SGU_GROUP_DIM = 512

POOL_WINDOWS = (2, 4, 8, 16)
POOL_GROUP_DIM = 512
POOL_HALO = max(POOL_WINDOWS)

XA_HEADS = 4
XA_HEAD_DIM = 128
XA_DIM = 512

FFN_HIDDEN = 5632
FFN_CONV = 3

LANE = 128
BF16_SUBLANES = 16
SSD_HALO = BF16_SUBLANES
FFN_HALO = BF16_SUBLANES
VMEM_LIMIT = 56 * 1024 * 1024


def _cparams(*sem):
    return pltpu.CompilerParams(dimension_semantics=sem, vmem_limit_bytes=VMEM_LIMIT)


def _dot(a, b):
    return jnp.dot(a, b, preferred_element_type=F32)


def _dot_nt(a, b):
    return lax.dot_general(a, b, (((1,), (1,)), ((), ())), preferred_element_type=F32)


def _dot_tn(a, b):
    return lax.dot_general(a, b, (((0,), (0,)), ((), ())), preferred_element_type=F32)


def _split3(v):
    hi = v.astype(BF16)
    r1 = v - hi.astype(F32)
    mid = r1.astype(BF16)
    lo = (r1 - mid.astype(F32)).astype(BF16)
    return hi, mid, lo


def _dot_2piece_lhs(v, e2):
    hi = v.astype(BF16)
    mid = (v - hi.astype(F32)).astype(BF16)
    return _dot(jnp.concatenate([hi, mid], axis=1), e2)


def _dot_f32_rhs(e, v):
    hi, mid, lo = _split3(v)
    return _dot(e, hi) + _dot(e, mid) + _dot(e, lo)


def _rms(x, g):
    ms = jnp.mean(x * x, axis=-1, keepdims=True)
    return x * lax.rsqrt(ms + RMS_EPS) * g


def _sigmoid(x):
    return 1.0 / (1.0 + jnp.exp(-x))


def _silu(x):
    h = 0.5 * x
    return h + h * jnp.tanh(h)


def _gelu(x):
    c = math.sqrt(2.0 / math.pi)
    return x * (0.5 * (1.0 + jnp.tanh(c * (x + 0.044715 * (x * x * x)))))


def _softplus(x):
    return jnp.maximum(x, 0.0) + jnp.log1p(jnp.exp(-jnp.abs(x)))


def _norm_mm_kernel(x_ref, g_ref, w_ref, b_ref, o_ref, h_ref, *, act):
    @pl.when(pl.program_id(1) == 0)
    def _():
        h_ref[...] = _rms(x_ref[...], g_ref[...]).astype(BF16)

    y = _dot(h_ref[...], w_ref[...]) + b_ref[...]
    if act == "gelu":
        y = _gelu(y)
    o_ref[...] = y.astype(o_ref.dtype)


def _norm_matmul(x, g, w, li, b, *, act=None, out_dtype=F32, tm=1024, tn=512):
    m, k = x.shape
    n = w.shape[2]
    tm = min(tm, m)
    assert m % tm == 0 and n % tn == 0, (m, n, tm, tn)
    return pl.pallas_call(
        functools.partial(_norm_mm_kernel, act=act),
        grid=(m // tm, n // tn),
        in_specs=[
            pl.BlockSpec((tm, k), lambda i, j: (i, 0)),
            pl.BlockSpec((1, k), lambda i, j: (0, 0)),
            pl.BlockSpec((None, k, tn), lambda i, j: (li, 0, j)),
            pl.BlockSpec((1, tn), lambda i, j: (0, j)),
        ],
        out_specs=pl.BlockSpec((tm, tn), lambda i, j: (i, j)),
        out_shape=jax.ShapeDtypeStruct((m, n), out_dtype),
        scratch_shapes=[pltpu.VMEM((tm, k), BF16)],
        compiler_params=_cparams("parallel", "arbitrary"),
        name="norm_matmul",
    )(x, g.reshape(1, k), w, b.reshape(1, n))


def _mm_post_kernel(a_ref, w_ref, g_ref, r_ref, o_ref, *, nj, tn):
    j = pl.program_id(1)
    o_ref[:, pl.ds(pl.multiple_of(j * tn, tn), tn)] = _dot(a_ref[...], w_ref[...])

    @pl.when(j == nj - 1)
    def _():
        o_ref[...] = r_ref[...] + _rms(o_ref[...], g_ref[...])


def _matmul_post(a, w, li, g, res, *, tm=512, tn=512):
    m, kdim = a.shape
    n = w.shape[2]
    tm = min(tm, m)
    assert m % tm == 0 and n % tn == 0
    nj = n // tn
    return pl.pallas_call(
        functools.partial(_mm_post_kernel, nj=nj, tn=tn),
        grid=(m // tm, nj),
        in_specs=[
            pl.BlockSpec((tm, kdim), lambda i, j: (i, 0)),
            pl.BlockSpec((None, kdim, tn), lambda i, j: (li, 0, j)),
            pl.BlockSpec((1, n), lambda i, j: (0, 0)),
            pl.BlockSpec((tm, n), lambda i, j: (i, 0)),
        ],
        out_specs=pl.BlockSpec((tm, n), lambda i, j: (i, 0)),
        out_shape=jax.ShapeDtypeStruct((m, n), F32),
        compiler_params=_cparams("parallel", "arbitrary"),
        name="matmul_postnorm_residual",
    )(a, w, g.reshape(1, n), res)


def _ffn_up_kernel(x_ref, xh_ref, g_ref, wg_ref, wv_ref, cwg_ref, cwv_ref, cbg_ref, cbv_ref,
                   o_ref, h_ref, *, tm, blocks_per_seq):
    i = pl.program_id(0)

    @pl.when(pl.program_id(1) == 0)
    def _():
        keep = (i % blocks_per_seq != 0).astype(F32)
        h_ref[0:FFN_HALO, :] = (_rms(xh_ref[...], g_ref[...]) * keep).astype(BF16)
        h_ref[FFN_HALO:, :] = _rms(x_ref[...], g_ref[...]).astype(BF16)

    h = h_ref[...]

    def conv(u, cw_ref, cb_ref):
        y = cb_ref[...] + u[FFN_HALO:] * cw_ref[FFN_CONV - 1:FFN_CONV, :]
        for tap in range(FFN_CONV - 1):
            back = FFN_CONV - 1 - tap
            y = y + pltpu.roll(u, back, axis=0)[FFN_HALO:] * cw_ref[tap:tap + 1, :]
        return y

    gate = conv(_dot(h, wg_ref[...]), cwg_ref, cbg_ref)
    val = conv(_dot(h, wv_ref[...]), cwv_ref, cbv_ref)
    o_ref[...] = (_silu(gate) * val).astype(o_ref.dtype)


def _ffn_up(x, g, w_up, li, conv_w, conv_b, seq_len, *, tm=1024, tn=512):
    m, k = x.shape
    tm = min(tm, seq_len)
    assert m % tm == 0 and seq_len % tm == 0 and FFN_HIDDEN % tn == 0
    nj = FFN_HIDDEN // tn
    halo_blocks = tm // FFN_HALO
    return pl.pallas_call(
        functools.partial(_ffn_up_kernel, tm=tm, blocks_per_seq=seq_len // tm),
        grid=(m // tm, nj),
        in_specs=[
            pl.BlockSpec((tm, k), lambda i, j: (i, 0)),
            pl.BlockSpec((FFN_HALO, k), lambda i, j: (jnp.maximum(i * halo_blocks - 1, 0), 0)),
            pl.BlockSpec((1, k), lambda i, j: (0, 0)),
            pl.BlockSpec((None, k, tn), lambda i, j: (li, 0, j)),
            pl.BlockSpec((None, k, tn), lambda i, j: (li, 0, j + nj)),
            pl.BlockSpec((FFN_CONV, tn), lambda i, j: (0, j)),
            pl.BlockSpec((FFN_CONV, tn), lambda i, j: (0, j + nj)),
            pl.BlockSpec((1, tn), lambda i, j: (0, j)),
            pl.BlockSpec((1, tn), lambda i, j: (0, j + nj)),
        ],
        out_specs=pl.BlockSpec((tm, tn), lambda i, j: (i, j)),
        out_shape=jax.ShapeDtypeStruct((m, FFN_HIDDEN), BF16),
        scratch_shapes=[pltpu.VMEM((tm + FFN_HALO, k), BF16)],
        compiler_params=_cparams("parallel", "arbitrary"),
        name="ffn_up_conv_gate",
    )(x, x, g.reshape(1, k), w_up, w_up, conv_w, conv_w,
      conv_b.reshape(1, -1), conv_b.reshape(1, -1))


def _xattn_kernel(x_ref, kv_ref, gpre_ref, gpost_ref, wq_ref, wo_ref, o_ref, att_ref):
    x = x_ref[...]
    h = _rms(x, gpre_ref[...]).astype(BF16)
    c = (XA_HEAD_DIM ** -0.5) * math.log2(math.e)
    q = (_dot(h, wq_ref[...]) * c).astype(BF16)
    for hd in range(XA_HEADS):
        lo = hd * XA_HEAD_DIM
        s = _dot_nt(q[:, lo:lo + XA_HEAD_DIM], kv_ref[:, lo:lo + XA_HEAD_DIM])
        e = jnp.exp2(s - jnp.max(s, axis=-1, keepdims=True))
        pv = _dot(e.astype(BF16), kv_ref[:, XA_DIM + lo:XA_DIM + lo + XA_HEAD_DIM])
        att_ref[:, lo:lo + XA_HEAD_DIM] = (pv / jnp.sum(e, axis=-1, keepdims=True)).astype(BF16)
    a = _dot(att_ref[...], wo_ref[...])
    o_ref[...] = x + _rms(a, gpost_ref[...])


def _xattn(x, kv, g_pre, g_post, w_q, w_o, li, seq_len, *, tm=512):
    m, k = x.shape
    tm = min(tm, seq_len)
    bps = seq_len // tm
    return pl.pallas_call(
        _xattn_kernel,
        grid=(m // tm,),
        in_specs=[
            pl.BlockSpec((tm, k), lambda i: (i, 0)),
            pl.BlockSpec((MEM_LEN, 2 * XA_DIM), lambda i: (i // bps, 0)),
            pl.BlockSpec((1, k), lambda i: (0, 0)),
            pl.BlockSpec((1, k), lambda i: (0, 0)),
            pl.BlockSpec((None, k, XA_DIM), lambda i: (li, 0, 0)),
            pl.BlockSpec((None, XA_DIM, k), lambda i: (li, 0, 0)),
        ],
        out_specs=pl.BlockSpec((tm, k), lambda i: (i, 0)),
        out_shape=jax.ShapeDtypeStruct((m, k), F32),
        scratch_shapes=[pltpu.VMEM((tm, XA_DIM), BF16)],
        compiler_params=_cparams("parallel"),
        name="memory_cross_attention",
    )(x, kv, g_pre.reshape(1, k), g_post.reshape(1, k), w_q, w_o)


def _ssd_in_kernel(x_ref, xh_ref, g_ref, w_ref, cw_ref, cb_ref, *rest, blocks_per_seq, tn, n_riders):
    rider_in = rest[:n_riders]
    o_ref = rest[n_riders]
    rider_out = rest[n_riders + 1:2 * n_riders + 1]
    h_ref = rest[2 * n_riders + 1]
    i = pl.program_id(0)
    j = pl.program_id(1)
    z_tiles = SSD_D_INNER // tn
    conv_tiles = (SSD_D_INNER + 2 * SSD_BC_DIM) // tn
    def cast_riders():
        for src, dst in zip(rider_in, rider_out):
            dst[...] = src[...].astype(dst.dtype)

    @pl.when(j == 0)
    def _():
        keep = (i % blocks_per_seq != 0).astype(F32)
        h_ref[0:SSD_HALO, :] = (_rms(xh_ref[...], g_ref[...]) * keep).astype(BF16)
        h_ref[SSD_HALO:, :] = _rms(x_ref[...], g_ref[...]).astype(BF16)

    @pl.when(j < z_tiles)
    def _():
        cast_riders()
        o_ref[...] = _silu(_dot(h_ref[SSD_HALO:, :], w_ref[...]))

    @pl.when((j >= z_tiles) & (j < z_tiles + conv_tiles))
    def _():
        cast_riders()
        u = _dot(h_ref[...], w_ref[...])
        y = cb_ref[...] + u[SSD_HALO:] * cw_ref[SSD_CONV - 1:SSD_CONV, :]
        for tap in range(SSD_CONV - 1):
            back = SSD_CONV - 1 - tap
            y = y + pltpu.roll(u, back, axis=0)[SSD_HALO:] * cw_ref[tap:tap + 1, :]
        o_ref[...] = _silu(y)

    @pl.when(j >= z_tiles + conv_tiles)
    def _():
        cast_riders()
        o_ref[...] = _dot(h_ref[SSD_HALO:, :], w_ref[...])


def _rider_rows(rows, steps):
    r = BF16_SUBLANES
    while rows % r or rows // r > steps:
        r += BF16_SUBLANES
        assert r <= rows, (rows, steps)
    return r


def _ssd_in_proj(x, g, w_in, li, conv_w, conv_b, seq_len, riders=(), *, tm=1024, tn=512):
    m, k = x.shape
    n = w_in.shape[2]
    tm = min(tm, seq_len)
    assert m % tm == 0 and seq_len % tm == 0 and n % tn == 0
    z_tiles = SSD_D_INNER // tn
    conv_tiles = (SSD_D_INNER + 2 * SSD_BC_DIM) // tn
    halo_blocks = tm // SSD_HALO
    conv_tile = lambda j: jnp.clip(j - z_tiles, 0, conv_tiles - 1)
    nj = n // tn
    steps = (m // tm) * nj
    flat = [r.reshape(-1, r.shape[-1]) for r in riders]
    rider_specs = []
    for r in flat:
        rows = _rider_rows(r.shape[0], steps)
        last = r.shape[0] // rows - 1
        rider_specs.append(pl.BlockSpec(
            (rows, r.shape[1]), lambda i, j, last=last: (jnp.minimum(i * nj + j, last), 0)))
    outs = pl.pallas_call(
        functools.partial(_ssd_in_kernel, blocks_per_seq=seq_len // tm, tn=tn, n_riders=len(flat)),
        grid=(m // tm, nj),
        in_specs=[
            pl.BlockSpec((tm, k), lambda i, j: (i, 0)),
            pl.BlockSpec((SSD_HALO, k), lambda i, j: (jnp.maximum(i * halo_blocks - 1, 0), 0)),
            pl.BlockSpec((1, k), lambda i, j: (0, 0)),
            pl.BlockSpec((None, k, tn), lambda i, j: (li, 0, j)),
            pl.BlockSpec((SSD_CONV, tn), lambda i, j: (0, conv_tile(j))),
            pl.BlockSpec((1, tn), lambda i, j: (0, conv_tile(j))),
        ] + rider_specs,
        out_specs=[pl.BlockSpec((tm, tn), lambda i, j: (i, j))] + rider_specs,
        out_shape=[jax.ShapeDtypeStruct((m, n), F32)]
        + [jax.ShapeDtypeStruct(r.shape, BF16) for r in flat],
        scratch_shapes=[pltpu.VMEM((tm + SSD_HALO, k), BF16)],
        compiler_params=_cparams("arbitrary", "arbitrary"),
        name="ssd_in_proj_conv",
    )(x, x, g.reshape(1, k), w_in, conv_w, conv_b.reshape(1, -1), *flat)
    return outs[0], [o.reshape(r.shape) for o, r in zip(outs[1:], riders)]


def _ssd_kernel(zs_ref, xs_ref, bm_ref, cm_ref, dtr_ref, dtb_ref, alog_ref,
                dskip_ref, ng_ref, expand_ref, tril_ref, o_ref,
                wx_ref, expax_ref, cdx_ref, acsg_ref, acst_ref, state_ref, y_ref):
    q = SSD_CHUNK
    c = pl.program_id(1)

    @pl.when(c == 0)
    def _():
        state_ref[...] = jnp.zeros(state_ref.shape, F32)

    dt = _softplus(dtr_ref[...] + dtb_ref[...])
    a = dt * (-jnp.exp(alog_ref[...]))
    a_cs = _dot_f32_rhs(tril_ref[...], a)
    a_last = a_cs[q - 1:q, :]
    expand = expand_ref[...]
    wx_ref[...] = _dot_2piece_lhs(dt * jnp.exp(a_last - a_cs), expand)
    expax_ref[...] = _dot_2piece_lhs(jnp.exp(a_cs), expand)
    cdx_ref[...] = _dot_2piece_lhs(jnp.broadcast_to(jnp.exp(a_last), (8, LANE)), expand)
    acst_ref[...] = (a_cs - jnp.log(dt)).T.reshape(LANE // SSD_HPG, SSD_HPG, q)
    for g in range(SSD_N_GROUPS):
        acsg_ref[g] = a_cs if g == 0 else pltpu.roll(a_cs, LANE - g * SSD_HPG, axis=1)

    row = lax.broadcasted_iota(jnp.int32, (q, q), 0)
    col = lax.broadcasted_iota(jnp.int32, (q, q), 1)
    causal = row >= col
    left_half = col < SSD_HEAD_DIM

    def group_body(g, carry):
        off = pl.multiple_of(g * SSD_GROUP_W, SSD_GROUP_W)
        offn = pl.multiple_of(g * SSD_D_STATE, SSD_D_STATE)
        cm = cm_ref[:, pl.ds(offn, SSD_D_STATE)]
        bm = bm_ref[:, pl.ds(offn, SSD_D_STATE)]
        cmb = cm.astype(BF16)
        cb = _dot_nt(cmb, bm.astype(BF16))
        xs = xs_ref[:, pl.ds(off, SSD_GROUP_W)]
        st = state_ref[g]
        y_off = _dot(cmb, st.astype(BF16)) * expax_ref[:, pl.ds(off, SSD_GROUP_W)]
        s_new = _dot_tn(bm.astype(BF16), (xs * wx_ref[:, pl.ds(off, SSD_GROUP_W)]).astype(BF16))
        state_ref[g] = st * cdx_ref[0:1, pl.ds(off, SSD_GROUP_W)] + s_new
        acs_g = acsg_ref[g]
        acst_g = acst_ref[g]
        skip = xs * dskip_ref[:, pl.ds(off, SSD_GROUP_W)]
        for pr in range(SSD_HPG // 2):
            scs = []
            for e in range(2):
                j = 2 * pr + e
                diff = acs_g[:, j:j + 1] - acst_g[j:j + 1, :]
                dec = jnp.where(causal, jnp.exp(diff), 0.0)
                scs.append((cb * dec).astype(BF16))
            xp = xs[:, pr * LANE:(pr + 1) * LANE]
            x0 = jnp.where(left_half, xp, 0.0).astype(BF16)
            x1 = jnp.where(left_half, 0.0, xp).astype(BF16)
            y_diag = _dot(jnp.concatenate(scs, axis=1), jnp.concatenate([x0, x1], axis=0))
            y = y_diag + y_off[:, pr * LANE:(pr + 1) * LANE] + skip[:, pr * LANE:(pr + 1) * LANE]
            y_ref[:, pl.ds(pl.multiple_of(off + pr * LANE, LANE), LANE)] = y
        return carry

    lax.fori_loop(0, SSD_N_GROUPS, group_body, 0, unroll=8)

    gated = y_ref[...] * zs_ref[...]
    o_ref[...] = _rms(gated, ng_ref[...]).astype(o_ref.dtype)


def _ssd_core(proj, dt_bias, a_log, d_skip, norm_g, bsz, seq_len):
    q = SSD_CHUNK
    nc = seq_len // q
    m = bsz * seq_len
    pad = LANE - SSD_N_HEADS
    dtb = jnp.pad(dt_bias, (0, pad)).reshape(1, LANE)
    alog = jnp.pad(a_log, (0, pad)).reshape(1, LANE)
    dskip = jnp.repeat(d_skip, SSD_HEAD_DIM).reshape(1, SSD_D_INNER)
    heads = np.arange(2 * LANE)[:, None] % LANE
    chans = np.arange(SSD_D_INNER)[None, :] // SSD_HEAD_DIM
    expand = jnp.asarray((heads == chans).astype(np.float32), dtype=BF16)
    tril = jnp.asarray(np.tril(np.ones((q, q), np.float32)), dtype=BF16)
    row = lambda b, c: b * nc + c
    const = lambda b, c: (0, 0)
    return pl.pallas_call(
        _ssd_kernel,
        grid=(bsz, nc),
        in_specs=[
            pl.BlockSpec((q, SSD_D_INNER), lambda b, c: (row(b, c), 0)),
            pl.BlockSpec((q, SSD_D_INNER), lambda b, c: (row(b, c), 1)),
            pl.BlockSpec((q, SSD_BC_DIM), lambda b, c: (row(b, c), 8)),
            pl.BlockSpec((q, SSD_BC_DIM), lambda b, c: (row(b, c), 9)),
            pl.BlockSpec((q, LANE), lambda b, c: (row(b, c), 80)),
            pl.BlockSpec((1, LANE), const),
            pl.BlockSpec((1, LANE), const),
            pl.BlockSpec((1, SSD_D_INNER), const),
            pl.BlockSpec((1, SSD_D_INNER), const),
            pl.BlockSpec((2 * LANE, SSD_D_INNER), const),
            pl.BlockSpec((q, q), const),
        ],
        out_specs=pl.BlockSpec((q, SSD_D_INNER), lambda b, c: (row(b, c), 0)),
        out_shape=jax.ShapeDtypeStruct((m, SSD_D_INNER), BF16),
        scratch_shapes=[
            pltpu.VMEM((q, SSD_D_INNER), F32),
            pltpu.VMEM((q, SSD_D_INNER), F32),
            pltpu.VMEM((8, SSD_D_INNER), F32),
            pltpu.VMEM((SSD_N_GROUPS, q, LANE), F32),
            pltpu.VMEM((LANE // SSD_HPG, SSD_HPG, q), F32),
            pltpu.VMEM((SSD_N_GROUPS, SSD_D_STATE, SSD_GROUP_W), F32),
            pltpu.VMEM((q, SSD_D_INNER), F32),
        ],
        compiler_params=_cparams("parallel", "arbitrary"),
        name="ssd_core",
    )(proj, proj, proj, proj, proj, dtb, alog, dskip, norm_g.reshape(1, SSD_D_INNER), expand, tril)


def _nsa_compress_kernel(t_ref, pos_ref, w1_ref, w2_ref, o_ref, *, nchunk):
    half = NSA_CMP_STRIDE
    dh = NSA_HEAD_DIM
    p_lo = jnp.zeros((nchunk, NSA_CMP_HIDDEN), F32)
    p_hi = jnp.zeros((nchunk, NSA_CMP_HIDDEN), F32)
    for l in range(half):
        rows = t_ref[pl.ds(l, nchunk, stride=half), :]
        a_lo = (rows + pos_ref[l:l + 1, :]).astype(BF16)
        a_hi = (rows + pos_ref[half + l:half + l + 1, :]).astype(BF16)
        p_lo = p_lo + _dot(a_lo, w1_ref[l * dh:(l + 1) * dh, :].astype(BF16))
        p_hi = p_hi + _dot(a_hi, w1_ref[(half + l) * dh:(half + l + 1) * dh, :].astype(BF16))
    pre = p_lo + pltpu.roll(p_hi, nchunk - 1, axis=0)
    o_ref[...] = _dot(_gelu(pre).astype(BF16), w2_ref[...].astype(BF16))


def _nsa_compress(proj, cmp_pos, cmp_w1, cmp_w2, bsz, seq_len):
    nchunk = seq_len // NSA_CMP_STRIDE
    g, dh = NSA_KV_GROUPS, NSA_HEAD_DIM
    col0 = NSA_N_HEADS * dh // dh
    return pl.pallas_call(
        functools.partial(_nsa_compress_kernel, nchunk=nchunk),
        grid=(bsz, 2, g),
        in_specs=[
            pl.BlockSpec((seq_len, dh), lambda b, s, gi: (b, col0 + s * g + gi)),
            pl.BlockSpec((None, NSA_CMP_BLOCK, dh), lambda b, s, gi: (s, 0, 0)),
            pl.BlockSpec((None, NSA_CMP_BLOCK * dh, NSA_CMP_HIDDEN), lambda b, s, gi: (s, 0, 0)),
            pl.BlockSpec((None, NSA_CMP_HIDDEN, dh), lambda b, s, gi: (s, 0, 0)),
        ],
        out_specs=pl.BlockSpec((None, None, None, nchunk, dh), lambda b, s, gi: (b, s, gi, 0, 0)),
        out_shape=jax.ShapeDtypeStruct((bsz, 2, g, nchunk, dh), F32),
        compiler_params=_cparams("parallel", "parallel", "parallel"),
        name="nsa_compress",
    )(proj, cmp_pos, cmp_w1, cmp_w2)


def _nsa_select_kernel(q_ref, kc_ref, vc_ref, cover_ref, ocmp_ref, sel_ref, *, ncmp, nslc):
    tq = NSA_SEL_TQ
    dh = NSA_HEAD_DIM
    scale = dh ** -0.5
    nck = kc_ref.shape[0]
    t0 = pl.program_id(2) * tq
    t_row = t0 + lax.broadcasted_iota(jnp.int32, (1, tq), 1)
    ci_col = lax.broadcasted_iota(jnp.int32, (nck, 1), 0)
    valid_t = (ci_col * NSA_CMP_STRIDE + (NSA_CMP_BLOCK - 1) <= t_row) & (ci_col < ncmp)
    c = scale * math.log2(math.e)
    bias_t = jnp.where(valid_t, 0.0, NEG)
    some_t = jnp.where(t_row >= NSA_CMP_BLOCK - 1, 1.0, 0.0)
    kcb = kc_ref[...].astype(BF16)
    vcb = vc_ref[...].astype(BF16)
    psum_t = jnp.zeros((nck, tq), F32)
    for j in range(NSA_HPG):
        qj = q_ref[:, j * dh:(j + 1) * dh].astype(BF16)
        st = _dot_nt(kcb, qj) * c + bias_t
        et = jnp.exp2(st - jnp.max(st, axis=0, keepdims=True))
        pt = et * (some_t / jnp.sum(et, axis=0, keepdims=True))
        psum_t = psum_t + pt
        ocmp_ref[:, j * dh:(j + 1) * dh] = _dot_tn(pt.astype(BF16), vcb)
    imp_t = _dot_f32_rhs(cover_ref[...], psum_t)
    blk = lax.broadcasted_iota(jnp.int32, (nslc, tq), 0)
    cur = t_row // NSA_SLC_BLOCK
    forced = (blk == 0) | ((blk <= cur) & (blk > cur - 1 - NSA_N_LOCAL))
    future = blk * NSA_SLC_BLOCK > t_row
    score = jnp.where(forced, BIG, jnp.where(future, NEG, imp_t))
    sub = 8
    slabs = [score[r * sub:(r + 1) * sub, :] for r in range(nslc // sub)]
    ranks = [jnp.zeros((sub, tq), F32) for _ in slabs]
    row_in_slab = lax.broadcasted_iota(jnp.int32, (sub, tq), 0)
    for j in range(nslc):
        cj = jnp.broadcast_to(score[j:j + 1, :], (sub, tq))
        for r, s_r in enumerate(slabs):
            if r * sub > j:
                before = cj >= s_r
            elif (r + 1) * sub - 1 < j:
                before = cj > s_r
            else:
                before = (cj > s_r) | ((cj == s_r) & (row_in_slab > j - r * sub))
            ranks[r] = ranks[r] + jnp.where(before, 1.0, 0.0)
    topk = float(min(NSA_TOPK, nslc))
    for r, rank in enumerate(ranks):
        sel_ref[r * sub:(r + 1) * sub, :] = jnp.where(rank < topk, 1.0, 0.0)


def _nsa_select(proj, kvc, bsz, seq_len):
    tq, dh, g = NSA_SEL_TQ, NSA_HEAD_DIM, NSA_KV_GROUPS
    nq = seq_len // tq
    ncmp = (seq_len - NSA_CMP_BLOCK) // NSA_CMP_STRIDE + 1
    nck = seq_len // NSA_CMP_STRIDE
    nslc = seq_len // NSA_SLC_BLOCK
    ci = np.arange(nck)[:, None] * NSA_CMP_STRIDE
    sj = np.arange(nslc)[None, :] * NSA_SLC_BLOCK
    cover = ((ci <= sj + NSA_SLC_BLOCK - 1) & (ci + NSA_CMP_BLOCK - 1 >= sj)).astype(np.float32)
    cover = jnp.asarray(cover.T, dtype=BF16)
    return pl.pallas_call(
        functools.partial(_nsa_select_kernel, ncmp=ncmp, nslc=nslc),
        grid=(bsz, g, nq),
        in_specs=[
            pl.BlockSpec((tq, NSA_HPG * dh), lambda b, gi, qi: (b * nq + qi, gi)),
            pl.BlockSpec((None, None, None, nck, dh), lambda b, gi, qi: (b, 0, gi, 0, 0)),
            pl.BlockSpec((None, None, None, nck, dh), lambda b, gi, qi: (b, 1, gi, 0, 0)),
            pl.BlockSpec((nslc, nck), lambda b, gi, qi: (0, 0)),
        ],
        out_specs=[
            pl.BlockSpec((tq, NSA_HPG * dh), lambda b, gi, qi: (b * nq + qi, gi)),
            pl.BlockSpec((None, None, nslc, tq), lambda b, gi, qi: (b, gi, 0, qi)),
        ],
        out_shape=[
            jax.ShapeDtypeStruct((bsz * seq_len, NSA_N_HEADS * dh), F32),
            jax.ShapeDtypeStruct((bsz, g, nslc, seq_len), F32),
        ],
        compiler_params=_cparams("parallel", "parallel", "parallel"),
        name="nsa_cmp_select",
    )(proj, kvc, kvc, cover)


def _nsa_attn_kernel(q_ref, ks_ref, vs_ref, kw_ref, vw_ref, sel_ref, expand_ref, ocmp_ref, gl_ref,
                     o_ref, ksb_ref, vsx_ref, kwb_ref, vwx_ref, selb_ref, q4_ref, pw_ref, m_ref, acc_ref):
    tq = NSA_TQ
    dh = NSA_HEAD_DIM
    kt_w = NSA_KEY_TILE
    win_w = NSA_WINDOW + tq
    c = (dh ** -0.5) * math.log2(math.e)
    qi = pl.program_id(2)

    @pl.when(qi == 0)
    def _():
        ones = jnp.ones((ks_ref.shape[0], dh), BF16)
        for blk in range(ks_ref.shape[0] // LANE):
            cols = slice(blk * LANE, (blk + 1) * LANE)
            ksb_ref[:, cols] = ks_ref[cols, :].T.astype(BF16)
            kwb_ref[:, cols] = kw_ref[cols, :].T.astype(BF16)
        vsx_ref[:, 0:dh] = vs_ref[...].astype(BF16)
        vsx_ref[:, dh:] = ones
        vwx_ref[:, 0:dh] = vw_ref[...].astype(BF16)
        vwx_ref[:, dh:] = ones

    selb_ref[...] = sel_ref[...].T.astype(BF16)
    for j in range(NSA_HPG):
        q4_ref[j * tq:(j + 1) * tq, :] = (q_ref[:, j * dh:(j + 1) * dh] * c).astype(BF16)
    t1 = qi * tq + lax.broadcasted_iota(jnp.int32, (tq, 1), 0)
    m_ref[...] = jnp.full(m_ref.shape, NEG, F32)
    acc_ref[...] = jnp.zeros(acc_ref.shape, F32)
    half = NSA_HPG // 2 * tq

    def slc_tile(off, width, on_diagonal):
        chosen = _dot(selb_ref[...], expand_ref[:, pl.ds(off, width)])
        bias = (chosen - 1.0) * BIG
        if on_diagonal:
            pos = off + lax.broadcasted_iota(jnp.int32, (1, width), 1)
            bias = jnp.where(pos <= t1, bias, NEG)
        k_t = ksb_ref[:, pl.ds(off, width)]
        v_x = vsx_ref[pl.ds(off, width), :]
        pairs = [slice(hp * half, (hp + 1) * half) for hp in range(2)]
        scores = [_dot(q4_ref[pr], k_t) for pr in pairs]
        for hp, pr in enumerate(pairs):
            m_prev = m_ref[pr]
            ps, m_news = [], []
            for e in range(NSA_HPG // 2):
                sj = scores[hp][e * tq:(e + 1) * tq] + bias
                m_new = jnp.maximum(m_prev[e * tq:(e + 1) * tq], jnp.max(sj, axis=-1, keepdims=True))
                m_wide = jnp.concatenate([m_new] * (width // LANE), axis=1)
                ps.append(jnp.exp2(sj - m_wide).astype(BF16))
                m_news.append(m_new)
            m_new = jnp.concatenate(m_news, axis=0)
            alpha = jnp.exp2(m_prev - m_new)
            m_ref[pr] = m_new
            pv = _dot(jnp.concatenate(ps, axis=0), v_x)
            acc_ref[pr] = jnp.concatenate([alpha, alpha], axis=1) * acc_ref[pr] + pv

    n_wide = (qi * tq) // kt_w

    def wide_step(kt, carry):
        slc_tile(pl.multiple_of(kt * kt_w, kt_w), kt_w, False)
        return carry

    def narrow_step(s, carry):
        slc_tile(pl.multiple_of(n_wide * kt_w + s * tq, tq), tq, True)
        return carry

    lax.fori_loop(0, n_wide, wide_step, 0)
    lax.fori_loop(0, qi + 1 - n_wide * (kt_w // tq), narrow_step, 0)

    w0 = pl.multiple_of(jnp.maximum(qi * tq - NSA_WINDOW, 0), tq)
    sw = _dot(q4_ref[...], kwb_ref[:, pl.ds(w0, win_w)])
    diff = t1 - (w0 + lax.broadcasted_iota(jnp.int32, (1, win_w), 1))
    bias_w = jnp.where((diff >= 0) & (diff < NSA_WINDOW), 0.0, NEG)
    for j in range(NSA_HPG):
        rows = slice(j * tq, (j + 1) * tq)
        sj = sw[rows] + bias_w
        pw_ref[rows] = jnp.exp2(sj - jnp.max(sj, axis=-1, keepdims=True)).astype(BF16)
    ow = _dot(pw_ref[...], vwx_ref[pl.ds(w0, win_w), :])

    gates = _sigmoid(gl_ref[...])
    for j in range(NSA_HPG):
        rows = slice(j * tq, (j + 1) * tq)
        o_slc = acc_ref[rows, 0:dh] / acc_ref[rows, dh:]
        o_win = ow[rows, 0:dh] / ow[rows, dh:]
        o = (gates[:, 3 * j:3 * j + 1] * ocmp_ref[:, j * dh:(j + 1) * dh]
             + gates[:, 3 * j + 1:3 * j + 2] * o_slc
             + gates[:, 3 * j + 2:3 * j + 3] * o_win)
        o_ref[:, j * dh:(j + 1) * dh] = o.astype(o_ref.dtype)


def _nsa_attn(proj, sel, ocmp, gate_logits, bsz, seq_len):
    tq, dh, g = NSA_TQ, NSA_HEAD_DIM, NSA_KV_GROUPS
    assert seq_len % NSA_KEY_TILE == 0 and seq_len >= NSA_WINDOW + tq
    rows = NSA_HPG * tq
    nq = seq_len // tq
    nslc = seq_len // NSA_SLC_BLOCK
    expand = (np.arange(nslc)[:, None] == np.arange(seq_len)[None, :] // NSA_SLC_BLOCK)
    expand = jnp.asarray(expand.astype(np.float32), dtype=BF16)
    kv0 = (NSA_N_HEADS * dh + 2 * NSA_KV_DIM) // dh

    def kv_spec(idx):
        return pl.BlockSpec((seq_len, dh), lambda b, gi, qi: (b, kv0 + idx * g + gi))

    qo_spec = pl.BlockSpec((tq, NSA_HPG * dh), lambda b, gi, qi: (b * nq + qi, gi))
    return pl.pallas_call(
        _nsa_attn_kernel,
        grid=(bsz, g, nq),
        in_specs=[
            qo_spec,
            kv_spec(0), kv_spec(1), kv_spec(2), kv_spec(3),
            pl.BlockSpec((None, None, nslc, tq), lambda b, gi, qi: (b, gi, 0, qi)),
            pl.BlockSpec((nslc, seq_len), lambda b, gi, qi: (0, 0)),
            qo_spec,
            pl.BlockSpec((None, tq, 3 * NSA_HPG), lambda b, gi, qi: (gi, b * nq + qi, 0)),
        ],
        out_specs=qo_spec,
        out_shape=jax.ShapeDtypeStruct((bsz * seq_len, NSA_N_HEADS * dh), BF16),
        scratch_shapes=[
            pltpu.VMEM((dh, seq_len), BF16),
            pltpu.VMEM((seq_len, 2 * dh), BF16),
            pltpu.VMEM((dh, seq_len), BF16),
            pltpu.VMEM((seq_len, 2 * dh), BF16),
            pltpu.VMEM((tq, nslc), BF16),
            pltpu.VMEM((rows, dh), BF16),
            pltpu.VMEM((rows, NSA_WINDOW + tq), BF16),
            pltpu.VMEM((rows, LANE), F32),
            pltpu.VMEM((rows, 2 * dh), F32),
        ],
        compiler_params=_cparams("parallel", "parallel", "arbitrary"),
        name="nsa_slc_win_attention",
    )(proj, proj, proj, proj, proj, sel, expand, ocmp, gate_logits)


def _sgu_kernel(u_ref, v_ref, g_ref, ws_ref, bst_ref, o_ref):
    v = v_ref[...].astype(F32)
    mu = jnp.mean(v, axis=-1, keepdims=True)
    vc = v - mu
    vn = (vc * lax.rsqrt(jnp.mean(vc * vc, axis=-1, keepdims=True) + RMS_EPS) * g_ref[...]).astype(BF16)
    q = SGU_CHUNK
    tri = lax.broadcasted_iota(jnp.int32, (q, q), 0) >= lax.broadcasted_iota(jnp.int32, (q, q), 1)
    for g in range(SGU_GROUPS):
        cols = slice(g * SGU_GROUP_DIM, (g + 1) * SGU_GROUP_DIM)
        w_m = jnp.where(tri, ws_ref[g], 0.0).astype(BF16)
        sv = _dot(w_m, vn[:, cols]) + bst_ref[:, g:g + 1]
        o_ref[:, cols] = (u_ref[:, cols].astype(F32) * sv).astype(o_ref.dtype)


def _sgu_core(proj, ln_g, w_spatial, b_spatial):
    m = proj.shape[0]
    q = SGU_CHUNK
    return pl.pallas_call(
        _sgu_kernel,
        grid=(m // q,),
        in_specs=[
            pl.BlockSpec((q, SGU_WIDTH), lambda i: (i, 0)),
            pl.BlockSpec((q, SGU_WIDTH), lambda i: (i, 1)),
            pl.BlockSpec((1, SGU_WIDTH), lambda i: (0, 0)),
            pl.BlockSpec((SGU_GROUPS, q, q), lambda i: (0, 0, 0)),
            pl.BlockSpec((q, SGU_GROUPS), lambda i: (0, 0)),
        ],
        out_specs=pl.BlockSpec((q, SGU_WIDTH), lambda i: (i, 0)),
        out_shape=jax.ShapeDtypeStruct((m, SGU_WIDTH), BF16),
        compiler_params=_cparams("parallel"),
        name="sgu_core",
    )(proj, proj, ln_g.reshape(1, SGU_WIDTH), w_spatial, b_spatial.T)


def _pool_kernel(z_ref, zh_ref, wg_ref, sc_ref, o_ref, zz_ref, *, tm, blocks_per_seq):
    i = pl.program_id(0)
    keep = (i % blocks_per_seq != 0).astype(F32)
    zz_ref[0:POOL_HALO, :] = zh_ref[...] * keep
    zz_ref[POOL_HALO:, :] = z_ref[...]
    t = (i % blocks_per_seq) * tm + lax.broadcasted_iota(jnp.int32, (tm, 1), 0)
    for gi, win in enumerate(POOL_WINDOWS):
        cols = slice(gi * POOL_GROUP_DIM, (gi + 1) * POOL_GROUP_DIM)
        s = zz_ref[POOL_HALO:POOL_HALO + tm, cols]
        for k in range(1, win):
            s = s + zz_ref[POOL_HALO - k:POOL_HALO - k + tm, cols]
        count = jnp.minimum(t + 1, win).astype(F32)
        pooled = s / count - z_ref[:, cols]
        y = _dot(pooled.astype(BF16), wg_ref[gi]) * sc_ref[:, cols]
        o_ref[:, cols] = y.astype(o_ref.dtype)


def _pool_core(z, w_group, scale, seq_len, *, tm=512):
    m, n = z.shape
    tm = min(tm, seq_len)
    halo_blocks = tm // POOL_HALO
    return pl.pallas_call(
        functools.partial(_pool_kernel, tm=tm, blocks_per_seq=seq_len // tm),
        grid=(m // tm,),
        in_specs=[
            pl.BlockSpec((tm, n), lambda i: (i, 0)),
            pl.BlockSpec((POOL_HALO, n), lambda i: (jnp.maximum(i * halo_blocks - 1, 0), 0)),
            pl.BlockSpec(w_group.shape, lambda i: (0, 0, 0)),
            pl.BlockSpec((1, n), lambda i: (0, 0)),
        ],
        out_specs=pl.BlockSpec((tm, n), lambda i: (i, 0)),
        out_shape=jax.ShapeDtypeStruct((m, n), BF16),
        scratch_shapes=[pltpu.VMEM((tm + POOL_HALO, n), F32)],
        compiler_params=_cparams("parallel"),
        name="pool_core",
    )(z, z, w_group, scale.reshape(1, n))


def _pad_cols(w, n):
    return jnp.pad(w, ((0, 0), (0, 0), (0, n - w.shape[2])))


def _ssd_layer(x, g_pre, g_post, w_in, j, conv_w, conv_b, dt_bias, a_log, d_skip, norm_g, w_out, bsz, seq_len):
    proj, _ = _ssd_in_proj(x, g_pre, w_in, j, conv_w, conv_b, seq_len)
    y = _ssd_core(proj, dt_bias, a_log, d_skip, norm_g, bsz, seq_len)
    return _matmul_post(y, w_out, j, g_post, x)


def _first_ssd_layer(x, g_pre, g_post, w_in, conv_w, conv_b, dt_bias, a_log, d_skip, norm_g, f32_weights,
                     bsz, seq_len):
    proj, cast = _ssd_in_proj(x, g_pre, w_in, 0, conv_w, conv_b, seq_len, tuple(f32_weights.values()))
    wb = dict(zip(f32_weights, cast))
    y = _ssd_core(proj, dt_bias, a_log, d_skip, norm_g, bsz, seq_len)
    return _matmul_post(y, wb["ssd_out"], 0, g_post, x), wb


def _nsa_layer(x, g_pre, g_post, w_in, j, cmp_pos, cmp_w1, cmp_w2, w_out, bsz, seq_len):
    proj = _norm_matmul(x, g_pre, w_in, j, jnp.zeros((NSA_IN_PAD,), F32))
    kvc = _nsa_compress(proj, cmp_pos, cmp_w1, cmp_w2, bsz, seq_len)
    ocmp, sel = _nsa_select(proj, kvc, bsz, seq_len)
    ngate = 3 * NSA_N_HEADS
    gl = proj[:, NSA_GATE_COL:NSA_GATE_COL + ngate].reshape(-1, NSA_KV_GROUPS, 3 * NSA_HPG).transpose(1, 0, 2)
    o = _nsa_attn(proj, sel, ocmp, gl, bsz, seq_len)
    return _matmul_post(o, w_out, j, g_post, x)


def _sgu_layer(x, g_pre, g_post, w_in, j, b_in, ln_g, w_spatial, b_spatial, w_out):
    proj = _norm_matmul(x, g_pre, w_in, j, b_in, act="gelu", out_dtype=BF16)
    y = _sgu_core(proj, ln_g, w_spatial, b_spatial)
    return _matmul_post(y, w_out, j, g_post, x)


def _pool_layer(x, g_pre, g_post, w_in, j, w_group, scale, w_out, seq_len):
    z = _norm_matmul(x, g_pre, w_in, j, jnp.zeros((D_MODEL,), F32))
    y = _pool_core(z, w_group.astype(BF16), scale, seq_len)
    return _matmul_post(y, w_out, j, g_post, x)


def kernel(x, mem, norm_pre, norm_post, norm_mem, ssd_w_in, ssd_conv_w, ssd_conv_b, ssd_dt_bias, ssd_a_log, ssd_d, ssd_norm_g, ssd_w_out, nsa_w_in, nsa_cmp_pos, nsa_cmp_w1, nsa_cmp_w2, nsa_w_out, sgu_w_in, sgu_b_in, sgu_ln_g, sgu_w_spatial, sgu_b_spatial, sgu_w_out, pool_w_in, pool_w_group, pool_scale, pool_w_out, xa_w_q, xa_w_kv, xa_w_o, ffn_w_up, ffn_conv_w, ffn_conv_b, ffn_w_down):
    bsz, seq_len, d = x.shape
    depth = norm_pre.shape[0]
    xf = x.reshape(bsz * seq_len, d)
    memf = mem.reshape(bsz * mem.shape[1], d)
    ssd_in = _pad_cols(ssd_w_in, SSD_IN_PAD).astype(BF16)
    nsa_in = _pad_cols(nsa_w_in, NSA_IN_PAD).astype(BF16)
    f32_weights = dict(ssd_out=ssd_w_out, nsa_out=nsa_w_out, sgu_in=sgu_w_in, sgu_out=sgu_w_out,
                       pool_in=pool_w_in, pool_out=pool_w_out, xa_q=xa_w_q, xa_kv=xa_w_kv, xa_o=xa_w_o,
                       ffn_up=ffn_w_up, ffn_down=ffn_w_down)
    wb = None
    for i in range(depth):
        kind, j = i % 4, i // 4
        if i == 0:
            xf, wb = _first_ssd_layer(xf, norm_pre[i, 0], norm_post[i, 0], ssd_in, ssd_conv_w[j], ssd_conv_b[j],
                                      ssd_dt_bias[j], ssd_a_log[j], ssd_d[j], ssd_norm_g[j], f32_weights,
                                      bsz, seq_len)
        elif kind == 0:
            xf = _ssd_layer(xf, norm_pre[i, 0], norm_post[i, 0], ssd_in, j, ssd_conv_w[j], ssd_conv_b[j],
                            ssd_dt_bias[j], ssd_a_log[j], ssd_d[j], ssd_norm_g[j], wb["ssd_out"], bsz, seq_len)
        elif kind == 1:
            xf = _nsa_layer(xf, norm_pre[i, 0], norm_post[i, 0], nsa_in, j, nsa_cmp_pos[j], nsa_cmp_w1[j],
                            nsa_cmp_w2[j], wb["nsa_out"], bsz, seq_len)
        elif kind == 2:
            xf = _sgu_layer(xf, norm_pre[i, 0], norm_post[i, 0], wb["sgu_in"], j, sgu_b_in[j], sgu_ln_g[j],
                            sgu_w_spatial[j], sgu_b_spatial[j], wb["sgu_out"])
        else:
            xf = _pool_layer(xf, norm_pre[i, 0], norm_post[i, 0], wb["pool_in"], j, pool_w_group[j],
                             pool_scale[j], wb["pool_out"], seq_len)
        kv = _norm_matmul(memf, norm_mem[i], wb["xa_kv"], i, jnp.zeros((2 * XA_DIM,), F32),
                          out_dtype=BF16, tm=MEM_LEN)
        xf = _xattn(xf, kv, norm_pre[i, 1], norm_post[i, 1], wb["xa_q"], wb["xa_o"], i, seq_len)
        act = _ffn_up(xf, norm_pre[i, 2], wb["ffn_up"], i, ffn_conv_w[i], ffn_conv_b[i], seq_len)
        xf = _matmul_post(act, wb["ffn_down"], i, norm_post[i, 2], xf)
    return xf.reshape(bsz, seq_len, d)
```

```python
import functools
import math

import jax
import jax.numpy as jnp
import numpy as np
from jax import lax
from jax.experimental import pallas as pl
from jax.experimental.pallas import tpu as pltpu

F32 = jnp.float32
BF16 = jnp.bfloat16

D_MODEL = 2048
RMS_EPS = 1e-6
NEG = -1e30
BIG = 1e30
MEM_LEN = 256

SSD_D_INNER = 4096
SSD_HEAD_DIM = 64
SSD_N_HEADS = 64
SSD_N_GROUPS = 8
SSD_HPG = 8
SSD_D_STATE = 128
SSD_CONV = 4
SSD_CHUNK = 128
SSD_GROUP_W = SSD_HPG * SSD_HEAD_DIM
SSD_BC_DIM = SSD_N_GROUPS * SSD_D_STATE
SSD_IN_PAD = 10752

NSA_HEAD_DIM = 128
NSA_N_HEADS = 16
NSA_KV_GROUPS = 4
NSA_HPG = 4
NSA_KV_DIM = 512
NSA_CMP_BLOCK = 32
NSA_CMP_STRIDE = 16
NSA_CMP_HIDDEN = 256
NSA_SLC_BLOCK = 64
NSA_TOPK = 16
NSA_N_LOCAL = 2
NSA_WINDOW = 512
NSA_TQ = 256
NSA_KEY_TILE = 1024
NSA_SEL_TQ = 512
NSA_IN_PAD = 5632
NSA_GATE_COL = 5120

SGU_CHUNK = 128
SGU_WIDTH = 4096
SGU_GROUPS = 8
SGU_GROUP_DIM = 512

POOL_WINDOWS = (2, 4, 8, 16)
POOL_GROUP_DIM = 512
POOL_HALO = max(POOL_WINDOWS)

XA_HEADS = 4
XA_HEAD_DIM = 128
XA_DIM = 512

FFN_HIDDEN = 5632
FFN_CONV = 3

LANE = 128
F32_SUBLANES = 8
BF16_SUBLANES = 16
SSD_HALO = BF16_SUBLANES
FFN_HALO = BF16_SUBLANES
VMEM_LIMIT = 56 * 1024 * 1024


def _cparams(*sem):
    return pltpu.CompilerParams(dimension_semantics=sem, vmem_limit_bytes=VMEM_LIMIT)


def _dot(a, b):
    return jnp.dot(a, b, preferred_element_type=F32)


def _dot_nt(a, b):
    return lax.dot_general(a, b, (((1,), (1,)), ((), ())), preferred_element_type=F32)


def _dot_tn(a, b):
    return lax.dot_general(a, b, (((0,), (0,)), ((), ())), preferred_element_type=F32)


def _split3(v):
    hi = v.astype(BF16)
    r1 = v - hi.astype(F32)
    mid = r1.astype(BF16)
    lo = (r1 - mid.astype(F32)).astype(BF16)
    return hi, mid, lo


def _dot_2piece_lhs(v, e2):
    hi = v.astype(BF16)
    mid = (v - hi.astype(F32)).astype(BF16)
    return _dot(jnp.concatenate([hi, mid], axis=1), e2)


def _dot_f32_rhs(e, v):
    hi, mid, lo = _split3(v)
    return _dot(e, hi) + _dot(e, mid) + _dot(e, lo)


def _rms(x, g):
    ms = jnp.mean(x * x, axis=-1, keepdims=True)
    return x * lax.rsqrt(ms + RMS_EPS) * g


def _sigmoid(x):
    return 1.0 / (1.0 + jnp.exp(-x))


def _silu(x):
    h = 0.5 * x
    return h + h * jnp.tanh(h)


def _gelu(x):
    c = math.sqrt(2.0 / math.pi)
    return x * (0.5 * (1.0 + jnp.tanh(c * (x + 0.044715 * (x * x * x)))))


def _softplus(x):
    return jnp.maximum(x, 0.0) + jnp.log1p(jnp.exp(-jnp.abs(x)))


def _norm_mm_kernel(x_ref, g_ref, w_ref, b_ref, o_ref, h_ref, *, act):
    @pl.when(pl.program_id(1) == 0)
    def _():
        h_ref[...] = _rms(x_ref[...], g_ref[...]).astype(BF16)

    y = _dot(h_ref[...], w_ref[...]) + b_ref[...]
    if act == "gelu":
        y = _gelu(y)
    o_ref[...] = y.astype(o_ref.dtype)


def _norm_matmul(x, g, w, li, b, *, act=None, out_dtype=F32, tm=1024, tn=512):
    m, k = x.shape
    n = w.shape[2]
    tm = min(tm, m)
    assert m % tm == 0 and n % tn == 0, (m, n, tm, tn)
    return pl.pallas_call(
        functools.partial(_norm_mm_kernel, act=act),
        grid=(m // tm, n // tn),
        in_specs=[
            pl.BlockSpec((tm, k), lambda i, j: (i, 0)),
            pl.BlockSpec((1, k), lambda i, j: (0, 0)),
            pl.BlockSpec((None, k, tn), lambda i, j: (li, 0, j)),
            pl.BlockSpec((1, tn), lambda i, j: (0, j)),
        ],
        out_specs=pl.BlockSpec((tm, tn), lambda i, j: (i, j)),
        out_shape=jax.ShapeDtypeStruct((m, n), out_dtype),
        scratch_shapes=[pltpu.VMEM((tm, k), BF16)],
        compiler_params=_cparams("parallel", "arbitrary"),
        name="norm_matmul",
    )(x, g.reshape(1, k), w, b.reshape(1, n))


def _mm_post_kernel(a_ref, w_ref, g_ref, r_ref, o_ref, *, nj, tn):
    j = pl.program_id(1)
    o_ref[:, pl.ds(pl.multiple_of(j * tn, tn), tn)] = _dot(a_ref[...], w_ref[...])

    @pl.when(j == nj - 1)
    def _():
        o_ref[...] = r_ref[...] + _rms(o_ref[...], g_ref[...])


def _matmul_post(a, w, li, g, res, *, tm=512, tn=512):
    m, kdim = a.shape
    n = w.shape[2]
    tm = min(tm, m)
    assert m % tm == 0 and n % tn == 0
    nj = n // tn
    return pl.pallas_call(
        functools.partial(_mm_post_kernel, nj=nj, tn=tn),
        grid=(m // tm, nj),
        in_specs=[
            pl.BlockSpec((tm, kdim), lambda i, j: (i, 0)),
            pl.BlockSpec((None, kdim, tn), lambda i, j: (li, 0, j)),
            pl.BlockSpec((1, n), lambda i, j: (0, 0)),
            pl.BlockSpec((tm, n), lambda i, j: (i, 0)),
        ],
        out_specs=pl.BlockSpec((tm, n), lambda i, j: (i, 0)),
        out_shape=jax.ShapeDtypeStruct((m, n), F32),
        compiler_params=_cparams("parallel", "arbitrary"),
        name="matmul_postnorm_residual",
    )(a, w, g.reshape(1, n), res)


def _ffn_up_kernel(x_ref, xh_ref, g_ref, wg_ref, wv_ref, cwg_ref, cwv_ref, cbg_ref, cbv_ref,
                   o_ref, h_ref, *, tm, blocks_per_seq):
    i = pl.program_id(0)

    @pl.when(pl.program_id(1) == 0)
    def _():
        keep = (i % blocks_per_seq != 0).astype(F32)
        h_ref[0:FFN_HALO, :] = (_rms(xh_ref[...], g_ref[...]) * keep).astype(BF16)
        h_ref[FFN_HALO:, :] = _rms(x_ref[...], g_ref[...]).astype(BF16)

    h = h_ref[...]

    def conv(u, cw_ref, cb_ref):
        y = cb_ref[...] + u[FFN_HALO:] * cw_ref[FFN_CONV - 1:FFN_CONV, :]
        for tap in range(FFN_CONV - 1):
            back = FFN_CONV - 1 - tap
            y = y + pltpu.roll(u, back, axis=0)[FFN_HALO:] * cw_ref[tap:tap + 1, :]
        return y

    gate = conv(_dot(h, wg_ref[...]), cwg_ref, cbg_ref)
    val = conv(_dot(h, wv_ref[...]), cwv_ref, cbv_ref)
    o_ref[...] = (_silu(gate) * val).astype(o_ref.dtype)


def _ffn_up(x, g, w_up, li, conv_w, conv_b, seq_len, *, tm=1024, tn=512):
    m, k = x.shape
    tm = min(tm, seq_len)
    assert m % tm == 0 and seq_len % tm == 0 and FFN_HIDDEN % tn == 0
    nj = FFN_HIDDEN // tn
    halo_blocks = tm // FFN_HALO
    return pl.pallas_call(
        functools.partial(_ffn_up_kernel, tm=tm, blocks_per_seq=seq_len // tm),
        grid=(m // tm, nj),
        in_specs=[
            pl.BlockSpec((tm, k), lambda i, j: (i, 0)),
            pl.BlockSpec((FFN_HALO, k), lambda i, j: (jnp.maximum(i * halo_blocks - 1, 0), 0)),
            pl.BlockSpec((1, k), lambda i, j: (0, 0)),
            pl.BlockSpec((None, k, tn), lambda i, j: (li, 0, j)),
            pl.BlockSpec((None, k, tn), lambda i, j: (li, 0, j + nj)),
            pl.BlockSpec((FFN_CONV, tn), lambda i, j: (0, j)),
            pl.BlockSpec((FFN_CONV, tn), lambda i, j: (0, j + nj)),
            pl.BlockSpec((1, tn), lambda i, j: (0, j)),
            pl.BlockSpec((1, tn), lambda i, j: (0, j + nj)),
        ],
        out_specs=pl.BlockSpec((tm, tn), lambda i, j: (i, j)),
        out_shape=jax.ShapeDtypeStruct((m, FFN_HIDDEN), BF16),
        scratch_shapes=[pltpu.VMEM((tm + FFN_HALO, k), BF16)],
        compiler_params=_cparams("parallel", "arbitrary"),
        name="ffn_up_conv_gate",
    )(x, x, g.reshape(1, k), w_up, w_up, conv_w, conv_w,
      conv_b.reshape(1, -1), conv_b.reshape(1, -1))


def _xattn_kernel(x_ref, kv_ref, gpre_ref, gpost_ref, wq_ref, wo_ref, o_ref, att_ref):
    x = x_ref[...]
    h = _rms(x, gpre_ref[...]).astype(BF16)
    c = (XA_HEAD_DIM ** -0.5) * math.log2(math.e)
    q = (_dot(h, wq_ref[...]) * c).astype(BF16)
    for hd in range(XA_HEADS):
        lo = hd * XA_HEAD_DIM
        s = _dot_nt(q[:, lo:lo + XA_HEAD_DIM], kv_ref[:, lo:lo + XA_HEAD_DIM])
        e = jnp.exp2(s - jnp.max(s, axis=-1, keepdims=True))
        pv = _dot(e.astype(BF16), kv_ref[:, XA_DIM + lo:XA_DIM + lo + XA_HEAD_DIM])
        att_ref[:, lo:lo + XA_HEAD_DIM] = (pv / jnp.sum(e, axis=-1, keepdims=True)).astype(BF16)
    a = _dot(att_ref[...], wo_ref[...])
    o_ref[...] = x + _rms(a, gpost_ref[...])


def _xattn(x, kv, g_pre, g_post, w_q, w_o, li, seq_len, *, tm=512):
    m, k = x.shape
    tm = min(tm, seq_len)
    bps = seq_len // tm
    return pl.pallas_call(
        _xattn_kernel,
        grid=(m // tm,),
        in_specs=[
            pl.BlockSpec((tm, k), lambda i: (i, 0)),
            pl.BlockSpec((MEM_LEN, 2 * XA_DIM), lambda i: (i // bps, 0)),
            pl.BlockSpec((1, k), lambda i: (0, 0)),
            pl.BlockSpec((1, k), lambda i: (0, 0)),
            pl.BlockSpec((None, k, XA_DIM), lambda i: (li, 0, 0)),
            pl.BlockSpec((None, XA_DIM, k), lambda i: (li, 0, 0)),
        ],
        out_specs=pl.BlockSpec((tm, k), lambda i: (i, 0)),
        out_shape=jax.ShapeDtypeStruct((m, k), F32),
        scratch_shapes=[pltpu.VMEM((tm, XA_DIM), BF16)],
        compiler_params=_cparams("parallel"),
        name="memory_cross_attention",
    )(x, kv, g_pre.reshape(1, k), g_post.reshape(1, k), w_q, w_o)


def _ssd_in_kernel(x_ref, xh_ref, g_ref, w_ref, cw_ref, cb_ref, *rest, blocks_per_seq, tn, n_riders):
    rider_in = rest[:n_riders]
    o_ref = rest[n_riders]
    rider_out = rest[n_riders + 1:2 * n_riders + 1]
    h_ref = rest[2 * n_riders + 1]
    i = pl.program_id(0)
    j = pl.program_id(1)
    z_tiles = SSD_D_INNER // tn
    conv_tiles = (SSD_D_INNER + 2 * SSD_BC_DIM) // tn

    def cast_riders():
        for src, dst in zip(rider_in, rider_out):
            dst[...] = src[...].astype(dst.dtype)

    @pl.when(j == 0)
    def _():
        keep = (i % blocks_per_seq != 0).astype(F32)
        h_ref[0:SSD_HALO, :] = (_rms(xh_ref[...], g_ref[...]) * keep).astype(BF16)
        h_ref[SSD_HALO:, :] = _rms(x_ref[...], g_ref[...]).astype(BF16)

    @pl.when(j < z_tiles)
    def _():
        cast_riders()
        o_ref[...] = _silu(_dot(h_ref[SSD_HALO:, :], w_ref[...]))

    @pl.when((j >= z_tiles) & (j < z_tiles + conv_tiles))
    def _():
        cast_riders()
        u = _dot(h_ref[...], w_ref[...])
        y = cb_ref[...] + u[SSD_HALO:] * cw_ref[SSD_CONV - 1:SSD_CONV, :]
        for tap in range(SSD_CONV - 1):
            back = SSD_CONV - 1 - tap
            y = y + pltpu.roll(u, back, axis=0)[SSD_HALO:] * cw_ref[tap:tap + 1, :]
        o_ref[...] = _silu(y)

    @pl.when(j >= z_tiles + conv_tiles)
    def _():
        cast_riders()
        o_ref[...] = _dot(h_ref[SSD_HALO:, :], w_ref[...])


def _rider_rows(rows, steps):
    r = BF16_SUBLANES
    while rows % r or rows // r > steps:
        r += BF16_SUBLANES
        assert r <= rows, (rows, steps)
    return r


def _ssd_in_proj(x, g, w_in, li, conv_w, conv_b, seq_len, riders=(), *, tm=1024, tn=512):
    m, k = x.shape
    n = w_in.shape[2]
    tm = min(tm, seq_len)
    assert m % tm == 0 and seq_len % tm == 0 and n % tn == 0
    z_tiles = SSD_D_INNER // tn
    conv_tiles = (SSD_D_INNER + 2 * SSD_BC_DIM) // tn
    halo_blocks = tm // SSD_HALO
    conv_tile = lambda j: jnp.clip(j - z_tiles, 0, conv_tiles - 1)
    nj = n // tn
    steps = (m // tm) * nj
    flat = [r.reshape(-1, r.shape[-1]) for r in riders]
    rider_specs = []
    for r in flat:
        rows = _rider_rows(r.shape[0], steps)
        last = r.shape[0] // rows - 1
        rider_specs.append(pl.BlockSpec(
            (rows, r.shape[1]), lambda i, j, last=last: (jnp.minimum(i * nj + j, last), 0)))
    outs = pl.pallas_call(
        functools.partial(_ssd_in_kernel, blocks_per_seq=seq_len // tm, tn=tn, n_riders=len(flat)),
        grid=(m // tm, nj),
        in_specs=[
            pl.BlockSpec((tm, k), lambda i, j: (i, 0)),
            pl.BlockSpec((SSD_HALO, k), lambda i, j: (jnp.maximum(i * halo_blocks - 1, 0), 0)),
            pl.BlockSpec((1, k), lambda i, j: (0, 0)),
            pl.BlockSpec((None, k, tn), lambda i, j: (li, 0, j)),
            pl.BlockSpec((SSD_CONV, tn), lambda i, j: (0, conv_tile(j))),
            pl.BlockSpec((1, tn), lambda i, j: (0, conv_tile(j))),
        ] + rider_specs,
        out_specs=[pl.BlockSpec((tm, tn), lambda i, j: (i, j))] + rider_specs,
        out_shape=[jax.ShapeDtypeStruct((m, n), F32)]
        + [jax.ShapeDtypeStruct(r.shape, BF16) for r in flat],
        scratch_shapes=[pltpu.VMEM((tm + SSD_HALO, k), BF16)],
        compiler_params=_cparams("arbitrary", "arbitrary"),
        name="ssd_in_proj_conv",
    )(x, x, g.reshape(1, k), w_in, conv_w, conv_b.reshape(1, -1), *flat)
    return outs[0], [o.reshape(r.shape) for o, r in zip(outs[1:], riders)]


def _ssd_kernel(zs_ref, xs_ref, bm_ref, cm_ref, dtr_ref, dtb_ref, alog_ref,
                dskip_ref, ng_ref, expand_ref, tril_ref, o_ref,
                wx_ref, expax_ref, cdx_ref, acsg_ref, acst_ref, state_ref, y_ref):
    q = SSD_CHUNK
    c = pl.program_id(1)

    @pl.when(c == 0)
    def _():
        state_ref[...] = jnp.zeros(state_ref.shape, F32)

    dt = _softplus(dtr_ref[...] + dtb_ref[...])
    a = dt * (-jnp.exp(alog_ref[...]))
    a_cs = _dot_f32_rhs(tril_ref[...], a)
    a_last = a_cs[q - 1:q, :]
    expand = expand_ref[...]
    wx_ref[...] = _dot_2piece_lhs(dt * jnp.exp(a_last - a_cs), expand)
    expax_ref[...] = _dot_2piece_lhs(jnp.exp(a_cs), expand)
    cdx_ref[...] = _dot_2piece_lhs(jnp.broadcast_to(jnp.exp(a_last), (8, LANE)), expand)
    acst_ref[...] = (a_cs - jnp.log(dt)).T.reshape(LANE // SSD_HPG, SSD_HPG, q)
    for g in range(SSD_N_GROUPS):
        acsg_ref[g] = a_cs if g == 0 else pltpu.roll(a_cs, LANE - g * SSD_HPG, axis=1)

    row = lax.broadcasted_iota(jnp.int32, (q, q), 0)
    col = lax.broadcasted_iota(jnp.int32, (q, q), 1)
    causal = row >= col
    left_half = col < SSD_HEAD_DIM

    def group_body(g, carry):
        off = pl.multiple_of(g * SSD_GROUP_W, SSD_GROUP_W)
        offn = pl.multiple_of(g * SSD_D_STATE, SSD_D_STATE)
        cm = cm_ref[:, pl.ds(offn, SSD_D_STATE)]
        bm = bm_ref[:, pl.ds(offn, SSD_D_STATE)]
        cmb = cm.astype(BF16)
        cb = _dot_nt(cmb, bm.astype(BF16))
        xs = xs_ref[:, pl.ds(off, SSD_GROUP_W)]
        st = state_ref[g]
        y_off = _dot(cmb, st.astype(BF16)) * expax_ref[:, pl.ds(off, SSD_GROUP_W)]
        s_new = _dot_tn(bm.astype(BF16), (xs * wx_ref[:, pl.ds(off, SSD_GROUP_W)]).astype(BF16))
        state_ref[g] = st * cdx_ref[0:1, pl.ds(off, SSD_GROUP_W)] + s_new
        acs_g = acsg_ref[g]
        acst_g = acst_ref[g]
        skip = xs * dskip_ref[:, pl.ds(off, SSD_GROUP_W)]
        for pr in range(SSD_HPG // 2):
            scs = []
            for e in range(2):
                j = 2 * pr + e
                diff = acs_g[:, j:j + 1] - acst_g[j:j + 1, :]
                dec = jnp.where(causal, jnp.exp(diff), 0.0)
                scs.append((cb * dec).astype(BF16))
            xp = xs[:, pr * LANE:(pr + 1) * LANE]
            x0 = jnp.where(left_half, xp, 0.0).astype(BF16)
            x1 = jnp.where(left_half, 0.0, xp).astype(BF16)
            y_diag = _dot(jnp.concatenate(scs, axis=1), jnp.concatenate([x0, x1], axis=0))
            y = y_diag + y_off[:, pr * LANE:(pr + 1) * LANE] + skip[:, pr * LANE:(pr + 1) * LANE]
            y_ref[:, pl.ds(pl.multiple_of(off + pr * LANE, LANE), LANE)] = y
        return carry

    lax.fori_loop(0, SSD_N_GROUPS, group_body, 0, unroll=8)

    gated = y_ref[...] * zs_ref[...]
    o_ref[...] = _rms(gated, ng_ref[...]).astype(o_ref.dtype)


def _ssd_core(proj, dt_bias, a_log, d_skip, norm_g, bsz, seq_len):
    q = SSD_CHUNK
    nc = seq_len // q
    m = bsz * seq_len
    pad = LANE - SSD_N_HEADS
    dtb = jnp.pad(dt_bias, (0, pad)).reshape(1, LANE)
    alog = jnp.pad(a_log, (0, pad)).reshape(1, LANE)
    dskip = jnp.repeat(d_skip, SSD_HEAD_DIM).reshape(1, SSD_D_INNER)
    heads = np.arange(2 * LANE)[:, None] % LANE
    chans = np.arange(SSD_D_INNER)[None, :] // SSD_HEAD_DIM
    expand = jnp.asarray((heads == chans).astype(np.float32), dtype=BF16)
    tril = jnp.asarray(np.tril(np.ones((q, q), np.float32)), dtype=BF16)
    row = lambda b, c: b * nc + c
    b_blk = 2 * SSD_D_INNER // SSD_BC_DIM
    dt_blk = (2 * SSD_D_INNER + 2 * SSD_BC_DIM) // LANE
    const = lambda b, c: (0, 0)
    return pl.pallas_call(
        _ssd_kernel,
        grid=(bsz, nc),
        in_specs=[
            pl.BlockSpec((q, SSD_D_INNER), lambda b, c: (row(b, c), 0)),
            pl.BlockSpec((q, SSD_D_INNER), lambda b, c: (row(b, c), 1)),
            pl.BlockSpec((q, SSD_BC_DIM), lambda b, c: (row(b, c), b_blk)),
            pl.BlockSpec((q, SSD_BC_DIM), lambda b, c: (row(b, c), b_blk + 1)),
            pl.BlockSpec((q, LANE), lambda b, c: (row(b, c), dt_blk)),
            pl.BlockSpec((1, LANE), const),
            pl.BlockSpec((1, LANE), const),
            pl.BlockSpec((1, SSD_D_INNER), const),
            pl.BlockSpec((1, SSD_D_INNER), const),
            pl.BlockSpec((2 * LANE, SSD_D_INNER), const),
            pl.BlockSpec((q, q), const),
        ],
        out_specs=pl.BlockSpec((q, SSD_D_INNER), lambda b, c: (row(b, c), 0)),
        out_shape=jax.ShapeDtypeStruct((m, SSD_D_INNER), BF16),
        scratch_shapes=[
            pltpu.VMEM((q, SSD_D_INNER), F32),
            pltpu.VMEM((q, SSD_D_INNER), F32),
            pltpu.VMEM((8, SSD_D_INNER), F32),
            pltpu.VMEM((SSD_N_GROUPS, q, LANE), F32),
            pltpu.VMEM((LANE // SSD_HPG, SSD_HPG, q), F32),
            pltpu.VMEM((SSD_N_GROUPS, SSD_D_STATE, SSD_GROUP_W), F32),
            pltpu.VMEM((q, SSD_D_INNER), F32),
        ],
        compiler_params=_cparams("parallel", "arbitrary"),
        name="ssd_core",
    )(proj, proj, proj, proj, proj, dtb, alog, dskip, norm_g.reshape(1, SSD_D_INNER), expand, tril)


def _nsa_compress_kernel(t_ref, pos_ref, w1_ref, w2_ref, o_ref, *, nchunk):
    half = NSA_CMP_STRIDE
    dh = NSA_HEAD_DIM
    p_lo = jnp.zeros((nchunk, NSA_CMP_HIDDEN), F32)
    p_hi = jnp.zeros((nchunk, NSA_CMP_HIDDEN), F32)
    for l in range(half):
        rows = t_ref[pl.ds(l, nchunk, stride=half), :]
        a_lo = (rows + pos_ref[l:l + 1, :]).astype(BF16)
        a_hi = (rows + pos_ref[half + l:half + l + 1, :]).astype(BF16)
        p_lo = p_lo + _dot(a_lo, w1_ref[l * dh:(l + 1) * dh, :].astype(BF16))
        p_hi = p_hi + _dot(a_hi, w1_ref[(half + l) * dh:(half + l + 1) * dh, :].astype(BF16))
    pre = p_lo + pltpu.roll(p_hi, nchunk - 1, axis=0)
    o_ref[...] = _dot(_gelu(pre).astype(BF16), w2_ref[...].astype(BF16))


def _nsa_compress(proj, cmp_pos, cmp_w1, cmp_w2, bsz, seq_len):
    nchunk = seq_len // NSA_CMP_STRIDE
    g, dh = NSA_KV_GROUPS, NSA_HEAD_DIM
    col0 = NSA_N_HEADS * dh // LANE
    return pl.pallas_call(
        functools.partial(_nsa_compress_kernel, nchunk=nchunk),
        grid=(bsz, 2, g),
        in_specs=[
            pl.BlockSpec((seq_len, dh), lambda b, s, gi: (b, col0 + s * g + gi)),
            pl.BlockSpec((None, NSA_CMP_BLOCK, dh), lambda b, s, gi: (s, 0, 0)),
            pl.BlockSpec((None, NSA_CMP_BLOCK * dh, NSA_CMP_HIDDEN), lambda b, s, gi: (s, 0, 0)),
            pl.BlockSpec((None, NSA_CMP_HIDDEN, dh), lambda b, s, gi: (s, 0, 0)),
        ],
        out_specs=pl.BlockSpec((None, None, None, nchunk, dh), lambda b, s, gi: (b, s, gi, 0, 0)),
        out_shape=jax.ShapeDtypeStruct((bsz, 2, g, nchunk, dh), F32),
        compiler_params=_cparams("parallel", "parallel", "parallel"),
        name="nsa_compress",
    )(proj, cmp_pos, cmp_w1, cmp_w2)


def _nsa_select_kernel(q_ref, kc_ref, vc_ref, cover_ref, ocmp_ref, sel_ref, *, ncmp, nslc):
    tq = NSA_SEL_TQ
    dh = NSA_HEAD_DIM
    scale = dh ** -0.5
    nck = kc_ref.shape[0]
    t0 = pl.program_id(2) * tq
    t_row = t0 + lax.broadcasted_iota(jnp.int32, (1, tq), 1)
    ci_col = lax.broadcasted_iota(jnp.int32, (nck, 1), 0)
    valid_t = (ci_col * NSA_CMP_STRIDE + (NSA_CMP_BLOCK - 1) <= t_row) & (ci_col < ncmp)
    c = scale * math.log2(math.e)
    bias_t = jnp.where(valid_t, 0.0, NEG)
    some_t = jnp.where(t_row >= NSA_CMP_BLOCK - 1, 1.0, 0.0)
    kcb = kc_ref[...].astype(BF16)
    vcb = vc_ref[...].astype(BF16)
    psum_t = jnp.zeros((nck, tq), F32)
    for j in range(NSA_HPG):
        qj = q_ref[:, j * dh:(j + 1) * dh].astype(BF16)
        st = _dot_nt(kcb, qj) * c + bias_t
        et = jnp.exp2(st - jnp.max(st, axis=0, keepdims=True))
        pt = et * (some_t / jnp.sum(et, axis=0, keepdims=True))
        psum_t = psum_t + pt
        ocmp_ref[:, j * dh:(j + 1) * dh] = _dot_tn(pt.astype(BF16), vcb)
    imp_t = _dot_f32_rhs(cover_ref[...], psum_t)
    blk = lax.broadcasted_iota(jnp.int32, (nslc, tq), 0)
    cur = t_row // NSA_SLC_BLOCK
    forced = (blk == 0) | ((blk <= cur) & (blk > cur - 1 - NSA_N_LOCAL))
    future = blk * NSA_SLC_BLOCK > t_row
    score = jnp.where(forced, BIG, jnp.where(future, NEG, imp_t))
    sub = F32_SUBLANES
    slabs = [score[r * sub:(r + 1) * sub, :] for r in range(nslc // sub)]
    ranks = [jnp.zeros((sub, tq), F32) for _ in slabs]
    row_in_slab = lax.broadcasted_iota(jnp.int32, (sub, tq), 0)
    for j in range(nslc):
        cj = jnp.broadcast_to(score[j:j + 1, :], (sub, tq))
        for r, s_r in enumerate(slabs):
            if r * sub > j:
                before = cj >= s_r
            elif (r + 1) * sub - 1 < j:
                before = cj > s_r
            else:
                before = (cj > s_r) | ((cj == s_r) & (row_in_slab > j - r * sub))
            ranks[r] = ranks[r] + jnp.where(before, 1.0, 0.0)
    topk = float(min(NSA_TOPK, nslc))
    for r, rank in enumerate(ranks):
        sel_ref[r * sub:(r + 1) * sub, :] = jnp.where(rank < topk, 1.0, 0.0)


def _nsa_select(proj, kvc, bsz, seq_len):
    tq, dh, g = NSA_SEL_TQ, NSA_HEAD_DIM, NSA_KV_GROUPS
    nq = seq_len // tq
    ncmp = (seq_len - NSA_CMP_BLOCK) // NSA_CMP_STRIDE + 1
    nck = seq_len // NSA_CMP_STRIDE
    nslc = seq_len // NSA_SLC_BLOCK
    ci = np.arange(nck)[:, None] * NSA_CMP_STRIDE
    sj = np.arange(nslc)[None, :] * NSA_SLC_BLOCK
    cover = ((ci <= sj + NSA_SLC_BLOCK - 1) & (ci + NSA_CMP_BLOCK - 1 >= sj)).astype(np.float32)
    cover = jnp.asarray(cover.T, dtype=BF16)
    return pl.pallas_call(
        functools.partial(_nsa_select_kernel, ncmp=ncmp, nslc=nslc),
        grid=(bsz, g, nq),
        in_specs=[
            pl.BlockSpec((tq, NSA_HPG * dh), lambda b, gi, qi: (b * nq + qi, gi)),
            pl.BlockSpec((None, None, None, nck, dh), lambda b, gi, qi: (b, 0, gi, 0, 0)),
            pl.BlockSpec((None, None, None, nck, dh), lambda b, gi, qi: (b, 1, gi, 0, 0)),
            pl.BlockSpec((nslc, nck), lambda b, gi, qi: (0, 0)),
        ],
        out_specs=[
            pl.BlockSpec((tq, NSA_HPG * dh), lambda b, gi, qi: (b * nq + qi, gi)),
            pl.BlockSpec((None, None, nslc, tq), lambda b, gi, qi: (b, gi, 0, qi)),
        ],
        out_shape=[
            jax.ShapeDtypeStruct((bsz * seq_len, NSA_N_HEADS * dh), F32),
            jax.ShapeDtypeStruct((bsz, g, nslc, seq_len), F32),
        ],
        compiler_params=_cparams("parallel", "parallel", "parallel"),
        name="nsa_cmp_select",
    )(proj, kvc, kvc, cover)


def _nsa_attn_kernel(q_ref, ks_ref, vs_ref, kw_ref, vw_ref, sel_ref, expand_ref, ocmp_ref, gl_ref,
                     o_ref, ksb_ref, vsx_ref, kwb_ref, vwx_ref, selb_ref, q4_ref, pw_ref, m_ref, acc_ref):
    tq = NSA_TQ
    dh = NSA_HEAD_DIM
    kt_w = NSA_KEY_TILE
    win_w = NSA_WINDOW + tq
    c = (dh ** -0.5) * math.log2(math.e)
    qi = pl.program_id(2)

    @pl.when(qi == 0)
    def _():
        ones = jnp.ones((ks_ref.shape[0], dh), BF16)
        for blk in range(ks_ref.shape[0] // LANE):
            cols = slice(blk * LANE, (blk + 1) * LANE)
            ksb_ref[:, cols] = ks_ref[cols, :].T.astype(BF16)
            kwb_ref[:, cols] = kw_ref[cols, :].T.astype(BF16)
        vsx_ref[:, 0:dh] = vs_ref[...].astype(BF16)
        vsx_ref[:, dh:] = ones
        vwx_ref[:, 0:dh] = vw_ref[...].astype(BF16)
        vwx_ref[:, dh:] = ones

    selb_ref[...] = sel_ref[...].T.astype(BF16)
    for j in range(NSA_HPG):
        q4_ref[j * tq:(j + 1) * tq, :] = (q_ref[:, j * dh:(j + 1) * dh] * c).astype(BF16)
    t1 = qi * tq + lax.broadcasted_iota(jnp.int32, (tq, 1), 0)
    m_ref[...] = jnp.full(m_ref.shape, NEG, F32)
    acc_ref[...] = jnp.zeros(acc_ref.shape, F32)
    half = NSA_HPG // 2 * tq

    def slc_tile(off, width, on_diagonal):
        chosen = _dot(selb_ref[...], expand_ref[:, pl.ds(off, width)])
        bias = (chosen - 1.0) * BIG
        if on_diagonal:
            pos = off + lax.broadcasted_iota(jnp.int32, (1, width), 1)
            bias = jnp.where(pos <= t1, bias, NEG)
        k_t = ksb_ref[:, pl.ds(off, width)]
        v_x = vsx_ref[pl.ds(off, width), :]
        pairs = [slice(hp * half, (hp + 1) * half) for hp in range(2)]
        scores = [_dot(q4_ref[pr], k_t) for pr in pairs]
        for hp, pr in enumerate(pairs):
            m_prev = m_ref[pr]
            ps, m_news = [], []
            for e in range(NSA_HPG // 2):
                sj = scores[hp][e * tq:(e + 1) * tq] + bias
                m_new = jnp.maximum(m_prev[e * tq:(e + 1) * tq], jnp.max(sj, axis=-1, keepdims=True))
                m_wide = jnp.concatenate([m_new] * (width // LANE), axis=1)
                ps.append(jnp.exp2(sj - m_wide).astype(BF16))
                m_news.append(m_new)
            m_new = jnp.concatenate(m_news, axis=0)
            alpha = jnp.exp2(m_prev - m_new)
            m_ref[pr] = m_new
            pv = _dot(jnp.concatenate(ps, axis=0), v_x)
            acc_ref[pr] = jnp.concatenate([alpha, alpha], axis=1) * acc_ref[pr] + pv

    n_wide = (qi * tq) // kt_w

    def wide_step(kt, carry):
        slc_tile(pl.multiple_of(kt * kt_w, kt_w), kt_w, False)
        return carry

    def narrow_step(s, carry):
        slc_tile(pl.multiple_of(n_wide * kt_w + s * tq, tq), tq, True)
        return carry

    lax.fori_loop(0, n_wide, wide_step, 0)
    lax.fori_loop(0, qi + 1 - n_wide * (kt_w // tq), narrow_step, 0)

    w0 = pl.multiple_of(jnp.maximum(qi * tq - NSA_WINDOW, 0), tq)
    sw = _dot(q4_ref[...], kwb_ref[:, pl.ds(w0, win_w)])
    diff = t1 - (w0 + lax.broadcasted_iota(jnp.int32, (1, win_w), 1))
    bias_w = jnp.where((diff >= 0) & (diff < NSA_WINDOW), 0.0, NEG)
    for j in range(NSA_HPG):
        rows = slice(j * tq, (j + 1) * tq)
        sj = sw[rows] + bias_w
        pw_ref[rows] = jnp.exp2(sj - jnp.max(sj, axis=-1, keepdims=True)).astype(BF16)
    ow = _dot(pw_ref[...], vwx_ref[pl.ds(w0, win_w), :])

    gates = _sigmoid(gl_ref[...])
    for j in range(NSA_HPG):
        rows = slice(j * tq, (j + 1) * tq)
        o_slc = acc_ref[rows, 0:dh] / acc_ref[rows, dh:]
        o_win = ow[rows, 0:dh] / ow[rows, dh:]
        o = (gates[:, 3 * j:3 * j + 1] * ocmp_ref[:, j * dh:(j + 1) * dh]
             + gates[:, 3 * j + 1:3 * j + 2] * o_slc
             + gates[:, 3 * j + 2:3 * j + 3] * o_win)
        o_ref[:, j * dh:(j + 1) * dh] = o.astype(o_ref.dtype)


def _nsa_attn(proj, sel, ocmp, gate_logits, bsz, seq_len):
    tq, dh, g = NSA_TQ, NSA_HEAD_DIM, NSA_KV_GROUPS
    assert seq_len % NSA_KEY_TILE == 0 and seq_len >= NSA_WINDOW + tq
    rows = NSA_HPG * tq
    nq = seq_len // tq
    nslc = seq_len // NSA_SLC_BLOCK
    expand = (np.arange(nslc)[:, None] == np.arange(seq_len)[None, :] // NSA_SLC_BLOCK)
    expand = jnp.asarray(expand.astype(np.float32), dtype=BF16)
    kv0 = (NSA_N_HEADS * dh + 2 * NSA_KV_DIM) // dh

    def kv_spec(idx):
        return pl.BlockSpec((seq_len, dh), lambda b, gi, qi: (b, kv0 + idx * g + gi))

    qo_spec = pl.BlockSpec((tq, NSA_HPG * dh), lambda b, gi, qi: (b * nq + qi, gi))
    return pl.pallas_call(
        _nsa_attn_kernel,
        grid=(bsz, g, nq),
        in_specs=[
            qo_spec,
            kv_spec(0), kv_spec(1), kv_spec(2), kv_spec(3),
            pl.BlockSpec((None, None, nslc, tq), lambda b, gi, qi: (b, gi, 0, qi)),
            pl.BlockSpec((nslc, seq_len), lambda b, gi, qi: (0, 0)),
            qo_spec,
            pl.BlockSpec((None, tq, 3 * NSA_HPG), lambda b, gi, qi: (gi, b * nq + qi, 0)),
        ],
        out_specs=qo_spec,
        out_shape=jax.ShapeDtypeStruct((bsz * seq_len, NSA_N_HEADS * dh), BF16),
        scratch_shapes=[
            pltpu.VMEM((dh, seq_len), BF16),
            pltpu.VMEM((seq_len, 2 * dh), BF16),
            pltpu.VMEM((dh, seq_len), BF16),
            pltpu.VMEM((seq_len, 2 * dh), BF16),
            pltpu.VMEM((tq, nslc), BF16),
            pltpu.VMEM((rows, dh), BF16),
            pltpu.VMEM((rows, NSA_WINDOW + tq), BF16),
            pltpu.VMEM((rows, LANE), F32),
            pltpu.VMEM((rows, 2 * dh), F32),
        ],
        compiler_params=_cparams("parallel", "parallel", "arbitrary"),
        name="nsa_slc_win_attention",
    )(proj, proj, proj, proj, proj, sel, expand, ocmp, gate_logits)


def _sgu_kernel(u_ref, v_ref, g_ref, ws_ref, bst_ref, o_ref):
    v = v_ref[...].astype(F32)
    mu = jnp.mean(v, axis=-1, keepdims=True)
    vc = v - mu
    vn = (vc * lax.rsqrt(jnp.mean(vc * vc, axis=-1, keepdims=True) + RMS_EPS) * g_ref[...]).astype(BF16)
    q = SGU_CHUNK
    tri = lax.broadcasted_iota(jnp.int32, (q, q), 0) >= lax.broadcasted_iota(jnp.int32, (q, q), 1)
    for g in range(SGU_GROUPS):
        cols = slice(g * SGU_GROUP_DIM, (g + 1) * SGU_GROUP_DIM)
        w_m = jnp.where(tri, ws_ref[g], 0.0).astype(BF16)
        sv = _dot(w_m, vn[:, cols]) + bst_ref[:, g:g + 1]
        o_ref[:, cols] = (u_ref[:, cols].astype(F32) * sv).astype(o_ref.dtype)


def _sgu_core(proj, ln_g, w_spatial, b_spatial):
    m = proj.shape[0]
    q = SGU_CHUNK
    return pl.pallas_call(
        _sgu_kernel,
        grid=(m // q,),
        in_specs=[
            pl.BlockSpec((q, SGU_WIDTH), lambda i: (i, 0)),
            pl.BlockSpec((q, SGU_WIDTH), lambda i: (i, 1)),
            pl.BlockSpec((1, SGU_WIDTH), lambda i: (0, 0)),
            pl.BlockSpec((SGU_GROUPS, q, q), lambda i: (0, 0, 0)),
            pl.BlockSpec((q, SGU_GROUPS), lambda i: (0, 0)),
        ],
        out_specs=pl.BlockSpec((q, SGU_WIDTH), lambda i: (i, 0)),
        out_shape=jax.ShapeDtypeStruct((m, SGU_WIDTH), BF16),
        compiler_params=_cparams("parallel"),
        name="sgu_core",
    )(proj, proj, ln_g.reshape(1, SGU_WIDTH), w_spatial, b_spatial.T)


def _pool_kernel(z_ref, zh_ref, wg_ref, sc_ref, o_ref, zz_ref, *, tm, blocks_per_seq):
    i = pl.program_id(0)
    keep = (i % blocks_per_seq != 0).astype(F32)
    zz_ref[0:POOL_HALO, :] = zh_ref[...] * keep
    zz_ref[POOL_HALO:, :] = z_ref[...]
    t = (i % blocks_per_seq) * tm + lax.broadcasted_iota(jnp.int32, (tm, 1), 0)
    for gi, win in enumerate(POOL_WINDOWS):
        cols = slice(gi * POOL_GROUP_DIM, (gi + 1) * POOL_GROUP_DIM)
        s = zz_ref[POOL_HALO:POOL_HALO + tm, cols]
        for k in range(1, win):
            s = s + zz_ref[POOL_HALO - k:POOL_HALO - k + tm, cols]
        count = jnp.minimum(t + 1, win).astype(F32)
        pooled = s / count - z_ref[:, cols]
        y = _dot(pooled.astype(BF16), wg_ref[gi]) * sc_ref[:, cols]
        o_ref[:, cols] = y.astype(o_ref.dtype)


def _pool_core(z, w_group, scale, seq_len, *, tm=512):
    m, n = z.shape
    tm = min(tm, seq_len)
    halo_blocks = tm // POOL_HALO
    return pl.pallas_call(
        functools.partial(_pool_kernel, tm=tm, blocks_per_seq=seq_len // tm),
        grid=(m // tm,),
        in_specs=[
            pl.BlockSpec((tm, n), lambda i: (i, 0)),
            pl.BlockSpec((POOL_HALO, n), lambda i: (jnp.maximum(i * halo_blocks - 1, 0), 0)),
            pl.BlockSpec(w_group.shape, lambda i: (0, 0, 0)),
            pl.BlockSpec((1, n), lambda i: (0, 0)),
        ],
        out_specs=pl.BlockSpec((tm, n), lambda i: (i, 0)),
        out_shape=jax.ShapeDtypeStruct((m, n), BF16),
        scratch_shapes=[pltpu.VMEM((tm + POOL_HALO, n), F32)],
        compiler_params=_cparams("parallel"),
        name="pool_core",
    )(z, z, w_group, scale.reshape(1, n))


def _pad_cols(w, n):
    return jnp.pad(w, ((0, 0), (0, 0), (0, n - w.shape[2])))


def _ssd_layer(x, g_pre, g_post, w_in, j, conv_w, conv_b, dt_bias, a_log, d_skip, norm_g, w_out, bsz, seq_len):
    proj, _ = _ssd_in_proj(x, g_pre, w_in, j, conv_w, conv_b, seq_len)
    y = _ssd_core(proj, dt_bias, a_log, d_skip, norm_g, bsz, seq_len)
    return _matmul_post(y, w_out, j, g_post, x)


def _first_ssd_layer(x, g_pre, g_post, w_in, conv_w, conv_b, dt_bias, a_log, d_skip, norm_g, f32_weights,
                     bsz, seq_len):
    proj, cast = _ssd_in_proj(x, g_pre, w_in, 0, conv_w, conv_b, seq_len, tuple(f32_weights.values()))
    wb = dict(zip(f32_weights, cast))
    y = _ssd_core(proj, dt_bias, a_log, d_skip, norm_g, bsz, seq_len)
    return _matmul_post(y, wb["ssd_out"], 0, g_post, x), wb


def _nsa_layer(x, g_pre, g_post, w_in, j, cmp_pos, cmp_w1, cmp_w2, w_out, bsz, seq_len):
    proj = _norm_matmul(x, g_pre, w_in, j, jnp.zeros((NSA_IN_PAD,), F32))
    kvc = _nsa_compress(proj, cmp_pos, cmp_w1, cmp_w2, bsz, seq_len)
    ocmp, sel = _nsa_select(proj, kvc, bsz, seq_len)
    ngate = 3 * NSA_N_HEADS
    gl = proj[:, NSA_GATE_COL:NSA_GATE_COL + ngate].reshape(-1, NSA_KV_GROUPS, 3 * NSA_HPG).transpose(1, 0, 2)
    o = _nsa_attn(proj, sel, ocmp, gl, bsz, seq_len)
    return _matmul_post(o, w_out, j, g_post, x)


def _sgu_layer(x, g_pre, g_post, w_in, j, b_in, ln_g, w_spatial, b_spatial, w_out):
    proj = _norm_matmul(x, g_pre, w_in, j, b_in, act="gelu", out_dtype=BF16)
    y = _sgu_core(proj, ln_g, w_spatial, b_spatial)
    return _matmul_post(y, w_out, j, g_post, x)


def _pool_layer(x, g_pre, g_post, w_in, j, w_group, scale, w_out, seq_len):
    z = _norm_matmul(x, g_pre, w_in, j, jnp.zeros((D_MODEL,), F32))
    y = _pool_core(z, w_group.astype(BF16), scale, seq_len)
    return _matmul_post(y, w_out, j, g_post, x)


def kernel(x, mem, norm_pre, norm_post, norm_mem, ssd_w_in, ssd_conv_w, ssd_conv_b, ssd_dt_bias, ssd_a_log, ssd_d, ssd_norm_g, ssd_w_out, nsa_w_in, nsa_cmp_pos, nsa_cmp_w1, nsa_cmp_w2, nsa_w_out, sgu_w_in, sgu_b_in, sgu_ln_g, sgu_w_spatial, sgu_b_spatial, sgu_w_out, pool_w_in, pool_w_group, pool_scale, pool_w_out, xa_w_q, xa_w_kv, xa_w_o, ffn_w_up, ffn_conv_w, ffn_conv_b, ffn_w_down):
    bsz, seq_len, d = x.shape
    depth = norm_pre.shape[0]
    xf = x.reshape(bsz * seq_len, d)
    memf = mem.reshape(bsz * mem.shape[1], d)
    ssd_in = _pad_cols(ssd_w_in, SSD_IN_PAD).astype(BF16)
    nsa_in = _pad_cols(nsa_w_in, NSA_IN_PAD).astype(BF16)
    f32_weights = dict(ssd_out=ssd_w_out, nsa_out=nsa_w_out, sgu_in=sgu_w_in, sgu_out=sgu_w_out,
                       pool_in=pool_w_in, pool_out=pool_w_out, xa_q=xa_w_q, xa_kv=xa_w_kv, xa_o=xa_w_o,
                       ffn_up=ffn_w_up, ffn_down=ffn_w_down)
    wb = None
    for i in range(depth):
        kind, j = i % 4, i // 4
        if i == 0:
            xf, wb = _first_ssd_layer(xf, norm_pre[i, 0], norm_post[i, 0], ssd_in, ssd_conv_w[j], ssd_conv_b[j],
                                      ssd_dt_bias[j], ssd_a_log[j], ssd_d[j], ssd_norm_g[j], f32_weights,
                                      bsz, seq_len)
        elif kind == 0:
            xf = _ssd_layer(xf, norm_pre[i, 0], norm_post[i, 0], ssd_in, j, ssd_conv_w[j], ssd_conv_b[j],
                            ssd_dt_bias[j], ssd_a_log[j], ssd_d[j], ssd_norm_g[j], wb["ssd_out"], bsz, seq_len)
        elif kind == 1:
            xf = _nsa_layer(xf, norm_pre[i, 0], norm_post[i, 0], nsa_in, j, nsa_cmp_pos[j], nsa_cmp_w1[j],
                            nsa_cmp_w2[j], wb["nsa_out"], bsz, seq_len)
        elif kind == 2:
            xf = _sgu_layer(xf, norm_pre[i, 0], norm_post[i, 0], wb["sgu_in"], j, sgu_b_in[j], sgu_ln_g[j],
                            sgu_w_spatial[j], sgu_b_spatial[j], wb["sgu_out"])
        else:
            xf = _pool_layer(xf, norm_pre[i, 0], norm_post[i, 0], wb["pool_in"], j, pool_w_group[j],
                             pool_scale[j], wb["pool_out"], seq_len)
        kv = _norm_matmul(memf, norm_mem[i], wb["xa_kv"], i, jnp.zeros((2 * XA_DIM,), F32),
                          out_dtype=BF16, tm=MEM_LEN)
        xf = _xattn(xf, kv, norm_pre[i, 1], norm_post[i, 1], wb["xa_q"], wb["xa_o"], i, seq_len)
        act = _ffn_up(xf, norm_pre[i, 2], wb["ffn_up"], i, ffn_conv_w[i], ffn_conv_b[i], seq_len)
        xf = _matmul_post(act, wb["ffn_down"], i, norm_post[i, 2], xf)
    return xf.reshape(bsz, seq_len, d)
```

```python
import functools
import math

import jax
import jax.numpy as jnp
import numpy as np
from jax import lax
from jax.experimental import pallas as pl
from jax.experimental.pallas import tpu as pltpu

F32 = jnp.float32
BF16 = jnp.bfloat16

D_MODEL = 2048
RMS_EPS = 1e-6
NEG = -1e30
BIG = 1e30
MEM_LEN = 256

SSD_D_INNER = 4096
SSD_HEAD_DIM = 64
SSD_N_HEADS = 64
SSD_N_GROUPS = 8
SSD_HPG = 8
SSD_D_STATE = 128
SSD_CONV = 4
SSD_CHUNK = 128
SSD_GROUP_W = SSD_HPG * SSD_HEAD_DIM
SSD_BC_DIM = SSD_N_GROUPS * SSD_D_STATE
SSD_IN_PAD = 10752

NSA_HEAD_DIM = 128
NSA_N_HEADS = 16
NSA_KV_GROUPS = 4
NSA_HPG = 4
NSA_KV_DIM = 512
NSA_CMP_BLOCK = 32
NSA_CMP_STRIDE = 16
NSA_CMP_HIDDEN = 256
NSA_SLC_BLOCK = 64
NSA_TOPK = 16
NSA_N_LOCAL = 2
NSA_WINDOW = 512
NSA_TQ = 256
NSA_KEY_TILE = 1024
NSA_SEL_TQ = 512
NSA_IN_PAD = 5632
NSA_GATE_COL = 5120

SGU_CHUNK = 128
SGU_WIDTH = 4096
SGU_GROUPS = 8
SGU_GROUP_DIM = 512

POOL_WINDOWS = (2, 4, 8, 16)
POOL_GROUP_DIM = 512
POOL_HALO = max(POOL_WINDOWS)

XA_HEADS = 4
XA_HEAD_DIM = 128
XA_DIM = 512

FFN_HIDDEN = 5632
FFN_CONV = 3

LANE = 128
F32_SUBLANES = 8
BF16_SUBLANES = 16
SSD_HALO = BF16_SUBLANES
FFN_HALO = BF16_SUBLANES
VMEM_LIMIT = 56 * 1024 * 1024


def _cparams(*sem):
    return pltpu.CompilerParams(dimension_semantics=sem, vmem_limit_bytes=VMEM_LIMIT)


def _rider_plan(riders, steps, nj):
    flats, in_specs, out_specs, out_shapes = [], [], [], []
    for w, first, count in riders:
        k, n = w.shape[1], w.shape[2]
        start, rows = first * k, count * k
        r = BF16_SUBLANES
        while rows % r or start % r or rows // r > steps:
            r += BF16_SUBLANES
            assert r <= rows, (w.shape, first, count, steps)
        base, last = start // r, rows // r - 1
        flats.append(w.reshape(-1, n))
        in_specs.append(pl.BlockSpec(
            (r, n), lambda i, j, base=base, last=last: (base + jnp.minimum(i * nj + j, last), 0)))
        out_specs.append(pl.BlockSpec((r, n), lambda i, j, last=last: (jnp.minimum(i * nj + j, last), 0)))
        out_shapes.append(jax.ShapeDtypeStruct((rows, n), BF16))
    return flats, in_specs, out_specs, out_shapes


def _cast_riders(rider_in, rider_out):
    for src, dst in zip(rider_in, rider_out):
        dst[...] = src[...].astype(dst.dtype)


def _rider_results(outs, riders):
    return [o.reshape(count, w.shape[1], w.shape[2]) for o, (w, _, count) in zip(outs, riders)]


def _dot(a, b):
    return jnp.dot(a, b, preferred_element_type=F32)


def _dot_nt(a, b):
    return lax.dot_general(a, b, (((1,), (1,)), ((), ())), preferred_element_type=F32)


def _dot_tn(a, b):
    return lax.dot_general(a, b, (((0,), (0,)), ((), ())), preferred_element_type=F32)


def _split3(v):
    hi = v.astype(BF16)
    r1 = v - hi.astype(F32)
    mid = r1.astype(BF16)
    lo = (r1 - mid.astype(F32)).astype(BF16)
    return hi, mid, lo


def _dot_2piece_lhs(v, e2):
    hi = v.astype(BF16)
    mid = (v - hi.astype(F32)).astype(BF16)
    return _dot(jnp.concatenate([hi, mid], axis=1), e2)


def _dot_f32_rhs(e, v):
    hi, mid, lo = _split3(v)
    return _dot(e, hi) + _dot(e, mid) + _dot(e, lo)


def _rms(x, g):
    ms = jnp.mean(x * x, axis=-1, keepdims=True)
    return x * lax.rsqrt(ms + RMS_EPS) * g


def _sigmoid(x):
    return 1.0 / (1.0 + jnp.exp(-x))


def _silu(x):
    h = 0.5 * x
    return h + h * jnp.tanh(h)


def _gelu(x):
    c = math.sqrt(2.0 / math.pi)
    return x * (0.5 * (1.0 + jnp.tanh(c * (x + 0.044715 * (x * x * x)))))


def _softplus(x):
    return jnp.maximum(x, 0.0) + jnp.log1p(jnp.exp(-jnp.abs(x)))


def _norm_mm_kernel(x_ref, g_ref, w_ref, b_ref, o_ref, h_ref, *, act):
    @pl.when(pl.program_id(1) == 0)
    def _():
        h_ref[...] = _rms(x_ref[...], g_ref[...]).astype(BF16)

    y = _dot(h_ref[...], w_ref[...]) + b_ref[...]
    if act == "gelu":
        y = _gelu(y)
    o_ref[...] = y.astype(o_ref.dtype)


def _norm_matmul(x, g, w, li, b, *, act=None, out_dtype=F32, tm=1024, tn=512):
    m, k = x.shape
    n = w.shape[2]
    tm = min(tm, m)
    assert m % tm == 0 and n % tn == 0, (m, n, tm, tn)
    return pl.pallas_call(
        functools.partial(_norm_mm_kernel, act=act),
        grid=(m // tm, n // tn),
        in_specs=[
            pl.BlockSpec((tm, k), lambda i, j: (i, 0)),
            pl.BlockSpec((1, k), lambda i, j: (0, 0)),
            pl.BlockSpec((None, k, tn), lambda i, j: (li, 0, j)),
            pl.BlockSpec((1, tn), lambda i, j: (0, j)),
        ],
        out_specs=pl.BlockSpec((tm, tn), lambda i, j: (i, j)),
        out_shape=jax.ShapeDtypeStruct((m, n), out_dtype),
        scratch_shapes=[pltpu.VMEM((tm, k), BF16)],
        compiler_params=_cparams("parallel", "arbitrary"),
        name="norm_matmul",
    )(x, g.reshape(1, k), w, b.reshape(1, n))


def _mm_post_kernel(a_ref, w_ref, g_ref, r_ref, o_ref, *, nj, tn):
    j = pl.program_id(1)
    o_ref[:, pl.ds(pl.multiple_of(j * tn, tn), tn)] = _dot(a_ref[...], w_ref[...])

    @pl.when(j == nj - 1)
    def _():
        o_ref[...] = r_ref[...] + _rms(o_ref[...], g_ref[...])


def _matmul_post(a, w, li, g, res, *, tm=512, tn=512):
    m, kdim = a.shape
    n = w.shape[2]
    tm = min(tm, m)
    assert m % tm == 0 and n % tn == 0
    nj = n // tn
    return pl.pallas_call(
        functools.partial(_mm_post_kernel, nj=nj, tn=tn),
        grid=(m // tm, nj),
        in_specs=[
            pl.BlockSpec((tm, kdim), lambda i, j: (i, 0)),
            pl.BlockSpec((None, kdim, tn), lambda i, j: (li, 0, j)),
            pl.BlockSpec((1, n), lambda i, j: (0, 0)),
            pl.BlockSpec((tm, n), lambda i, j: (i, 0)),
        ],
        out_specs=pl.BlockSpec((tm, n), lambda i, j: (i, 0)),
        out_shape=jax.ShapeDtypeStruct((m, n), F32),
        compiler_params=_cparams("parallel", "arbitrary"),
        name="matmul_postnorm_residual",
    )(a, w, g.reshape(1, n), res)


def _ffn_up_kernel(x_ref, xh_ref, g_ref, wg_ref, wv_ref, cwg_ref, cwv_ref, cbg_ref, cbv_ref,
                   *rest, blocks_per_seq, n_riders):
    rider_in = rest[:n_riders]
    o_ref = rest[n_riders]
    rider_out = rest[n_riders + 1:2 * n_riders + 1]
    h_ref = rest[2 * n_riders + 1]
    i = pl.program_id(0)

    @pl.when(pl.program_id(1) == 0)
    def _():
        keep = (i % blocks_per_seq != 0).astype(F32)
        h_ref[0:FFN_HALO, :] = (_rms(xh_ref[...], g_ref[...]) * keep).astype(BF16)
        h_ref[FFN_HALO:, :] = _rms(x_ref[...], g_ref[...]).astype(BF16)

    _cast_riders(rider_in, rider_out)
    h = h_ref[...]

    def conv(u, cw_ref, cb_ref):
        y = cb_ref[...] + u[FFN_HALO:] * cw_ref[FFN_CONV - 1:FFN_CONV, :]
        for tap in range(FFN_CONV - 1):
            back = FFN_CONV - 1 - tap
            y = y + pltpu.roll(u, back, axis=0)[FFN_HALO:] * cw_ref[tap:tap + 1, :]
        return y

    gate = conv(_dot(h, wg_ref[...]), cwg_ref, cbg_ref)
    val = conv(_dot(h, wv_ref[...]), cwv_ref, cbv_ref)
    o_ref[...] = (_silu(gate) * val).astype(o_ref.dtype)


def _ffn_up(x, g, w_up, li, conv_w, conv_b, seq_len, riders=(), *, tm=1024, tn=512):
    m, k = x.shape
    tm = min(tm, seq_len)
    assert m % tm == 0 and seq_len % tm == 0 and FFN_HIDDEN % tn == 0
    nj = FFN_HIDDEN // tn
    halo_blocks = tm // FFN_HALO
    flat, rider_in_specs, rider_out_specs, rider_shapes = _rider_plan(riders, (m // tm) * nj, nj)
    outs = pl.pallas_call(
        functools.partial(_ffn_up_kernel, blocks_per_seq=seq_len // tm, n_riders=len(flat)),
        grid=(m // tm, nj),
        in_specs=[
            pl.BlockSpec((tm, k), lambda i, j: (i, 0)),
            pl.BlockSpec((FFN_HALO, k), lambda i, j: (jnp.maximum(i * halo_blocks - 1, 0), 0)),
            pl.BlockSpec((1, k), lambda i, j: (0, 0)),
            pl.BlockSpec((None, k, tn), lambda i, j: (li, 0, j)),
            pl.BlockSpec((None, k, tn), lambda i, j: (li, 0, j + nj)),
            pl.BlockSpec((FFN_CONV, tn), lambda i, j: (0, j)),
            pl.BlockSpec((FFN_CONV, tn), lambda i, j: (0, j + nj)),
            pl.BlockSpec((1, tn), lambda i, j: (0, j)),
            pl.BlockSpec((1, tn), lambda i, j: (0, j + nj)),
        ] + rider_in_specs,
        out_specs=[pl.BlockSpec((tm, tn), lambda i, j: (i, j))] + rider_out_specs,
        out_shape=[jax.ShapeDtypeStruct((m, FFN_HIDDEN), BF16)] + rider_shapes,
        scratch_shapes=[pltpu.VMEM((tm + FFN_HALO, k), BF16)],
        compiler_params=_cparams("arbitrary", "arbitrary"),
        name="ffn_up_conv_gate",
    )(x, x, g.reshape(1, k), w_up, w_up, conv_w, conv_w,
      conv_b.reshape(1, -1), conv_b.reshape(1, -1), *flat)
    return outs[0], _rider_results(outs[1:], riders)


def _xattn_kernel(x_ref, kv_ref, gpre_ref, gpost_ref, wq_ref, wo_ref, o_ref, att_ref):
    x = x_ref[...]
    h = _rms(x, gpre_ref[...]).astype(BF16)
    c = (XA_HEAD_DIM ** -0.5) * math.log2(math.e)
    q = (_dot(h, wq_ref[...]) * c).astype(BF16)
    for hd in range(XA_HEADS):
        lo = hd * XA_HEAD_DIM
        s = _dot_nt(q[:, lo:lo + XA_HEAD_DIM], kv_ref[:, lo:lo + XA_HEAD_DIM])
        e = jnp.exp2(s - jnp.max(s, axis=-1, keepdims=True))
        pv = _dot(e.astype(BF16), kv_ref[:, XA_DIM + lo:XA_DIM + lo + XA_HEAD_DIM])
        att_ref[:, lo:lo + XA_HEAD_DIM] = (pv / jnp.sum(e, axis=-1, keepdims=True)).astype(BF16)
    a = _dot(att_ref[...], wo_ref[...])
    o_ref[...] = x + _rms(a, gpost_ref[...])


def _xattn(x, kv, g_pre, g_post, w_q, w_o, li, seq_len, *, tm=512):
    m, k = x.shape
    tm = min(tm, seq_len)
    bps = seq_len // tm
    return pl.pallas_call(
        _xattn_kernel,
        grid=(m // tm,),
        in_specs=[
            pl.BlockSpec((tm, k), lambda i: (i, 0)),
            pl.BlockSpec((MEM_LEN, 2 * XA_DIM), lambda i: (i // bps, 0)),
            pl.BlockSpec((1, k), lambda i: (0, 0)),
            pl.BlockSpec((1, k), lambda i: (0, 0)),
            pl.BlockSpec((None, k, XA_DIM), lambda i: (li, 0, 0)),
            pl.BlockSpec((None, XA_DIM, k), lambda i: (li, 0, 0)),
        ],
        out_specs=pl.BlockSpec((tm, k), lambda i: (i, 0)),
        out_shape=jax.ShapeDtypeStruct((m, k), F32),
        scratch_shapes=[pltpu.VMEM((tm, XA_DIM), BF16)],
        compiler_params=_cparams("parallel"),
        name="memory_cross_attention",
    )(x, kv, g_pre.reshape(1, k), g_post.reshape(1, k), w_q, w_o)


def _ssd_in_kernel(x_ref, xh_ref, g_ref, w_ref, cw_ref, cb_ref, *rest, blocks_per_seq, tn, n_riders):
    rider_in = rest[:n_riders]
    o_ref = rest[n_riders]
    rider_out = rest[n_riders + 1:2 * n_riders + 1]
    h_ref = rest[2 * n_riders + 1]
    i = pl.program_id(0)
    j = pl.program_id(1)
    z_tiles = SSD_D_INNER // tn
    conv_tiles = (SSD_D_INNER + 2 * SSD_BC_DIM) // tn
    cast_riders = functools.partial(_cast_riders, rider_in, rider_out)

    @pl.when(j == 0)
    def _():
        keep = (i % blocks_per_seq != 0).astype(F32)
        h_ref[0:SSD_HALO, :] = (_rms(xh_ref[...], g_ref[...]) * keep).astype(BF16)
        h_ref[SSD_HALO:, :] = _rms(x_ref[...], g_ref[...]).astype(BF16)

    @pl.when(j < z_tiles)
    def _():
        cast_riders()
        o_ref[...] = _silu(_dot(h_ref[SSD_HALO:, :], w_ref[...]))

    @pl.when((j >= z_tiles) & (j < z_tiles + conv_tiles))
    def _():
        cast_riders()
        u = _dot(h_ref[...], w_ref[...])
        y = cb_ref[...] + u[SSD_HALO:] * cw_ref[SSD_CONV - 1:SSD_CONV, :]
        for tap in range(SSD_CONV - 1):
            back = SSD_CONV - 1 - tap
            y = y + pltpu.roll(u, back, axis=0)[SSD_HALO:] * cw_ref[tap:tap + 1, :]
        o_ref[...] = _silu(y)

    @pl.when(j >= z_tiles + conv_tiles)
    def _():
        cast_riders()
        o_ref[...] = _dot(h_ref[SSD_HALO:, :], w_ref[...])


def _ssd_in_proj(x, g, w_in, li, conv_w, conv_b, seq_len, riders=(), *, tm=1024, tn=512):
    m, k = x.shape
    n = w_in.shape[2]
    tm = min(tm, seq_len)
    assert m % tm == 0 and seq_len % tm == 0 and n % tn == 0
    z_tiles = SSD_D_INNER // tn
    conv_tiles = (SSD_D_INNER + 2 * SSD_BC_DIM) // tn
    halo_blocks = tm // SSD_HALO
    conv_tile = lambda j: jnp.clip(j - z_tiles, 0, conv_tiles - 1)
    nj = n // tn
    flat, rider_in_specs, rider_out_specs, rider_shapes = _rider_plan(riders, (m // tm) * nj, nj)
    outs = pl.pallas_call(
        functools.partial(_ssd_in_kernel, blocks_per_seq=seq_len // tm, tn=tn, n_riders=len(flat)),
        grid=(m // tm, nj),
        in_specs=[
            pl.BlockSpec((tm, k), lambda i, j: (i, 0)),
            pl.BlockSpec((SSD_HALO, k), lambda i, j: (jnp.maximum(i * halo_blocks - 1, 0), 0)),
            pl.BlockSpec((1, k), lambda i, j: (0, 0)),
            pl.BlockSpec((None, k, tn), lambda i, j: (li, 0, j)),
            pl.BlockSpec((SSD_CONV, tn), lambda i, j: (0, conv_tile(j))),
            pl.BlockSpec((1, tn), lambda i, j: (0, conv_tile(j))),
        ] + rider_in_specs,
        out_specs=[pl.BlockSpec((tm, tn), lambda i, j: (i, j))] + rider_out_specs,
        out_shape=[jax.ShapeDtypeStruct((m, n), F32)] + rider_shapes,
        scratch_shapes=[pltpu.VMEM((tm + SSD_HALO, k), BF16)],
        compiler_params=_cparams("arbitrary", "arbitrary"),
        name="ssd_in_proj_conv",
    )(x, x, g.reshape(1, k), w_in, conv_w, conv_b.reshape(1, -1), *flat)
    return outs[0], _rider_results(outs[1:], riders)


def _ssd_kernel(zs_ref, xs_ref, bm_ref, cm_ref, dtr_ref, dtb_ref, alog_ref,
                dskip_ref, ng_ref, expand_ref, tril_ref, o_ref,
                wx_ref, expax_ref, cdx_ref, acsg_ref, acst_ref, state_ref, y_ref):
    q = SSD_CHUNK
    c = pl.program_id(1)

    @pl.when(c == 0)
    def _():
        state_ref[...] = jnp.zeros(state_ref.shape, F32)

    dt = _softplus(dtr_ref[...] + dtb_ref[...])
    a = dt * (-jnp.exp(alog_ref[...]))
    a_cs = _dot_f32_rhs(tril_ref[...], a)
    a_last = a_cs[q - 1:q, :]
    expand = expand_ref[...]
    wx_ref[...] = _dot_2piece_lhs(dt * jnp.exp(a_last - a_cs), expand)
    expax_ref[...] = _dot_2piece_lhs(jnp.exp(a_cs), expand)
    cdx_ref[...] = _dot_2piece_lhs(jnp.broadcast_to(jnp.exp(a_last), (8, LANE)), expand)
    acst_ref[...] = (a_cs - jnp.log(dt)).T.reshape(LANE // SSD_HPG, SSD_HPG, q)
    for g in range(SSD_N_GROUPS):
        acsg_ref[g] = a_cs if g == 0 else pltpu.roll(a_cs, LANE - g * SSD_HPG, axis=1)

    row = lax.broadcasted_iota(jnp.int32, (q, q), 0)
    col = lax.broadcasted_iota(jnp.int32, (q, q), 1)
    causal = row >= col
    left_half = col < SSD_HEAD_DIM

    def group_body(g, carry):
        off = pl.multiple_of(g * SSD_GROUP_W, SSD_GROUP_W)
        offn = pl.multiple_of(g * SSD_D_STATE, SSD_D_STATE)
        cm = cm_ref[:, pl.ds(offn, SSD_D_STATE)]
        bm = bm_ref[:, pl.ds(offn, SSD_D_STATE)]
        cmb = cm.astype(BF16)
        cb = _dot_nt(cmb, bm.astype(BF16))
        xs = xs_ref[:, pl.ds(off, SSD_GROUP_W)]
        st = state_ref[g]
        y_off = _dot(cmb, st.astype(BF16)) * expax_ref[:, pl.ds(off, SSD_GROUP_W)]
        s_new = _dot_tn(bm.astype(BF16), (xs * wx_ref[:, pl.ds(off, SSD_GROUP_W)]).astype(BF16))
        state_ref[g] = st * cdx_ref[0:1, pl.ds(off, SSD_GROUP_W)] + s_new
        acs_g = acsg_ref[g]
        acst_g = acst_ref[g]
        skip = xs * dskip_ref[:, pl.ds(off, SSD_GROUP_W)]
        for pr in range(SSD_HPG // 2):
            scs = []
            for e in range(2):
                j = 2 * pr + e
                diff = acs_g[:, j:j + 1] - acst_g[j:j + 1, :]
                dec = jnp.where(causal, jnp.exp(diff), 0.0)
                scs.append((cb * dec).astype(BF16))
            xp = xs[:, pr * LANE:(pr + 1) * LANE]
            x0 = jnp.where(left_half, xp, 0.0).astype(BF16)
            x1 = jnp.where(left_half, 0.0, xp).astype(BF16)
            y_diag = _dot(jnp.concatenate(scs, axis=1), jnp.concatenate([x0, x1], axis=0))
            y = y_diag + y_off[:, pr * LANE:(pr + 1) * LANE] + skip[:, pr * LANE:(pr + 1) * LANE]
            y_ref[:, pl.ds(pl.multiple_of(off + pr * LANE, LANE), LANE)] = y
        return carry

    lax.fori_loop(0, SSD_N_GROUPS, group_body, 0, unroll=8)

    gated = y_ref[...] * zs_ref[...]
    o_ref[...] = _rms(gated, ng_ref[...]).astype(o_ref.dtype)


def _ssd_core(proj, dt_bias, a_log, d_skip, norm_g, bsz, seq_len):
    q = SSD_CHUNK
    nc = seq_len // q
    m = bsz * seq_len
    pad = LANE - SSD_N_HEADS
    dtb = jnp.pad(dt_bias, (0, pad)).reshape(1, LANE)
    alog = jnp.pad(a_log, (0, pad)).reshape(1, LANE)
    dskip = jnp.repeat(d_skip, SSD_HEAD_DIM).reshape(1, SSD_D_INNER)
    heads = np.arange(2 * LANE)[:, None] % LANE
    chans = np.arange(SSD_D_INNER)[None, :] // SSD_HEAD_DIM
    expand = jnp.asarray((heads == chans).astype(np.float32), dtype=BF16)
    tril = jnp.asarray(np.tril(np.ones((q, q), np.float32)), dtype=BF16)
    row = lambda b, c: b * nc + c
    b_blk = 2 * SSD_D_INNER // SSD_BC_DIM
    dt_blk = (2 * SSD_D_INNER + 2 * SSD_BC_DIM) // LANE
    const = lambda b, c: (0, 0)
    return pl.pallas_call(
        _ssd_kernel,
        grid=(bsz, nc),
        in_specs=[
            pl.BlockSpec((q, SSD_D_INNER), lambda b, c: (row(b, c), 0)),
            pl.BlockSpec((q, SSD_D_INNER), lambda b, c: (row(b, c), 1)),
            pl.BlockSpec((q, SSD_BC_DIM), lambda b, c: (row(b, c), b_blk)),
            pl.BlockSpec((q, SSD_BC_DIM), lambda b, c: (row(b, c), b_blk + 1)),
            pl.BlockSpec((q, LANE), lambda b, c: (row(b, c), dt_blk)),
            pl.BlockSpec((1, LANE), const),
            pl.BlockSpec((1, LANE), const),
            pl.BlockSpec((1, SSD_D_INNER), const),
            pl.BlockSpec((1, SSD_D_INNER), const),
            pl.BlockSpec((2 * LANE, SSD_D_INNER), const),
            pl.BlockSpec((q, q), const),
        ],
        out_specs=pl.BlockSpec((q, SSD_D_INNER), lambda b, c: (row(b, c), 0)),
        out_shape=jax.ShapeDtypeStruct((m, SSD_D_INNER), BF16),
        scratch_shapes=[
            pltpu.VMEM((q, SSD_D_INNER), F32),
            pltpu.VMEM((q, SSD_D_INNER), F32),
            pltpu.VMEM((8, SSD_D_INNER), F32),
            pltpu.VMEM((SSD_N_GROUPS, q, LANE), F32),
            pltpu.VMEM((LANE // SSD_HPG, SSD_HPG, q), F32),
            pltpu.VMEM((SSD_N_GROUPS, SSD_D_STATE, SSD_GROUP_W), F32),
            pltpu.VMEM((q, SSD_D_INNER), F32),
        ],
        compiler_params=_cparams("parallel", "arbitrary"),
        name="ssd_core",
    )(proj, proj, proj, proj, proj, dtb, alog, dskip, norm_g.reshape(1, SSD_D_INNER), expand, tril)


def _nsa_compress_kernel(t_ref, pos_ref, w1_ref, w2_ref, o_ref, *, nchunk):
    half = NSA_CMP_STRIDE
    dh = NSA_HEAD_DIM
    p_lo = jnp.zeros((nchunk, NSA_CMP_HIDDEN), F32)
    p_hi = jnp.zeros((nchunk, NSA_CMP_HIDDEN), F32)
    for l in range(half):
        rows = t_ref[pl.ds(l, nchunk, stride=half), :]
        a_lo = (rows + pos_ref[l:l + 1, :]).astype(BF16)
        a_hi = (rows + pos_ref[half + l:half + l + 1, :]).astype(BF16)
        p_lo = p_lo + _dot(a_lo, w1_ref[l * dh:(l + 1) * dh, :].astype(BF16))
        p_hi = p_hi + _dot(a_hi, w1_ref[(half + l) * dh:(half + l + 1) * dh, :].astype(BF16))
    pre = p_lo + pltpu.roll(p_hi, nchunk - 1, axis=0)
    o_ref[...] = _dot(_gelu(pre).astype(BF16), w2_ref[...].astype(BF16))


def _nsa_compress(proj, cmp_pos, cmp_w1, cmp_w2, bsz, seq_len):
    nchunk = seq_len // NSA_CMP_STRIDE
    g, dh = NSA_KV_GROUPS, NSA_HEAD_DIM
    col0 = NSA_N_HEADS * dh // LANE
    return pl.pallas_call(
        functools.partial(_nsa_compress_kernel, nchunk=nchunk),
        grid=(bsz, 2, g),
        in_specs=[
            pl.BlockSpec((seq_len, dh), lambda b, s, gi: (b, col0 + s * g + gi)),
            pl.BlockSpec((None, NSA_CMP_BLOCK, dh), lambda b, s, gi: (s, 0, 0)),
            pl.BlockSpec((None, NSA_CMP_BLOCK * dh, NSA_CMP_HIDDEN), lambda b, s, gi: (s, 0, 0)),
            pl.BlockSpec((None, NSA_CMP_HIDDEN, dh), lambda b, s, gi: (s, 0, 0)),
        ],
        out_specs=pl.BlockSpec((None, None, None, nchunk, dh), lambda b, s, gi: (b, s, gi, 0, 0)),
        out_shape=jax.ShapeDtypeStruct((bsz, 2, g, nchunk, dh), F32),
        compiler_params=_cparams("parallel", "parallel", "parallel"),
        name="nsa_compress",
    )(proj, cmp_pos, cmp_w1, cmp_w2)


def _nsa_select_kernel(q_ref, kc_ref, vc_ref, cover_ref, ocmp_ref, sel_ref, *, ncmp, nslc):
    tq = NSA_SEL_TQ
    dh = NSA_HEAD_DIM
    scale = dh ** -0.5
    nck = kc_ref.shape[0]
    t0 = pl.program_id(2) * tq
    t_row = t0 + lax.broadcasted_iota(jnp.int32, (1, tq), 1)
    ci_col = lax.broadcasted_iota(jnp.int32, (nck, 1), 0)
    valid_t = (ci_col * NSA_CMP_STRIDE + (NSA_CMP_BLOCK - 1) <= t_row) & (ci_col < ncmp)
    c = scale * math.log2(math.e)
    bias_t = jnp.where(valid_t, 0.0, NEG)
    some_t = jnp.where(t_row >= NSA_CMP_BLOCK - 1, 1.0, 0.0)
    kcb = kc_ref[...].astype(BF16)
    vcb = vc_ref[...].astype(BF16)
    psum_t = jnp.zeros((nck, tq), F32)
    for j in range(NSA_HPG):
        qj = q_ref[:, j * dh:(j + 1) * dh].astype(BF16)
        st = _dot_nt(kcb, qj) * c + bias_t
        et = jnp.exp2(st - jnp.max(st, axis=0, keepdims=True))
        pt = et * (some_t / jnp.sum(et, axis=0, keepdims=True))
        psum_t = psum_t + pt
        ocmp_ref[:, j * dh:(j + 1) * dh] = _dot_tn(pt.astype(BF16), vcb)
    imp_t = _dot_f32_rhs(cover_ref[...], psum_t)
    blk = lax.broadcasted_iota(jnp.int32, (nslc, tq), 0)
    cur = t_row // NSA_SLC_BLOCK
    forced = (blk == 0) | ((blk <= cur) & (blk > cur - 1 - NSA_N_LOCAL))
    future = blk * NSA_SLC_BLOCK > t_row
    score = jnp.where(forced, BIG, jnp.where(future, NEG, imp_t))
    sub = F32_SUBLANES
    slabs = [score[r * sub:(r + 1) * sub, :] for r in range(nslc // sub)]
    ranks = [jnp.zeros((sub, tq), F32) for _ in slabs]
    row_in_slab = lax.broadcasted_iota(jnp.int32, (sub, tq), 0)
    for j in range(nslc):
        cj = jnp.broadcast_to(score[j:j + 1, :], (sub, tq))
        for r, s_r in enumerate(slabs):
            if r * sub > j:
                before = cj >= s_r
            elif (r + 1) * sub - 1 < j:
                before = cj > s_r
            else:
                before = (cj > s_r) | ((cj == s_r) & (row_in_slab > j - r * sub))
            ranks[r] = ranks[r] + jnp.where(before, 1.0, 0.0)
    topk = float(min(NSA_TOPK, nslc))
    for r, rank in enumerate(ranks):
        sel_ref[r * sub:(r + 1) * sub, :] = jnp.where(rank < topk, 1.0, 0.0)


def _nsa_select(proj, kvc, bsz, seq_len):
    tq, dh, g = NSA_SEL_TQ, NSA_HEAD_DIM, NSA_KV_GROUPS
    nq = seq_len // tq
    ncmp = (seq_len - NSA_CMP_BLOCK) // NSA_CMP_STRIDE + 1
    nck = seq_len // NSA_CMP_STRIDE
    nslc = seq_len // NSA_SLC_BLOCK
    ci = np.arange(nck)[:, None] * NSA_CMP_STRIDE
    sj = np.arange(nslc)[None, :] * NSA_SLC_BLOCK
    cover = ((ci <= sj + NSA_SLC_BLOCK - 1) & (ci + NSA_CMP_BLOCK - 1 >= sj)).astype(np.float32)
    cover = jnp.asarray(cover.T, dtype=BF16)
    return pl.pallas_call(
        functools.partial(_nsa_select_kernel, ncmp=ncmp, nslc=nslc),
        grid=(bsz, g, nq),
        in_specs=[
            pl.BlockSpec((tq, NSA_HPG * dh), lambda b, gi, qi: (b * nq + qi, gi)),
            pl.BlockSpec((None, None, None, nck, dh), lambda b, gi, qi: (b, 0, gi, 0, 0)),
            pl.BlockSpec((None, None, None, nck, dh), lambda b, gi, qi: (b, 1, gi, 0, 0)),
            pl.BlockSpec((nslc, nck), lambda b, gi, qi: (0, 0)),
        ],
        out_specs=[
            pl.BlockSpec((tq, NSA_HPG * dh), lambda b, gi, qi: (b * nq + qi, gi)),
            pl.BlockSpec((None, None, nslc, tq), lambda b, gi, qi: (b, gi, 0, qi)),
        ],
        out_shape=[
            jax.ShapeDtypeStruct((bsz * seq_len, NSA_N_HEADS * dh), F32),
            jax.ShapeDtypeStruct((bsz, g, nslc, seq_len), F32),
        ],
        compiler_params=_cparams("parallel", "parallel", "parallel"),
        name="nsa_cmp_select",
    )(proj, kvc, kvc, cover)


def _nsa_attn_kernel(q_ref, ks_ref, vs_ref, kw_ref, vw_ref, sel_ref, expand_ref, ocmp_ref, gl_ref,
                     o_ref, ksb_ref, vsx_ref, kwb_ref, vwx_ref, selb_ref, q4_ref, pw_ref, m_ref, acc_ref):
    tq = NSA_TQ
    dh = NSA_HEAD_DIM
    kt_w = NSA_KEY_TILE
    win_w = NSA_WINDOW + tq
    c = (dh ** -0.5) * math.log2(math.e)
    qi = pl.program_id(2)

    @pl.when(qi == 0)
    def _():
        ones = jnp.ones((ks_ref.shape[0], dh), BF16)
        for blk in range(ks_ref.shape[0] // LANE):
            cols = slice(blk * LANE, (blk + 1) * LANE)
            ksb_ref[:, cols] = ks_ref[cols, :].T.astype(BF16)
            kwb_ref[:, cols] = kw_ref[cols, :].T.astype(BF16)
        vsx_ref[:, 0:dh] = vs_ref[...].astype(BF16)
        vsx_ref[:, dh:] = ones
        vwx_ref[:, 0:dh] = vw_ref[...].astype(BF16)
        vwx_ref[:, dh:] = ones

    selb_ref[...] = sel_ref[...].T.astype(BF16)
    for j in range(NSA_HPG):
        q4_ref[j * tq:(j + 1) * tq, :] = (q_ref[:, j * dh:(j + 1) * dh] * c).astype(BF16)
    t1 = qi * tq + lax.broadcasted_iota(jnp.int32, (tq, 1), 0)
    m_ref[...] = jnp.full(m_ref.shape, NEG, F32)
    acc_ref[...] = jnp.zeros(acc_ref.shape, F32)
    half = NSA_HPG // 2 * tq

    def slc_tile(off, width, on_diagonal):
        chosen = _dot(selb_ref[...], expand_ref[:, pl.ds(off, width)])
        bias = (chosen - 1.0) * BIG
        if on_diagonal:
            pos = off + lax.broadcasted_iota(jnp.int32, (1, width), 1)
            bias = jnp.where(pos <= t1, bias, NEG)
        k_t = ksb_ref[:, pl.ds(off, width)]
        v_x = vsx_ref[pl.ds(off, width), :]
        pairs = [slice(hp * half, (hp + 1) * half) for hp in range(2)]
        scores = [_dot(q4_ref[pr], k_t) for pr in pairs]
        for hp, pr in enumerate(pairs):
            m_prev = m_ref[pr]
            ps, m_news = [], []
            for e in range(NSA_HPG // 2):
                sj = scores[hp][e * tq:(e + 1) * tq] + bias
                m_new = jnp.maximum(m_prev[e * tq:(e + 1) * tq], jnp.max(sj, axis=-1, keepdims=True))
                m_wide = jnp.concatenate([m_new] * (width // LANE), axis=1)
                ps.append(jnp.exp2(sj - m_wide).astype(BF16))
                m_news.append(m_new)
            m_new = jnp.concatenate(m_news, axis=0)
            alpha = jnp.exp2(m_prev - m_new)
            m_ref[pr] = m_new
            pv = _dot(jnp.concatenate(ps, axis=0), v_x)
            acc_ref[pr] = jnp.concatenate([alpha, alpha], axis=1) * acc_ref[pr] + pv

    n_wide = (qi * tq) // kt_w

    def wide_step(kt, carry):
        slc_tile(pl.multiple_of(kt * kt_w, kt_w), kt_w, False)
        return carry

    def narrow_step(s, carry):
        slc_tile(pl.multiple_of(n_wide * kt_w + s * tq, tq), tq, True)
        return carry

    lax.fori_loop(0, n_wide, wide_step, 0)
    lax.fori_loop(0, qi + 1 - n_wide * (kt_w // tq), narrow_step, 0)

    w0 = pl.multiple_of(jnp.maximum(qi * tq - NSA_WINDOW, 0), tq)
    sw = _dot(q4_ref[...], kwb_ref[:, pl.ds(w0, win_w)])
    diff = t1 - (w0 + lax.broadcasted_iota(jnp.int32, (1, win_w), 1))
    bias_w = jnp.where((diff >= 0) & (diff < NSA_WINDOW), 0.0, NEG)
    for j in range(NSA_HPG):
        rows = slice(j * tq, (j + 1) * tq)
        sj = sw[rows] + bias_w
        pw_ref[rows] = jnp.exp2(sj - jnp.max(sj, axis=-1, keepdims=True)).astype(BF16)
    ow = _dot(pw_ref[...], vwx_ref[pl.ds(w0, win_w), :])

    gates = _sigmoid(gl_ref[...])
    for j in range(NSA_HPG):
        rows = slice(j * tq, (j + 1) * tq)
        o_slc = acc_ref[rows, 0:dh] / acc_ref[rows, dh:]
        o_win = ow[rows, 0:dh] / ow[rows, dh:]
        o = (gates[:, 3 * j:3 * j + 1] * ocmp_ref[:, j * dh:(j + 1) * dh]
             + gates[:, 3 * j + 1:3 * j + 2] * o_slc
             + gates[:, 3 * j + 2:3 * j + 3] * o_win)
        o_ref[:, j * dh:(j + 1) * dh] = o.astype(o_ref.dtype)


def _nsa_attn(proj, sel, ocmp, gate_logits, bsz, seq_len):
    tq, dh, g = NSA_TQ, NSA_HEAD_DIM, NSA_KV_GROUPS
    assert seq_len % NSA_KEY_TILE == 0 and seq_len >= NSA_WINDOW + tq
    rows = NSA_HPG * tq
    nq = seq_len // tq
    nslc = seq_len // NSA_SLC_BLOCK
    expand = (np.arange(nslc)[:, None] == np.arange(seq_len)[None, :] // NSA_SLC_BLOCK)
    expand = jnp.asarray(expand.astype(np.float32), dtype=BF16)
    kv0 = (NSA_N_HEADS * dh + 2 * NSA_KV_DIM) // dh

    def kv_spec(idx):
        return pl.BlockSpec((seq_len, dh), lambda b, gi, qi: (b, kv0 + idx * g + gi))

    qo_spec = pl.BlockSpec((tq, NSA_HPG * dh), lambda b, gi, qi: (b * nq + qi, gi))
    return pl.pallas_call(
        _nsa_attn_kernel,
        grid=(bsz, g, nq),
        in_specs=[
            qo_spec,
            kv_spec(0), kv_spec(1), kv_spec(2), kv_spec(3),
            pl.BlockSpec((None, None, nslc, tq), lambda b, gi, qi: (b, gi, 0, qi)),
            pl.BlockSpec((nslc, seq_len), lambda b, gi, qi: (0, 0)),
            qo_spec,
            pl.BlockSpec((None, tq, 3 * NSA_HPG), lambda b, gi, qi: (gi, b * nq + qi, 0)),
        ],
        out_specs=qo_spec,
        out_shape=jax.ShapeDtypeStruct((bsz * seq_len, NSA_N_HEADS * dh), BF16),
        scratch_shapes=[
            pltpu.VMEM((dh, seq_len), BF16),
            pltpu.VMEM((seq_len, 2 * dh), BF16),
            pltpu.VMEM((dh, seq_len), BF16),
            pltpu.VMEM((seq_len, 2 * dh), BF16),
            pltpu.VMEM((tq, nslc), BF16),
            pltpu.VMEM((rows, dh), BF16),
            pltpu.VMEM((rows, NSA_WINDOW + tq), BF16),
            pltpu.VMEM((rows, LANE), F32),
            pltpu.VMEM((rows, 2 * dh), F32),
        ],
        compiler_params=_cparams("parallel", "parallel", "arbitrary"),
        name="nsa_slc_win_attention",
    )(proj, proj, proj, proj, proj, sel, expand, ocmp, gate_logits)


def _sgu_kernel(u_ref, v_ref, g_ref, ws_ref, bst_ref, o_ref):
    v = v_ref[...].astype(F32)
    mu = jnp.mean(v, axis=-1, keepdims=True)
    vc = v - mu
    vn = (vc * lax.rsqrt(jnp.mean(vc * vc, axis=-1, keepdims=True) + RMS_EPS) * g_ref[...]).astype(BF16)
    q = SGU_CHUNK
    tri = lax.broadcasted_iota(jnp.int32, (q, q), 0) >= lax.broadcasted_iota(jnp.int32, (q, q), 1)
    for g in range(SGU_GROUPS):
        cols = slice(g * SGU_GROUP_DIM, (g + 1) * SGU_GROUP_DIM)
        w_m = jnp.where(tri, ws_ref[g], 0.0).astype(BF16)
        sv = _dot(w_m, vn[:, cols]) + bst_ref[:, g:g + 1]
        o_ref[:, cols] = (u_ref[:, cols].astype(F32) * sv).astype(o_ref.dtype)


def _sgu_core(proj, ln_g, w_spatial, b_spatial):
    m = proj.shape[0]
    q = SGU_CHUNK
    return pl.pallas_call(
        _sgu_kernel,
        grid=(m // q,),
        in_specs=[
            pl.BlockSpec((q, SGU_WIDTH), lambda i: (i, 0)),
            pl.BlockSpec((q, SGU_WIDTH), lambda i: (i, 1)),
            pl.BlockSpec((1, SGU_WIDTH), lambda i: (0, 0)),
            pl.BlockSpec((SGU_GROUPS, q, q), lambda i: (0, 0, 0)),
            pl.BlockSpec((q, SGU_GROUPS), lambda i: (0, 0)),
        ],
        out_specs=pl.BlockSpec((q, SGU_WIDTH), lambda i: (i, 0)),
        out_shape=jax.ShapeDtypeStruct((m, SGU_WIDTH), BF16),
        compiler_params=_cparams("parallel"),
        name="sgu_core",
    )(proj, proj, ln_g.reshape(1, SGU_WIDTH), w_spatial, b_spatial.T)


def _pool_kernel(z_ref, zh_ref, wg_ref, sc_ref, o_ref, zz_ref, *, tm, blocks_per_seq):
    i = pl.program_id(0)
    keep = (i % blocks_per_seq != 0).astype(F32)
    zz_ref[0:POOL_HALO, :] = zh_ref[...] * keep
    zz_ref[POOL_HALO:, :] = z_ref[...]
    t = (i % blocks_per_seq) * tm + lax.broadcasted_iota(jnp.int32, (tm, 1), 0)
    for gi, win in enumerate(POOL_WINDOWS):
        cols = slice(gi * POOL_GROUP_DIM, (gi + 1) * POOL_GROUP_DIM)
        s = zz_ref[POOL_HALO:POOL_HALO + tm, cols]
        for k in range(1, win):
            s = s + zz_ref[POOL_HALO - k:POOL_HALO - k + tm, cols]
        count = jnp.minimum(t + 1, win).astype(F32)
        pooled = s / count - z_ref[:, cols]
        y = _dot(pooled.astype(BF16), wg_ref[gi]) * sc_ref[:, cols]
        o_ref[:, cols] = y.astype(o_ref.dtype)


def _pool_core(z, w_group, scale, seq_len, *, tm=512):
    m, n = z.shape
    tm = min(tm, seq_len)
    halo_blocks = tm // POOL_HALO
    return pl.pallas_call(
        functools.partial(_pool_kernel, tm=tm, blocks_per_seq=seq_len // tm),
        grid=(m // tm,),
        in_specs=[
            pl.BlockSpec((tm, n), lambda i: (i, 0)),
            pl.BlockSpec((POOL_HALO, n), lambda i: (jnp.maximum(i * halo_blocks - 1, 0), 0)),
            pl.BlockSpec(w_group.shape, lambda i: (0, 0, 0)),
            pl.BlockSpec((1, n), lambda i: (0, 0)),
        ],
        out_specs=pl.BlockSpec((tm, n), lambda i: (i, 0)),
        out_shape=jax.ShapeDtypeStruct((m, n), BF16),
        scratch_shapes=[pltpu.VMEM((tm + POOL_HALO, n), F32)],
        compiler_params=_cparams("parallel"),
        name="pool_core",
    )(z, z, w_group, scale.reshape(1, n))


def _pad_cols(w, n):
    return jnp.pad(w, ((0, 0), (0, 0), (0, n - w.shape[2])))


def _ssd_layer(x, g_pre, g_post, w_in, j, conv_w, conv_b, dt_bias, a_log, d_skip, norm_g, w_out, bsz, seq_len):
    proj, _ = _ssd_in_proj(x, g_pre, w_in, j, conv_w, conv_b, seq_len)
    y = _ssd_core(proj, dt_bias, a_log, d_skip, norm_g, bsz, seq_len)
    return _matmul_post(y, w_out, j, g_post, x)


def _first_ssd_layer(x, g_pre, g_post, w_in, conv_w, conv_b, dt_bias, a_log, d_skip, norm_g, riders,
                     bsz, seq_len):
    proj, cast = _ssd_in_proj(x, g_pre, w_in, 0, conv_w, conv_b, seq_len, tuple(riders.values()))
    wb = dict(zip(riders, cast))
    y = _ssd_core(proj, dt_bias, a_log, d_skip, norm_g, bsz, seq_len)
    return _matmul_post(y, wb["ssd_out"], 0, g_post, x), wb


def _nsa_layer(x, g_pre, g_post, w_in, j, cmp_pos, cmp_w1, cmp_w2, w_out, bsz, seq_len):
    proj = _norm_matmul(x, g_pre, w_in, j, jnp.zeros((NSA_IN_PAD,), F32))
    kvc = _nsa_compress(proj, cmp_pos, cmp_w1, cmp_w2, bsz, seq_len)
    ocmp, sel = _nsa_select(proj, kvc, bsz, seq_len)
    ngate = 3 * NSA_N_HEADS
    gl = proj[:, NSA_GATE_COL:NSA_GATE_COL + ngate].reshape(-1, NSA_KV_GROUPS, 3 * NSA_HPG).transpose(1, 0, 2)
    o = _nsa_attn(proj, sel, ocmp, gl, bsz, seq_len)
    return _matmul_post(o, w_out, j, g_post, x)


def _sgu_layer(x, g_pre, g_post, w_in, j, b_in, ln_g, w_spatial, b_spatial, w_out):
    proj = _norm_matmul(x, g_pre, w_in, j, b_in, act="gelu", out_dtype=BF16)
    y = _sgu_core(proj, ln_g, w_spatial, b_spatial)
    return _matmul_post(y, w_out, j, g_post, x)


def _pool_layer(x, g_pre, g_post, w_in, j, w_group, scale, w_out, seq_len):
    z = _norm_matmul(x, g_pre, w_in, j, jnp.zeros((D_MODEL,), F32))
    y = _pool_core(z, w_group.astype(BF16), scale, seq_len)
    return _matmul_post(y, w_out, j, g_post, x)


def kernel(x, mem, norm_pre, norm_post, norm_mem, ssd_w_in, ssd_conv_w, ssd_conv_b, ssd_dt_bias, ssd_a_log, ssd_d, ssd_norm_g, ssd_w_out, nsa_w_in, nsa_cmp_pos, nsa_cmp_w1, nsa_cmp_w2, nsa_w_out, sgu_w_in, sgu_b_in, sgu_ln_g, sgu_w_spatial, sgu_b_spatial, sgu_w_out, pool_w_in, pool_w_group, pool_scale, pool_w_out, xa_w_q, xa_w_kv, xa_w_o, ffn_w_up, ffn_conv_w, ffn_conv_b, ffn_w_down):
    bsz, seq_len, d = x.shape
    depth = norm_pre.shape[0]
    xf = x.reshape(bsz * seq_len, d)
    memf = mem.reshape(bsz * mem.shape[1], d)
    ssd_in = _pad_cols(ssd_w_in, SSD_IN_PAD).astype(BF16)
    nsa_in = _pad_cols(nsa_w_in, NSA_IN_PAD).astype(BF16)
    whole = lambda w: (w, 0, w.shape[0])
    riders0 = dict(ssd_out=whole(ssd_w_out), nsa_out=whole(nsa_w_out), sgu_in=whole(sgu_w_in),
                   sgu_out=whole(sgu_w_out), pool_in=whole(pool_w_in), pool_out=whole(pool_w_out),
                   xa_q=whole(xa_w_q), xa_kv=whole(xa_w_kv), xa_o=whole(xa_w_o),
                   ffn_up=(ffn_w_up, 0, 1), ffn_down=(ffn_w_down, 0, 1))
    ffn_riders = ((ffn_w_up, 1, depth - 1), (ffn_w_down, 1, depth - 1)) if depth > 1 else ()
    wb = None
    for i in range(depth):
        kind, j = i % 4, i // 4
        if i == 0:
            xf, wb = _first_ssd_layer(xf, norm_pre[i, 0], norm_post[i, 0], ssd_in, ssd_conv_w[j], ssd_conv_b[j],
                                      ssd_dt_bias[j], ssd_a_log[j], ssd_d[j], ssd_norm_g[j], riders0,
                                      bsz, seq_len)
        elif kind == 0:
            xf = _ssd_layer(xf, norm_pre[i, 0], norm_post[i, 0], ssd_in, j, ssd_conv_w[j], ssd_conv_b[j],
                            ssd_dt_bias[j], ssd_a_log[j], ssd_d[j], ssd_norm_g[j], wb["ssd_out"], bsz, seq_len)
        elif kind == 1:
            xf = _nsa_layer(xf, norm_pre[i, 0], norm_post[i, 0], nsa_in, j, nsa_cmp_pos[j], nsa_cmp_w1[j],
                            nsa_cmp_w2[j], wb["nsa_out"], bsz, seq_len)
        elif kind == 2:
            xf = _sgu_layer(xf, norm_pre[i, 0], norm_post[i, 0], wb["sgu_in"], j, sgu_b_in[j], sgu_ln_g[j],
                            sgu_w_spatial[j], sgu_b_spatial[j], wb["sgu_out"])
        else:
            xf = _pool_layer(xf, norm_pre[i, 0], norm_post[i, 0], wb["pool_in"], j, pool_w_group[j],
                             pool_scale[j], wb["pool_out"], seq_len)
        kv = _norm_matmul(memf, norm_mem[i], wb["xa_kv"], i, jnp.zeros((2 * XA_DIM,), F32),
                          out_dtype=BF16, tm=MEM_LEN)
        xf = _xattn(xf, kv, norm_pre[i, 1], norm_post[i, 1], wb["xa_q"], wb["xa_o"], i, seq_len)
        if i == 0:
            act, rest = _ffn_up(xf, norm_pre[i, 2], wb["ffn_up"], 0, ffn_conv_w[i], ffn_conv_b[i], seq_len,
                                ffn_riders)
            ffn_up_w, ffn_down_w, first = wb["ffn_up"], wb["ffn_down"], 0
        else:
            (ffn_up_w, ffn_down_w), first = rest, 1
            act, _ = _ffn_up(xf, norm_pre[i, 2], ffn_up_w, i - first, ffn_conv_w[i], ffn_conv_b[i], seq_len)
        xf = _matmul_post(act, ffn_down_w, i - first, norm_post[i, 2], xf)
    return xf.reshape(bsz, seq_len, d)
```

```python
import functools
import math

import jax
import jax.numpy as jnp
import numpy as np
from jax import lax
from jax.experimental import pallas as pl
from jax.experimental.pallas import tpu as pltpu

F32 = jnp.float32
BF16 = jnp.bfloat16

D_MODEL = 2048
RMS_EPS = 1e-6
NEG = -1e30
BIG = 1e30
MEM_LEN = 256

SSD_D_INNER = 4096
SSD_HEAD_DIM = 64
SSD_N_HEADS = 64
SSD_N_GROUPS = 8
SSD_HPG = 8
SSD_D_STATE = 128
SSD_CONV = 4
SSD_CHUNK = 128
SSD_GROUP_W = SSD_HPG * SSD_HEAD_DIM
SSD_BC_DIM = SSD_N_GROUPS * SSD_D_STATE
SSD_IN_PAD = 10752

NSA_HEAD_DIM = 128
NSA_N_HEADS = 16
NSA_KV_GROUPS = 4
NSA_HPG = 4
NSA_KV_DIM = 512
NSA_CMP_BLOCK = 32
NSA_CMP_STRIDE = 16
NSA_CMP_HIDDEN = 256
NSA_SLC_BLOCK = 64
NSA_TOPK = 16
NSA_N_LOCAL = 2
NSA_WINDOW = 512
NSA_TQ = 256
NSA_KEY_TILE = 1024
NSA_SEL_TQ = 512
NSA_IN_PAD = 5632
NSA_GATE_COL = 5120

SGU_CHUNK = 128
SGU_WIDTH = 4096
SGU_GROUPS = 8
SGU_GROUP_DIM = 512

POOL_WINDOWS = (2, 4, 8, 16)
POOL_GROUP_DIM = 512
POOL_HALO = max(POOL_WINDOWS)

XA_HEADS = 4
XA_HEAD_DIM = 128
XA_DIM = 512

FFN_HIDDEN = 5632
FFN_CONV = 3

LANE = 128
F32_SUBLANES = 8
BF16_SUBLANES = 16
SSD_HALO = BF16_SUBLANES
FFN_HALO = BF16_SUBLANES
VMEM_LIMIT = 56 * 1024 * 1024


def _cparams(*sem):
    return pltpu.CompilerParams(dimension_semantics=sem, vmem_limit_bytes=VMEM_LIMIT)


def _rider_plan(riders, steps, nj):
    flats, in_specs, out_specs, out_shapes = [], [], [], []
    for w, first, count in riders:
        k, n = w.shape[1], w.shape[2]
        start, rows = first * k, count * k
        r = BF16_SUBLANES
        while rows % r or start % r or rows // r > steps:
            r += BF16_SUBLANES
            assert r <= rows, (w.shape, first, count, steps)
        base, last = start // r, rows // r - 1
        flats.append(w.reshape(-1, n))
        in_specs.append(pl.BlockSpec(
            (r, n), lambda i, j, base=base, last=last: (base + jnp.minimum(i * nj + j, last), 0)))
        out_specs.append(pl.BlockSpec((r, n), lambda i, j, last=last: (jnp.minimum(i * nj + j, last), 0)))
        out_shapes.append(jax.ShapeDtypeStruct((rows, n), BF16))
    return flats, in_specs, out_specs, out_shapes


def _cast_riders(rider_in, rider_out):
    for src, dst in zip(rider_in, rider_out):
        dst[...] = src[...].astype(dst.dtype)


def _rider_results(outs, riders):
    return [o.reshape(count, w.shape[1], w.shape[2]) for o, (w, _, count) in zip(outs, riders)]


def _dot(a, b):
    return jnp.dot(a, b, preferred_element_type=F32)


def _dot_nt(a, b):
    return lax.dot_general(a, b, (((1,), (1,)), ((), ())), preferred_element_type=F32)


def _dot_tn(a, b):
    return lax.dot_general(a, b, (((0,), (0,)), ((), ())), preferred_element_type=F32)


def _split3(v):
    hi = v.astype(BF16)
    r1 = v - hi.astype(F32)
    mid = r1.astype(BF16)
    lo = (r1 - mid.astype(F32)).astype(BF16)
    return hi, mid, lo


def _dot_2piece_lhs(v, e2):
    hi = v.astype(BF16)
    mid = (v - hi.astype(F32)).astype(BF16)
    return _dot(jnp.concatenate([hi, mid], axis=1), e2)


def _dot_f32_rhs(e, v):
    hi, mid, lo = _split3(v)
    return _dot(e, hi) + _dot(e, mid) + _dot(e, lo)


def _rms(x, g):
    ms = jnp.mean(x * x, axis=-1, keepdims=True)
    return x * lax.rsqrt(ms + RMS_EPS) * g


def _sigmoid(x):
    return 1.0 / (1.0 + jnp.exp(-x))


def _silu(x):
    h = 0.5 * x
    return h + h * jnp.tanh(h)


def _gelu(x):
    c = math.sqrt(2.0 / math.pi)
    return x * (0.5 + 0.5 * jnp.tanh(x * (c + (c * 0.044715) * (x * x))))


def _softplus(x):
    return jnp.maximum(x, 0.0) + jnp.log1p(jnp.exp(-jnp.abs(x)))


def _norm_mm_kernel(x_ref, g_ref, w_ref, b_ref, o_ref, h_ref, *, act):
    @pl.when(pl.program_id(1) == 0)
    def _():
        h_ref[...] = _rms(x_ref[...], g_ref[...]).astype(BF16)

    y = _dot(h_ref[...], w_ref[...]) + b_ref[...]
    if act == "gelu":
        y = _gelu(y)
    o_ref[...] = y.astype(o_ref.dtype)


def _norm_matmul(x, g, w, li, b, *, act=None, out_dtype=F32, tm=1024, tn=512):
    m, k = x.shape
    n = w.shape[2]
    tm = min(tm, m)
    assert m % tm == 0 and n % tn == 0, (m, n, tm, tn)
    return pl.pallas_call(
        functools.partial(_norm_mm_kernel, act=act),
        grid=(m // tm, n // tn),
        in_specs=[
            pl.BlockSpec((tm, k), lambda i, j: (i, 0)),
            pl.BlockSpec((1, k), lambda i, j: (0, 0)),
            pl.BlockSpec((None, k, tn), lambda i, j: (li, 0, j)),
            pl.BlockSpec((1, tn), lambda i, j: (0, j)),
        ],
        out_specs=pl.BlockSpec((tm, tn), lambda i, j: (i, j)),
        out_shape=jax.ShapeDtypeStruct((m, n), out_dtype),
        scratch_shapes=[pltpu.VMEM((tm, k), BF16)],
        compiler_params=_cparams("parallel", "arbitrary"),
        name="norm_matmul",
    )(x, g.reshape(1, k), w, b.reshape(1, n))


def _mm_post_kernel(a_ref, w_ref, g_ref, r_ref, o_ref, *, nj, tn):
    j = pl.program_id(1)
    o_ref[:, pl.ds(pl.multiple_of(j * tn, tn), tn)] = _dot(a_ref[...], w_ref[...])

    @pl.when(j == nj - 1)
    def _():
        o_ref[...] = r_ref[...] + _rms(o_ref[...], g_ref[...])


def _matmul_post(a, w, li, g, res, *, tm=512, tn=512):
    m, kdim = a.shape
    n = w.shape[2]
    tm = min(tm, m)
    assert m % tm == 0 and n % tn == 0
    nj = n // tn
    return pl.pallas_call(
        functools.partial(_mm_post_kernel, nj=nj, tn=tn),
        grid=(m // tm, nj),
        in_specs=[
            pl.BlockSpec((tm, kdim), lambda i, j: (i, 0)),
            pl.BlockSpec((None, kdim, tn), lambda i, j: (li, 0, j)),
            pl.BlockSpec((1, n), lambda i, j: (0, 0)),
            pl.BlockSpec((tm, n), lambda i, j: (i, 0)),
        ],
        out_specs=pl.BlockSpec((tm, n), lambda i, j: (i, 0)),
        out_shape=jax.ShapeDtypeStruct((m, n), F32),
        compiler_params=_cparams("parallel", "arbitrary"),
        name="matmul_postnorm_residual",
    )(a, w, g.reshape(1, n), res)


def _ffn_up_kernel(x_ref, xh_ref, g_ref, wg_ref, wv_ref, cwg_ref, cwv_ref, cbg_ref, cbv_ref,
                   *rest, blocks_per_seq, n_riders):
    rider_in = rest[:n_riders]
    o_ref = rest[n_riders]
    rider_out = rest[n_riders + 1:2 * n_riders + 1]
    h_ref = rest[2 * n_riders + 1]
    i = pl.program_id(0)

    @pl.when(pl.program_id(1) == 0)
    def _():
        keep = (i % blocks_per_seq != 0).astype(F32)
        h_ref[0:FFN_HALO, :] = (_rms(xh_ref[...], g_ref[...]) * keep).astype(BF16)
        h_ref[FFN_HALO:, :] = _rms(x_ref[...], g_ref[...]).astype(BF16)

    _cast_riders(rider_in, rider_out)
    h = h_ref[...]

    def conv(u, cw_ref, cb_ref):
        y = cb_ref[...] + u[FFN_HALO:] * cw_ref[FFN_CONV - 1:FFN_CONV, :]
        for tap in range(FFN_CONV - 1):
            back = FFN_CONV - 1 - tap
            y = y + pltpu.roll(u, back, axis=0)[FFN_HALO:] * cw_ref[tap:tap + 1, :]
        return y

    gate = conv(_dot(h, wg_ref[...]), cwg_ref, cbg_ref)
    val = conv(_dot(h, wv_ref[...]), cwv_ref, cbv_ref)
    o_ref[...] = (_silu(gate) * val).astype(o_ref.dtype)


def _ffn_up(x, g, w_up, li, conv_w, conv_b, seq_len, riders=(), *, tm=1024, tn=512):
    m, k = x.shape
    tm = min(tm, seq_len)
    assert m % tm == 0 and seq_len % tm == 0 and FFN_HIDDEN % tn == 0
    nj = FFN_HIDDEN // tn
    halo_blocks = tm // FFN_HALO
    flat, rider_in_specs, rider_out_specs, rider_shapes = _rider_plan(riders, (m // tm) * nj, nj)
    outs = pl.pallas_call(
        functools.partial(_ffn_up_kernel, blocks_per_seq=seq_len // tm, n_riders=len(flat)),
        grid=(m // tm, nj),
        in_specs=[
            pl.BlockSpec((tm, k), lambda i, j: (i, 0)),
            pl.BlockSpec((FFN_HALO, k), lambda i, j: (jnp.maximum(i * halo_blocks - 1, 0), 0)),
            pl.BlockSpec((1, k), lambda i, j: (0, 0)),
            pl.BlockSpec((None, k, tn), lambda i, j: (li, 0, j)),
            pl.BlockSpec((None, k, tn), lambda i, j: (li, 0, j + nj)),
            pl.BlockSpec((FFN_CONV, tn), lambda i, j: (0, j)),
            pl.BlockSpec((FFN_CONV, tn), lambda i, j: (0, j + nj)),
            pl.BlockSpec((1, tn), lambda i, j: (0, j)),
            pl.BlockSpec((1, tn), lambda i, j: (0, j + nj)),
        ] + rider_in_specs,
        out_specs=[pl.BlockSpec((tm, tn), lambda i, j: (i, j))] + rider_out_specs,
        out_shape=[jax.ShapeDtypeStruct((m, FFN_HIDDEN), BF16)] + rider_shapes,
        scratch_shapes=[pltpu.VMEM((tm + FFN_HALO, k), BF16)],
        compiler_params=_cparams("arbitrary", "arbitrary"),
        name="ffn_up_conv_gate",
    )(x, x, g.reshape(1, k), w_up, w_up, conv_w, conv_w,
      conv_b.reshape(1, -1), conv_b.reshape(1, -1), *flat)
    return outs[0], _rider_results(outs[1:], riders)


def _xattn_kernel(x_ref, kv_ref, gpre_ref, gpost_ref, wq_ref, wo_ref, o_ref, att_ref):
    x = x_ref[...]
    h = _rms(x, gpre_ref[...]).astype(BF16)
    c = (XA_HEAD_DIM ** -0.5) * math.log2(math.e)
    q = (_dot(h, wq_ref[...]) * c).astype(BF16)
    for hd in range(XA_HEADS):
        lo = hd * XA_HEAD_DIM
        s = _dot_nt(q[:, lo:lo + XA_HEAD_DIM], kv_ref[:, lo:lo + XA_HEAD_DIM])
        e = jnp.exp2(s - jnp.max(s, axis=-1, keepdims=True))
        pv = _dot(e.astype(BF16), kv_ref[:, XA_DIM + lo:XA_DIM + lo + XA_HEAD_DIM])
        att_ref[:, lo:lo + XA_HEAD_DIM] = (pv / jnp.sum(e, axis=-1, keepdims=True)).astype(BF16)
    a = _dot(att_ref[...], wo_ref[...])
    o_ref[...] = x + _rms(a, gpost_ref[...])


def _xattn(x, kv, g_pre, g_post, w_q, w_o, li, seq_len, *, tm=512):
    m, k = x.shape
    tm = min(tm, seq_len)
    bps = seq_len // tm
    return pl.pallas_call(
        _xattn_kernel,
        grid=(m // tm,),
        in_specs=[
            pl.BlockSpec((tm, k), lambda i: (i, 0)),
            pl.BlockSpec((MEM_LEN, 2 * XA_DIM), lambda i: (i // bps, 0)),
            pl.BlockSpec((1, k), lambda i: (0, 0)),
            pl.BlockSpec((1, k), lambda i: (0, 0)),
            pl.BlockSpec((None, k, XA_DIM), lambda i: (li, 0, 0)),
            pl.BlockSpec((None, XA_DIM, k), lambda i: (li, 0, 0)),
        ],
        out_specs=pl.BlockSpec((tm, k), lambda i: (i, 0)),
        out_shape=jax.ShapeDtypeStruct((m, k), F32),
        scratch_shapes=[pltpu.VMEM((tm, XA_DIM), BF16)],
        compiler_params=_cparams("parallel"),
        name="memory_cross_attention",
    )(x, kv, g_pre.reshape(1, k), g_post.reshape(1, k), w_q, w_o)


def _ssd_in_kernel(x_ref, xh_ref, g_ref, w_ref, cw_ref, cb_ref, *rest, blocks_per_seq, tn, n_riders):
    rider_in = rest[:n_riders]
    o_ref = rest[n_riders]
    rider_out = rest[n_riders + 1:2 * n_riders + 1]
    h_ref = rest[2 * n_riders + 1]
    i = pl.program_id(0)
    j = pl.program_id(1)
    z_tiles = SSD_D_INNER // tn
    conv_tiles = (SSD_D_INNER + 2 * SSD_BC_DIM) // tn
    cast_riders = functools.partial(_cast_riders, rider_in, rider_out)

    @pl.when(j == 0)
    def _():
        keep = (i % blocks_per_seq != 0).astype(F32)
        h_ref[0:SSD_HALO, :] = (_rms(xh_ref[...], g_ref[...]) * keep).astype(BF16)
        h_ref[SSD_HALO:, :] = _rms(x_ref[...], g_ref[...]).astype(BF16)

    @pl.when(j < z_tiles)
    def _():
        cast_riders()
        o_ref[...] = _silu(_dot(h_ref[SSD_HALO:, :], w_ref[...]))

    @pl.when((j >= z_tiles) & (j < z_tiles + conv_tiles))
    def _():
        cast_riders()
        u = _dot(h_ref[...], w_ref[...])
        y = cb_ref[...] + u[SSD_HALO:] * cw_ref[SSD_CONV - 1:SSD_CONV, :]
        for tap in range(SSD_CONV - 1):
            back = SSD_CONV - 1 - tap
            y = y + pltpu.roll(u, back, axis=0)[SSD_HALO:] * cw_ref[tap:tap + 1, :]
        o_ref[...] = _silu(y)

    @pl.when(j >= z_tiles + conv_tiles)
    def _():
        cast_riders()
        o_ref[...] = _dot(h_ref[SSD_HALO:, :], w_ref[...])


def _ssd_in_proj(x, g, w_in, li, conv_w, conv_b, seq_len, riders=(), *, tm=1024, tn=512):
    m, k = x.shape
    n = w_in.shape[2]
    tm = min(tm, seq_len)
    assert m % tm == 0 and seq_len % tm == 0 and n % tn == 0
    z_tiles = SSD_D_INNER // tn
    conv_tiles = (SSD_D_INNER + 2 * SSD_BC_DIM) // tn
    halo_blocks = tm // SSD_HALO
    conv_tile = lambda j: jnp.clip(j - z_tiles, 0, conv_tiles - 1)
    nj = n // tn
    flat, rider_in_specs, rider_out_specs, rider_shapes = _rider_plan(riders, (m // tm) * nj, nj)
    outs = pl.pallas_call(
        functools.partial(_ssd_in_kernel, blocks_per_seq=seq_len // tm, tn=tn, n_riders=len(flat)),
        grid=(m // tm, nj),
        in_specs=[
            pl.BlockSpec((tm, k), lambda i, j: (i, 0)),
            pl.BlockSpec((SSD_HALO, k), lambda i, j: (jnp.maximum(i * halo_blocks - 1, 0), 0)),
            pl.BlockSpec((1, k), lambda i, j: (0, 0)),
            pl.BlockSpec((None, k, tn), lambda i, j: (li, 0, j)),
            pl.BlockSpec((SSD_CONV, tn), lambda i, j: (0, conv_tile(j))),
            pl.BlockSpec((1, tn), lambda i, j: (0, conv_tile(j))),
        ] + rider_in_specs,
        out_specs=[pl.BlockSpec((tm, tn), lambda i, j: (i, j))] + rider_out_specs,
        out_shape=[jax.ShapeDtypeStruct((m, n), F32)] + rider_shapes,
        scratch_shapes=[pltpu.VMEM((tm + SSD_HALO, k), BF16)],
        compiler_params=_cparams("arbitrary", "arbitrary"),
        name="ssd_in_proj_conv",
    )(x, x, g.reshape(1, k), w_in, conv_w, conv_b.reshape(1, -1), *flat)
    return outs[0], _rider_results(outs[1:], riders)


def _ssd_kernel(zs_ref, xs_ref, bm_ref, cm_ref, dtr_ref, dtb_ref, alog_ref,
                dskip_ref, ng_ref, expand_ref, tril_ref, o_ref,
                wx_ref, expax_ref, cdx_ref, acsg_ref, acst_ref, state_ref, y_ref):
    q = SSD_CHUNK
    c = pl.program_id(1)

    @pl.when(c == 0)
    def _():
        state_ref[...] = jnp.zeros(state_ref.shape, F32)

    dt = _softplus(dtr_ref[...] + dtb_ref[...])
    a = dt * (-jnp.exp(alog_ref[...]))
    a_cs = _dot_f32_rhs(tril_ref[...], a)
    a_last = a_cs[q - 1:q, :]
    expand = expand_ref[...]
    wx_ref[...] = _dot_2piece_lhs(dt * jnp.exp(a_last - a_cs), expand)
    expax_ref[...] = _dot_2piece_lhs(jnp.exp(a_cs), expand)
    cdx_ref[...] = _dot_2piece_lhs(jnp.broadcast_to(jnp.exp(a_last), (8, LANE)), expand)
    acst_ref[...] = (a_cs - jnp.log(dt)).T.reshape(LANE // SSD_HPG, SSD_HPG, q)
    for g in range(SSD_N_GROUPS):
        acsg_ref[g] = a_cs if g == 0 else pltpu.roll(a_cs, LANE - g * SSD_HPG, axis=1)

    row = lax.broadcasted_iota(jnp.int32, (q, q), 0)
    col = lax.broadcasted_iota(jnp.int32, (q, q), 1)
    causal = row >= col
    left_half = col < SSD_HEAD_DIM

    def group_body(g, carry):
        off = pl.multiple_of(g * SSD_GROUP_W, SSD_GROUP_W)
        offn = pl.multiple_of(g * SSD_D_STATE, SSD_D_STATE)
        cm = cm_ref[:, pl.ds(offn, SSD_D_STATE)]
        bm = bm_ref[:, pl.ds(offn, SSD_D_STATE)]
        cmb = cm.astype(BF16)
        cb = _dot_nt(cmb, bm.astype(BF16))
        xs = xs_ref[:, pl.ds(off, SSD_GROUP_W)]
        st = state_ref[g]
        y_off = _dot(cmb, st.astype(BF16)) * expax_ref[:, pl.ds(off, SSD_GROUP_W)]
        s_new = _dot_tn(bm.astype(BF16), (xs * wx_ref[:, pl.ds(off, SSD_GROUP_W)]).astype(BF16))
        state_ref[g] = st * cdx_ref[0:1, pl.ds(off, SSD_GROUP_W)] + s_new
        acs_g = acsg_ref[g]
        acst_g = acst_ref[g]
        skip = xs * dskip_ref[:, pl.ds(off, SSD_GROUP_W)]
        for pr in range(SSD_HPG // 2):
            scs = []
            for e in range(2):
                j = 2 * pr + e
                diff = acs_g[:, j:j + 1] - acst_g[j:j + 1, :]
                dec = jnp.where(causal, jnp.exp(diff), 0.0)
                scs.append((cb * dec).astype(BF16))
            xp = xs[:, pr * LANE:(pr + 1) * LANE]
            x0 = jnp.where(left_half, xp, 0.0).astype(BF16)
            x1 = jnp.where(left_half, 0.0, xp).astype(BF16)
            y_diag = _dot(jnp.concatenate(scs, axis=1), jnp.concatenate([x0, x1], axis=0))
            y = y_diag + y_off[:, pr * LANE:(pr + 1) * LANE] + skip[:, pr * LANE:(pr + 1) * LANE]
            y_ref[:, pl.ds(pl.multiple_of(off + pr * LANE, LANE), LANE)] = y
        return carry

    lax.fori_loop(0, SSD_N_GROUPS, group_body, 0, unroll=8)

    gated = y_ref[...] * zs_ref[...]
    o_ref[...] = _rms(gated, ng_ref[...]).astype(o_ref.dtype)


def _ssd_core(proj, dt_bias, a_log, d_skip, norm_g, bsz, seq_len):
    q = SSD_CHUNK
    nc = seq_len // q
    m = bsz * seq_len
    pad = LANE - SSD_N_HEADS
    dtb = jnp.pad(dt_bias, (0, pad)).reshape(1, LANE)
    alog = jnp.pad(a_log, (0, pad)).reshape(1, LANE)
    dskip = jnp.repeat(d_skip, SSD_HEAD_DIM).reshape(1, SSD_D_INNER)
    heads = np.arange(2 * LANE)[:, None] % LANE
    chans = np.arange(SSD_D_INNER)[None, :] // SSD_HEAD_DIM
    expand = jnp.asarray((heads == chans).astype(np.float32), dtype=BF16)
    tril = jnp.asarray(np.tril(np.ones((q, q), np.float32)), dtype=BF16)
    row = lambda b, c: b * nc + c
    b_blk = 2 * SSD_D_INNER // SSD_BC_DIM
    dt_blk = (2 * SSD_D_INNER + 2 * SSD_BC_DIM) // LANE
    const = lambda b, c: (0, 0)
    return pl.pallas_call(
        _ssd_kernel,
        grid=(bsz, nc),
        in_specs=[
            pl.BlockSpec((q, SSD_D_INNER), lambda b, c: (row(b, c), 0)),
            pl.BlockSpec((q, SSD_D_INNER), lambda b, c: (row(b, c), 1)),
            pl.BlockSpec((q, SSD_BC_DIM), lambda b, c: (row(b, c), b_blk)),
            pl.BlockSpec((q, SSD_BC_DIM), lambda b, c: (row(b, c), b_blk + 1)),
            pl.BlockSpec((q, LANE), lambda b, c: (row(b, c), dt_blk)),
            pl.BlockSpec((1, LANE), const),
            pl.BlockSpec((1, LANE), const),
            pl.BlockSpec((1, SSD_D_INNER), const),
            pl.BlockSpec((1, SSD_D_INNER), const),
            pl.BlockSpec((2 * LANE, SSD_D_INNER), const),
            pl.BlockSpec((q, q), const),
        ],
        out_specs=pl.BlockSpec((q, SSD_D_INNER), lambda b, c: (row(b, c), 0)),
        out_shape=jax.ShapeDtypeStruct((m, SSD_D_INNER), BF16),
        scratch_shapes=[
            pltpu.VMEM((q, SSD_D_INNER), F32),
            pltpu.VMEM((q, SSD_D_INNER), F32),
            pltpu.VMEM((8, SSD_D_INNER), F32),
            pltpu.VMEM((SSD_N_GROUPS, q, LANE), F32),
            pltpu.VMEM((LANE // SSD_HPG, SSD_HPG, q), F32),
            pltpu.VMEM((SSD_N_GROUPS, SSD_D_STATE, SSD_GROUP_W), F32),
            pltpu.VMEM((q, SSD_D_INNER), F32),
        ],
        compiler_params=_cparams("parallel", "arbitrary"),
        name="ssd_core",
    )(proj, proj, proj, proj, proj, dtb, alog, dskip, norm_g.reshape(1, SSD_D_INNER), expand, tril)


def _nsa_compress_kernel(t_ref, pos_ref, w1_ref, w2_ref, o_ref, *, nchunk):
    half = NSA_CMP_STRIDE
    dh = NSA_HEAD_DIM
    p_lo = jnp.zeros((nchunk, NSA_CMP_HIDDEN), F32)
    p_hi = jnp.zeros((nchunk, NSA_CMP_HIDDEN), F32)
    for l in range(half):
        rows = t_ref[pl.ds(l, nchunk, stride=half), :]
        a_lo = (rows + pos_ref[l:l + 1, :]).astype(BF16)
        a_hi = (rows + pos_ref[half + l:half + l + 1, :]).astype(BF16)
        p_lo = p_lo + _dot(a_lo, w1_ref[l * dh:(l + 1) * dh, :].astype(BF16))
        p_hi = p_hi + _dot(a_hi, w1_ref[(half + l) * dh:(half + l + 1) * dh, :].astype(BF16))
    pre = p_lo + pltpu.roll(p_hi, nchunk - 1, axis=0)
    o_ref[...] = _dot(_gelu(pre).astype(BF16), w2_ref[...].astype(BF16))


def _nsa_compress(proj, cmp_pos, cmp_w1, cmp_w2, bsz, seq_len):
    nchunk = seq_len // NSA_CMP_STRIDE
    g, dh = NSA_KV_GROUPS, NSA_HEAD_DIM
    col0 = NSA_N_HEADS * dh // LANE
    return pl.pallas_call(
        functools.partial(_nsa_compress_kernel, nchunk=nchunk),
        grid=(bsz, 2, g),
        in_specs=[
            pl.BlockSpec((seq_len, dh), lambda b, s, gi: (b, col0 + s * g + gi)),
            pl.BlockSpec((None, NSA_CMP_BLOCK, dh), lambda b, s, gi: (s, 0, 0)),
            pl.BlockSpec((None, NSA_CMP_BLOCK * dh, NSA_CMP_HIDDEN), lambda b, s, gi: (s, 0, 0)),
            pl.BlockSpec((None, NSA_CMP_HIDDEN, dh), lambda b, s, gi: (s, 0, 0)),
        ],
        out_specs=pl.BlockSpec((None, None, None, nchunk, dh), lambda b, s, gi: (b, s, gi, 0, 0)),
        out_shape=jax.ShapeDtypeStruct((bsz, 2, g, nchunk, dh), F32),
        compiler_params=_cparams("parallel", "parallel", "parallel"),
        name="nsa_compress",
    )(proj, cmp_pos, cmp_w1, cmp_w2)


def _nsa_select_kernel(q_ref, kc_ref, vc_ref, cover_ref, ocmp_ref, sel_ref, *, ncmp, nslc):
    tq = NSA_SEL_TQ
    dh = NSA_HEAD_DIM
    scale = dh ** -0.5
    nck = kc_ref.shape[0]
    t0 = pl.program_id(2) * tq
    t_row = t0 + lax.broadcasted_iota(jnp.int32, (1, tq), 1)
    ci_col = lax.broadcasted_iota(jnp.int32, (nck, 1), 0)
    valid_t = (ci_col * NSA_CMP_STRIDE + (NSA_CMP_BLOCK - 1) <= t_row) & (ci_col < ncmp)
    c = scale * math.log2(math.e)
    bias_t = jnp.where(valid_t, 0.0, NEG)
    some_t = jnp.where(t_row >= NSA_CMP_BLOCK - 1, 1.0, 0.0)
    kcb = kc_ref[...].astype(BF16)
    vcb = vc_ref[...].astype(BF16)
    psum_t = jnp.zeros((nck, tq), F32)
    for j in range(NSA_HPG):
        qj = q_ref[:, j * dh:(j + 1) * dh].astype(BF16)
        st = _dot_nt(kcb, qj) * c + bias_t
        et = jnp.exp2(st - jnp.max(st, axis=0, keepdims=True))
        pt = et * (some_t / jnp.sum(et, axis=0, keepdims=True))
        psum_t = psum_t + pt
        ocmp_ref[:, j * dh:(j + 1) * dh] = _dot_tn(pt.astype(BF16), vcb)
    imp_t = _dot_f32_rhs(cover_ref[...], psum_t)
    blk = lax.broadcasted_iota(jnp.int32, (nslc, tq), 0)
    cur = t_row // NSA_SLC_BLOCK
    forced = (blk == 0) | ((blk <= cur) & (blk > cur - 1 - NSA_N_LOCAL))
    future = blk * NSA_SLC_BLOCK > t_row
    score = jnp.where(forced, BIG, jnp.where(future, NEG, imp_t))
    sub = F32_SUBLANES
    slabs = [score[r * sub:(r + 1) * sub, :] for r in range(nslc // sub)]
    ranks = [jnp.zeros((sub, tq), F32) for _ in slabs]
    row_in_slab = lax.broadcasted_iota(jnp.int32, (sub, tq), 0)
    for j in range(nslc):
        cj = jnp.broadcast_to(score[j:j + 1, :], (sub, tq))
        for r, s_r in enumerate(slabs):
            if r * sub > j:
                before = cj >= s_r
            elif (r + 1) * sub - 1 < j:
                before = cj > s_r
            else:
                before = (cj > s_r) | ((cj == s_r) & (row_in_slab > j - r * sub))
            ranks[r] = ranks[r] + jnp.where(before, 1.0, 0.0)
    topk = float(min(NSA_TOPK, nslc))
    for r, rank in enumerate(ranks):
        sel_ref[r * sub:(r + 1) * sub, :] = jnp.where(rank < topk, 1.0, 0.0)


def _nsa_select(proj, kvc, bsz, seq_len):
    tq, dh, g = NSA_SEL_TQ, NSA_HEAD_DIM, NSA_KV_GROUPS
    nq = seq_len // tq
    ncmp = (seq_len - NSA_CMP_BLOCK) // NSA_CMP_STRIDE + 1
    nck = seq_len // NSA_CMP_STRIDE
    nslc = seq_len // NSA_SLC_BLOCK
    ci = np.arange(nck)[:, None] * NSA_CMP_STRIDE
    sj = np.arange(nslc)[None, :] * NSA_SLC_BLOCK
    cover = ((ci <= sj + NSA_SLC_BLOCK - 1) & (ci + NSA_CMP_BLOCK - 1 >= sj)).astype(np.float32)
    cover = jnp.asarray(cover.T, dtype=BF16)
    return pl.pallas_call(
        functools.partial(_nsa_select_kernel, ncmp=ncmp, nslc=nslc),
        grid=(bsz, g, nq),
        in_specs=[
            pl.BlockSpec((tq, NSA_HPG * dh), lambda b, gi, qi: (b * nq + qi, gi)),
            pl.BlockSpec((None, None, None, nck, dh), lambda b, gi, qi: (b, 0, gi, 0, 0)),
            pl.BlockSpec((None, None, None, nck, dh), lambda b, gi, qi: (b, 1, gi, 0, 0)),
            pl.BlockSpec((nslc, nck), lambda b, gi, qi: (0, 0)),
        ],
        out_specs=[
            pl.BlockSpec((tq, NSA_HPG * dh), lambda b, gi, qi: (b * nq + qi, gi)),
            pl.BlockSpec((None, None, nslc, tq), lambda b, gi, qi: (b, gi, 0, qi)),
        ],
        out_shape=[
            jax.ShapeDtypeStruct((bsz * seq_len, NSA_N_HEADS * dh), F32),
            jax.ShapeDtypeStruct((bsz, g, nslc, seq_len), F32),
        ],
        compiler_params=_cparams("parallel", "parallel", "parallel"),
        name="nsa_cmp_select",
    )(proj, kvc, kvc, cover)


def _nsa_attn_kernel(q_ref, ks_ref, vs_ref, kw_ref, vw_ref, sel_ref, expand_ref, ocmp_ref, gl_ref,
                     o_ref, ksb_ref, vsx_ref, kwb_ref, vwx_ref, selb_ref, q4_ref, pw_ref, m_ref, acc_ref):
    tq = NSA_TQ
    dh = NSA_HEAD_DIM
    kt_w = NSA_KEY_TILE
    win_w = NSA_WINDOW + tq
    c = (dh ** -0.5) * math.log2(math.e)
    qi = pl.program_id(2)

    @pl.when(qi == 0)
    def _():
        ones = jnp.ones((ks_ref.shape[0], dh), BF16)
        for blk in range(ks_ref.shape[0] // LANE):
            cols = slice(blk * LANE, (blk + 1) * LANE)
            ksb_ref[:, cols] = ks_ref[cols, :].T.astype(BF16)
            kwb_ref[:, cols] = kw_ref[cols, :].T.astype(BF16)
        vsx_ref[:, 0:dh] = vs_ref[...].astype(BF16)
        vsx_ref[:, dh:] = ones
        vwx_ref[:, 0:dh] = vw_ref[...].astype(BF16)
        vwx_ref[:, dh:] = ones

    selb_ref[...] = sel_ref[...].T.astype(BF16)
    for j in range(NSA_HPG):
        q4_ref[j * tq:(j + 1) * tq, :] = (q_ref[:, j * dh:(j + 1) * dh] * c).astype(BF16)
    t1 = qi * tq + lax.broadcasted_iota(jnp.int32, (tq, 1), 0)
    m_ref[...] = jnp.full(m_ref.shape, NEG, F32)
    acc_ref[...] = jnp.zeros(acc_ref.shape, F32)
    half = NSA_HPG // 2 * tq

    def slc_tile(off, width, on_diagonal):
        chosen = _dot(selb_ref[...], expand_ref[:, pl.ds(off, width)])
        bias = (chosen - 1.0) * BIG
        if on_diagonal:
            pos = off + lax.broadcasted_iota(jnp.int32, (1, width), 1)
            bias = jnp.where(pos <= t1, bias, NEG)
        k_t = ksb_ref[:, pl.ds(off, width)]
        v_x = vsx_ref[pl.ds(off, width), :]
        pairs = [slice(hp * half, (hp + 1) * half) for hp in range(2)]
        scores = [_dot(q4_ref[pr], k_t) for pr in pairs]
        for hp, pr in enumerate(pairs):
            m_prev = m_ref[pr]
            ps, m_news = [], []
            for e in range(NSA_HPG // 2):
                sj = scores[hp][e * tq:(e + 1) * tq] + bias
                m_new = jnp.maximum(m_prev[e * tq:(e + 1) * tq], jnp.max(sj, axis=-1, keepdims=True))
                m_wide = jnp.concatenate([m_new] * (width // LANE), axis=1)
                ps.append(jnp.exp2(sj - m_wide).astype(BF16))
                m_news.append(m_new)
            m_new = jnp.concatenate(m_news, axis=0)
            alpha = jnp.exp2(m_prev - m_new)
            m_ref[pr] = m_new
            pv = _dot(jnp.concatenate(ps, axis=0), v_x)
            acc_ref[pr] = jnp.concatenate([alpha, alpha], axis=1) * acc_ref[pr] + pv

    n_wide = (qi * tq) // kt_w

    def wide_step(kt, carry):
        slc_tile(pl.multiple_of(kt * kt_w, kt_w), kt_w, False)
        return carry

    def narrow_step(s, carry):
        slc_tile(pl.multiple_of(n_wide * kt_w + s * tq, tq), tq, True)
        return carry

    lax.fori_loop(0, n_wide, wide_step, 0)
    lax.fori_loop(0, qi + 1 - n_wide * (kt_w // tq), narrow_step, 0)

    w0 = pl.multiple_of(jnp.maximum(qi * tq - NSA_WINDOW, 0), tq)
    sw = _dot(q4_ref[...], kwb_ref[:, pl.ds(w0, win_w)])
    diff = t1 - (w0 + lax.broadcasted_iota(jnp.int32, (1, win_w), 1))
    bias_w = jnp.where((diff >= 0) & (diff < NSA_WINDOW), 0.0, NEG)
    for j in range(NSA_HPG):
        rows = slice(j * tq, (j + 1) * tq)
        sj = sw[rows] + bias_w
        pw_ref[rows] = jnp.exp2(sj - jnp.max(sj, axis=-1, keepdims=True)).astype(BF16)
    ow = _dot(pw_ref[...], vwx_ref[pl.ds(w0, win_w), :])

    gates = _sigmoid(gl_ref[...])
    for j in range(NSA_HPG):
        rows = slice(j * tq, (j + 1) * tq)
        o_slc = acc_ref[rows, 0:dh] / acc_ref[rows, dh:]
        o_win = ow[rows, 0:dh] / ow[rows, dh:]
        o = (gates[:, 3 * j:3 * j + 1] * ocmp_ref[:, j * dh:(j + 1) * dh]
             + gates[:, 3 * j + 1:3 * j + 2] * o_slc
             + gates[:, 3 * j + 2:3 * j + 3] * o_win)
        o_ref[:, j * dh:(j + 1) * dh] = o.astype(o_ref.dtype)


def _nsa_attn(proj, sel, ocmp, gate_logits, bsz, seq_len):
    tq, dh, g = NSA_TQ, NSA_HEAD_DIM, NSA_KV_GROUPS
    assert seq_len % NSA_KEY_TILE == 0 and seq_len >= NSA_WINDOW + tq
    rows = NSA_HPG * tq
    nq = seq_len // tq
    nslc = seq_len // NSA_SLC_BLOCK
    expand = (np.arange(nslc)[:, None] == np.arange(seq_len)[None, :] // NSA_SLC_BLOCK)
    expand = jnp.asarray(expand.astype(np.float32), dtype=BF16)
    kv0 = (NSA_N_HEADS * dh + 2 * NSA_KV_DIM) // dh

    def kv_spec(idx):
        return pl.BlockSpec((seq_len, dh), lambda b, gi, qi: (b, kv0 + idx * g + gi))

    qo_spec = pl.BlockSpec((tq, NSA_HPG * dh), lambda b, gi, qi: (b * nq + qi, gi))
    return pl.pallas_call(
        _nsa_attn_kernel,
        grid=(bsz, g, nq),
        in_specs=[
            qo_spec,
            kv_spec(0), kv_spec(1), kv_spec(2), kv_spec(3),
            pl.BlockSpec((None, None, nslc, tq), lambda b, gi, qi: (b, gi, 0, qi)),
            pl.BlockSpec((nslc, seq_len), lambda b, gi, qi: (0, 0)),
            qo_spec,
            pl.BlockSpec((None, tq, 3 * NSA_HPG), lambda b, gi, qi: (gi, b * nq + qi, 0)),
        ],
        out_specs=qo_spec,
        out_shape=jax.ShapeDtypeStruct((bsz * seq_len, NSA_N_HEADS * dh), BF16),
        scratch_shapes=[
            pltpu.VMEM((dh, seq_len), BF16),
            pltpu.VMEM((seq_len, 2 * dh), BF16),
            pltpu.VMEM((dh, seq_len), BF16),
            pltpu.VMEM((seq_len, 2 * dh), BF16),
            pltpu.VMEM((tq, nslc), BF16),
            pltpu.VMEM((rows, dh), BF16),
            pltpu.VMEM((rows, NSA_WINDOW + tq), BF16),
            pltpu.VMEM((rows, LANE), F32),
            pltpu.VMEM((rows, 2 * dh), F32),
        ],
        compiler_params=_cparams("parallel", "parallel", "arbitrary"),
        name="nsa_slc_win_attention",
    )(proj, proj, proj, proj, proj, sel, expand, ocmp, gate_logits)


def _sgu_kernel(u_ref, v_ref, g_ref, ws_ref, bst_ref, o_ref):
    v = v_ref[...].astype(F32)
    mu = jnp.mean(v, axis=-1, keepdims=True)
    vc = v - mu
    vn = (vc * lax.rsqrt(jnp.mean(vc * vc, axis=-1, keepdims=True) + RMS_EPS) * g_ref[...]).astype(BF16)
    q = SGU_CHUNK
    tri = lax.broadcasted_iota(jnp.int32, (q, q), 0) >= lax.broadcasted_iota(jnp.int32, (q, q), 1)
    for g in range(SGU_GROUPS):
        cols = slice(g * SGU_GROUP_DIM, (g + 1) * SGU_GROUP_DIM)
        w_m = jnp.where(tri, ws_ref[g], 0.0).astype(BF16)
        sv = _dot(w_m, vn[:, cols]) + bst_ref[:, g:g + 1]
        o_ref[:, cols] = (u_ref[:, cols].astype(F32) * sv).astype(o_ref.dtype)


def _sgu_core(proj, ln_g, w_spatial, b_spatial):
    m = proj.shape[0]
    q = SGU_CHUNK
    return pl.pallas_call(
        _sgu_kernel,
        grid=(m // q,),
        in_specs=[
            pl.BlockSpec((q, SGU_WIDTH), lambda i: (i, 0)),
            pl.BlockSpec((q, SGU_WIDTH), lambda i: (i, 1)),
            pl.BlockSpec((1, SGU_WIDTH), lambda i: (0, 0)),
            pl.BlockSpec((SGU_GROUPS, q, q), lambda i: (0, 0, 0)),
            pl.BlockSpec((q, SGU_GROUPS), lambda i: (0, 0)),
        ],
        out_specs=pl.BlockSpec((q, SGU_WIDTH), lambda i: (i, 0)),
        out_shape=jax.ShapeDtypeStruct((m, SGU_WIDTH), BF16),
        compiler_params=_cparams("parallel"),
        name="sgu_core",
    )(proj, proj, ln_g.reshape(1, SGU_WIDTH), w_spatial, b_spatial.T)


def _pool_kernel(z_ref, zh_ref, wg_ref, sc_ref, o_ref, zz_ref, *, tm, blocks_per_seq):
    i = pl.program_id(0)
    keep = (i % blocks_per_seq != 0).astype(F32)
    zz_ref[0:POOL_HALO, :] = zh_ref[...] * keep
    zz_ref[POOL_HALO:, :] = z_ref[...]
    t = (i % blocks_per_seq) * tm + lax.broadcasted_iota(jnp.int32, (tm, 1), 0)
    for gi, win in enumerate(POOL_WINDOWS):
        cols = slice(gi * POOL_GROUP_DIM, (gi + 1) * POOL_GROUP_DIM)
        s = zz_ref[POOL_HALO:POOL_HALO + tm, cols]
        for k in range(1, win):
            s = s + zz_ref[POOL_HALO - k:POOL_HALO - k + tm, cols]
        count = jnp.minimum(t + 1, win).astype(F32)
        pooled = s / count - z_ref[:, cols]
        y = _dot(pooled.astype(BF16), wg_ref[gi]) * sc_ref[:, cols]
        o_ref[:, cols] = y.astype(o_ref.dtype)


def _pool_core(z, w_group, scale, seq_len, *, tm=512):
    m, n = z.shape
    tm = min(tm, seq_len)
    halo_blocks = tm // POOL_HALO
    return pl.pallas_call(
        functools.partial(_pool_kernel, tm=tm, blocks_per_seq=seq_len // tm),
        grid=(m // tm,),
        in_specs=[
            pl.BlockSpec((tm, n), lambda i: (i, 0)),
            pl.BlockSpec((POOL_HALO, n), lambda i: (jnp.maximum(i * halo_blocks - 1, 0), 0)),
            pl.BlockSpec(w_group.shape, lambda i: (0, 0, 0)),
            pl.BlockSpec((1, n), lambda i: (0, 0)),
        ],
        out_specs=pl.BlockSpec((tm, n), lambda i: (i, 0)),
        out_shape=jax.ShapeDtypeStruct((m, n), BF16),
        scratch_shapes=[pltpu.VMEM((tm + POOL_HALO, n), F32)],
        compiler_params=_cparams("parallel"),
        name="pool_core",
    )(z, z, w_group, scale.reshape(1, n))


def _pad_cols(w, n):
    return jnp.pad(w, ((0, 0), (0, 0), (0, n - w.shape[2])))


def _ssd_layer(x, g_pre, g_post, w_in, j, conv_w, conv_b, dt_bias, a_log, d_skip, norm_g, w_out, bsz, seq_len):
    proj, _ = _ssd_in_proj(x, g_pre, w_in, j, conv_w, conv_b, seq_len)
    y = _ssd_core(proj, dt_bias, a_log, d_skip, norm_g, bsz, seq_len)
    return _matmul_post(y, w_out, j, g_post, x)


def _first_ssd_layer(x, g_pre, g_post, w_in, conv_w, conv_b, dt_bias, a_log, d_skip, norm_g, riders,
                     bsz, seq_len):
    proj, cast = _ssd_in_proj(x, g_pre, w_in, 0, conv_w, conv_b, seq_len, tuple(riders.values()))
    wb = dict(zip(riders, cast))
    y = _ssd_core(proj, dt_bias, a_log, d_skip, norm_g, bsz, seq_len)
    return _matmul_post(y, wb["ssd_out"], 0, g_post, x), wb


def _nsa_layer(x, g_pre, g_post, w_in, j, cmp_pos, cmp_w1, cmp_w2, w_out, bsz, seq_len):
    proj = _norm_matmul(x, g_pre, w_in, j, jnp.zeros((NSA_IN_PAD,), F32))
    kvc = _nsa_compress(proj, cmp_pos, cmp_w1, cmp_w2, bsz, seq_len)
    ocmp, sel = _nsa_select(proj, kvc, bsz, seq_len)
    ngate = 3 * NSA_N_HEADS
    gl = proj[:, NSA_GATE_COL:NSA_GATE_COL + ngate].reshape(-1, NSA_KV_GROUPS, 3 * NSA_HPG).transpose(1, 0, 2)
    o = _nsa_attn(proj, sel, ocmp, gl, bsz, seq_len)
    return _matmul_post(o, w_out, j, g_post, x)


def _sgu_layer(x, g_pre, g_post, w_in, j, b_in, ln_g, w_spatial, b_spatial, w_out):
    proj = _norm_matmul(x, g_pre, w_in, j, b_in, act="gelu", out_dtype=BF16)
    y = _sgu_core(proj, ln_g, w_spatial, b_spatial)
    return _matmul_post(y, w_out, j, g_post, x)


def _pool_layer(x, g_pre, g_post, w_in, j, w_group, scale, w_out, seq_len):
    z = _norm_matmul(x, g_pre, w_in, j, jnp.zeros((D_MODEL,), F32))
    y = _pool_core(z, w_group.astype(BF16), scale, seq_len)
    return _matmul_post(y, w_out, j, g_post, x)


def kernel(x, mem, norm_pre, norm_post, norm_mem, ssd_w_in, ssd_conv_w, ssd_conv_b, ssd_dt_bias, ssd_a_log, ssd_d, ssd_norm_g, ssd_w_out, nsa_w_in, nsa_cmp_pos, nsa_cmp_w1, nsa_cmp_w2, nsa_w_out, sgu_w_in, sgu_b_in, sgu_ln_g, sgu_w_spatial, sgu_b_spatial, sgu_w_out, pool_w_in, pool_w_group, pool_scale, pool_w_out, xa_w_q, xa_w_kv, xa_w_o, ffn_w_up, ffn_conv_w, ffn_conv_b, ffn_w_down):
    bsz, seq_len, d = x.shape
    depth = norm_pre.shape[0]
    xf = x.reshape(bsz * seq_len, d)
    memf = mem.reshape(bsz * mem.shape[1], d)
    ssd_in = _pad_cols(ssd_w_in, SSD_IN_PAD).astype(BF16)
    nsa_in = _pad_cols(nsa_w_in, NSA_IN_PAD).astype(BF16)
    whole = lambda w: (w, 0, w.shape[0])
    riders0 = dict(ssd_out=whole(ssd_w_out), xa_q=whole(xa_w_q), xa_kv=whole(xa_w_kv), xa_o=whole(xa_w_o),
                   ffn_up=(ffn_w_up, 0, 1), ffn_down=(ffn_w_down, 0, 1))
    riders1 = dict(ffn_up_rest=(ffn_w_up, 1, depth - 1), ffn_down_rest=(ffn_w_down, 1, depth - 1),
                   nsa_out=whole(nsa_w_out), sgu_in=whole(sgu_w_in), sgu_out=whole(sgu_w_out),
                   pool_in=whole(pool_w_in), pool_out=whole(pool_w_out)) if depth > 1 else {}
    wb = None
    for i in range(depth):
        kind, j = i % 4, i // 4
        if i == 0:
            xf, wb = _first_ssd_layer(xf, norm_pre[i, 0], norm_post[i, 0], ssd_in, ssd_conv_w[j], ssd_conv_b[j],
                                      ssd_dt_bias[j], ssd_a_log[j], ssd_d[j], ssd_norm_g[j], riders0,
                                      bsz, seq_len)
        elif kind == 0:
            xf = _ssd_layer(xf, norm_pre[i, 0], norm_post[i, 0], ssd_in, j, ssd_conv_w[j], ssd_conv_b[j],
                            ssd_dt_bias[j], ssd_a_log[j], ssd_d[j], ssd_norm_g[j], wb["ssd_out"], bsz, seq_len)
        elif kind == 1:
            xf = _nsa_layer(xf, norm_pre[i, 0], norm_post[i, 0], nsa_in, j, nsa_cmp_pos[j], nsa_cmp_w1[j],
                            nsa_cmp_w2[j], wb["nsa_out"], bsz, seq_len)
        elif kind == 2:
            xf = _sgu_layer(xf, norm_pre[i, 0], norm_post[i, 0], wb["sgu_in"], j, sgu_b_in[j], sgu_ln_g[j],
                            sgu_w_spatial[j], sgu_b_spatial[j], wb["sgu_out"])
        else:
            xf = _pool_layer(xf, norm_pre[i, 0], norm_post[i, 0], wb["pool_in"], j, pool_w_group[j],
                             pool_scale[j], wb["pool_out"], seq_len)
        kv = _norm_matmul(memf, norm_mem[i], wb["xa_kv"], i, jnp.zeros((2 * XA_DIM,), F32),
                          out_dtype=BF16, tm=MEM_LEN)
        xf = _xattn(xf, kv, norm_pre[i, 1], norm_post[i, 1], wb["xa_q"], wb["xa_o"], i, seq_len)
        if i == 0:
            act, cast = _ffn_up(xf, norm_pre[i, 2], wb["ffn_up"], 0, ffn_conv_w[i], ffn_conv_b[i], seq_len,
                                tuple(riders1.values()))
            wb.update(zip(riders1, cast))
            ffn_up_w, ffn_down_w, first = wb["ffn_up"], wb["ffn_down"], 0
        else:
            ffn_up_w, ffn_down_w, first = wb["ffn_up_rest"], wb["ffn_down_rest"], 1
            act, _ = _ffn_up(xf, norm_pre[i, 2], ffn_up_w, i - first, ffn_conv_w[i], ffn_conv_b[i], seq_len)
        xf = _matmul_post(act, ffn_down_w, i - first, norm_post[i, 2], xf)
    return xf.reshape(bsz, seq_len, d)
```
